```python
import math
import jax, jax.numpy as jnp
from jax import lax
import numpy as np

D_MODEL = 1024
BATCH = 8
SEQ = 8192
DEPTH = 4

CHUNK = 64
N_PREV = 8
BAND = (N_PREV + 1) * CHUNK
N_HEADS = 16
HEAD_DIM = D_MODEL // N_HEADS
E_MIX = N_HEADS * HEAD_DIM
REL_CLIP = 128
N_REL = 2 * REL_CLIP + 1
CONV_W = 3
N_MEM = 256
MEM_HEADS = 4
MEM_HEAD_DIM = 128
E_MEM = MEM_HEADS * MEM_HEAD_DIM
E_BRANCH = E_MIX + E_MEM
N_IN = 3 * E_MIX + E_MEM + E_BRANCH
N_MIXERS = 2
N_ATTN_LAYERS = (DEPTH + 1) // 2
N_CONV_LAYERS = DEPTH // 2
DN_ALPHA = (2.0 * DEPTH) ** 0.25
DN_BETA = (8.0 * DEPTH) ** -0.25
LN_EPS = 1e-5

kernel_name = "hybrid_chunk_attn_shortconv_mem_deepnorm"


def layer_norm(x, g, b):
    xf = x.astype(jnp.float32)
    mu = jnp.mean(xf, axis=-1, keepdims=True)
    var = jnp.mean(jnp.square(xf - mu), axis=-1, keepdims=True)
    y = (xf - mu) * lax.rsqrt(var + LN_EPS) * g.astype(jnp.float32) + b.astype(jnp.float32)
    return y.astype(x.dtype)


def rel_bias_band(table):
    i = jnp.arange(CHUNK)[:, None]
    m = jnp.arange(BAND)[None, :]
    rel = N_PREV * CHUNK + i - m
    idx = jnp.clip(rel, -REL_CLIP, REL_CLIP) + REL_CLIP
    return table[:, idx]


def chunked_attention(q, k, v, bias_table):
    b, s, h, dh = q.shape
    n_chunks = s // CHUNK
    pad = ((0, 0), (N_PREV * CHUNK, 0), (0, 0), (0, 0))
    k_pad = jnp.pad(k, pad)
    v_pad = jnp.pad(v, pad)
    bias = rel_bias_band(bias_table).astype(jnp.float32)
    scale = 1.0 / math.sqrt(dh)
    q_blocks = jnp.moveaxis(q.reshape(b, n_chunks, CHUNK, h, dh), 1, 0)
    neg = jnp.finfo(jnp.float32).min

    def one_chunk(args):
        q_blk, c = args
        k_band = lax.dynamic_slice_in_dim(k_pad, c * CHUNK, BAND, axis=1)
        v_band = lax.dynamic_slice_in_dim(v_pad, c * CHUNK, BAND, axis=1)
        sc = jnp.einsum('bqhd,bkhd->bhqk', q_blk, k_band).astype(jnp.float32) * scale + bias[None]
        key_pos = (c - N_PREV) * CHUNK + jnp.arange(BAND)
        sc = jnp.where((key_pos >= 0)[None, None, None, :], sc, neg)
        p = jax.nn.softmax(sc, axis=-1).astype(v_band.dtype)
        return jnp.einsum('bhqk,bkhd->bqhd', p, v_band)

    out = lax.map(one_chunk, (q_blocks, jnp.arange(n_chunks)))
    return jnp.moveaxis(out, 0, 1).reshape(b, s, h * dh)


def causal_dwconv(u, w):
    c = u.shape[-1]
    return lax.conv_general_dilated(
        u, w[:, None, :].astype(u.dtype), window_strides=(1,),
        padding=[(CONV_W - 1, 0)], dimension_numbers=('NWC', 'WIO', 'NWC'),
        feature_group_count=c)


def short_gated_conv(bg, cg, u, w):
    return bg * causal_dwconv(cg * u, w)


def memory_attention(q_mem, kv_mem):
    b, s, _ = q_mem.shape
    q = q_mem.reshape(b, s, MEM_HEADS, MEM_HEAD_DIM)
    k, v = jnp.split(kv_mem, 2, axis=-1)
    k = k.reshape(b, -1, MEM_HEADS, MEM_HEAD_DIM)
    v = v.reshape(b, -1, MEM_HEADS, MEM_HEAD_DIM)
    sc = jnp.einsum('bshd,bmhd->bhsm', q, k).astype(jnp.float32) / math.sqrt(MEM_HEAD_DIM)
    p = jax.nn.softmax(sc, axis=-1).astype(v.dtype)
    return jnp.einsum('bhsm,bmhd->bshd', p, v).reshape(b, s, E_MEM)


def _fwd_setup_inputs(seed: int = 0) -> dict:
    key = jax.random.key(seed)
    ks = jax.random.split(key, 10)
    x = jax.random.normal(ks[0], (BATCH, SEQ, D_MODEL), jnp.float32)
    mem = jax.random.normal(ks[1], (BATCH, N_MEM, D_MODEL), jnp.float32)
    w_in = jax.random.normal(ks[2], (DEPTH, D_MODEL, N_IN), jnp.float32) * D_MODEL ** -0.5
    w_mem_kv = jax.random.normal(ks[3], (DEPTH, D_MODEL, 2 * E_MEM), jnp.float32) * D_MODEL ** -0.5
    w_out = jax.random.normal(ks[4], (DEPTH, E_BRANCH, D_MODEL), jnp.float32) * (E_BRANCH ** -0.5 * DN_BETA)
    rel_bias = jax.random.normal(ks[5], (N_ATTN_LAYERS, N_HEADS, N_REL), jnp.float32) * 0.5
    conv_w = jax.random.normal(ks[6], (N_CONV_LAYERS, CONV_W, E_MIX), jnp.float32) * CONV_W ** -0.5
    ln_g = 1.0 + 0.05 * jax.random.normal(ks[7], (DEPTH, D_MODEL), jnp.float32)
    ln_b = 0.02 * jax.random.normal(ks[8], (DEPTH, D_MODEL), jnp.float32)
    return {"x": x, "mem": mem, "w_in": w_in, "w_mem_kv": w_mem_kv, "w_out": w_out,
            "rel_bias": rel_bias, "conv_w": conv_w, "ln_g": ln_g, "ln_b": ln_b}


def _fwd_reference(x, mem, w_in, w_mem_kv, w_out, rel_bias, conv_w, ln_g, ln_b):
    b, s, _ = x.shape
    for layer in range(DEPTH):
        h = jnp.einsum('bsd,de->bse', x, w_in[layer])
        mix_in = h[..., :3 * E_MIX]
        q_mem = h[..., 3 * E_MIX:3 * E_MIX + E_MEM]
        z = h[..., 3 * E_MIX + E_MEM:]
        p0, p1, p2 = jnp.split(mix_in, 3, axis=-1)
        if layer % N_MIXERS == 0:
            q = p0.reshape(b, s, N_HEADS, HEAD_DIM)
            k = p1.reshape(b, s, N_HEADS, HEAD_DIM)
            v = p2.reshape(b, s, N_HEADS, HEAD_DIM)
            mix_out = chunked_attention(q, k, v, rel_bias[layer // N_MIXERS])
        else:
            mix_out = short_gated_conv(p0, p1, p2, conv_w[layer // N_MIXERS])
        kv_mem = jnp.einsum('bmd,de->bme', mem, w_mem_kv[layer])
        mem_out = memory_attention(q_mem, kv_mem)
        y = jnp.concatenate([mix_out, mem_out], axis=-1) * jax.nn.silu(z)
        out = jnp.einsum('bse,ed->bsd', y, w_out[layer])
        x = layer_norm(DN_ALPHA * x + out, ln_g[layer], ln_b[layer])
    return x


import jax as _jax
import jax.numpy as _jnp

TWIN_FORMAT = 'train_step'
FWD_PARAMS = ['x', 'mem', 'w_in', 'w_mem_kv', 'w_out', 'rel_bias', 'conv_w', 'ln_g', 'ln_b']
TWIN_WEIGHTS = ['w_in', 'w_mem_kv', 'w_out', 'rel_bias', 'conv_w', 'ln_g', 'ln_b']
TWIN_DIFF_INPUT = 'x'
TWIN_INPUTS = ['x', 'mem', 'w_in', 'w_mem_kv', 'w_out', 'rel_bias', 'conv_w', 'ln_g', 'ln_b', 'loss_target', 'm_w_in', 'm_w_mem_kv', 'm_w_out', 'm_rel_bias', 'm_conv_w', 'm_ln_g', 'm_ln_b', 'v_w_in', 'v_w_mem_kv', 'v_w_out', 'v_rel_bias', 'v_conv_w', 'v_ln_g', 'v_ln_b']
TWIN_OUTPUTS = ['loss', 'grad_x', 'grad_w_in', 'grad_w_mem_kv', 'grad_w_out', 'grad_rel_bias', 'grad_conv_w', 'grad_ln_g', 'grad_ln_b', 'delta_w_in', 'delta_w_mem_kv', 'delta_w_out', 'delta_rel_bias', 'delta_conv_w', 'delta_ln_g', 'delta_ln_b', 'new_m_w_in', 'new_m_w_mem_kv', 'new_m_w_out', 'new_m_rel_bias', 'new_m_conv_w', 'new_m_ln_g', 'new_m_ln_b', 'new_v_w_in', 'new_v_w_mem_kv', 'new_v_w_out', 'new_v_rel_bias', 'new_v_conv_w', 'new_v_ln_g', 'new_v_ln_b']
TWIN_LEAF_KINDS = {'loss': 'loss', 'grad_x': 'grad_x', 'grad_w_in': 'grad_w', 'grad_w_mem_kv': 'grad_w', 'grad_w_out': 'grad_w', 'grad_rel_bias': 'grad_w', 'grad_conv_w': 'grad_w', 'grad_ln_g': 'grad_w', 'grad_ln_b': 'grad_w', 'delta_w_in': 'delta_w', 'delta_w_mem_kv': 'delta_w', 'delta_w_out': 'delta_w', 'delta_rel_bias': 'delta_w', 'delta_conv_w': 'delta_w', 'delta_ln_g': 'delta_w', 'delta_ln_b': 'delta_w', 'new_m_w_in': 'new_m', 'new_m_w_mem_kv': 'new_m', 'new_m_w_out': 'new_m', 'new_m_rel_bias': 'new_m', 'new_m_conv_w': 'new_m', 'new_m_ln_g': 'new_m', 'new_m_ln_b': 'new_m', 'new_v_w_in': 'new_v', 'new_v_w_mem_kv': 'new_v', 'new_v_w_out': 'new_v', 'new_v_rel_bias': 'new_v', 'new_v_conv_w': 'new_v', 'new_v_ln_g': 'new_v', 'new_v_ln_b': 'new_v'}


def _forward(args):
    return _fwd_reference(*[args[k] for k in FWD_PARAMS])


def _output_shape():
    def fwd():
        inp = _fwd_setup_inputs(0)
        return _fwd_reference(*[inp[k] for k in FWD_PARAMS])
    out = _jax.eval_shape(fwd)
    return out.shape, out.dtype

N_MICROBATCH = 1
ADAM_LR = 0.001
ADAM_B1 = 0.9
ADAM_B2 = 0.999
ADAM_EPS = 1e-08
ADAM_WD = 0.01
ADAM_STEP = 10
PER_EXAMPLE_BATCH_AXIS = {'x': 0, 'mem': 0, 'loss_target': 0}
SHARED_INPUTS = []
_WEIGHT_DTYPES = {'w_in': _jnp.float32, 'w_mem_kv': _jnp.float32, 'w_out': _jnp.float32, 'rel_bias': _jnp.float32, 'conv_w': _jnp.float32, 'ln_g': _jnp.float32, 'ln_b': _jnp.float32}
MOMENT_SCALE = {'w_in': 2.001682e-02, 'w_mem_kv': 3.337138e-03, 'w_out': 5.297034e-02, 'rel_bias': 2.365559e-03, 'conv_w': 3.099265e-02, 'ln_g': 3.291883e+01, 'ln_b': 1.406438e+00}


def _to_microbatches(a, axis):
    t = _jnp.moveaxis(a, axis, 0)
    t = t.reshape((N_MICROBATCH, t.shape[0] // N_MICROBATCH) + t.shape[1:])
    return _jnp.moveaxis(t, 1, axis + 1)


def setup_inputs(seed: int = 0) -> dict:
    inp = _fwd_setup_inputs(seed)
    key = _jax.random.fold_in(_jax.random.key(seed), 7919)
    shape, _ = _output_shape()
    out = dict(inp)
    out["loss_target"] = _jax.random.normal(_jax.random.fold_in(key, 0), shape, _jnp.float32)
    for i, name in enumerate(TWIN_WEIGHTS):
        w = inp[name].astype(_jnp.float32)
        if MOMENT_SCALE is None:
            s = _jnp.sqrt(_jnp.mean(_jnp.square(w)) + 1e-30)
        else:
            s = MOMENT_SCALE[name]
        km, kv = _jax.random.split(_jax.random.fold_in(key, i + 1))
        out[name] = w
        out["m_" + name] = s * _jax.random.normal(km, w.shape, _jnp.float32)
        out["v_" + name] = (s * s) * _jax.random.uniform(kv, w.shape, _jnp.float32, 0.5, 1.5)
    if N_MICROBATCH > 1:
        for name, axis in PER_EXAMPLE_BATCH_AXIS.items():
            out[name] = _to_microbatches(out[name], axis)
    return {'x': out['x'], 'mem': out['mem'], 'w_in': out['w_in'], 'w_mem_kv': out['w_mem_kv'], 'w_out': out['w_out'], 'rel_bias': out['rel_bias'], 'conv_w': out['conv_w'], 'ln_g': out['ln_g'], 'ln_b': out['ln_b'], 'loss_target': out['loss_target'], 'm_w_in': out['m_w_in'], 'm_w_mem_kv': out['m_w_mem_kv'], 'm_w_out': out['m_w_out'], 'm_rel_bias': out['m_rel_bias'], 'm_conv_w': out['m_conv_w'], 'm_ln_g': out['m_ln_g'], 'm_ln_b': out['m_ln_b'], 'v_w_in': out['v_w_in'], 'v_w_mem_kv': out['v_w_mem_kv'], 'v_w_out': out['v_w_out'], 'v_rel_bias': out['v_rel_bias'], 'v_conv_w': out['v_conv_w'], 'v_ln_g': out['v_ln_g'], 'v_ln_b': out['v_ln_b']}


def _loss(weights, diff, rest, loss_target):
    with _jax.named_scope("forward"):
        args = {**rest, TWIN_DIFF_INPUT: diff, **{k: w.astype(_WEIGHT_DTYPES[k]) for k, w in weights.items()}}
        y = _forward(args)
    with _jax.named_scope("loss_head"):
        err = _jnp.square(y.astype(_jnp.float32) - loss_target)
        return 0.5 * _jnp.sum(_jnp.mean(err, axis=-1)) if err.ndim else 0.5 * err


def _adamw(w, g, m, v):
    m = ADAM_B1 * m + (1.0 - ADAM_B1) * g
    v = ADAM_B2 * v + (1.0 - ADAM_B2) * _jnp.square(g)
    m_hat = m / (1.0 - ADAM_B1 ** ADAM_STEP)
    v_hat = v / (1.0 - ADAM_B2 ** ADAM_STEP)
    delta = -ADAM_LR * (m_hat / (_jnp.sqrt(v_hat) + ADAM_EPS) + ADAM_WD * w)
    return delta, m, v


def reference(x, mem, w_in, w_mem_kv, w_out, rel_bias, conv_w, ln_g, ln_b, loss_target, m_w_in, m_w_mem_kv, m_w_out, m_rel_bias, m_conv_w, m_ln_g, m_ln_b, v_w_in, v_w_mem_kv, v_w_out, v_rel_bias, v_conv_w, v_ln_g, v_ln_b):
    given = dict(x=x, mem=mem, w_in=w_in, w_mem_kv=w_mem_kv, w_out=w_out, rel_bias=rel_bias, conv_w=conv_w, ln_g=ln_g, ln_b=ln_b, loss_target=loss_target, m_w_in=m_w_in, m_w_mem_kv=m_w_mem_kv, m_w_out=m_w_out, m_rel_bias=m_rel_bias, m_conv_w=m_conv_w, m_ln_g=m_ln_g, m_ln_b=m_ln_b, v_w_in=v_w_in, v_w_mem_kv=v_w_mem_kv, v_w_out=v_w_out, v_rel_bias=v_rel_bias, v_conv_w=v_conv_w, v_ln_g=v_ln_g, v_ln_b=v_ln_b)
    weights = {n: given[n] for n in TWIN_WEIGHTS}
    shared = {n: given[n] for n in SHARED_INPUTS}
    per_example = {n: given[n] for n in ['x', 'mem']}
    grad_fn = _jax.value_and_grad(_loss, argnums=(0, 1))

    def one_microbatch(ex, loss_target):
        ex = dict(ex)
        diff = ex.pop(TWIN_DIFF_INPUT)
        return grad_fn(weights, diff, {**shared, **ex}, loss_target)

    if N_MICROBATCH == 1:
        loss, (grad_w, grad_x) = one_microbatch(per_example, given["loss_target"])
    else:
        def body(carry, xs):
            loss_sum, grad_sum = carry
            l_k, (gw_k, gx_k) = one_microbatch(xs[0], xs[1])
            with _jax.named_scope("update"):
                return (loss_sum + l_k, _jax.tree.map(_jnp.add, grad_sum, gw_k)), gx_k

        init = (_jnp.zeros((), _jnp.float32), _jax.tree.map(_jnp.zeros_like, weights))
        (loss, grad_w), grad_x = _jax.lax.scan(body, init, (per_example, given["loss_target"]))
    with _jax.named_scope("update"):
        delta_w, new_m, new_v = {}, {}, {}
        for n in TWIN_WEIGHTS:
            delta_w[n], new_m[n], new_v[n] = _adamw(weights[n], grad_w[n], given["m_" + n], given["v_" + n])
    return (loss, grad_x, *[grad_w[n] for n in TWIN_WEIGHTS], *[delta_w[n] for n in TWIN_WEIGHTS],
            *[new_m[n] for n in TWIN_WEIGHTS], *[new_v[n] for n in TWIN_WEIGHTS])
```

```python
import functools
import math

import jax
import jax.numpy as jnp
from jax import lax
from jax.experimental import pallas as pl
from jax.experimental.pallas import tpu as pltpu

F32 = jnp.float32
BF16 = jnp.bfloat16
MXU_DTYPE = jnp.bfloat16

N_DEV = 8
D_MODEL = 1024
DEPTH = 4
CHUNK = 64
N_PREV = 8
N_HEADS = 16
HEAD_DIM = 64
E_MIX = 1024
REL_CLIP = 128
N_REL = 2 * REL_CLIP + 1
N_REL_PAD = 384
N_MEM = 256
MEM_HEADS = 4
MEM_HEAD_DIM = 128
E_MEM = 512
E_BRANCH = E_MIX + E_MEM
N_IN = 3 * E_MIX + E_MEM + E_BRANCH
DN_ALPHA = (2.0 * DEPTH) ** 0.25
LN_EPS = 1e-5
NEG = -1e30

ADAM_LR = 0.001
ADAM_B1 = 0.9
ADAM_B2 = 0.999
ADAM_EPS = 1e-08
ADAM_WD = 0.01
ADAM_STEP = 10

LANES = 128
SUBLANES = 8
VMEM_LIMIT = 56 * 1024 * 1024

TQ = 4 * CHUNK
TKEYS = 3 * TQ
ROLL_W = 1024
TS = 256
QM_BLK = 3 * E_MIX // E_MEM
Z_BLK = QM_BLK + 1


def _call(body, **kw):
    return pl.pallas_call(body, **kw)


def _cparams(*sem):
    return pltpu.CompilerParams(dimension_semantics=sem, vmem_limit_bytes=VMEM_LIMIT)


def _dot(a, b):
    return jnp.dot(a, b, preferred_element_type=F32)


def _dot_nt(a, b):
    return lax.dot_general(a, b, (((1,), (1,)), ((), ())), preferred_element_type=F32)


def _dot_tn(a, b):
    return lax.dot_general(a, b, (((0,), (0,)), ((), ())), preferred_element_type=F32)


def _inproj(x, w):
    s, d = x.shape
    n = w.shape[1]
    tm, tn = 512, 1024

    def body(x_ref, w_ref, o_ref):
        o_ref[...] = _dot(x_ref[...].astype(MXU_DTYPE), w_ref[...])

    return _call(
        body, name="inproj", grid=(s // tm, n // tn),
        in_specs=[pl.BlockSpec((tm, d), lambda i, j: (i, 0)),
                  pl.BlockSpec((d, tn), lambda i, j: (0, j))],
        out_specs=pl.BlockSpec((tm, tn), lambda i, j: (i, j)),
        out_shape=jax.ShapeDtypeStruct((s, n), F32),
        compiler_params=_cparams("parallel", "parallel"),
    )(x, w)


def _small_matmul(a, b, trans_a, out_dtype, name):
    m = a.shape[1] if trans_a else a.shape[0]
    n = b.shape[1]

    def body(a_ref, b_ref, o_ref):
        av = a_ref[...].astype(MXU_DTYPE)
        bv = b_ref[...].astype(MXU_DTYPE)
        r = _dot_tn(av, bv) if trans_a else _dot(av, bv)
        o_ref[...] = r.astype(out_dtype)

    return _call(
        body, name=name,
        in_specs=[pl.BlockSpec(memory_space=pltpu.VMEM)] * 2,
        out_specs=pl.BlockSpec(memory_space=pltpu.VMEM),
        out_shape=jax.ShapeDtypeStruct((m, n), out_dtype),
        compiler_params=pltpu.CompilerParams(vmem_limit_bytes=VMEM_LIMIT),
    )(a, b)


def _piece_blocks(pieces, blk):
    offs, nbs, o = [], [], 0
    for p in pieces:
        nb = p.shape[1] // blk
        offs.append(o)
        nbs.append(nb)
        o += nb
    return offs, nbs, o


def _dx_matmul(pieces, w, addend):
    s = pieces[0].shape[0]
    d = w.shape[0]
    tm, tk = 512, 512
    offs, nbs, nk = _piece_blocks(pieces, tk)
    np_ = len(pieces)

    def body(*refs):
        a_refs = refs[:np_]
        w_ref, add_ref, o_ref = refs[np_:]
        k = pl.program_id(1)

        @pl.when(k == 0)
        def _():
            o_ref[...] = add_ref[...]

        for p in range(np_):
            @pl.when((k >= offs[p]) & (k < offs[p] + nbs[p]))
            def _(p=p):
                o_ref[...] += _dot_nt(a_refs[p][...], w_ref[...])

    def amap(p):
        return lambda i, k: (i, jnp.clip(k - offs[p], 0, nbs[p] - 1))

    in_specs = [pl.BlockSpec((tm, tk), amap(p)) for p in range(np_)]
    in_specs += [pl.BlockSpec((d, tk), lambda i, k: (0, k)),
                 pl.BlockSpec((tm, d), lambda i, k: (i, 0))]
    return _call(
        body, name="dx_matmul", grid=(s // tm, nk),
        in_specs=in_specs,
        out_specs=pl.BlockSpec((tm, d), lambda i, k: (i, 0)),
        out_shape=jax.ShapeDtypeStruct((s, d), F32),
        compiler_params=_cparams("parallel", "arbitrary"),
    )(*pieces, w, addend)


def _dw_matmul(x, pieces):
    s, d = x.shape
    tn, tk = 512, 512
    offs, nbs, nj = _piece_blocks(pieces, tn)
    np_ = len(pieces)
    nk = s // tk

    def body(*refs):
        x_ref = refs[0]
        b_refs = refs[1:1 + np_]
        o_ref, acc = refs[1 + np_:]
        j = pl.program_id(0)
        k = pl.program_id(1)

        @pl.when(k == 0)
        def _():
            acc[...] = jnp.zeros_like(acc)

        xb = x_ref[...].astype(MXU_DTYPE)
        for p in range(np_):
            @pl.when((j >= offs[p]) & (j < offs[p] + nbs[p]))
            def _(p=p):
                acc[...] += _dot_tn(xb, b_refs[p][...])

        @pl.when(k == nk - 1)
        def _():
            o_ref[...] = acc[...].astype(o_ref.dtype)

    def bmap(p):
        def f(j, k):
            inside = (j >= offs[p]) & (j < offs[p] + nbs[p])
            return (jnp.where(inside, k, 0), jnp.clip(j - offs[p], 0, nbs[p] - 1))
        return f

    in_specs = [pl.BlockSpec((tk, d), lambda j, k: (k, 0))]
    in_specs += [pl.BlockSpec((tk, tn), bmap(p)) for p in range(np_)]
    return _call(
        body, name="dw_matmul", grid=(nj, nk),
        in_specs=in_specs,
        out_specs=pl.BlockSpec((d, tn), lambda j, k: (0, j)),
        out_shape=jax.ShapeDtypeStruct((d, nj * tn), BF16),
        scratch_shapes=[pltpu.VMEM((d, tn), F32)],
        compiler_params=_cparams("parallel", "arbitrary"),
    )(x, *pieces)


def _rel_onehot():
    j = lax.broadcasted_iota(jnp.int32, (N_REL_PAD, ROLL_W), 1)
    kk = lax.broadcasted_iota(jnp.int32, (N_REL_PAD, ROLL_W), 0)
    dd = jnp.where(j < TKEYS, j, j - ROLL_W)
    idx = jnp.clip(N_PREV * CHUNK - dd, -REL_CLIP, REL_CLIP) + REL_CLIP
    return jnp.where(idx == kk, 1.0, 0.0).astype(F32)


def _band_mask():
    r = lax.broadcasted_iota(jnp.int32, (TQ, TKEYS), 0) // CHUNK
    m = lax.broadcasted_iota(jnp.int32, (TQ, TKEYS), 1) // CHUNK
    return (m >= r) & (m <= r + N_PREV)


def _tile_bias(table_pad):
    def body(t_ref, o_ref):
        g = jnp.dot(t_ref[...], _rel_onehot(), preferred_element_type=F32,
                    precision=lax.Precision.HIGHEST)
        band = _band_mask()
        for h in range(N_HEADS):
            gh = jnp.broadcast_to(g[h:h + 1, :], (TQ, ROLL_W))
            rolled = pltpu.roll(gh, 0, 1, stride=1, stride_axis=0)
            o_ref[h] = jnp.where(band, rolled[:, :TKEYS], NEG)

    return _call(
        body, name="tile_bias",
        in_specs=[pl.BlockSpec(memory_space=pltpu.VMEM)],
        out_specs=pl.BlockSpec(memory_space=pltpu.VMEM),
        out_shape=jax.ShapeDtypeStruct((N_HEADS, TQ, TKEYS), F32),
        compiler_params=pltpu.CompilerParams(vmem_limit_bytes=VMEM_LIMIT),
    )(table_pad)


def _tile_bias_bwd(dtb):
    def body(d_ref, o_ref, g_ref):
        zpad = jnp.zeros((TQ, ROLL_W - TKEYS), F32)
        rr = lax.broadcasted_iota(jnp.int32, (TQ, TQ), 0)
        cc = lax.broadcasted_iota(jnp.int32, (TQ, TQ), 1)
        flip = jnp.where(rr + cc == TQ - 1, 1.0, 0.0).astype(F32)
        for h in range(N_HEADS):
            xh = jnp.concatenate([d_ref[h], zpad], axis=1)
            xf = jnp.dot(flip, xh, preferred_element_type=F32, precision=lax.Precision.HIGHEST)
            rolled = pltpu.roll(xf, 0, 1, stride=1, stride_axis=0)
            g_ref[h:h + 1, :] = jnp.sum(rolled, axis=0, keepdims=True)
        g = pltpu.roll(g_ref[...], ROLL_W - (TQ - 1), 1)
        o_ref[...] = lax.dot_general(g, _rel_onehot(), (((1,), (1,)), ((), ())),
                                     preferred_element_type=F32, precision=lax.Precision.HIGHEST)

    return _call(
        body, name="tile_bias_bwd",
        in_specs=[pl.BlockSpec(memory_space=pltpu.VMEM)],
        out_specs=pl.BlockSpec(memory_space=pltpu.VMEM),
        out_shape=jax.ShapeDtypeStruct((N_HEADS, N_REL_PAD), F32),
        scratch_shapes=[pltpu.VMEM((N_HEADS, ROLL_W), F32)],
        compiler_params=pltpu.CompilerParams(vmem_limit_bytes=VMEM_LIMIT),
    )(dtb)


def _head_masks():
    lane = lax.broadcasted_iota(jnp.int32, (1, 2 * HEAD_DIM), 1)
    return [lane < HEAD_DIM, lane >= HEAD_DIM]


def _attn_probs(qm, kcat, tb, valid):
    s = _dot_nt(qm, kcat) * (1.0 / math.sqrt(HEAD_DIM)) + tb
    s = jnp.where(valid, s, NEG)
    m = jnp.max(s, axis=-1, keepdims=True)
    e = jnp.exp(s - m)
    return e / jnp.sum(e, axis=-1, keepdims=True)


def _key_valid(i):
    col = lax.broadcasted_iota(jnp.int32, (TQ, TKEYS), 1)
    return col >= jnp.maximum(2 - i, 0) * TQ


def _kv_specs(col0, nt):
    def spec(back):
        return pl.BlockSpec((TQ, 2 * HEAD_DIM),
                            lambda hp, i: (jnp.clip(i - back, 0, nt - 1), col0 + hp))
    return [spec(2), spec(1), spec(0)]


def _attn_fwd(h, tb):
    s = h.shape[0]
    nt = s // TQ
    nhp = N_HEADS // 2

    def body(q_ref, k0, k1, k2, v0, v1, v2, tb_ref, o_ref):
        i = pl.program_id(1)
        valid = _key_valid(i)
        masks = _head_masks()
        q2 = q_ref[...].astype(MXU_DTYPE)
        kcat = jnp.concatenate([k0[...], k1[...], k2[...]], axis=0).astype(MXU_DTYPE)
        vcat = jnp.concatenate([v0[...], v1[...], v2[...]], axis=0).astype(MXU_DTYPE)
        outs = []
        for hh in range(2):
            qm = jnp.where(masks[hh], q2, jnp.zeros_like(q2))
            p = _attn_probs(qm, kcat, tb_ref[hh], valid)
            outs.append(_dot(p.astype(MXU_DTYPE), vcat))
        o_ref[...] = jnp.where(masks[0], outs[0], outs[1])

    in_specs = [pl.BlockSpec((TQ, 2 * HEAD_DIM), lambda hp, i: (i, hp))]
    in_specs += _kv_specs(nhp, nt) + _kv_specs(2 * nhp, nt)
    in_specs += [pl.BlockSpec((2, TQ, TKEYS), lambda hp, i: (hp, 0, 0))]
    return _call(
        body, name="attn_fwd", grid=(nhp, nt),
        in_specs=in_specs,
        out_specs=pl.BlockSpec((TQ, 2 * HEAD_DIM), lambda hp, i: (i, hp)),
        out_shape=jax.ShapeDtypeStruct((s, E_MIX), F32),
        compiler_params=_cparams("parallel", "parallel"),
    )(h, h, h, h, h, h, h, tb)


def _attn_bwd(h, tb, d_mix):
    s = h.shape[0]
    nt = s // TQ
    nhp = N_HEADS // 2
    scale = 1.0 / math.sqrt(HEAD_DIM)

    def body(q_ref, k0, k1, k2, v0, v1, v2, tb_ref, do_ref,
             dq_ref, dk_ref, dv_ref, dtb_ref, dk_acc, dv_acc):
        i = pl.program_id(1)

        @pl.when(i == 0)
        def _():
            dk_acc[...] = jnp.zeros_like(dk_acc)
            dv_acc[...] = jnp.zeros_like(dv_acc)
            dtb_ref[...] = jnp.zeros_like(dtb_ref)

        @pl.when((i > 0) & (i < nt))
        def _():
            dk_acc[i % 3] = jnp.zeros((TQ, 2 * HEAD_DIM), F32)
            dv_acc[i % 3] = jnp.zeros((TQ, 2 * HEAD_DIM), F32)

        @pl.when(i < nt)
        def _():
            valid = _key_valid(i)
            masks = _head_masks()
            q2 = q_ref[...].astype(MXU_DTYPE)
            do2 = do_ref[...].astype(MXU_DTYPE)
            kcat = jnp.concatenate([k0[...], k1[...], k2[...]], axis=0).astype(MXU_DTYPE)
            vcat = jnp.concatenate([v0[...], v1[...], v2[...]], axis=0).astype(MXU_DTYPE)
            dqs, dks, dvs = [], [], []
            for hh in range(2):
                qm = jnp.where(masks[hh], q2, jnp.zeros_like(q2))
                dom = jnp.where(masks[hh], do2, jnp.zeros_like(do2))
                p = _attn_probs(qm, kcat, tb_ref[hh], valid)
                dp = _dot_nt(dom, vcat)
                ds = p * (dp - jnp.sum(p * dp, axis=-1, keepdims=True))
                dtb_ref[hh] += ds
                dsb = (ds * scale).astype(MXU_DTYPE)
                dqs.append(_dot(dsb, kcat))
                dks.append(_dot_tn(dsb, q2))
                dvs.append(_dot_tn(p.astype(MXU_DTYPE), do2))
            dq_ref[...] = jnp.where(masks[0], dqs[0], dqs[1]).astype(dq_ref.dtype)
            dkc = jnp.where(masks[0], dks[0], dks[1])
            dvc = jnp.where(masks[0], dvs[0], dvs[1])
            for jj in range(3):
                slot = (i + 1 + jj) % 3
                dk_acc[slot] += dkc[jj * TQ:(jj + 1) * TQ]
                dv_acc[slot] += dvc[jj * TQ:(jj + 1) * TQ]

        @pl.when(i >= 2)
        def _():
            slot = (i - 2) % 3
            dk_ref[...] = dk_acc[slot].astype(dk_ref.dtype)
            dv_ref[...] = dv_acc[slot].astype(dv_ref.dtype)

    qmap = lambda hp, i: (jnp.minimum(i, nt - 1), hp)
    kvout = lambda hp, i: (jnp.maximum(i - 2, 0), hp)
    in_specs = [pl.BlockSpec((TQ, 2 * HEAD_DIM), qmap)]
    in_specs += _kv_specs(nhp, nt) + _kv_specs(2 * nhp, nt)
    in_specs += [pl.BlockSpec((2, TQ, TKEYS), lambda hp, i: (hp, 0, 0)),
                 pl.BlockSpec((TQ, 2 * HEAD_DIM), qmap)]
    blk = (TQ, 2 * HEAD_DIM)
    return _call(
        body, name="attn_bwd", grid=(nhp, nt + 2),
        in_specs=in_specs,
        out_specs=[pl.BlockSpec(blk, qmap), pl.BlockSpec(blk, kvout), pl.BlockSpec(blk, kvout),
                   pl.BlockSpec((2, TQ, TKEYS), lambda hp, i: (hp, 0, 0))],
        out_shape=[jax.ShapeDtypeStruct((s, E_MIX), BF16)] * 3
        + [jax.ShapeDtypeStruct((N_HEADS, TQ, TKEYS), F32)],
        scratch_shapes=[pltpu.VMEM((3, TQ, 2 * HEAD_DIM), F32)] * 2,
        compiler_params=_cparams("parallel", "arbitrary"),
    )(h, h, h, h, h, h, h, tb, d_mix)


CONV_TS = 256
HALO = SUBLANES


def _shift_down(prev, cur, k):
    rolled = pltpu.roll(cur, k, 0)
    row = lax.broadcasted_iota(jnp.int32, (HALO, cur.shape[1]), 0)
    top = jnp.where(row < k, pltpu.roll(prev, k, 0), rolled[:HALO])
    return jnp.concatenate([top, rolled[HALO:]], axis=0)


def _shift_up(cur, nxt, k):
    ts = cur.shape[0]
    rolled = pltpu.roll(cur, ts - k, 0)
    row = lax.broadcasted_iota(jnp.int32, (HALO, cur.shape[1]), 0)
    bottom = jnp.where(row >= HALO - k, pltpu.roll(nxt, HALO - k, 0), rolled[ts - HALO:])
    return jnp.concatenate([rolled[:ts - HALO], bottom], axis=0)


def _conv_specs(ts, nb):
    tile = lambda c: pl.BlockSpec((ts, E_MIX), lambda i: (i, c))
    prev = lambda c: pl.BlockSpec((HALO, E_MIX), lambda i: (jnp.maximum(i * (ts // HALO) - 1, 0), c))
    return tile, prev


def _conv_fwd(h, w8):
    s = h.shape[0]
    ts = CONV_TS
    nb = s // ts
    tile, prev = _conv_specs(ts, nb)

    def body(bg, cg, u, cgp, up, w_ref, o_ref):
        i = pl.program_id(0)
        a = cg[...] * u[...]
        ap = jnp.where(i > 0, cgp[...] * up[...], 0.0)
        w = w_ref[...]
        conv = w[0:1] * _shift_down(ap, a, 2) + w[1:2] * _shift_down(ap, a, 1) + w[2:3] * a
        o_ref[...] = bg[...] * conv

    return _call(
        body, name="conv_fwd", grid=(nb,),
        in_specs=[tile(0), tile(1), tile(2), prev(1), prev(2),
                  pl.BlockSpec((SUBLANES, E_MIX), lambda i: (0, 0))],
        out_specs=pl.BlockSpec((ts, E_MIX), lambda i: (i, 0)),
        out_shape=jax.ShapeDtypeStruct((s, E_MIX), F32),
        compiler_params=_cparams("parallel"),
    )(h, h, h, h, h, w8)


def _conv_bwd(h, w8, d_mix):
    s = h.shape[0]
    ts = CONV_TS
    nb = s // ts
    tile, prev = _conv_specs(ts, nb)
    nrow = s // HALO
    nxt = lambda c: pl.BlockSpec((HALO, E_MIX), lambda i: (jnp.minimum((i + 1) * (ts // HALO), nrow - 1), c))

    def body(bg, cg, u, cgp, up, bgn, dmix, dmixn, w_ref, dbg_ref, dcg_ref, du_ref, dw_ref):
        i = pl.program_id(0)

        @pl.when(i == 0)
        def _():
            dw_ref[...] = jnp.zeros_like(dw_ref)

        cgv, uv = cg[...], u[...]
        a = cgv * uv
        ap = jnp.where(i > 0, cgp[...] * up[...], 0.0)
        a1 = _shift_down(ap, a, 1)
        a2 = _shift_down(ap, a, 2)
        w = w_ref[...]
        conv = w[0:1] * a2 + w[1:2] * a1 + w[2:3] * a
        dm = dmix[...]
        dbg_ref[...] = (dm * conv).astype(dbg_ref.dtype)
        dc = dm * bg[...]
        dcn = jnp.where(i < nb - 1, dmixn[...] * bgn[...], 0.0)
        da = w[2:3] * dc + w[1:2] * _shift_up(dc, dcn, 1) + w[0:1] * _shift_up(dc, dcn, 2)
        dcg_ref[...] = (da * uv).astype(dcg_ref.dtype)
        du_ref[...] = (da * cgv).astype(du_ref.dtype)
        dw_ref[0:1, :] += jnp.sum(dc * a2, axis=0, keepdims=True)
        dw_ref[1:2, :] += jnp.sum(dc * a1, axis=0, keepdims=True)
        dw_ref[2:3, :] += jnp.sum(dc * a, axis=0, keepdims=True)

    full = lambda: pl.BlockSpec((ts, E_MIX), lambda i: (i, 0))
    return _call(
        body, name="conv_bwd", grid=(nb,),
        in_specs=[tile(0), tile(1), tile(2), prev(1), prev(2), nxt(0),
                  full(), pl.BlockSpec((HALO, E_MIX), lambda i: (jnp.minimum((i + 1) * (ts // HALO), nrow - 1), 0)),
                  pl.BlockSpec((SUBLANES, E_MIX), lambda i: (0, 0))],
        out_specs=[full(), full(), full(), pl.BlockSpec((SUBLANES, E_MIX), lambda i: (0, 0))],
        out_shape=[jax.ShapeDtypeStruct((s, E_MIX), BF16)] * 3
        + [jax.ShapeDtypeStruct((SUBLANES, E_MIX), F32)],
        compiler_params=_cparams("arbitrary"),
    )(h, h, h, h, h, h, d_mix, d_mix, w8)


def _mem_probs(qh, kh):
    s = _dot_nt(qh, kh) / math.sqrt(MEM_HEAD_DIM)
    m = jnp.max(s, axis=-1, keepdims=True)
    e = jnp.exp(s - m)
    return e / jnp.sum(e, axis=-1, keepdims=True)


def _sigmoid(z):
    return 1.0 / (1.0 + jnp.exp(-z))


def _layer_out_fwd(x, h, mix, kv, w_out, g, b):
    s, d = x.shape
    ts = TS

    def body(x_ref, mix_ref, qm_ref, z0, z1, z2, kv_ref, wo_ref, g_ref, b_ref,
             xn_ref, r_ref, mem_ref):
        qm = qm_ref[...].astype(MXU_DTYPE)
        kvb = kv_ref[...].astype(MXU_DTYPE)
        mems = []
        for hh in range(MEM_HEADS):
            lo = hh * MEM_HEAD_DIM
            p = _mem_probs(qm[:, lo:lo + MEM_HEAD_DIM], kvb[:, lo:lo + MEM_HEAD_DIM])
            mems.append(_dot(p.astype(MXU_DTYPE), kvb[:, E_MEM + lo:E_MEM + lo + MEM_HEAD_DIM]))
        mem = jnp.concatenate(mems, axis=1)
        mem_ref[...] = mem
        mixv = mix_ref[...]
        half = E_MIX // 2
        parts = [mixv[:, :half], mixv[:, half:], mem]
        out = jnp.zeros((ts, d), F32)
        for c, zr in enumerate((z0, z1, z2)):
            zv = zr[...]
            y = (parts[c] * (zv * _sigmoid(zv))).astype(MXU_DTYPE)
            out += _dot(y, wo_ref[c * half:(c + 1) * half, :])
        r = DN_ALPHA * x_ref[...] + out
        r_ref[...] = r
        mu = jnp.mean(r, axis=-1, keepdims=True)
        rc = r - mu
        var = jnp.mean(rc * rc, axis=-1, keepdims=True)
        xn_ref[...] = rc * lax.rsqrt(var + LN_EPS) * g_ref[...] + b_ref[...]

    row = lambda w, c: pl.BlockSpec((ts, w), lambda i: (i, c))
    const = lambda shp: pl.BlockSpec(shp, lambda i: (0, 0))
    return _call(
        body, name="layer_out_fwd", grid=(s // ts,),
        in_specs=[row(d, 0), row(E_MIX, 0), row(E_MEM, QM_BLK),
                  row(E_MEM, Z_BLK), row(E_MEM, Z_BLK + 1), row(E_MEM, Z_BLK + 2),
                  const((N_MEM, 2 * E_MEM)), const((E_BRANCH, d)), const((1, d)), const((1, d))],
        out_specs=[row(d, 0), row(d, 0), row(E_MEM, 0)],
        out_shape=[jax.ShapeDtypeStruct((s, d), F32), jax.ShapeDtypeStruct((s, d), F32),
                   jax.ShapeDtypeStruct((s, E_MEM), F32)],
        compiler_params=_cparams("parallel"),
    )(x, mix, h, h, h, h, kv, w_out, g, b)


def _layer_out_bwd(dxn, r, g, h, mix, mem, w_out):
    s, d = r.shape
    ts = TS
    nb = s // ts
    half = E_MIX // 2

    def body(dxn_ref, r_ref, g_ref, mix_ref, mem_ref, z0, z1, z2, wo_ref,
             dxr_ref, dmix_ref, dmem_ref, dz_ref, dwo_ref, dg_ref, db_ref, dw_acc):
        i = pl.program_id(0)

        @pl.when(i == 0)
        def _():
            dw_acc[...] = jnp.zeros_like(dw_acc)
            dg_ref[...] = jnp.zeros_like(dg_ref)
            db_ref[...] = jnp.zeros_like(db_ref)

        dxn_v = dxn_ref[...]
        rv = r_ref[...]
        mu = jnp.mean(rv, axis=-1, keepdims=True)
        rc = rv - mu
        var = jnp.mean(rc * rc, axis=-1, keepdims=True)
        rstd = lax.rsqrt(var + LN_EPS)
        xhat = rc * rstd
        dg_ref[...] += jnp.sum(dxn_v * xhat, axis=0, keepdims=True)
        db_ref[...] += jnp.sum(dxn_v, axis=0, keepdims=True)
        dxh = dxn_v * g_ref[...]
        m1 = jnp.mean(dxh, axis=-1, keepdims=True)
        m2 = jnp.mean(dxh * xhat, axis=-1, keepdims=True)
        dr = rstd * (dxh - m1 - xhat * m2)
        dxr_ref[...] = DN_ALPHA * dr
        dout = dr.astype(MXU_DTYPE)
        mixv = mix_ref[...]
        parts = [mixv[:, :half], mixv[:, half:], mem_ref[...]]
        dcs = []
        for c, zr in enumerate((z0, z1, z2)):
            lo = c * half
            zv = zr[...]
            sg = _sigmoid(zv)
            sl = zv * sg
            dy = _dot_nt(dout, wo_ref[lo:lo + half, :])
            y = (parts[c] * sl).astype(MXU_DTYPE)
            dw_acc[lo:lo + half, :] += _dot_tn(y, dout)
            dcs.append(dy * sl)
            dz_ref[:, lo:lo + half] = (dy * parts[c] * (sg * (1.0 + zv * (1.0 - sg)))).astype(dz_ref.dtype)
        dmix_ref[...] = jnp.concatenate(dcs[:2], axis=1)
        dmem_ref[...] = dcs[2]

        @pl.when(i == nb - 1)
        def _():
            dwo_ref[...] = dw_acc[...].astype(dwo_ref.dtype)

    row = lambda w, c: pl.BlockSpec((ts, w), lambda i: (i, c))
    const = lambda shp: pl.BlockSpec(shp, lambda i: (0, 0))
    return _call(
        body, name="layer_out_bwd", grid=(nb,),
        in_specs=[row(d, 0), row(d, 0), const((1, d)), row(E_MIX, 0), row(E_MEM, 0),
                  row(E_MEM, Z_BLK), row(E_MEM, Z_BLK + 1), row(E_MEM, Z_BLK + 2),
                  const((E_BRANCH, d))],
        out_specs=[row(d, 0), row(E_MIX, 0), row(E_MEM, 0), row(E_BRANCH, 0),
                   const((E_BRANCH, d)), const((1, d)), const((1, d))],
        out_shape=[jax.ShapeDtypeStruct((s, d), F32), jax.ShapeDtypeStruct((s, E_MIX), F32),
                   jax.ShapeDtypeStruct((s, E_MEM), F32), jax.ShapeDtypeStruct((s, E_BRANCH), BF16),
                   jax.ShapeDtypeStruct((E_BRANCH, d), BF16),
                   jax.ShapeDtypeStruct((1, d), F32), jax.ShapeDtypeStruct((1, d), F32)],
        scratch_shapes=[pltpu.VMEM((E_BRANCH, d), F32)],
        compiler_params=_cparams("arbitrary"),
    )(dxn, r, g, mix, mem, h, h, h, w_out)


def _mem_attn_bwd(h, kv, d_mem):
    s = h.shape[0]
    ts = TS
    inv = 1.0 / math.sqrt(MEM_HEAD_DIM)

    def body(qm_ref, kv_ref, dm_ref, dq_ref, dkv_ref):
        i = pl.program_id(0)

        @pl.when(i == 0)
        def _():
            dkv_ref[...] = jnp.zeros_like(dkv_ref)

        qm = qm_ref[...].astype(MXU_DTYPE)
        kvb = kv_ref[...].astype(MXU_DTYPE)
        dmb = dm_ref[...].astype(MXU_DTYPE)
        for hh in range(MEM_HEADS):
            lo = hh * MEM_HEAD_DIM
            qh = qm[:, lo:lo + MEM_HEAD_DIM]
            kh = kvb[:, lo:lo + MEM_HEAD_DIM]
            vh = kvb[:, E_MEM + lo:E_MEM + lo + MEM_HEAD_DIM]
            dmh = dmb[:, lo:lo + MEM_HEAD_DIM]
            p = _mem_probs(qh, kh)
            dp = _dot_nt(dmh, vh)
            ds = p * (dp - jnp.sum(p * dp, axis=-1, keepdims=True))
            dsb = (ds * inv).astype(MXU_DTYPE)
            dq_ref[:, lo:lo + MEM_HEAD_DIM] = _dot(dsb, kh).astype(dq_ref.dtype)
            dkv_ref[:, lo:lo + MEM_HEAD_DIM] += _dot_tn(dsb, qh)
            dkv_ref[:, E_MEM + lo:E_MEM + lo + MEM_HEAD_DIM] += _dot_tn(p.astype(MXU_DTYPE), dmh)

    return _call(
        body, name="mem_attn_bwd", grid=(s // ts,),
        in_specs=[pl.BlockSpec((ts, E_MEM), lambda i: (i, QM_BLK)),
                  pl.BlockSpec((N_MEM, 2 * E_MEM), lambda i: (0, 0)),
                  pl.BlockSpec((ts, E_MEM), lambda i: (i, 0))],
        out_specs=[pl.BlockSpec((ts, E_MEM), lambda i: (i, 0)),
                   pl.BlockSpec((N_MEM, 2 * E_MEM), lambda i: (0, 0))],
        out_shape=[jax.ShapeDtypeStruct((s, E_MEM), BF16),
                   jax.ShapeDtypeStruct((N_MEM, 2 * E_MEM), F32)],
        compiler_params=_cparams("arbitrary"),
    )(h, kv, d_mem)


def _loss_head(y, target):
    s, d = y.shape
    ts = 512

    def body(y_ref, t_ref, l_ref, dy_ref):
        @pl.when(pl.program_id(0) == 0)
        def _():
            l_ref[...] = jnp.zeros_like(l_ref)

        e = y_ref[...] - t_ref[...]
        dy_ref[...] = e * (1.0 / d)
        l_ref[...] += (0.5 / d) * jnp.sum(jnp.sum(e * e, axis=1, keepdims=True), axis=0, keepdims=True)

    return _call(
        body, name="loss_head", grid=(s // ts,),
        in_specs=[pl.BlockSpec((ts, d), lambda i: (i, 0))] * 2,
        out_specs=[pl.BlockSpec((1, 1), lambda i: (0, 0)), pl.BlockSpec((ts, d), lambda i: (i, 0))],
        out_shape=[jax.ShapeDtypeStruct((1, 1), F32), jax.ShapeDtypeStruct((s, d), F32)],
        compiler_params=_cparams("arbitrary"),
    )(y, target)


def _local_step(x, mem, w_in, w_kv, w_out, rel_bias, conv_w, ln_g, ln_b, target):
    saved = []
    xl = x
    for layer in range(DEPTH):
        h = _inproj(xl, w_in[layer])
        if layer % 2 == 0:
            table = jnp.pad(rel_bias[layer // 2], ((0, 0), (0, N_REL_PAD - N_REL)))
            aux = _tile_bias(table)
            mix = _attn_fwd(h, aux)
        else:
            aux = jnp.pad(conv_w[layer // 2], ((0, SUBLANES - 3), (0, 0)))
            mix = _conv_fwd(h, aux)
        kv = _small_matmul(mem, w_kv[layer], False, F32, "kv_mem")
        xn, r, mem_out = _layer_out_fwd(xl, h, mix, kv, w_out[layer],
                                        ln_g[layer][None], ln_b[layer][None])
        saved.append((xl, h, aux, mix, kv, r, mem_out))
        xl = xn

    loss, dx = _loss_head(xl, target)

    dw_in, dw_kv, dw_out, dgs, dbs = [], [], [], [], []
    d_rel, d_conv = [], []
    for layer in reversed(range(DEPTH)):
        xl, h, aux, mix, kv, r, mem_out = saved[layer]
        dx_res, d_mix, d_mem, dz, dwo, dg, db = _layer_out_bwd(
            dx, r, ln_g[layer][None], h, mix, mem_out, w_out[layer])
        dqm, dkv = _mem_attn_bwd(h, kv, d_mem)
        if layer % 2 == 0:
            dq, dk, dv, dtb = _attn_bwd(h, aux, d_mix)
            d_rel.append(_tile_bias_bwd(dtb)[:, :N_REL])
            pieces = [dq, dk, dv, dqm, dz]
        else:
            dbg, dcg, du, dw8 = _conv_bwd(h, aux, d_mix)
            d_conv.append(dw8[:3])
            pieces = [dbg, dcg, du, dqm, dz]
        dw_in.append(_dw_matmul(xl, pieces))
        dw_kv.append(_small_matmul(mem, dkv, True, BF16, "dw_kv"))
        dw_out.append(dwo)
        dgs.append(dg[0])
        dbs.append(db[0])
        dx = _dx_matmul(pieces, w_in[layer], dx_res)

    rev = lambda lst: jnp.stack(lst[::-1])
    return (loss, dx, rev(dw_in), rev(dw_kv), rev(dw_out),
            rev(d_rel), rev(d_conv), rev(dgs), rev(dbs))


def _me():
    return lax.axis_index("x"), lax.axis_index("y"), lax.axis_index("c")


def _peer(k):
    x, y, c = _me()
    kx, ky, kc = (k >> 2) & 1, (k >> 1) & 1, k & 1
    return (1 - x if kx else x, 1 - y if ky else y, 1 - c if kc else c)


def _lin(dev):
    return 4 * dev[0] + 2 * dev[1] + dev[2]


ANY = pl.BlockSpec(memory_space=pl.ANY)


def _exchange(srcs, dst_shapes, src_slice, dst_slice, name):
    na = len(srcs)

    def body(*refs):
        src_refs = refs[:na]
        dst_refs = refs[na:2 * na]
        send_sems, recv_sems, local_sems = refs[2 * na:]
        me = _lin(_me())
        copies = []
        for a in range(na):
            loc = pltpu.make_async_copy(src_slice(a, src_refs[a], me), dst_slice(a, dst_refs[a], me),
                                        local_sems.at[a])
            loc.start()
            copies.append(loc)
            for k in range(1, N_DEV):
                peer = _peer(k)
                cp = pltpu.make_async_remote_copy(
                    src_ref=src_slice(a, src_refs[a], _lin(peer)),
                    dst_ref=dst_slice(a, dst_refs[a], me),
                    send_sem=send_sems.at[a, k - 1], recv_sem=recv_sems.at[a, k - 1],
                    device_id=peer, device_id_type=pl.DeviceIdType.MESH)
                cp.start()
                copies.append(cp)
        for cp in copies:
            cp.wait()

    return _call(
        body, name=name,
        in_specs=[ANY] * na, out_specs=[ANY] * na,
        out_shape=[jax.ShapeDtypeStruct(shp, s.dtype) for shp, s in zip(dst_shapes, srcs)],
        scratch_shapes=[pltpu.SemaphoreType.DMA((na, N_DEV - 1)),
                        pltpu.SemaphoreType.DMA((na, N_DEV - 1)),
                        pltpu.SemaphoreType.DMA((na,))],
    )(*srcs)


def _gather_weights(w_in_s, w_kv_s, w_out_s, conv_s):
    srcs = [w_in_s, w_kv_s, w_out_s, conv_s]
    c_in, r_kv, r_out = w_in_s.shape[2], w_kv_s.shape[1], w_out_s.shape[1]
    shapes = [(DEPTH, D_MODEL, N_DEV * c_in), (DEPTH, N_DEV * r_kv, w_kv_s.shape[2]),
              (DEPTH, N_DEV * r_out, D_MODEL), (N_DEV,) + conv_s.shape]

    def src_slice(a, ref, p):
        return ref

    def dst_slice(a, ref, me):
        if a == 0:
            return ref.at[:, :, pl.ds(pl.multiple_of(me * c_in, LANES), c_in)]
        if a == 1:
            return ref.at[:, pl.ds(pl.multiple_of(me * r_kv, 2 * SUBLANES), r_kv), :]
        if a == 2:
            return ref.at[:, pl.ds(pl.multiple_of(me * r_out, 2 * SUBLANES), r_out), :]
        return ref.at[me]

    return _exchange(srcs, shapes, src_slice, dst_slice, "gather_weights")


def _scatter_grads(dw_in, dw_kv, dw_out, small):
    srcs = [dw_in, dw_kv, dw_out, small]
    c_in, r_kv, r_out = dw_in.shape[2] // N_DEV, dw_kv.shape[1] // N_DEV, dw_out.shape[1] // N_DEV
    shapes = [(N_DEV, DEPTH, D_MODEL, c_in), (N_DEV, DEPTH, r_kv, dw_kv.shape[2]),
              (N_DEV, DEPTH, r_out, D_MODEL), (N_DEV,) + small.shape]

    def src_slice(a, ref, p):
        if a == 0:
            return ref.at[:, :, pl.ds(pl.multiple_of(p * c_in, LANES), c_in)]
        if a == 1:
            return ref.at[:, pl.ds(pl.multiple_of(p * r_kv, 2 * SUBLANES), r_kv), :]
        if a == 2:
            return ref.at[:, pl.ds(pl.multiple_of(p * r_out, 2 * SUBLANES), r_out), :]
        return ref

    def dst_slice(a, ref, me):
        return ref.at[me]

    return _exchange(srcs, shapes, src_slice, dst_slice, "scatter_grads")


def _adamw_math(w, g, m, v):
    m = ADAM_B1 * m + (1.0 - ADAM_B1) * g
    v = ADAM_B2 * v + (1.0 - ADAM_B2) * (g * g)
    m_hat = m / (1.0 - ADAM_B1 ** ADAM_STEP)
    v_hat = v / (1.0 - ADAM_B2 ** ADAM_STEP)
    delta = -ADAM_LR * (m_hat / (jnp.sqrt(v_hat) + ADAM_EPS) + ADAM_WD * w)
    return delta, m, v


def _reduce_adamw(parts, w, m, v, name):
    rows, cols = w.shape
    tr = rows
    for cand in (512, 256, 128, 64, 32, 16):
        if rows % cand == 0 and rows > cand:
            tr = cand
            break

    def body(p_ref, w_ref, m_ref, v_ref, g_out, d_out, m_out, v_out):
        g = p_ref[0].astype(F32)
        for s in range(1, N_DEV):
            g = g + p_ref[s].astype(F32)
        g_out[...] = g
        d_out[...], m_out[...], v_out[...] = _adamw_math(w_ref[...], g, m_ref[...], v_ref[...])

    blk = pl.BlockSpec((tr, cols), lambda i: (i, 0))
    return _call(
        body, name=name, grid=(rows // tr,),
        in_specs=[pl.BlockSpec((N_DEV, tr, cols), lambda i: (0, i, 0)), blk, blk, blk],
        out_specs=[blk] * 4,
        out_shape=[jax.ShapeDtypeStruct((rows, cols), F32)] * 4,
        compiler_params=_cparams("parallel"),
    )(parts, w, m, v)


SM_G, SM_B, SM_CONV, SM_REL = 0, 4, 8, 16
SM_ROWS = SM_REL + 2 * N_HEADS
REL_W = 384


def _pack_small(d_rel, d_conv, dg, db):
    buf = jnp.zeros((SM_ROWS, D_MODEL), F32)
    buf = buf.at[SM_G:SM_G + DEPTH].set(dg)
    buf = buf.at[SM_B:SM_B + DEPTH].set(db)
    buf = buf.at[SM_CONV:SM_CONV + 6].set(d_conv.reshape(6, E_MIX))
    buf = buf.at[SM_REL:, :N_REL].set(d_rel.reshape(2 * N_HEADS, N_REL))
    return buf


def kernel(x, mem, w_in, w_mem_kv, w_out, rel_bias, conv_w, ln_g, ln_b, loss_target, m_w_in, m_w_mem_kv, m_w_out, m_rel_bias, m_conv_w, m_ln_g, m_ln_b, v_w_in, v_w_mem_kv, v_w_out, v_rel_bias, v_conv_w, v_ln_g, v_ln_b):
    me = _lin(_me())
    c_in, r_kv, r_out, c_conv = w_in.shape[2], w_mem_kv.shape[1], w_out.shape[1], conv_w.shape[2]

    conv_tile = jnp.pad(conv_w.reshape(6, c_conv), ((0, SUBLANES - 6), (0, 0)))
    w_in_f, w_kv_f, w_out_f, conv_land = _gather_weights(
        w_in.astype(BF16), w_mem_kv.astype(BF16), w_out.astype(BF16), conv_tile)
    conv_f = jnp.transpose(conv_land[:, :6], (1, 0, 2)).reshape(2, 3, N_DEV * c_conv)

    (loss, grad_x, dw_in, dw_kv, dw_out, d_rel, d_conv, dg, db) = _local_step(
        x[0], mem[0], w_in_f, w_kv_f, w_out_f, rel_bias, conv_f, ln_g, ln_b, loss_target[0])

    small = _pack_small(d_rel, d_conv, dg, db)
    p_in, p_kv, p_out, p_small = _scatter_grads(dw_in, dw_kv, dw_out, small)

    def big(parts, w, m, v, name):
        shp = w.shape
        flat = lambda a: a.reshape(shp[0] * shp[1], shp[2])
        outs = _reduce_adamw(parts.reshape(N_DEV, shp[0] * shp[1], shp[2]), flat(w), flat(m), flat(v), name)
        return [o.reshape(shp) for o in outs]

    g_in, d_in, nm_in, nv_in = big(p_in, w_in, m_w_in, v_w_in, "adamw_w_in")
    g_kv, d_kv, nm_kv, nv_kv = big(p_kv, w_mem_kv, m_w_mem_kv, v_w_mem_kv, "adamw_w_kv")
    g_out, d_out, nm_out, nv_out = big(p_out, w_out, m_w_out, v_w_out, "adamw_w_out")

    def pack_state(rel, conv, g, b):
        conv_full = jnp.zeros((2, 3, E_MIX), F32)
        conv_full = lax.dynamic_update_slice(conv_full, conv, (0, 0, me * c_conv))
        return _pack_small(rel, conv_full, g, b)

    sm_w = pack_state(rel_bias, conv_w, ln_g, ln_b)
    sm_m = pack_state(m_rel_bias, m_conv_w, m_ln_g, m_ln_b)
    sm_v = pack_state(v_rel_bias, v_conv_w, v_ln_g, v_ln_b)
    sm_outs = _reduce_adamw(p_small, sm_w, sm_m, sm_v, "adamw_small")

    def unpack(buf):
        rel = buf[SM_REL:, :N_REL].reshape(2, N_HEADS, N_REL)
        conv = lax.dynamic_slice(buf[SM_CONV:SM_CONV + 6].reshape(2, 3, E_MIX), (0, 0, me * c_conv), (2, 3, c_conv))
        return rel, conv, buf[SM_G:SM_G + DEPTH], buf[SM_B:SM_B + DEPTH]

    g_sm, d_sm, nm_sm, nv_sm = [unpack(b) for b in sm_outs]

    loss = lax.psum(loss[0, 0], ("x", "y", "c"))
    return (loss, grad_x[None],
            g_in, g_kv, g_out, *g_sm,
            d_in, d_kv, d_out, *d_sm,
            nm_in, nm_kv, nm_out, *nm_sm,
            nv_in, nv_kv, nv_out, *nv_sm)
```

```python
import functools
import math

import jax
import jax.numpy as jnp
from jax import lax
from jax.experimental import pallas as pl
from jax.experimental.pallas import tpu as pltpu

F32 = jnp.float32
BF16 = jnp.bfloat16
MXU_DTYPE = jnp.bfloat16

N_DEV = 8
D_MODEL = 1024
DEPTH = 4
CHUNK = 64
N_PREV = 8
N_HEADS = 16
HEAD_DIM = 64
E_MIX = 1024
REL_CLIP = 128
N_REL = 2 * REL_CLIP + 1
N_REL_PAD = 384
N_MEM = 256
MEM_HEADS = 4
MEM_HEAD_DIM = 128
E_MEM = 512
E_BRANCH = E_MIX + E_MEM
N_IN = 3 * E_MIX + E_MEM + E_BRANCH
DN_ALPHA = (2.0 * DEPTH) ** 0.25
LN_EPS = 1e-5
NEG = -1e30

ADAM_LR = 0.001
ADAM_B1 = 0.9
ADAM_B2 = 0.999
ADAM_EPS = 1e-08
ADAM_WD = 0.01
ADAM_STEP = 10

LANES = 128
SUBLANES = 8
VMEM_LIMIT = 56 * 1024 * 1024

TQ = 4 * CHUNK
TKEYS = 3 * TQ
ROLL_W = 1024
TS = 256
QM_BLK = 3 * E_MIX // E_MEM
Z_BLK = QM_BLK + 1


def _call(body, **kw):
    return pl.pallas_call(body, **kw)


def _cparams(*sem):
    return pltpu.CompilerParams(dimension_semantics=sem, vmem_limit_bytes=VMEM_LIMIT)


def _dot(a, b):
    return jnp.dot(a, b, preferred_element_type=F32)


def _dot_nt(a, b):
    return lax.dot_general(a, b, (((1,), (1,)), ((), ())), preferred_element_type=F32)


def _dot_tn(a, b):
    return lax.dot_general(a, b, (((0,), (0,)), ((), ())), preferred_element_type=F32)


def _inproj(x, w):
    s, d = x.shape
    n = w.shape[1]
    tm = min(1024, s)
    tn = 1024

    def body(x_ref, w_ref, o_ref, xb_ref):
        xb = x_ref[...].astype(MXU_DTYPE)

        @pl.when(pl.program_id(1) == 0)
        def _():
            xb_ref[...] = xb.astype(xb_ref.dtype)

        o_ref[...] = _dot(xb, w_ref[...]).astype(o_ref.dtype)

    return _call(
        body, name="inproj", grid=(s // tm, n // tn),
        in_specs=[pl.BlockSpec((tm, d), lambda i, j: (i, 0)),
                  pl.BlockSpec((d, tn), lambda i, j: (0, j))],
        out_specs=[pl.BlockSpec((tm, tn), lambda i, j: (i, j)),
                   pl.BlockSpec((tm, d), lambda i, j: (i, 0))],
        out_shape=[jax.ShapeDtypeStruct((s, n), BF16), jax.ShapeDtypeStruct((s, d), BF16)],
        compiler_params=_cparams("parallel", "arbitrary"),
    )(x, w)


def _small_matmul(a, b, trans_a, out_dtype, name):
    m = a.shape[1] if trans_a else a.shape[0]
    n = b.shape[1]

    def body(a_ref, b_ref, o_ref):
        av = a_ref[...].astype(MXU_DTYPE)
        bv = b_ref[...].astype(MXU_DTYPE)
        r = _dot_tn(av, bv) if trans_a else _dot(av, bv)
        o_ref[...] = r.astype(out_dtype)

    return _call(
        body, name=name,
        in_specs=[pl.BlockSpec(memory_space=pltpu.VMEM)] * 2,
        out_specs=pl.BlockSpec(memory_space=pltpu.VMEM),
        out_shape=jax.ShapeDtypeStruct((m, n), out_dtype),
        compiler_params=pltpu.CompilerParams(vmem_limit_bytes=VMEM_LIMIT),
    )(a, b)


def _piece_blocks(pieces, blk):
    offs, nbs, o = [], [], 0
    for p in pieces:
        nb = p.shape[1] // blk
        offs.append(o)
        nbs.append(nb)
        o += nb
    return offs, nbs, o


def _dx_matmul(pieces, w, addend):
    s = pieces[0].shape[0]
    d = w.shape[0]
    tm = min(1024, s)
    tk = 1024
    offs, nbs, nk = _piece_blocks(pieces, tk)
    np_ = len(pieces)

    def body(*refs):
        a_refs = refs[:np_]
        w_ref, add_ref, o_ref = refs[np_:]
        k = pl.program_id(1)

        @pl.when(k == 0)
        def _():
            o_ref[...] = add_ref[...]

        for p in range(np_):
            @pl.when((k >= offs[p]) & (k < offs[p] + nbs[p]))
            def _(p=p):
                o_ref[...] += _dot_nt(a_refs[p][...], w_ref[...])

    def amap(p):
        return lambda i, k: (i, jnp.clip(k - offs[p], 0, nbs[p] - 1))

    in_specs = [pl.BlockSpec((tm, tk), amap(p)) for p in range(np_)]
    in_specs += [pl.BlockSpec((d, tk), lambda i, k: (0, k)),
                 pl.BlockSpec((tm, d), lambda i, k: (i, 0))]
    return _call(
        body, name="dx_matmul", grid=(s // tm, nk),
        in_specs=in_specs,
        out_specs=pl.BlockSpec((tm, d), lambda i, k: (i, 0)),
        out_shape=jax.ShapeDtypeStruct((s, d), F32),
        compiler_params=_cparams("parallel", "arbitrary"),
    )(*pieces, w, addend)


def _dw_matmul(x, pieces):
    s, d = x.shape
    tn = 1024
    tk = min(1024, s)
    offs, nbs, nj = _piece_blocks(pieces, tn)
    np_ = len(pieces)
    nk = s // tk

    def body(*refs):
        x_ref = refs[0]
        b_refs = refs[1:1 + np_]
        o_ref, acc = refs[1 + np_:]
        j = pl.program_id(0)
        k = pl.program_id(1)

        @pl.when(k == 0)
        def _():
            acc[...] = jnp.zeros_like(acc)

        for p in range(np_):
            @pl.when((j >= offs[p]) & (j < offs[p] + nbs[p]))
            def _(p=p):
                acc[...] += _dot_tn(x_ref[...], b_refs[p][...])

        @pl.when(k == nk - 1)
        def _():
            o_ref[...] = acc[...].astype(o_ref.dtype)

    def bmap(p):
        def f(j, k):
            inside = (j >= offs[p]) & (j < offs[p] + nbs[p])
            return (jnp.where(inside, k, 0), jnp.clip(j - offs[p], 0, nbs[p] - 1))
        return f

    in_specs = [pl.BlockSpec((tk, d), lambda j, k: (k, 0))]
    in_specs += [pl.BlockSpec((tk, tn), bmap(p)) for p in range(np_)]
    return _call(
        body, name="dw_matmul", grid=(nj, nk),
        in_specs=in_specs,
        out_specs=pl.BlockSpec((d, tn), lambda j, k: (0, j)),
        out_shape=jax.ShapeDtypeStruct((d, nj * tn), BF16),
        scratch_shapes=[pltpu.VMEM((d, tn), F32)],
        compiler_params=_cparams("parallel", "arbitrary"),
    )(x, *pieces)


def _rel_onehot():
    j = lax.broadcasted_iota(jnp.int32, (N_REL_PAD, ROLL_W), 1)
    kk = lax.broadcasted_iota(jnp.int32, (N_REL_PAD, ROLL_W), 0)
    dd = jnp.where(j < TKEYS, j, j - ROLL_W)
    idx = jnp.clip(N_PREV * CHUNK - dd, -REL_CLIP, REL_CLIP) + REL_CLIP
    return jnp.where(idx == kk, 1.0, 0.0).astype(F32)


def _band_mask():
    r = lax.broadcasted_iota(jnp.int32, (TQ, TKEYS), 0) // CHUNK
    m = lax.broadcasted_iota(jnp.int32, (TQ, TKEYS), 1) // CHUNK
    return (m >= r) & (m <= r + N_PREV)


def _tile_bias(table_pad):
    def body(t_ref, o_ref):
        g = jnp.dot(t_ref[...], _rel_onehot(), preferred_element_type=F32,
                    precision=lax.Precision.HIGHEST)
        band = _band_mask()
        for h in range(N_HEADS):
            gh = jnp.broadcast_to(g[h:h + 1, :], (TQ, ROLL_W))
            rolled = pltpu.roll(gh, 0, 1, stride=1, stride_axis=0)
            o_ref[h] = jnp.where(band, rolled[:, :TKEYS], NEG)

    return _call(
        body, name="tile_bias",
        in_specs=[pl.BlockSpec(memory_space=pltpu.VMEM)],
        out_specs=pl.BlockSpec(memory_space=pltpu.VMEM),
        out_shape=jax.ShapeDtypeStruct((N_HEADS, TQ, TKEYS), F32),
        compiler_params=pltpu.CompilerParams(vmem_limit_bytes=VMEM_LIMIT),
    )(table_pad)


def _tile_bias_bwd(dtb):
    def body(d_ref, o_ref, g_ref):
        zpad = jnp.zeros((TQ, ROLL_W - TKEYS), F32)
        rr = lax.broadcasted_iota(jnp.int32, (TQ, TQ), 0)
        cc = lax.broadcasted_iota(jnp.int32, (TQ, TQ), 1)
        flip = jnp.where(rr + cc == TQ - 1, 1.0, 0.0).astype(F32)
        for h in range(N_HEADS):
            xh = jnp.concatenate([d_ref[h], zpad], axis=1)
            xf = jnp.dot(flip, xh, preferred_element_type=F32, precision=lax.Precision.HIGHEST)
            rolled = pltpu.roll(xf, 0, 1, stride=1, stride_axis=0)
            g_ref[h:h + 1, :] = jnp.sum(rolled, axis=0, keepdims=True)
        g = pltpu.roll(g_ref[...], ROLL_W - (TQ - 1), 1)
        o_ref[...] = lax.dot_general(g, _rel_onehot(), (((1,), (1,)), ((), ())),
                                     preferred_element_type=F32, precision=lax.Precision.HIGHEST)

    return _call(
        body, name="tile_bias_bwd",
        in_specs=[pl.BlockSpec(memory_space=pltpu.VMEM)],
        out_specs=pl.BlockSpec(memory_space=pltpu.VMEM),
        out_shape=jax.ShapeDtypeStruct((N_HEADS, N_REL_PAD), F32),
        scratch_shapes=[pltpu.VMEM((N_HEADS, ROLL_W), F32)],
        compiler_params=pltpu.CompilerParams(vmem_limit_bytes=VMEM_LIMIT),
    )(dtb)


def _head_masks():
    lane = lax.broadcasted_iota(jnp.int32, (1, 2 * HEAD_DIM), 1)
    return [lane < HEAD_DIM, lane >= HEAD_DIM]


def _attn_probs(qm, kcat, tb, valid):
    s = _dot_nt(qm, kcat) * (1.0 / math.sqrt(HEAD_DIM)) + tb
    s = jnp.where(valid, s, NEG)
    m = jnp.max(s, axis=-1, keepdims=True)
    e = jnp.exp(s - m)
    return e / jnp.sum(e, axis=-1, keepdims=True)


def _key_valid(i):
    col = lax.broadcasted_iota(jnp.int32, (TQ, TKEYS), 1)
    return col >= jnp.maximum(2 - i, 0) * TQ


def _kv_specs(col0, nt):
    def spec(back):
        return pl.BlockSpec((TQ, 2 * HEAD_DIM),
                            lambda hp, i: (jnp.clip(i - back, 0, nt - 1), col0 + hp))
    return [spec(2), spec(1), spec(0)]


def _attn_fwd(h, tb):
    s = h.shape[0]
    nt = s // TQ
    nhp = N_HEADS // 2

    def body(q_ref, k0, k1, k2, v0, v1, v2, tb_ref, o_ref):
        i = pl.program_id(1)
        valid = _key_valid(i)
        masks = _head_masks()
        q2 = q_ref[...].astype(MXU_DTYPE)
        kcat = jnp.concatenate([k0[...], k1[...], k2[...]], axis=0).astype(MXU_DTYPE)
        vcat = jnp.concatenate([v0[...], v1[...], v2[...]], axis=0).astype(MXU_DTYPE)
        outs = []
        for hh in range(2):
            qm = jnp.where(masks[hh], q2, jnp.zeros_like(q2))
            p = _attn_probs(qm, kcat, tb_ref[hh], valid)
            outs.append(_dot(p.astype(MXU_DTYPE), vcat))
        o_ref[...] = jnp.where(masks[0], outs[0], outs[1])

    in_specs = [pl.BlockSpec((TQ, 2 * HEAD_DIM), lambda hp, i: (i, hp))]
    in_specs += _kv_specs(nhp, nt) + _kv_specs(2 * nhp, nt)
    in_specs += [pl.BlockSpec((2, TQ, TKEYS), lambda hp, i: (hp, 0, 0))]
    return _call(
        body, name="attn_fwd", grid=(nhp, nt),
        in_specs=in_specs,
        out_specs=pl.BlockSpec((TQ, 2 * HEAD_DIM), lambda hp, i: (i, hp)),
        out_shape=jax.ShapeDtypeStruct((s, E_MIX), F32),
        compiler_params=_cparams("parallel", "parallel"),
    )(h, h, h, h, h, h, h, tb)


def _attn_bwd(h, tb, d_mix):
    s = h.shape[0]
    nt = s // TQ
    nhp = N_HEADS // 2
    scale = 1.0 / math.sqrt(HEAD_DIM)

    def body(q_ref, k0, k1, k2, v0, v1, v2, tb_ref, do_ref,
             dq_ref, dk_ref, dv_ref, dtb_ref, dk_acc, dv_acc):
        i = pl.program_id(1)

        @pl.when(i == 0)
        def _():
            dk_acc[...] = jnp.zeros_like(dk_acc)
            dv_acc[...] = jnp.zeros_like(dv_acc)
            dtb_ref[...] = jnp.zeros_like(dtb_ref)

        @pl.when((i > 0) & (i < nt))
        def _():
            dk_acc[i % 3] = jnp.zeros((TQ, 2 * HEAD_DIM), F32)
            dv_acc[i % 3] = jnp.zeros((TQ, 2 * HEAD_DIM), F32)

        @pl.when(i < nt)
        def _():
            valid = _key_valid(i)
            masks = _head_masks()
            q2 = q_ref[...].astype(MXU_DTYPE)
            do2 = do_ref[...].astype(MXU_DTYPE)
            kcat = jnp.concatenate([k0[...], k1[...], k2[...]], axis=0).astype(MXU_DTYPE)
            vcat = jnp.concatenate([v0[...], v1[...], v2[...]], axis=0).astype(MXU_DTYPE)
            dqs, dks, dvs = [], [], []
            for hh in range(2):
                qm = jnp.where(masks[hh], q2, jnp.zeros_like(q2))
                dom = jnp.where(masks[hh], do2, jnp.zeros_like(do2))
                p = _attn_probs(qm, kcat, tb_ref[hh], valid)
                dp = _dot_nt(dom, vcat)
                ds = p * (dp - jnp.sum(p * dp, axis=-1, keepdims=True))
                dtb_ref[hh] += ds
                dsb = (ds * scale).astype(MXU_DTYPE)
                dqs.append(_dot(dsb, kcat))
                dks.append(_dot_tn(dsb, q2))
                dvs.append(_dot_tn(p.astype(MXU_DTYPE), do2))
            dq_ref[...] = jnp.where(masks[0], dqs[0], dqs[1]).astype(dq_ref.dtype)
            dkc = jnp.where(masks[0], dks[0], dks[1])
            dvc = jnp.where(masks[0], dvs[0], dvs[1])
            for jj in range(3):
                slot = (i + 1 + jj) % 3
                dk_acc[slot] += dkc[jj * TQ:(jj + 1) * TQ]
                dv_acc[slot] += dvc[jj * TQ:(jj + 1) * TQ]

        @pl.when(i >= 2)
        def _():
            slot = (i - 2) % 3
            dk_ref[...] = dk_acc[slot].astype(dk_ref.dtype)
            dv_ref[...] = dv_acc[slot].astype(dv_ref.dtype)

    qmap = lambda hp, i: (jnp.minimum(i, nt - 1), hp)
    kvout = lambda hp, i: (jnp.maximum(i - 2, 0), hp)
    in_specs = [pl.BlockSpec((TQ, 2 * HEAD_DIM), qmap)]
    in_specs += _kv_specs(nhp, nt) + _kv_specs(2 * nhp, nt)
    in_specs += [pl.BlockSpec((2, TQ, TKEYS), lambda hp, i: (hp, 0, 0)),
                 pl.BlockSpec((TQ, 2 * HEAD_DIM), qmap)]
    blk = (TQ, 2 * HEAD_DIM)
    return _call(
        body, name="attn_bwd", grid=(nhp, nt + 2),
        in_specs=in_specs,
        out_specs=[pl.BlockSpec(blk, qmap), pl.BlockSpec(blk, kvout), pl.BlockSpec(blk, kvout),
                   pl.BlockSpec((2, TQ, TKEYS), lambda hp, i: (hp, 0, 0))],
        out_shape=[jax.ShapeDtypeStruct((s, E_MIX), BF16)] * 3
        + [jax.ShapeDtypeStruct((N_HEADS, TQ, TKEYS), F32)],
        scratch_shapes=[pltpu.VMEM((3, TQ, 2 * HEAD_DIM), F32)] * 2,
        compiler_params=_cparams("parallel", "arbitrary"),
    )(h, h, h, h, h, h, h, tb, d_mix)


CONV_TS = 256
HALO = 2 * SUBLANES


def _shift_down(prev, cur, k):
    rolled = pltpu.roll(cur, k, 0)
    row = lax.broadcasted_iota(jnp.int32, (HALO, cur.shape[1]), 0)
    top = jnp.where(row < k, pltpu.roll(prev, k, 0), rolled[:HALO])
    return jnp.concatenate([top, rolled[HALO:]], axis=0)


def _shift_up(cur, nxt, k):
    ts = cur.shape[0]
    rolled = pltpu.roll(cur, ts - k, 0)
    row = lax.broadcasted_iota(jnp.int32, (HALO, cur.shape[1]), 0)
    bottom = jnp.where(row >= HALO - k, pltpu.roll(nxt, HALO - k, 0), rolled[ts - HALO:])
    return jnp.concatenate([rolled[:ts - HALO], bottom], axis=0)


def _conv_specs(ts, nb):
    tile = lambda c: pl.BlockSpec((ts, E_MIX), lambda i: (i, c))
    prev = lambda c: pl.BlockSpec((HALO, E_MIX), lambda i: (jnp.maximum(i * (ts // HALO) - 1, 0), c))
    return tile, prev


def _conv_fwd(h, w8):
    s = h.shape[0]
    ts = CONV_TS
    nb = s // ts
    tile, prev = _conv_specs(ts, nb)

    def body(bg, cg, u, cgp, up, w_ref, o_ref):
        i = pl.program_id(0)
        a = cg[...].astype(F32) * u[...].astype(F32)
        ap = jnp.where(i > 0, cgp[...].astype(F32) * up[...].astype(F32), 0.0)
        w = w_ref[...]
        conv = w[0:1] * _shift_down(ap, a, 2) + w[1:2] * _shift_down(ap, a, 1) + w[2:3] * a
        o_ref[...] = bg[...].astype(F32) * conv

    return _call(
        body, name="conv_fwd", grid=(nb,),
        in_specs=[tile(0), tile(1), tile(2), prev(1), prev(2),
                  pl.BlockSpec((SUBLANES, E_MIX), lambda i: (0, 0))],
        out_specs=pl.BlockSpec((ts, E_MIX), lambda i: (i, 0)),
        out_shape=jax.ShapeDtypeStruct((s, E_MIX), F32),
        compiler_params=_cparams("parallel"),
    )(h, h, h, h, h, w8)


def _conv_bwd(h, w8, d_mix):
    s = h.shape[0]
    ts = CONV_TS
    nb = s // ts
    tile, prev = _conv_specs(ts, nb)
    nrow = s // HALO
    nxt = lambda c: pl.BlockSpec((HALO, E_MIX), lambda i: (jnp.minimum((i + 1) * (ts // HALO), nrow - 1), c))

    def body(bg, cg, u, cgp, up, bgn, dmix, dmixn, w_ref, dbg_ref, dcg_ref, du_ref, dw_ref):
        i = pl.program_id(0)

        @pl.when(i == 0)
        def _():
            dw_ref[...] = jnp.zeros_like(dw_ref)

        cgv, uv = cg[...].astype(F32), u[...].astype(F32)
        a = cgv * uv
        ap = jnp.where(i > 0, cgp[...].astype(F32) * up[...].astype(F32), 0.0)
        a1 = _shift_down(ap, a, 1)
        a2 = _shift_down(ap, a, 2)
        w = w_ref[...]
        conv = w[0:1] * a2 + w[1:2] * a1 + w[2:3] * a
        dm = dmix[...]
        dbg_ref[...] = (dm * conv).astype(dbg_ref.dtype)
        dc = dm * bg[...].astype(F32)
        dcn = jnp.where(i < nb - 1, dmixn[...] * bgn[...].astype(F32), 0.0)
        da = w[2:3] * dc + w[1:2] * _shift_up(dc, dcn, 1) + w[0:1] * _shift_up(dc, dcn, 2)
        dcg_ref[...] = (da * uv).astype(dcg_ref.dtype)
        du_ref[...] = (da * cgv).astype(du_ref.dtype)
        dw_ref[0:1, :] += jnp.sum(dc * a2, axis=0, keepdims=True)
        dw_ref[1:2, :] += jnp.sum(dc * a1, axis=0, keepdims=True)
        dw_ref[2:3, :] += jnp.sum(dc * a, axis=0, keepdims=True)

    full = lambda: pl.BlockSpec((ts, E_MIX), lambda i: (i, 0))
    return _call(
        body, name="conv_bwd", grid=(nb,),
        in_specs=[tile(0), tile(1), tile(2), prev(1), prev(2), nxt(0),
                  full(), pl.BlockSpec((HALO, E_MIX), lambda i: (jnp.minimum((i + 1) * (ts // HALO), nrow - 1), 0)),
                  pl.BlockSpec((SUBLANES, E_MIX), lambda i: (0, 0))],
        out_specs=[full(), full(), full(), pl.BlockSpec((SUBLANES, E_MIX), lambda i: (0, 0))],
        out_shape=[jax.ShapeDtypeStruct((s, E_MIX), BF16)] * 3
        + [jax.ShapeDtypeStruct((SUBLANES, E_MIX), F32)],
        compiler_params=_cparams("arbitrary"),
    )(h, h, h, h, h, h, d_mix, d_mix, w8)


def _mem_probs(qh, kh):
    s = _dot_nt(qh, kh) / math.sqrt(MEM_HEAD_DIM)
    m = jnp.max(s, axis=-1, keepdims=True)
    e = jnp.exp(s - m)
    return e / jnp.sum(e, axis=-1, keepdims=True)


def _sigmoid(z):
    return 1.0 / (1.0 + jnp.exp(-z))


def _layer_out_fwd(x, h, mix, kv, w_out, g, b):
    s, d = x.shape
    ts = TS

    def body(x_ref, mix_ref, qm_ref, z0, z1, z2, kv_ref, wo_ref, g_ref, b_ref,
             xn_ref, r_ref, mem_ref):
        qm = qm_ref[...].astype(MXU_DTYPE)
        kvb = kv_ref[...].astype(MXU_DTYPE)
        mems = []
        for hh in range(MEM_HEADS):
            lo = hh * MEM_HEAD_DIM
            p = _mem_probs(qm[:, lo:lo + MEM_HEAD_DIM], kvb[:, lo:lo + MEM_HEAD_DIM])
            mems.append(_dot(p.astype(MXU_DTYPE), kvb[:, E_MEM + lo:E_MEM + lo + MEM_HEAD_DIM]))
        mem = jnp.concatenate(mems, axis=1)
        mem_ref[...] = mem
        mixv = mix_ref[...]
        half = E_MIX // 2
        parts = [mixv[:, :half], mixv[:, half:], mem]
        out = jnp.zeros((ts, d), F32)
        for c, zr in enumerate((z0, z1, z2)):
            zv = zr[...].astype(F32)
            y = (parts[c] * (zv * _sigmoid(zv))).astype(MXU_DTYPE)
            out += _dot(y, wo_ref[c * half:(c + 1) * half, :])
        r = DN_ALPHA * x_ref[...] + out
        r_ref[...] = r
        mu = jnp.mean(r, axis=-1, keepdims=True)
        rc = r - mu
        var = jnp.mean(rc * rc, axis=-1, keepdims=True)
        xn_ref[...] = rc * lax.rsqrt(var + LN_EPS) * g_ref[...] + b_ref[...]

    row = lambda w, c: pl.BlockSpec((ts, w), lambda i: (i, c))
    const = lambda shp: pl.BlockSpec(shp, lambda i: (0, 0))
    return _call(
        body, name="layer_out_fwd", grid=(s // ts,),
        in_specs=[row(d, 0), row(E_MIX, 0), row(E_MEM, QM_BLK),
                  row(E_MEM, Z_BLK), row(E_MEM, Z_BLK + 1), row(E_MEM, Z_BLK + 2),
                  const((N_MEM, 2 * E_MEM)), const((E_BRANCH, d)), const((1, d)), const((1, d))],
        out_specs=[row(d, 0), row(d, 0), row(E_MEM, 0)],
        out_shape=[jax.ShapeDtypeStruct((s, d), F32), jax.ShapeDtypeStruct((s, d), F32),
                   jax.ShapeDtypeStruct((s, E_MEM), F32)],
        compiler_params=_cparams("parallel"),
    )(x, mix, h, h, h, h, kv, w_out, g, b)


def _layer_out_bwd(dxn, r, g, h, mix, mem, kv, w_out):
    s, d = r.shape
    ts = TS
    nb = s // ts
    half = E_MIX // 2
    inv = 1.0 / math.sqrt(MEM_HEAD_DIM)

    def body(dxn_ref, r_ref, g_ref, mix_ref, mem_ref, qm_ref, z0, z1, z2, kv_ref, wo_ref,
             dxr_ref, dmix_ref, dqz_ref, dwo_ref, dkv_ref, dg_ref, db_ref, dw_acc):
        i = pl.program_id(0)

        @pl.when(i == 0)
        def _():
            dw_acc[...] = jnp.zeros_like(dw_acc)
            dkv_ref[...] = jnp.zeros_like(dkv_ref)
            dg_ref[...] = jnp.zeros_like(dg_ref)
            db_ref[...] = jnp.zeros_like(db_ref)

        dxn_v = dxn_ref[...]
        rv = r_ref[...]
        mu = jnp.mean(rv, axis=-1, keepdims=True)
        rc = rv - mu
        var = jnp.mean(rc * rc, axis=-1, keepdims=True)
        rstd = lax.rsqrt(var + LN_EPS)
        xhat = rc * rstd
        dg_ref[...] += jnp.sum(dxn_v * xhat, axis=0, keepdims=True)
        db_ref[...] += jnp.sum(dxn_v, axis=0, keepdims=True)
        dxh = dxn_v * g_ref[...]
        m1 = jnp.mean(dxh, axis=-1, keepdims=True)
        m2 = jnp.mean(dxh * xhat, axis=-1, keepdims=True)
        dr = rstd * (dxh - m1 - xhat * m2)
        dxr_ref[...] = DN_ALPHA * dr
        dout = dr.astype(MXU_DTYPE)
        mixv = mix_ref[...]
        parts = [mixv[:, :half], mixv[:, half:], mem_ref[...]]
        dcs = []
        for c, zr in enumerate((z0, z1, z2)):
            lo = c * half
            zv = zr[...].astype(F32)
            sg = _sigmoid(zv)
            sl = zv * sg
            dy = _dot_nt(dout, wo_ref[lo:lo + half, :])
            y = (parts[c] * sl).astype(MXU_DTYPE)
            dw_acc[lo:lo + half, :] += _dot_tn(y, dout)
            dcs.append(dy * sl)
            dqz_ref[:, E_MEM + lo:E_MEM + lo + half] = (
                dy * parts[c] * (sg * (1.0 + zv * (1.0 - sg)))).astype(dqz_ref.dtype)
        dmix_ref[...] = jnp.concatenate(dcs[:2], axis=1)

        qm = qm_ref[...].astype(MXU_DTYPE)
        kvb = kv_ref[...].astype(MXU_DTYPE)
        dmb = dcs[2].astype(MXU_DTYPE)
        for hh in range(MEM_HEADS):
            lo = hh * MEM_HEAD_DIM
            qh = qm[:, lo:lo + MEM_HEAD_DIM]
            kh = kvb[:, lo:lo + MEM_HEAD_DIM]
            vh = kvb[:, E_MEM + lo:E_MEM + lo + MEM_HEAD_DIM]
            dmh = dmb[:, lo:lo + MEM_HEAD_DIM]
            p = _mem_probs(qh, kh)
            dp = _dot_nt(dmh, vh)
            ds = p * (dp - jnp.sum(p * dp, axis=-1, keepdims=True))
            dsb = (ds * inv).astype(MXU_DTYPE)
            dqz_ref[:, lo:lo + MEM_HEAD_DIM] = _dot(dsb, kh).astype(dqz_ref.dtype)
            dkv_ref[:, lo:lo + MEM_HEAD_DIM] += _dot_tn(dsb, qh)
            dkv_ref[:, E_MEM + lo:E_MEM + lo + MEM_HEAD_DIM] += _dot_tn(p.astype(MXU_DTYPE), dmh)

        @pl.when(i == nb - 1)
        def _():
            dwo_ref[...] = dw_acc[...].astype(dwo_ref.dtype)

    row = lambda w, c: pl.BlockSpec((ts, w), lambda i: (i, c))
    const = lambda shp: pl.BlockSpec(shp, lambda i: (0, 0))
    return _call(
        body, name="layer_out_bwd", grid=(nb,),
        in_specs=[row(d, 0), row(d, 0), const((1, d)), row(E_MIX, 0), row(E_MEM, 0),
                  row(E_MEM, QM_BLK), row(E_MEM, Z_BLK), row(E_MEM, Z_BLK + 1), row(E_MEM, Z_BLK + 2),
                  const((N_MEM, 2 * E_MEM)), const((E_BRANCH, d))],
        out_specs=[row(d, 0), row(E_MIX, 0), row(E_MEM + E_BRANCH, 0),
                   const((E_BRANCH, d)), const((N_MEM, 2 * E_MEM)), const((1, d)), const((1, d))],
        out_shape=[jax.ShapeDtypeStruct((s, d), F32), jax.ShapeDtypeStruct((s, E_MIX), F32),
                   jax.ShapeDtypeStruct((s, E_MEM + E_BRANCH), BF16),
                   jax.ShapeDtypeStruct((E_BRANCH, d), BF16),
                   jax.ShapeDtypeStruct((N_MEM, 2 * E_MEM), F32),
                   jax.ShapeDtypeStruct((1, d), F32), jax.ShapeDtypeStruct((1, d), F32)],
        scratch_shapes=[pltpu.VMEM((E_BRANCH, d), F32)],
        compiler_params=_cparams("arbitrary"),
    )(dxn, r, g, mix, mem, h, h, h, h, kv, w_out)


def _loss_head(y, target):
    s, d = y.shape
    ts = 512

    def body(y_ref, t_ref, l_ref, dy_ref):
        @pl.when(pl.program_id(0) == 0)
        def _():
            l_ref[...] = jnp.zeros_like(l_ref)

        e = y_ref[...] - t_ref[...]
        dy_ref[...] = e * (1.0 / d)
        l_ref[...] += (0.5 / d) * jnp.sum(jnp.sum(e * e, axis=1, keepdims=True), axis=0, keepdims=True)

    return _call(
        body, name="loss_head", grid=(s // ts,),
        in_specs=[pl.BlockSpec((ts, d), lambda i: (i, 0))] * 2,
        out_specs=[pl.BlockSpec((1, 1), lambda i: (0, 0)), pl.BlockSpec((ts, d), lambda i: (i, 0))],
        out_shape=[jax.ShapeDtypeStruct((1, 1), F32), jax.ShapeDtypeStruct((s, d), F32)],
        compiler_params=_cparams("arbitrary"),
    )(y, target)


def _local_step(x, mem, w_in, w_kv, w_out, rel_bias, conv_w, ln_g, ln_b, target):
    saved = []
    xl = x
    for layer in range(DEPTH):
        h, xb = _inproj(xl, w_in[layer])
        if layer % 2 == 0:
            table = jnp.pad(rel_bias[layer // 2], ((0, 0), (0, N_REL_PAD - N_REL)))
            aux = _tile_bias(table)
            mix = _attn_fwd(h, aux)
        else:
            aux = jnp.pad(conv_w[layer // 2], ((0, SUBLANES - 3), (0, 0)))
            mix = _conv_fwd(h, aux)
        kv = _small_matmul(mem, w_kv[layer], False, F32, "kv_mem")
        xn, r, mem_out = _layer_out_fwd(xl, h, mix, kv, w_out[layer],
                                        ln_g[layer][None], ln_b[layer][None])
        saved.append((xb, h, aux, mix, kv, r, mem_out))
        xl = xn

    loss, dx = _loss_head(xl, target)

    dw_in, dw_kv, dw_out, dgs, dbs = [], [], [], [], []
    d_rel, d_conv = [], []
    for layer in reversed(range(DEPTH)):
        xb, h, aux, mix, kv, r, mem_out = saved[layer]
        dx_res, d_mix, dqz, dwo, dkv, dg, db = _layer_out_bwd(
            dx, r, ln_g[layer][None], h, mix, mem_out, kv, w_out[layer])
        if layer % 2 == 0:
            dq, dk, dv, dtb = _attn_bwd(h, aux, d_mix)
            d_rel.append(_tile_bias_bwd(dtb)[:, :N_REL])
            pieces = [dq, dk, dv, dqz]
        else:
            dbg, dcg, du, dw8 = _conv_bwd(h, aux, d_mix)
            d_conv.append(dw8[:3])
            pieces = [dbg, dcg, du, dqz]
        dw_in.append(_dw_matmul(xb, pieces))
        dw_kv.append(_small_matmul(mem, dkv, True, BF16, "dw_kv"))
        dw_out.append(dwo)
        dgs.append(dg[0])
        dbs.append(db[0])
        dx = _dx_matmul(pieces, w_in[layer], dx_res)

    rev = lambda lst: jnp.stack(lst[::-1])
    return (loss, dx, rev(dw_in), rev(dw_kv), rev(dw_out),
            rev(d_rel), rev(d_conv), rev(dgs), rev(dbs))


def _me():
    return lax.axis_index("x"), lax.axis_index("y"), lax.axis_index("c")


def _peer(k):
    x, y, c = _me()
    kx, ky, kc = (k >> 2) & 1, (k >> 1) & 1, k & 1
    return (1 - x if kx else x, 1 - y if ky else y, 1 - c if kc else c)


def _lin(dev):
    return 4 * dev[0] + 2 * dev[1] + dev[2]


ANY = pl.BlockSpec(memory_space=pl.ANY)


def _exchange(srcs, dst_shapes, src_slice, dst_slice, name):
    na = len(srcs)

    def body(*refs):
        src_refs = refs[:na]
        dst_refs = refs[na:2 * na]
        send_sems, recv_sems, local_sems = refs[2 * na:]
        me = _lin(_me())
        copies = []
        for a in range(na):
            loc = pltpu.make_async_copy(src_slice(a, src_refs[a], me), dst_slice(a, dst_refs[a], me),
                                        local_sems.at[a])
            loc.start()
            copies.append(loc)
            for k in range(1, N_DEV):
                peer = _peer(k)
                cp = pltpu.make_async_remote_copy(
                    src_ref=src_slice(a, src_refs[a], _lin(peer)),
                    dst_ref=dst_slice(a, dst_refs[a], me),
                    send_sem=send_sems.at[a, k - 1], recv_sem=recv_sems.at[a, k - 1],
                    device_id=peer, device_id_type=pl.DeviceIdType.MESH)
                cp.start()
                copies.append(cp)
        for cp in copies:
            cp.wait()

    return _call(
        body, name=name,
        in_specs=[ANY] * na, out_specs=[ANY] * na,
        out_shape=[jax.ShapeDtypeStruct(shp, s.dtype) for shp, s in zip(dst_shapes, srcs)],
        scratch_shapes=[pltpu.SemaphoreType.DMA((na, N_DEV - 1)),
                        pltpu.SemaphoreType.DMA((na, N_DEV - 1)),
                        pltpu.SemaphoreType.DMA((na,))],
    )(*srcs)


def _gather_weights(w_in_s, w_kv_s, w_out_s, conv_s):
    srcs = [w_in_s, w_kv_s, w_out_s, conv_s]
    c_in, r_kv, r_out = w_in_s.shape[2], w_kv_s.shape[1], w_out_s.shape[1]
    shapes = [(DEPTH, D_MODEL, N_DEV * c_in), (DEPTH, N_DEV * r_kv, w_kv_s.shape[2]),
              (DEPTH, N_DEV * r_out, D_MODEL), (N_DEV,) + conv_s.shape]

    def src_slice(a, ref, p):
        return ref

    def dst_slice(a, ref, me):
        if a == 0:
            return ref.at[:, :, pl.ds(pl.multiple_of(me * c_in, LANES), c_in)]
        if a == 1:
            return ref.at[:, pl.ds(pl.multiple_of(me * r_kv, 2 * SUBLANES), r_kv), :]
        if a == 2:
            return ref.at[:, pl.ds(pl.multiple_of(me * r_out, 2 * SUBLANES), r_out), :]
        return ref.at[me]

    return _exchange(srcs, shapes, src_slice, dst_slice, "gather_weights")


def _scatter_grads(dw_in, dw_kv, dw_out, small):
    srcs = [dw_in, dw_kv, dw_out, small]
    c_in, r_kv, r_out = dw_in.shape[2] // N_DEV, dw_kv.shape[1] // N_DEV, dw_out.shape[1] // N_DEV
    shapes = [(N_DEV, DEPTH, D_MODEL, c_in), (N_DEV, DEPTH, r_kv, dw_kv.shape[2]),
              (N_DEV, DEPTH, r_out, D_MODEL), (N_DEV,) + small.shape]

    def src_slice(a, ref, p):
        if a == 0:
            return ref.at[:, :, pl.ds(pl.multiple_of(p * c_in, LANES), c_in)]
        if a == 1:
            return ref.at[:, pl.ds(pl.multiple_of(p * r_kv, 2 * SUBLANES), r_kv), :]
        if a == 2:
            return ref.at[:, pl.ds(pl.multiple_of(p * r_out, 2 * SUBLANES), r_out), :]
        return ref

    def dst_slice(a, ref, me):
        return ref.at[me]

    return _exchange(srcs, shapes, src_slice, dst_slice, "scatter_grads")


def _adamw_math(w, g, m, v):
    m = ADAM_B1 * m + (1.0 - ADAM_B1) * g
    v = ADAM_B2 * v + (1.0 - ADAM_B2) * (g * g)
    m_hat = m / (1.0 - ADAM_B1 ** ADAM_STEP)
    v_hat = v / (1.0 - ADAM_B2 ** ADAM_STEP)
    delta = -ADAM_LR * (m_hat / (jnp.sqrt(v_hat) + ADAM_EPS) + ADAM_WD * w)
    return delta, m, v


def _reduce_adamw(parts, w, m, v, name):
    rows, cols = w.shape
    tr = rows
    for cand in (512, 256, 128, 64, 32, 16):
        if rows % cand == 0 and rows > cand:
            tr = cand
            break

    def body(p_ref, w_ref, m_ref, v_ref, g_out, d_out, m_out, v_out):
        g = p_ref[0].astype(F32)
        for s in range(1, N_DEV):
            g = g + p_ref[s].astype(F32)
        g_out[...] = g
        d_out[...], m_out[...], v_out[...] = _adamw_math(w_ref[...], g, m_ref[...], v_ref[...])

    blk = pl.BlockSpec((tr, cols), lambda i: (i, 0))
    return _call(
        body, name=name, grid=(rows // tr,),
        in_specs=[pl.BlockSpec((N_DEV, tr, cols), lambda i: (0, i, 0)), blk, blk, blk],
        out_specs=[blk] * 4,
        out_shape=[jax.ShapeDtypeStruct((rows, cols), F32)] * 4,
        compiler_params=_cparams("parallel"),
    )(parts, w, m, v)


SM_G, SM_B, SM_CONV, SM_REL = 0, 4, 8, 16
SM_ROWS = SM_REL + 2 * N_HEADS
REL_W = 384


def _pack_small(d_rel, d_conv, dg, db):
    buf = jnp.zeros((SM_ROWS, D_MODEL), F32)
    buf = buf.at[SM_G:SM_G + DEPTH].set(dg)
    buf = buf.at[SM_B:SM_B + DEPTH].set(db)
    buf = buf.at[SM_CONV:SM_CONV + 6].set(d_conv.reshape(6, E_MIX))
    buf = buf.at[SM_REL:, :N_REL].set(d_rel.reshape(2 * N_HEADS, N_REL))
    return buf


def kernel(x, mem, w_in, w_mem_kv, w_out, rel_bias, conv_w, ln_g, ln_b, loss_target, m_w_in, m_w_mem_kv, m_w_out, m_rel_bias, m_conv_w, m_ln_g, m_ln_b, v_w_in, v_w_mem_kv, v_w_out, v_rel_bias, v_conv_w, v_ln_g, v_ln_b):
    me = _lin(_me())
    c_in, r_kv, r_out, c_conv = w_in.shape[2], w_mem_kv.shape[1], w_out.shape[1], conv_w.shape[2]

    conv_tile = jnp.pad(conv_w.reshape(6, c_conv), ((0, SUBLANES - 6), (0, 0)))
    w_in_f, w_kv_f, w_out_f, conv_land = _gather_weights(
        w_in.astype(BF16), w_mem_kv.astype(BF16), w_out.astype(BF16), conv_tile)
    conv_f = jnp.transpose(conv_land[:, :6], (1, 0, 2)).reshape(2, 3, N_DEV * c_conv)

    (loss, grad_x, dw_in, dw_kv, dw_out, d_rel, d_conv, dg, db) = _local_step(
        x[0], mem[0], w_in_f, w_kv_f, w_out_f, rel_bias, conv_f, ln_g, ln_b, loss_target[0])

    small = _pack_small(d_rel, d_conv, dg, db)
    p_in, p_kv, p_out, p_small = _scatter_grads(dw_in, dw_kv, dw_out, small)

    def big(parts, w, m, v, name):
        shp = w.shape
        flat = lambda a: a.reshape(shp[0] * shp[1], shp[2])
        outs = _reduce_adamw(parts.reshape(N_DEV, shp[0] * shp[1], shp[2]), flat(w), flat(m), flat(v), name)
        return [o.reshape(shp) for o in outs]

    g_in, d_in, nm_in, nv_in = big(p_in, w_in, m_w_in, v_w_in, "adamw_w_in")
    g_kv, d_kv, nm_kv, nv_kv = big(p_kv, w_mem_kv, m_w_mem_kv, v_w_mem_kv, "adamw_w_kv")
    g_out, d_out, nm_out, nv_out = big(p_out, w_out, m_w_out, v_w_out, "adamw_w_out")

    def pack_state(rel, conv, g, b):
        conv_full = jnp.zeros((2, 3, E_MIX), F32)
        conv_full = lax.dynamic_update_slice(conv_full, conv, (0, 0, me * c_conv))
        return _pack_small(rel, conv_full, g, b)

    sm_w = pack_state(rel_bias, conv_w, ln_g, ln_b)
    sm_m = pack_state(m_rel_bias, m_conv_w, m_ln_g, m_ln_b)
    sm_v = pack_state(v_rel_bias, v_conv_w, v_ln_g, v_ln_b)
    sm_outs = _reduce_adamw(p_small, sm_w, sm_m, sm_v, "adamw_small")

    def unpack(buf):
        rel = buf[SM_REL:, :N_REL].reshape(2, N_HEADS, N_REL)
        conv = lax.dynamic_slice(buf[SM_CONV:SM_CONV + 6].reshape(2, 3, E_MIX), (0, 0, me * c_conv), (2, 3, c_conv))
        return rel, conv, buf[SM_G:SM_G + DEPTH], buf[SM_B:SM_B + DEPTH]

    g_sm, d_sm, nm_sm, nv_sm = [unpack(b) for b in sm_outs]

    loss = lax.psum(loss[0, 0], ("x", "y", "c"))
    return (loss, grad_x[None],
            g_in, g_kv, g_out, *g_sm,
            d_in, d_kv, d_out, *d_sm,
            nm_in, nm_kv, nm_out, *nm_sm,
            nv_in, nv_kv, nv_out, *nv_sm)
```

```python
import functools
import math

import jax
import jax.numpy as jnp
from jax import lax
from jax.experimental import pallas as pl
from jax.experimental.pallas import tpu as pltpu

F32 = jnp.float32
BF16 = jnp.bfloat16
MXU_DTYPE = jnp.bfloat16

N_DEV = 8
D_MODEL = 1024
DEPTH = 4
CHUNK = 64
N_PREV = 8
N_HEADS = 16
HEAD_DIM = 64
E_MIX = 1024
REL_CLIP = 128
N_REL = 2 * REL_CLIP + 1
N_REL_PAD = 384
N_MEM = 256
MEM_HEADS = 4
MEM_HEAD_DIM = 128
E_MEM = 512
E_BRANCH = E_MIX + E_MEM
N_IN = 3 * E_MIX + E_MEM + E_BRANCH
DN_ALPHA = (2.0 * DEPTH) ** 0.25
LN_EPS = 1e-5
NEG = -1e30

ADAM_LR = 0.001
ADAM_B1 = 0.9
ADAM_B2 = 0.999
ADAM_EPS = 1e-08
ADAM_WD = 0.01
ADAM_STEP = 10

LANES = 128
SUBLANES = 8
VMEM_LIMIT = 56 * 1024 * 1024

TQ = 4 * CHUNK
TKEYS = 3 * TQ
ROLL_W = 1024
TS = 256
QM_BLK = 3 * E_MIX // E_MEM
Z_BLK = QM_BLK + 1


def _call(body, **kw):
    return pl.pallas_call(body, **kw)


def _cparams(*sem):
    return pltpu.CompilerParams(dimension_semantics=sem, vmem_limit_bytes=VMEM_LIMIT)


def _dot(a, b):
    return jnp.dot(a, b, preferred_element_type=F32)


def _dot_nt(a, b):
    return lax.dot_general(a, b, (((1,), (1,)), ((), ())), preferred_element_type=F32)


def _dot_tn(a, b):
    return lax.dot_general(a, b, (((0,), (0,)), ((), ())), preferred_element_type=F32)


def _inproj(x, w):
    s, d = x.shape
    n = w.shape[1]
    tm = min(1024, s)
    tn = 1024

    def body(x_ref, w_ref, o_ref, xb_ref):
        xb = x_ref[...].astype(MXU_DTYPE)

        @pl.when(pl.program_id(1) == 0)
        def _():
            xb_ref[...] = xb.astype(xb_ref.dtype)

        o_ref[...] = _dot(xb, w_ref[...]).astype(o_ref.dtype)

    return _call(
        body, name="inproj", grid=(s // tm, n // tn),
        in_specs=[pl.BlockSpec((tm, d), lambda i, j: (i, 0)),
                  pl.BlockSpec((d, tn), lambda i, j: (0, j))],
        out_specs=[pl.BlockSpec((tm, tn), lambda i, j: (i, j)),
                   pl.BlockSpec((tm, d), lambda i, j: (i, 0))],
        out_shape=[jax.ShapeDtypeStruct((s, n), BF16), jax.ShapeDtypeStruct((s, d), BF16)],
        compiler_params=_cparams("parallel", "arbitrary"),
    )(x, w)


def _small_matmul(a, b, trans_a, out_dtype, name):
    m = a.shape[1] if trans_a else a.shape[0]
    n = b.shape[1]

    def body(a_ref, b_ref, o_ref):
        av = a_ref[...].astype(MXU_DTYPE)
        bv = b_ref[...].astype(MXU_DTYPE)
        r = _dot_tn(av, bv) if trans_a else _dot(av, bv)
        o_ref[...] = r.astype(out_dtype)

    return _call(
        body, name=name,
        in_specs=[pl.BlockSpec(memory_space=pltpu.VMEM)] * 2,
        out_specs=pl.BlockSpec(memory_space=pltpu.VMEM),
        out_shape=jax.ShapeDtypeStruct((m, n), out_dtype),
        compiler_params=pltpu.CompilerParams(vmem_limit_bytes=VMEM_LIMIT),
    )(a, b)


def _piece_blocks(pieces, blk):
    offs, nbs, o = [], [], 0
    for p in pieces:
        nb = p.shape[1] // blk
        offs.append(o)
        nbs.append(nb)
        o += nb
    return offs, nbs, o


def _dx_matmul(pieces, w, addend, token=None):
    s = pieces[0].shape[0]
    d = w.shape[0]
    tm = min(1024, s)
    tk = 1024
    offs, nbs, nk = _piece_blocks(pieces, tk)
    np_ = len(pieces)
    extra = [] if token is None else [token]

    def body(*refs):
        a_refs = refs[:np_]
        w_ref, add_ref = refs[np_:np_ + 2]
        o_ref = refs[-1]
        k = pl.program_id(1)

        @pl.when(k == 0)
        def _():
            o_ref[...] = add_ref[...]

        for p in range(np_):
            @pl.when((k >= offs[p]) & (k < offs[p] + nbs[p]))
            def _(p=p):
                o_ref[...] += _dot_nt(a_refs[p][...], w_ref[...])

    def amap(p):
        return lambda i, k: (i, jnp.clip(k - offs[p], 0, nbs[p] - 1))

    in_specs = [pl.BlockSpec((tm, tk), amap(p)) for p in range(np_)]
    in_specs += [pl.BlockSpec((d, tk), lambda i, k: (0, k)),
                 pl.BlockSpec((tm, d), lambda i, k: (i, 0))]
    in_specs += [pl.BlockSpec((SUBLANES, LANES), lambda i, k: (0, 0)) for _ in extra]
    return _call(
        body, name="dx_matmul", grid=(s // tm, nk),
        in_specs=in_specs,
        out_specs=pl.BlockSpec((tm, d), lambda i, k: (i, 0)),
        out_shape=jax.ShapeDtypeStruct((s, d), F32),
        compiler_params=_cparams("parallel", "arbitrary"),
    )(*pieces, w, addend, *extra)


def _dw_matmul(x, pieces):
    s, d = x.shape
    tn = 1024
    tk = min(1024, s)
    offs, nbs, nj = _piece_blocks(pieces, tn)
    np_ = len(pieces)
    nk = s // tk

    def body(*refs):
        x_ref = refs[0]
        b_refs = refs[1:1 + np_]
        o_ref, acc = refs[1 + np_:]
        j = pl.program_id(0)
        k = pl.program_id(1)

        @pl.when(k == 0)
        def _():
            acc[...] = jnp.zeros_like(acc)

        for p in range(np_):
            @pl.when((j >= offs[p]) & (j < offs[p] + nbs[p]))
            def _(p=p):
                acc[...] += _dot_tn(x_ref[...], b_refs[p][...])

        @pl.when(k == nk - 1)
        def _():
            o_ref[...] = acc[...].astype(o_ref.dtype)

    def bmap(p):
        def f(j, k):
            inside = (j >= offs[p]) & (j < offs[p] + nbs[p])
            return (jnp.where(inside, k, 0), jnp.clip(j - offs[p], 0, nbs[p] - 1))
        return f

    in_specs = [pl.BlockSpec((tk, d), lambda j, k: (k, 0))]
    in_specs += [pl.BlockSpec((tk, tn), bmap(p)) for p in range(np_)]
    return _call(
        body, name="dw_matmul", grid=(nj, nk),
        in_specs=in_specs,
        out_specs=pl.BlockSpec((d, tn), lambda j, k: (0, j)),
        out_shape=jax.ShapeDtypeStruct((d, nj * tn), BF16),
        scratch_shapes=[pltpu.VMEM((d, tn), F32)],
        compiler_params=_cparams("parallel", "arbitrary"),
    )(x, *pieces)


def _rel_onehot():
    j = lax.broadcasted_iota(jnp.int32, (N_REL_PAD, ROLL_W), 1)
    kk = lax.broadcasted_iota(jnp.int32, (N_REL_PAD, ROLL_W), 0)
    dd = jnp.where(j < TKEYS, j, j - ROLL_W)
    idx = jnp.clip(N_PREV * CHUNK - dd, -REL_CLIP, REL_CLIP) + REL_CLIP
    return jnp.where(idx == kk, 1.0, 0.0).astype(F32)


def _band_mask():
    r = lax.broadcasted_iota(jnp.int32, (TQ, TKEYS), 0) // CHUNK
    m = lax.broadcasted_iota(jnp.int32, (TQ, TKEYS), 1) // CHUNK
    return (m >= r) & (m <= r + N_PREV)


def _tile_bias(table_pad):
    def body(t_ref, o_ref):
        g = jnp.dot(t_ref[...], _rel_onehot(), preferred_element_type=F32,
                    precision=lax.Precision.HIGHEST)
        band = _band_mask()
        for h in range(N_HEADS):
            gh = jnp.broadcast_to(g[h:h + 1, :], (TQ, ROLL_W))
            rolled = pltpu.roll(gh, 0, 1, stride=1, stride_axis=0)
            o_ref[h] = jnp.where(band, rolled[:, :TKEYS], NEG)

    return _call(
        body, name="tile_bias",
        in_specs=[pl.BlockSpec(memory_space=pltpu.VMEM)],
        out_specs=pl.BlockSpec(memory_space=pltpu.VMEM),
        out_shape=jax.ShapeDtypeStruct((N_HEADS, TQ, TKEYS), F32),
        compiler_params=pltpu.CompilerParams(vmem_limit_bytes=VMEM_LIMIT),
    )(table_pad)


def _tile_bias_bwd(dtb):
    def body(d_ref, o_ref, g_ref):
        zpad = jnp.zeros((TQ, ROLL_W - TKEYS), F32)
        rr = lax.broadcasted_iota(jnp.int32, (TQ, TQ), 0)
        cc = lax.broadcasted_iota(jnp.int32, (TQ, TQ), 1)
        flip = jnp.where(rr + cc == TQ - 1, 1.0, 0.0).astype(F32)
        for h in range(N_HEADS):
            xh = jnp.concatenate([d_ref[h], zpad], axis=1)
            xf = jnp.dot(flip, xh, preferred_element_type=F32, precision=lax.Precision.HIGHEST)
            rolled = pltpu.roll(xf, 0, 1, stride=1, stride_axis=0)
            g_ref[h:h + 1, :] = jnp.sum(rolled, axis=0, keepdims=True)
        g = pltpu.roll(g_ref[...], ROLL_W - (TQ - 1), 1)
        o_ref[...] = lax.dot_general(g, _rel_onehot(), (((1,), (1,)), ((), ())),
                                     preferred_element_type=F32, precision=lax.Precision.HIGHEST)

    return _call(
        body, name="tile_bias_bwd",
        in_specs=[pl.BlockSpec(memory_space=pltpu.VMEM)],
        out_specs=pl.BlockSpec(memory_space=pltpu.VMEM),
        out_shape=jax.ShapeDtypeStruct((N_HEADS, N_REL_PAD), F32),
        scratch_shapes=[pltpu.VMEM((N_HEADS, ROLL_W), F32)],
        compiler_params=pltpu.CompilerParams(vmem_limit_bytes=VMEM_LIMIT),
    )(dtb)


def _head_masks():
    lane = lax.broadcasted_iota(jnp.int32, (1, 2 * HEAD_DIM), 1)
    return [lane < HEAD_DIM, lane >= HEAD_DIM]


def _attn_probs(qm, kcat, tb, valid):
    s = _dot_nt(qm, kcat) * (1.0 / math.sqrt(HEAD_DIM)) + tb
    s = jnp.where(valid, s, NEG)
    m = jnp.max(s, axis=-1, keepdims=True)
    e = jnp.exp(s - m)
    return e / jnp.sum(e, axis=-1, keepdims=True)


def _key_valid(i):
    col = lax.broadcasted_iota(jnp.int32, (TQ, TKEYS), 1)
    return col >= jnp.maximum(2 - i, 0) * TQ


def _kv_specs(col0, nt):
    def spec(back):
        return pl.BlockSpec((TQ, 2 * HEAD_DIM),
                            lambda hp, i: (jnp.clip(i - back, 0, nt - 1), col0 + hp))
    return [spec(2), spec(1), spec(0)]


def _attn_fwd(h, tb):
    s = h.shape[0]
    nt = s // TQ
    nhp = N_HEADS // 2

    def body(q_ref, k0, k1, k2, v0, v1, v2, tb_ref, o_ref):
        i = pl.program_id(1)
        valid = _key_valid(i)
        masks = _head_masks()
        q2 = q_ref[...].astype(MXU_DTYPE)
        kcat = jnp.concatenate([k0[...], k1[...], k2[...]], axis=0).astype(MXU_DTYPE)
        vcat = jnp.concatenate([v0[...], v1[...], v2[...]], axis=0).astype(MXU_DTYPE)
        outs = []
        for hh in range(2):
            qm = jnp.where(masks[hh], q2, jnp.zeros_like(q2))
            p = _attn_probs(qm, kcat, tb_ref[hh], valid)
            outs.append(_dot(p.astype(MXU_DTYPE), vcat))
        o_ref[...] = jnp.where(masks[0], outs[0], outs[1])

    in_specs = [pl.BlockSpec((TQ, 2 * HEAD_DIM), lambda hp, i: (i, hp))]
    in_specs += _kv_specs(nhp, nt) + _kv_specs(2 * nhp, nt)
    in_specs += [pl.BlockSpec((2, TQ, TKEYS), lambda hp, i: (hp, 0, 0))]
    return _call(
        body, name="attn_fwd", grid=(nhp, nt),
        in_specs=in_specs,
        out_specs=pl.BlockSpec((TQ, 2 * HEAD_DIM), lambda hp, i: (i, hp)),
        out_shape=jax.ShapeDtypeStruct((s, E_MIX), F32),
        compiler_params=_cparams("parallel", "parallel"),
    )(h, h, h, h, h, h, h, tb)


def _attn_bwd(h, tb, d_mix):
    s = h.shape[0]
    nt = s // TQ
    nhp = N_HEADS // 2
    scale = 1.0 / math.sqrt(HEAD_DIM)

    def body(q_ref, k0, k1, k2, v0, v1, v2, tb_ref, do_ref,
             dq_ref, dk_ref, dv_ref, dtb_ref, dk_acc, dv_acc):
        i = pl.program_id(1)

        @pl.when(i == 0)
        def _():
            dk_acc[...] = jnp.zeros_like(dk_acc)
            dv_acc[...] = jnp.zeros_like(dv_acc)
            dtb_ref[...] = jnp.zeros_like(dtb_ref)

        @pl.when((i > 0) & (i < nt))
        def _():
            dk_acc[i % 3] = jnp.zeros((TQ, 2 * HEAD_DIM), F32)
            dv_acc[i % 3] = jnp.zeros((TQ, 2 * HEAD_DIM), F32)

        @pl.when(i < nt)
        def _():
            valid = _key_valid(i)
            masks = _head_masks()
            q2 = q_ref[...].astype(MXU_DTYPE)
            do2 = do_ref[...].astype(MXU_DTYPE)
            kcat = jnp.concatenate([k0[...], k1[...], k2[...]], axis=0).astype(MXU_DTYPE)
            vcat = jnp.concatenate([v0[...], v1[...], v2[...]], axis=0).astype(MXU_DTYPE)
            dqs, dks, dvs = [], [], []
            for hh in range(2):
                qm = jnp.where(masks[hh], q2, jnp.zeros_like(q2))
                dom = jnp.where(masks[hh], do2, jnp.zeros_like(do2))
                p = _attn_probs(qm, kcat, tb_ref[hh], valid)
                dp = _dot_nt(dom, vcat)
                ds = p * (dp - jnp.sum(p * dp, axis=-1, keepdims=True))
                dtb_ref[hh] += ds
                dsb = (ds * scale).astype(MXU_DTYPE)
                dqs.append(_dot(dsb, kcat))
                dks.append(_dot_tn(dsb, q2))
                dvs.append(_dot_tn(p.astype(MXU_DTYPE), do2))
            dq_ref[...] = jnp.where(masks[0], dqs[0], dqs[1]).astype(dq_ref.dtype)
            dkc = jnp.where(masks[0], dks[0], dks[1])
            dvc = jnp.where(masks[0], dvs[0], dvs[1])
            for jj in range(3):
                slot = (i + 1 + jj) % 3
                dk_acc[slot] += dkc[jj * TQ:(jj + 1) * TQ]
                dv_acc[slot] += dvc[jj * TQ:(jj + 1) * TQ]

        @pl.when(i >= 2)
        def _():
            slot = (i - 2) % 3
            dk_ref[...] = dk_acc[slot].astype(dk_ref.dtype)
            dv_ref[...] = dv_acc[slot].astype(dv_ref.dtype)

    qmap = lambda hp, i: (jnp.minimum(i, nt - 1), hp)
    kvout = lambda hp, i: (jnp.maximum(i - 2, 0), hp)
    in_specs = [pl.BlockSpec((TQ, 2 * HEAD_DIM), qmap)]
    in_specs += _kv_specs(nhp, nt) + _kv_specs(2 * nhp, nt)
    in_specs += [pl.BlockSpec((2, TQ, TKEYS), lambda hp, i: (hp, 0, 0)),
                 pl.BlockSpec((TQ, 2 * HEAD_DIM), qmap)]
    blk = (TQ, 2 * HEAD_DIM)
    return _call(
        body, name="attn_bwd", grid=(nhp, nt + 2),
        in_specs=in_specs,
        out_specs=[pl.BlockSpec(blk, qmap), pl.BlockSpec(blk, kvout), pl.BlockSpec(blk, kvout),
                   pl.BlockSpec((2, TQ, TKEYS), lambda hp, i: (hp, 0, 0))],
        out_shape=[jax.ShapeDtypeStruct((s, E_MIX), BF16)] * 3
        + [jax.ShapeDtypeStruct((N_HEADS, TQ, TKEYS), F32)],
        scratch_shapes=[pltpu.VMEM((3, TQ, 2 * HEAD_DIM), F32)] * 2,
        compiler_params=_cparams("parallel", "arbitrary"),
    )(h, h, h, h, h, h, h, tb, d_mix)


CONV_TS = 256
HALO = 2 * SUBLANES


def _shift_down(prev, cur, k):
    rolled = pltpu.roll(cur, k, 0)
    row = lax.broadcasted_iota(jnp.int32, (HALO, cur.shape[1]), 0)
    top = jnp.where(row < k, pltpu.roll(prev, k, 0), rolled[:HALO])
    return jnp.concatenate([top, rolled[HALO:]], axis=0)


def _shift_up(cur, nxt, k):
    ts = cur.shape[0]
    rolled = pltpu.roll(cur, ts - k, 0)
    row = lax.broadcasted_iota(jnp.int32, (HALO, cur.shape[1]), 0)
    bottom = jnp.where(row >= HALO - k, pltpu.roll(nxt, HALO - k, 0), rolled[ts - HALO:])
    return jnp.concatenate([rolled[:ts - HALO], bottom], axis=0)


def _conv_specs(ts, nb):
    tile = lambda c: pl.BlockSpec((ts, E_MIX), lambda i: (i, c))
    prev = lambda c: pl.BlockSpec((HALO, E_MIX), lambda i: (jnp.maximum(i * (ts // HALO) - 1, 0), c))
    return tile, prev


def _conv_fwd(h, w8):
    s = h.shape[0]
    ts = CONV_TS
    nb = s // ts
    tile, prev = _conv_specs(ts, nb)

    def body(bg, cg, u, cgp, up, w_ref, o_ref):
        i = pl.program_id(0)
        a = cg[...].astype(F32) * u[...].astype(F32)
        ap = jnp.where(i > 0, cgp[...].astype(F32) * up[...].astype(F32), 0.0)
        w = w_ref[...]
        conv = w[0:1] * _shift_down(ap, a, 2) + w[1:2] * _shift_down(ap, a, 1) + w[2:3] * a
        o_ref[...] = bg[...].astype(F32) * conv

    return _call(
        body, name="conv_fwd", grid=(nb,),
        in_specs=[tile(0), tile(1), tile(2), prev(1), prev(2),
                  pl.BlockSpec((SUBLANES, E_MIX), lambda i: (0, 0))],
        out_specs=pl.BlockSpec((ts, E_MIX), lambda i: (i, 0)),
        out_shape=jax.ShapeDtypeStruct((s, E_MIX), F32),
        compiler_params=_cparams("parallel"),
    )(h, h, h, h, h, w8)


def _conv_bwd(h, w8, d_mix):
    s = h.shape[0]
    ts = CONV_TS
    nb = s // ts
    tile, prev = _conv_specs(ts, nb)
    nrow = s // HALO
    nxt = lambda c: pl.BlockSpec((HALO, E_MIX), lambda i: (jnp.minimum((i + 1) * (ts // HALO), nrow - 1), c))

    def body(bg, cg, u, cgp, up, bgn, dmix, dmixn, w_ref, dbg_ref, dcg_ref, du_ref, dw_ref):
        i = pl.program_id(0)

        @pl.when(i == 0)
        def _():
            dw_ref[...] = jnp.zeros_like(dw_ref)

        cgv, uv = cg[...].astype(F32), u[...].astype(F32)
        a = cgv * uv
        ap = jnp.where(i > 0, cgp[...].astype(F32) * up[...].astype(F32), 0.0)
        a1 = _shift_down(ap, a, 1)
        a2 = _shift_down(ap, a, 2)
        w = w_ref[...]
        conv = w[0:1] * a2 + w[1:2] * a1 + w[2:3] * a
        dm = dmix[...]
        dbg_ref[...] = (dm * conv).astype(dbg_ref.dtype)
        dc = dm * bg[...].astype(F32)
        dcn = jnp.where(i < nb - 1, dmixn[...] * bgn[...].astype(F32), 0.0)
        da = w[2:3] * dc + w[1:2] * _shift_up(dc, dcn, 1) + w[0:1] * _shift_up(dc, dcn, 2)
        dcg_ref[...] = (da * uv).astype(dcg_ref.dtype)
        du_ref[...] = (da * cgv).astype(du_ref.dtype)
        dw_ref[0:1, :] += jnp.sum(dc * a2, axis=0, keepdims=True)
        dw_ref[1:2, :] += jnp.sum(dc * a1, axis=0, keepdims=True)
        dw_ref[2:3, :] += jnp.sum(dc * a, axis=0, keepdims=True)

    full = lambda: pl.BlockSpec((ts, E_MIX), lambda i: (i, 0))
    return _call(
        body, name="conv_bwd", grid=(nb,),
        in_specs=[tile(0), tile(1), tile(2), prev(1), prev(2), nxt(0),
                  full(), pl.BlockSpec((HALO, E_MIX), lambda i: (jnp.minimum((i + 1) * (ts // HALO), nrow - 1), 0)),
                  pl.BlockSpec((SUBLANES, E_MIX), lambda i: (0, 0))],
        out_specs=[full(), full(), full(), pl.BlockSpec((SUBLANES, E_MIX), lambda i: (0, 0))],
        out_shape=[jax.ShapeDtypeStruct((s, E_MIX), BF16)] * 3
        + [jax.ShapeDtypeStruct((SUBLANES, E_MIX), F32)],
        compiler_params=_cparams("arbitrary"),
    )(h, h, h, h, h, h, d_mix, d_mix, w8)


def _mem_probs(qh, kh):
    s = _dot_nt(qh, kh) / math.sqrt(MEM_HEAD_DIM)
    m = jnp.max(s, axis=-1, keepdims=True)
    e = jnp.exp(s - m)
    return e / jnp.sum(e, axis=-1, keepdims=True)


def _sigmoid(z):
    return 1.0 / (1.0 + jnp.exp(-z))


def _layer_out_fwd(x, h, mix, kv, w_out, g, b):
    s, d = x.shape
    ts = TS

    def body(x_ref, mix_ref, qm_ref, z0, z1, z2, kv_ref, wo_ref, g_ref, b_ref,
             xn_ref, r_ref, mem_ref):
        qm = qm_ref[...].astype(MXU_DTYPE)
        kvb = kv_ref[...].astype(MXU_DTYPE)
        mems = []
        for hh in range(MEM_HEADS):
            lo = hh * MEM_HEAD_DIM
            p = _mem_probs(qm[:, lo:lo + MEM_HEAD_DIM], kvb[:, lo:lo + MEM_HEAD_DIM])
            mems.append(_dot(p.astype(MXU_DTYPE), kvb[:, E_MEM + lo:E_MEM + lo + MEM_HEAD_DIM]))
        mem = jnp.concatenate(mems, axis=1)
        mem_ref[...] = mem
        mixv = mix_ref[...]
        half = E_MIX // 2
        parts = [mixv[:, :half], mixv[:, half:], mem]
        out = jnp.zeros((ts, d), F32)
        for c, zr in enumerate((z0, z1, z2)):
            zv = zr[...].astype(F32)
            y = (parts[c] * (zv * _sigmoid(zv))).astype(MXU_DTYPE)
            out += _dot(y, wo_ref[c * half:(c + 1) * half, :])
        r = DN_ALPHA * x_ref[...] + out
        r_ref[...] = r
        mu = jnp.mean(r, axis=-1, keepdims=True)
        rc = r - mu
        var = jnp.mean(rc * rc, axis=-1, keepdims=True)
        xn_ref[...] = rc * lax.rsqrt(var + LN_EPS) * g_ref[...] + b_ref[...]

    row = lambda w, c: pl.BlockSpec((ts, w), lambda i: (i, c))
    const = lambda shp: pl.BlockSpec(shp, lambda i: (0, 0))
    return _call(
        body, name="layer_out_fwd", grid=(s // ts,),
        in_specs=[row(d, 0), row(E_MIX, 0), row(E_MEM, QM_BLK),
                  row(E_MEM, Z_BLK), row(E_MEM, Z_BLK + 1), row(E_MEM, Z_BLK + 2),
                  const((N_MEM, 2 * E_MEM)), const((E_BRANCH, d)), const((1, d)), const((1, d))],
        out_specs=[row(d, 0), row(d, 0), row(E_MEM, 0)],
        out_shape=[jax.ShapeDtypeStruct((s, d), F32), jax.ShapeDtypeStruct((s, d), F32),
                   jax.ShapeDtypeStruct((s, E_MEM), F32)],
        compiler_params=_cparams("parallel"),
    )(x, mix, h, h, h, h, kv, w_out, g, b)


def _layer_out_bwd(dxn, r, g, h, mix, mem, kv, w_out):
    s, d = r.shape
    ts = TS
    nb = s // ts
    half = E_MIX // 2
    inv = 1.0 / math.sqrt(MEM_HEAD_DIM)

    def body(dxn_ref, r_ref, g_ref, mix_ref, mem_ref, qm_ref, z0, z1, z2, kv_ref, wo_ref,
             dxr_ref, dmix_ref, dqz_ref, dwo_ref, dkv_ref, dg_ref, db_ref, dw_acc):
        i = pl.program_id(0)

        @pl.when(i == 0)
        def _():
            dw_acc[...] = jnp.zeros_like(dw_acc)
            dkv_ref[...] = jnp.zeros_like(dkv_ref)
            dg_ref[...] = jnp.zeros_like(dg_ref)
            db_ref[...] = jnp.zeros_like(db_ref)

        dxn_v = dxn_ref[...]
        rv = r_ref[...]
        mu = jnp.mean(rv, axis=-1, keepdims=True)
        rc = rv - mu
        var = jnp.mean(rc * rc, axis=-1, keepdims=True)
        rstd = lax.rsqrt(var + LN_EPS)
        xhat = rc * rstd
        dg_ref[...] += jnp.sum(dxn_v * xhat, axis=0, keepdims=True)
        db_ref[...] += jnp.sum(dxn_v, axis=0, keepdims=True)
        dxh = dxn_v * g_ref[...]
        m1 = jnp.mean(dxh, axis=-1, keepdims=True)
        m2 = jnp.mean(dxh * xhat, axis=-1, keepdims=True)
        dr = rstd * (dxh - m1 - xhat * m2)
        dxr_ref[...] = DN_ALPHA * dr
        dout = dr.astype(MXU_DTYPE)
        mixv = mix_ref[...]
        parts = [mixv[:, :half], mixv[:, half:], mem_ref[...]]
        dcs = []
        for c, zr in enumerate((z0, z1, z2)):
            lo = c * half
            zv = zr[...].astype(F32)
            sg = _sigmoid(zv)
            sl = zv * sg
            dy = _dot_nt(dout, wo_ref[lo:lo + half, :])
            y = (parts[c] * sl).astype(MXU_DTYPE)
            dw_acc[lo:lo + half, :] += _dot_tn(y, dout)
            dcs.append(dy * sl)
            dqz_ref[:, E_MEM + lo:E_MEM + lo + half] = (
                dy * parts[c] * (sg * (1.0 + zv * (1.0 - sg)))).astype(dqz_ref.dtype)
        dmix_ref[...] = jnp.concatenate(dcs[:2], axis=1)

        qm = qm_ref[...].astype(MXU_DTYPE)
        kvb = kv_ref[...].astype(MXU_DTYPE)
        dmb = dcs[2].astype(MXU_DTYPE)
        for hh in range(MEM_HEADS):
            lo = hh * MEM_HEAD_DIM
            qh = qm[:, lo:lo + MEM_HEAD_DIM]
            kh = kvb[:, lo:lo + MEM_HEAD_DIM]
            vh = kvb[:, E_MEM + lo:E_MEM + lo + MEM_HEAD_DIM]
            dmh = dmb[:, lo:lo + MEM_HEAD_DIM]
            p = _mem_probs(qh, kh)
            dp = _dot_nt(dmh, vh)
            ds = p * (dp - jnp.sum(p * dp, axis=-1, keepdims=True))
            dsb = (ds * inv).astype(MXU_DTYPE)
            dqz_ref[:, lo:lo + MEM_HEAD_DIM] = _dot(dsb, kh).astype(dqz_ref.dtype)
            dkv_ref[:, lo:lo + MEM_HEAD_DIM] += _dot_tn(dsb, qh)
            dkv_ref[:, E_MEM + lo:E_MEM + lo + MEM_HEAD_DIM] += _dot_tn(p.astype(MXU_DTYPE), dmh)

        @pl.when(i == nb - 1)
        def _():
            dwo_ref[...] = dw_acc[...].astype(dwo_ref.dtype)

    row = lambda w, c: pl.BlockSpec((ts, w), lambda i: (i, c))
    const = lambda shp: pl.BlockSpec(shp, lambda i: (0, 0))
    return _call(
        body, name="layer_out_bwd", grid=(nb,),
        in_specs=[row(d, 0), row(d, 0), const((1, d)), row(E_MIX, 0), row(E_MEM, 0),
                  row(E_MEM, QM_BLK), row(E_MEM, Z_BLK), row(E_MEM, Z_BLK + 1), row(E_MEM, Z_BLK + 2),
                  const((N_MEM, 2 * E_MEM)), const((E_BRANCH, d))],
        out_specs=[row(d, 0), row(E_MIX, 0), row(E_MEM + E_BRANCH, 0),
                   const((E_BRANCH, d)), const((N_MEM, 2 * E_MEM)), const((1, d)), const((1, d))],
        out_shape=[jax.ShapeDtypeStruct((s, d), F32), jax.ShapeDtypeStruct((s, E_MIX), F32),
                   jax.ShapeDtypeStruct((s, E_MEM + E_BRANCH), BF16),
                   jax.ShapeDtypeStruct((E_BRANCH, d), BF16),
                   jax.ShapeDtypeStruct((N_MEM, 2 * E_MEM), F32),
                   jax.ShapeDtypeStruct((1, d), F32), jax.ShapeDtypeStruct((1, d), F32)],
        scratch_shapes=[pltpu.VMEM((E_BRANCH, d), F32)],
        compiler_params=_cparams("arbitrary"),
    )(dxn, r, g, mix, mem, h, h, h, h, kv, w_out)


def _loss_head(y, target):
    s, d = y.shape
    ts = 512

    def body(y_ref, t_ref, l_ref, dy_ref):
        @pl.when(pl.program_id(0) == 0)
        def _():
            l_ref[...] = jnp.zeros_like(l_ref)

        e = y_ref[...] - t_ref[...]
        dy_ref[...] = e * (1.0 / d)
        l_ref[...] += (0.5 / d) * jnp.sum(jnp.sum(e * e, axis=1, keepdims=True), axis=0, keepdims=True)

    return _call(
        body, name="loss_head", grid=(s // ts,),
        in_specs=[pl.BlockSpec((ts, d), lambda i: (i, 0))] * 2,
        out_specs=[pl.BlockSpec((1, 1), lambda i: (0, 0)), pl.BlockSpec((ts, d), lambda i: (i, 0))],
        out_shape=[jax.ShapeDtypeStruct((1, 1), F32), jax.ShapeDtypeStruct((s, d), F32)],
        compiler_params=_cparams("arbitrary"),
    )(y, target)


def _local_step(x, mem, get_weights, put_grads, rel_bias, conv_w, ln_g, ln_b, target):
    saved = []
    xl = x
    for layer in range(DEPTH):
        w_in_l, w_kv_l, w_out_l = get_weights(layer, xl)
        h, xb = _inproj(xl, w_in_l)
        if layer % 2 == 0:
            table = jnp.pad(rel_bias[layer // 2], ((0, 0), (0, N_REL_PAD - N_REL)))
            aux = _tile_bias(table)
            mix = _attn_fwd(h, aux)
        else:
            aux = jnp.pad(conv_w[layer // 2], ((0, SUBLANES - 3), (0, 0)))
            mix = _conv_fwd(h, aux)
        kv = _small_matmul(mem, w_kv_l, False, F32, "kv_mem")
        xn, r, mem_out = _layer_out_fwd(xl, h, mix, kv, w_out_l,
                                        ln_g[layer][None], ln_b[layer][None])
        saved.append((xb, h, aux, mix, kv, r, mem_out, w_in_l, w_out_l))
        xl = xn

    loss, dx = _loss_head(xl, target)

    dgs, dbs, d_rel, d_conv = [], [], [], []
    for layer in reversed(range(DEPTH)):
        xb, h, aux, mix, kv, r, mem_out, w_in_l, w_out_l = saved[layer]
        dx_res, d_mix, dqz, dwo, dkv, dg, db = _layer_out_bwd(
            dx, r, ln_g[layer][None], h, mix, mem_out, kv, w_out_l)
        if layer % 2 == 0:
            dq, dk, dv, dtb = _attn_bwd(h, aux, d_mix)
            d_rel.append(_tile_bias_bwd(dtb)[:, :N_REL])
            pieces = [dq, dk, dv, dqz]
        else:
            dbg, dcg, du, dw8 = _conv_bwd(h, aux, d_mix)
            d_conv.append(dw8[:3])
            pieces = [dbg, dcg, du, dqz]
        token = put_grads(layer, _dw_matmul(xb, pieces), _small_matmul(mem, dkv, True, BF16, "dw_kv"), dwo)
        dgs.append(dg[0])
        dbs.append(db[0])
        dx = _dx_matmul(pieces, w_in_l, dx_res, token)

    rev = lambda lst: jnp.stack(lst[::-1])
    return loss, dx, rev(d_rel), rev(d_conv), rev(dgs), rev(dbs)


def _me():
    return lax.axis_index("x"), lax.axis_index("y"), lax.axis_index("c")


def _peer(k):
    x, y, c = _me()
    kx, ky, kc = (k >> 2) & 1, (k >> 1) & 1, k & 1
    return (1 - x if kx else x, 1 - y if ky else y, 1 - c if kc else c)


def _lin(dev):
    return 4 * dev[0] + 2 * dev[1] + dev[2]


ANY = pl.BlockSpec(memory_space=pl.ANY)


def _exchange(srcs, dst_shapes, src_slice, dst_slice, name):
    na = len(srcs)

    def body(*refs):
        src_refs = refs[:na]
        dst_refs = refs[na:2 * na]
        send_sems, recv_sems, local_sems = refs[2 * na:]
        me = _lin(_me())
        copies = []
        for a in range(na):
            loc = pltpu.make_async_copy(src_slice(a, src_refs[a], me), dst_slice(a, dst_refs[a], me),
                                        local_sems.at[a])
            loc.start()
            copies.append(loc)
            for k in range(1, N_DEV):
                peer = _peer(k)
                cp = pltpu.make_async_remote_copy(
                    src_ref=src_slice(a, src_refs[a], _lin(peer)),
                    dst_ref=dst_slice(a, dst_refs[a], me),
                    send_sem=send_sems.at[a, k - 1], recv_sem=recv_sems.at[a, k - 1],
                    device_id=peer, device_id_type=pl.DeviceIdType.MESH)
                cp.start()
                copies.append(cp)
        for cp in copies:
            cp.wait()

    return _call(
        body, name=name,
        in_specs=[ANY] * na, out_specs=[ANY] * na,
        out_shape=[jax.ShapeDtypeStruct(shp, s.dtype) for shp, s in zip(dst_shapes, srcs)],
        scratch_shapes=[pltpu.SemaphoreType.DMA((na, N_DEV - 1)),
                        pltpu.SemaphoreType.DMA((na, N_DEV - 1)),
                        pltpu.SemaphoreType.DMA((na,))],
    )(*srcs)


def _gather_to_all(src, name):
    return _exchange([src], [(N_DEV,) + src.shape], lambda a, ref, p: ref,
                     lambda a, ref, me: ref.at[me], name)[0]


HBM = pl.BlockSpec(memory_space=pltpu.HBM)
SEM = pl.BlockSpec(memory_space=pltpu.SEMAPHORE)
EFFECT = pltpu.SideEffectType.DATAFLOW_SIDE_EFFECTING
N_PEER = N_DEV - 1
N_KIND = 3


def _group_copies(src_refs, land_refs, send, recv, src_slice, dst_slice):
    me = _lin(_me())
    copies = []
    for j in range(N_KIND):
        for k in range(1, N_DEV):
            peer = _peer(k)
            copies.append(pltpu.make_async_remote_copy(
                src_ref=src_slice(j, src_refs[j], _lin(peer)),
                dst_ref=dst_slice(j, land_refs[j], me, k),
                send_sem=send.at[j * N_PEER + k - 1], recv_sem=recv.at[j * N_PEER + k - 1],
                device_id=peer, device_id_type=pl.DeviceIdType.MESH))
    return copies


def _split_start(srcs, land_shapes, src_slice, dst_slice, name):
    na = len(srcs)
    ng = na // N_KIND

    def body(*refs):
        src_refs, land_refs = refs[:na], refs[na:2 * na]
        sems = refs[2 * na:2 * na + 2 * ng]
        token = refs[-1]
        for g in range(ng):
            lo = g * N_KIND
            for cp in _group_copies(src_refs[lo:lo + N_KIND], land_refs[lo:lo + N_KIND],
                                    sems[2 * g], sems[2 * g + 1], src_slice, dst_slice):
                cp.start()
        token[...] = jnp.zeros_like(token)

    sem_shape = pltpu.SemaphoreType.DMA((N_KIND * N_PEER,))
    lands = [lax.empty(shp, s.dtype) for shp, s in zip(land_shapes, srcs)]
    outs = _call(
        body, name=name,
        in_specs=[HBM] * (2 * na),
        out_specs=[SEM] * (2 * ng) + [HBM] * (2 * na) + [pl.BlockSpec(memory_space=pltpu.VMEM)],
        out_shape=[sem_shape] * (2 * ng)
        + [pltpu.HBM(s.shape, s.dtype) for s in srcs]
        + [pltpu.HBM(shp, s.dtype) for shp, s in zip(land_shapes, srcs)]
        + [jax.ShapeDtypeStruct((SUBLANES, LANES), F32)],
        input_output_aliases={i: 2 * ng + i for i in range(2 * na)},
        compiler_params=pltpu.CompilerParams(has_side_effects=EFFECT),
    )(*[pltpu.with_memory_space_constraint(a, pltpu.HBM) for a in list(srcs) + lands])
    sems = [(outs[2 * g], outs[2 * g + 1]) for g in range(ng)]
    thrus = outs[2 * ng:2 * ng + na]
    lands = outs[2 * ng + na:2 * ng + 2 * na]
    return sems, thrus, lands, outs[-1]


def _split_wait(sems, thrus, lands, src_slice, dst_slice, after, name):
    def body(*refs):
        src_refs, land_refs = refs[:N_KIND], refs[N_KIND:2 * N_KIND]
        send, recv = refs[2 * N_KIND], refs[2 * N_KIND + 1]
        for cp in _group_copies(src_refs, land_refs, send, recv, src_slice, dst_slice):
            cp.wait_send()
            cp.wait_recv()

    outs = _call(
        body, name=name,
        in_specs=[HBM] * (2 * N_KIND) + [SEM, SEM, ANY],
        out_specs=[HBM] * (2 * N_KIND),
        out_shape=[pltpu.HBM(a.shape, a.dtype) for a in list(thrus) + list(lands)],
        input_output_aliases={i: i for i in range(2 * N_KIND)},
        compiler_params=pltpu.CompilerParams(has_side_effects=EFFECT),
    )(*thrus, *lands, sems[0], sems[1], after)
    return outs[N_KIND:]


def _shard_dims(c_in, r_kv, r_out):
    def sl(j, ref, p):
        if j == 0:
            return ref.at[:, pl.ds(pl.multiple_of(p * c_in, LANES), c_in)]
        r = r_kv if j == 1 else r_out
        return ref.at[pl.ds(pl.multiple_of(p * r, 2 * SUBLANES), r), :]
    return sl


def _adamw_math(w, g, m, v):
    m = ADAM_B1 * m + (1.0 - ADAM_B1) * g
    v = ADAM_B2 * v + (1.0 - ADAM_B2) * (g * g)
    m_hat = m / (1.0 - ADAM_B1 ** ADAM_STEP)
    v_hat = v / (1.0 - ADAM_B2 ** ADAM_STEP)
    delta = -ADAM_LR * (m_hat / (jnp.sqrt(v_hat) + ADAM_EPS) + ADAM_WD * w)
    return delta, m, v


def _reduce_adamw(parts, w, m, v, name):
    rows, cols = w.shape
    tr = rows
    for cand in (512, 256, 128, 64, 32, 16):
        if rows % cand == 0 and rows > cand:
            tr = cand
            break

    def body(p_ref, w_ref, m_ref, v_ref, g_out, d_out, m_out, v_out):
        g = p_ref[0].astype(F32)
        for s in range(1, N_DEV):
            g = g + p_ref[s].astype(F32)
        g_out[...] = g
        d_out[...], m_out[...], v_out[...] = _adamw_math(w_ref[...], g, m_ref[...], v_ref[...])

    blk = pl.BlockSpec((tr, cols), lambda i: (i, 0))
    return _call(
        body, name=name, grid=(rows // tr,),
        in_specs=[pl.BlockSpec((N_DEV, tr, cols), lambda i: (0, i, 0)), blk, blk, blk],
        out_specs=[blk] * 4,
        out_shape=[jax.ShapeDtypeStruct((rows, cols), F32)] * 4,
        compiler_params=_cparams("parallel"),
    )(parts, w, m, v)


def _reduce_adamw_layers(lands, owns, w, m, v, name):
    nl, rows, cols = w.shape
    tr = rows
    for cand in (256, 192, 128):
        if rows % cand == 0:
            tr = cand
            break

    def body(*refs):
        land_refs, own_refs = refs[:nl], refs[nl:2 * nl]
        w_ref, m_ref, v_ref, g_out, d_out, m_out, v_out = refs[2 * nl:]
        layer = pl.program_id(0)
        for a in range(nl):
            @pl.when(layer == a)
            def _(a=a):
                g = own_refs[a][...].astype(F32)
                for k in range(N_PEER):
                    g = g + land_refs[a][k].astype(F32)
                g_out[...] = g
                d_out[...], m_out[...], v_out[...] = _adamw_math(w_ref[...], g, m_ref[...], v_ref[...])

    def lmap(a):
        return lambda l, i: (0, jnp.where(l == a, i, 0), 0)

    def omap(a):
        return lambda l, i: (jnp.where(l == a, i, 0), 0)

    blk = pl.BlockSpec((None, tr, cols), lambda l, i: (l, i, 0))
    return _call(
        body, name=name, grid=(nl, rows // tr),
        in_specs=[pl.BlockSpec((N_PEER, tr, cols), lmap(a)) for a in range(nl)]
        + [pl.BlockSpec((tr, cols), omap(a)) for a in range(nl)] + [blk, blk, blk],
        out_specs=[blk] * 4,
        out_shape=[jax.ShapeDtypeStruct((nl, rows, cols), F32)] * 4,
        compiler_params=_cparams("arbitrary", "arbitrary"),
    )(*lands, *owns, w, m, v)


SM_G, SM_B, SM_CONV, SM_REL = 0, 4, 8, 16
SM_ROWS = SM_REL + 2 * N_HEADS
REL_W = 384


def _pack_small(d_rel, d_conv, dg, db):
    buf = jnp.zeros((SM_ROWS, D_MODEL), F32)
    buf = buf.at[SM_G:SM_G + DEPTH].set(dg)
    buf = buf.at[SM_B:SM_B + DEPTH].set(db)
    buf = buf.at[SM_CONV:SM_CONV + 6].set(d_conv.reshape(6, E_MIX))
    buf = buf.at[SM_REL:, :N_REL].set(d_rel.reshape(2 * N_HEADS, N_REL))
    return buf


def kernel(x, mem, w_in, w_mem_kv, w_out, rel_bias, conv_w, ln_g, ln_b, loss_target, m_w_in, m_w_mem_kv, m_w_out, m_rel_bias, m_conv_w, m_ln_g, m_ln_b, v_w_in, v_w_mem_kv, v_w_out, v_rel_bias, v_conv_w, v_ln_g, v_ln_b):
    me = _lin(_me())
    c_in, r_kv, r_out, c_conv = w_in.shape[2], w_mem_kv.shape[1], w_out.shape[1], conv_w.shape[2]

    shard = _shard_dims(c_in, r_kv, r_out)
    own_start = lambda j: (0, me * c_in) if j == 0 else (me * (r_kv if j == 1 else r_out), 0)

    w_sh = [w_in.astype(BF16), w_mem_kv.astype(BF16), w_out.astype(BF16)]
    full_shapes = [(D_MODEL, N_DEV * c_in), (N_DEV * r_kv, w_mem_kv.shape[2]), (N_DEV * r_out, D_MODEL)]
    ag_src = lambda j, ref, p: ref
    ag_dst = lambda j, ref, me_, k: shard(j, ref, me_)
    ag_sems, ag_thrus, ag_lands, _ = _split_start(
        [w_sh[j][layer] for layer in range(DEPTH) for j in range(N_KIND)],
        full_shapes * DEPTH, ag_src, ag_dst, "ag_start")

    def get_weights(layer, x_layer):
        lo = layer * N_KIND
        lands = _split_wait(ag_sems[layer], ag_thrus[lo:lo + N_KIND], ag_lands[lo:lo + N_KIND],
                            ag_src, ag_dst, x_layer, "ag_wait_%d" % layer)
        return [lax.dynamic_update_slice(lands[j], w_sh[j][layer], own_start(j)) for j in range(N_KIND)]

    rs_src = shard
    rs_dst = lambda j, ref, me_, k: ref.at[k - 1]
    rs_shapes = [(N_PEER, D_MODEL, c_in), (N_PEER, r_kv, w_mem_kv.shape[2]), (N_PEER, r_out, D_MODEL)]
    own_sizes = [(D_MODEL, c_in), (r_kv, w_mem_kv.shape[2]), (r_out, D_MODEL)]
    pending = {}

    def put_grads(layer, dwi, dwkv, dwo):
        parts = [dwi, dwkv, dwo]
        owns = [lax.dynamic_slice(parts[j], own_start(j), own_sizes[j]) for j in range(N_KIND)]
        sems, thrus, lands, token = _split_start(parts, rs_shapes, rs_src, rs_dst, "rs_start_%d" % layer)
        pending[layer] = (sems[0], thrus, lands, owns)
        return token

    conv_tile = jnp.pad(conv_w.reshape(6, c_conv), ((0, SUBLANES - 6), (0, 0)))
    conv_land = _gather_to_all(conv_tile, "gather_conv")
    conv_f = jnp.transpose(conv_land[:, :6], (1, 0, 2)).reshape(2, 3, N_DEV * c_conv)

    loss, grad_x, d_rel, d_conv, dg, db = _local_step(
        x[0], mem[0], get_weights, put_grads, rel_bias, conv_f, ln_g, ln_b, loss_target[0])

    p_small = _gather_to_all(_pack_small(d_rel, d_conv, dg, db), "gather_small_grads")

    rs_lands, rs_owns = [], []
    for layer in range(DEPTH):
        sems, thrus, lands, owns = pending[layer]
        rs_lands.append(_split_wait(sems, thrus, lands, rs_src, rs_dst, grad_x, "rs_wait_%d" % layer))
        rs_owns.append(owns)

    def big(j, w, m, v, name):
        return _reduce_adamw_layers([rs_lands[layer][j] for layer in range(DEPTH)],
                                    [rs_owns[layer][j] for layer in range(DEPTH)], w, m, v, name)

    g_in, d_in, nm_in, nv_in = big(0, w_in, m_w_in, v_w_in, "adamw_w_in")
    g_kv, d_kv, nm_kv, nv_kv = big(1, w_mem_kv, m_w_mem_kv, v_w_mem_kv, "adamw_w_kv")
    g_out, d_out, nm_out, nv_out = big(2, w_out, m_w_out, v_w_out, "adamw_w_out")

    def pack_state(rel, conv, g, b):
        conv_full = jnp.zeros((2, 3, E_MIX), F32)
        conv_full = lax.dynamic_update_slice(conv_full, conv, (0, 0, me * c_conv))
        return _pack_small(rel, conv_full, g, b)

    sm_w = pack_state(rel_bias, conv_w, ln_g, ln_b)
    sm_m = pack_state(m_rel_bias, m_conv_w, m_ln_g, m_ln_b)
    sm_v = pack_state(v_rel_bias, v_conv_w, v_ln_g, v_ln_b)
    sm_outs = _reduce_adamw(p_small, sm_w, sm_m, sm_v, "adamw_small")

    def unpack(buf):
        rel = buf[SM_REL:, :N_REL].reshape(2, N_HEADS, N_REL)
        conv = lax.dynamic_slice(buf[SM_CONV:SM_CONV + 6].reshape(2, 3, E_MIX), (0, 0, me * c_conv), (2, 3, c_conv))
        return rel, conv, buf[SM_G:SM_G + DEPTH], buf[SM_B:SM_B + DEPTH]

    g_sm, d_sm, nm_sm, nv_sm = [unpack(b) for b in sm_outs]

    loss = lax.psum(loss[0, 0], ("x", "y", "c"))
    return (loss, grad_x[None],
            g_in, g_kv, g_out, *g_sm,
            d_in, d_kv, d_out, *d_sm,
            nm_in, nm_kv, nm_out, *nm_sm,
            nv_in, nv_kv, nv_out, *nv_sm)
```

```python
import functools
import math

import jax
import jax.numpy as jnp
from jax import lax
from jax.experimental import pallas as pl
from jax.experimental.pallas import tpu as pltpu

F32 = jnp.float32
BF16 = jnp.bfloat16
MXU_DTYPE = jnp.bfloat16

N_DEV = 8
D_MODEL = 1024
DEPTH = 4
CHUNK = 64
N_PREV = 8
N_HEADS = 16
HEAD_DIM = 64
E_MIX = 1024
REL_CLIP = 128
N_REL = 2 * REL_CLIP + 1
N_REL_PAD = 384
N_MEM = 256
MEM_HEADS = 4
MEM_HEAD_DIM = 128
E_MEM = 512
E_BRANCH = E_MIX + E_MEM
N_IN = 3 * E_MIX + E_MEM + E_BRANCH
DN_ALPHA = (2.0 * DEPTH) ** 0.25
LN_EPS = 1e-5
NEG = -1e30

ADAM_LR = 0.001
ADAM_B1 = 0.9
ADAM_B2 = 0.999
ADAM_EPS = 1e-08
ADAM_WD = 0.01
ADAM_STEP = 10

LANES = 128
SUBLANES = 8
VMEM_LIMIT = 56 * 1024 * 1024

TQ = 4 * CHUNK
TKEYS = 3 * TQ
ROLL_W = 1024
TS = 256
QM_BLK = 3 * E_MIX // E_MEM
Z_BLK = QM_BLK + 1


def _call(body, **kw):
    return pl.pallas_call(body, **kw)


def _cparams(*sem):
    return pltpu.CompilerParams(dimension_semantics=sem, vmem_limit_bytes=VMEM_LIMIT)


def _dot(a, b):
    return jnp.dot(a, b, preferred_element_type=F32)


def _dot_nt(a, b):
    return lax.dot_general(a, b, (((1,), (1,)), ((), ())), preferred_element_type=F32)


def _dot_tn(a, b):
    return lax.dot_general(a, b, (((0,), (0,)), ((), ())), preferred_element_type=F32)


def _inproj(x, w):
    s, d = x.shape
    n = w.shape[1]
    tm = min(1024, s)
    tn = 1024

    def body(x_ref, w_ref, o_ref, xb_ref):
        xb = x_ref[...].astype(MXU_DTYPE)

        @pl.when(pl.program_id(1) == 0)
        def _():
            xb_ref[...] = xb.astype(xb_ref.dtype)

        o_ref[...] = _dot(xb, w_ref[...]).astype(o_ref.dtype)

    return _call(
        body, name="inproj", grid=(s // tm, n // tn),
        in_specs=[pl.BlockSpec((tm, d), lambda i, j: (i, 0)),
                  pl.BlockSpec((d, tn), lambda i, j: (0, j))],
        out_specs=[pl.BlockSpec((tm, tn), lambda i, j: (i, j)),
                   pl.BlockSpec((tm, d), lambda i, j: (i, 0))],
        out_shape=[jax.ShapeDtypeStruct((s, n), BF16), jax.ShapeDtypeStruct((s, d), BF16)],
        compiler_params=_cparams("parallel", "arbitrary"),
    )(x, w)


def _small_matmul(a, b, trans_a, out_dtype, name):
    m = a.shape[1] if trans_a else a.shape[0]
    n = b.shape[1]

    def body(a_ref, b_ref, o_ref):
        av = a_ref[...].astype(MXU_DTYPE)
        bv = b_ref[...].astype(MXU_DTYPE)
        r = _dot_tn(av, bv) if trans_a else _dot(av, bv)
        o_ref[...] = r.astype(out_dtype)

    return _call(
        body, name=name,
        in_specs=[pl.BlockSpec(memory_space=pltpu.VMEM)] * 2,
        out_specs=pl.BlockSpec(memory_space=pltpu.VMEM),
        out_shape=jax.ShapeDtypeStruct((m, n), out_dtype),
        compiler_params=pltpu.CompilerParams(vmem_limit_bytes=VMEM_LIMIT),
    )(a, b)


def _piece_blocks(pieces, blk):
    offs, nbs, o = [], [], 0
    for p in pieces:
        nb = p.shape[1] // blk
        offs.append(o)
        nbs.append(nb)
        o += nb
    return offs, nbs, o


def _dx_matmul(pieces, w, addend, token=None):
    s = pieces[0].shape[0]
    d = w.shape[0]
    tm = min(1024, s)
    tk = 1024
    offs, nbs, nk = _piece_blocks(pieces, tk)
    np_ = len(pieces)
    extra = [] if token is None else [token]

    def body(*refs):
        a_refs = refs[:np_]
        w_ref, add_ref = refs[np_:np_ + 2]
        o_ref = refs[-1]
        k = pl.program_id(1)

        @pl.when(k == 0)
        def _():
            o_ref[...] = add_ref[...]

        for p in range(np_):
            @pl.when((k >= offs[p]) & (k < offs[p] + nbs[p]))
            def _(p=p):
                o_ref[...] += _dot_nt(a_refs[p][...], w_ref[...])

    def amap(p):
        return lambda i, k: (i, jnp.clip(k - offs[p], 0, nbs[p] - 1))

    in_specs = [pl.BlockSpec((tm, tk), amap(p)) for p in range(np_)]
    in_specs += [pl.BlockSpec((d, tk), lambda i, k: (0, k)),
                 pl.BlockSpec((tm, d), lambda i, k: (i, 0))]
    in_specs += [pl.BlockSpec((SUBLANES, LANES), lambda i, k: (0, 0)) for _ in extra]
    return _call(
        body, name="dx_matmul", grid=(s // tm, nk),
        in_specs=in_specs,
        out_specs=pl.BlockSpec((tm, d), lambda i, k: (i, 0)),
        out_shape=jax.ShapeDtypeStruct((s, d), F32),
        compiler_params=_cparams("parallel", "arbitrary"),
    )(*pieces, w, addend, *extra)


def _dw_matmul(x, pieces):
    s, d = x.shape
    tn = 1024
    tk = min(1024, s)
    offs, nbs, nj = _piece_blocks(pieces, tn)
    np_ = len(pieces)
    nk = s // tk

    def body(*refs):
        x_ref = refs[0]
        b_refs = refs[1:1 + np_]
        o_ref, acc = refs[1 + np_:]
        j = pl.program_id(0)
        k = pl.program_id(1)

        @pl.when(k == 0)
        def _():
            acc[...] = jnp.zeros_like(acc)

        for p in range(np_):
            @pl.when((j >= offs[p]) & (j < offs[p] + nbs[p]))
            def _(p=p):
                acc[...] += _dot_tn(x_ref[...], b_refs[p][...])

        @pl.when(k == nk - 1)
        def _():
            o_ref[...] = acc[...].astype(o_ref.dtype)

    def bmap(p):
        def f(j, k):
            inside = (j >= offs[p]) & (j < offs[p] + nbs[p])
            return (jnp.where(inside, k, 0), jnp.clip(j - offs[p], 0, nbs[p] - 1))
        return f

    in_specs = [pl.BlockSpec((tk, d), lambda j, k: (k, 0))]
    in_specs += [pl.BlockSpec((tk, tn), bmap(p)) for p in range(np_)]
    return _call(
        body, name="dw_matmul", grid=(nj, nk),
        in_specs=in_specs,
        out_specs=pl.BlockSpec((d, tn), lambda j, k: (0, j)),
        out_shape=jax.ShapeDtypeStruct((d, nj * tn), BF16),
        scratch_shapes=[pltpu.VMEM((d, tn), F32)],
        compiler_params=_cparams("parallel", "arbitrary"),
    )(x, *pieces)


def _rel_onehot():
    j = lax.broadcasted_iota(jnp.int32, (N_REL_PAD, ROLL_W), 1)
    kk = lax.broadcasted_iota(jnp.int32, (N_REL_PAD, ROLL_W), 0)
    dd = jnp.where(j < TKEYS, j, j - ROLL_W)
    idx = jnp.clip(N_PREV * CHUNK - dd, -REL_CLIP, REL_CLIP) + REL_CLIP
    return jnp.where(idx == kk, 1.0, 0.0).astype(F32)


def _band_mask():
    r = lax.broadcasted_iota(jnp.int32, (TQ, TKEYS), 0) // CHUNK
    m = lax.broadcasted_iota(jnp.int32, (TQ, TKEYS), 1) // CHUNK
    return (m >= r) & (m <= r + N_PREV)


def _tile_bias(table_pad):
    def body(t_ref, o_ref):
        g = jnp.dot(t_ref[...], _rel_onehot(), preferred_element_type=F32,
                    precision=lax.Precision.HIGHEST)
        band = _band_mask()
        for h in range(N_HEADS):
            gh = jnp.broadcast_to(g[h:h + 1, :], (TQ, ROLL_W))
            rolled = pltpu.roll(gh, 0, 1, stride=1, stride_axis=0)
            o_ref[h] = jnp.where(band, rolled[:, :TKEYS], NEG)

    return _call(
        body, name="tile_bias",
        in_specs=[pl.BlockSpec(memory_space=pltpu.VMEM)],
        out_specs=pl.BlockSpec(memory_space=pltpu.VMEM),
        out_shape=jax.ShapeDtypeStruct((N_HEADS, TQ, TKEYS), F32),
        compiler_params=pltpu.CompilerParams(vmem_limit_bytes=VMEM_LIMIT),
    )(table_pad)


def _tile_bias_bwd(dtb):
    def body(d_ref, o_ref, g_ref):
        zpad = jnp.zeros((TQ, ROLL_W - TKEYS), F32)
        rr = lax.broadcasted_iota(jnp.int32, (TQ, TQ), 0)
        cc = lax.broadcasted_iota(jnp.int32, (TQ, TQ), 1)
        flip = jnp.where(rr + cc == TQ - 1, 1.0, 0.0).astype(F32)
        for h in range(N_HEADS):
            xh = jnp.concatenate([d_ref[h], zpad], axis=1)
            xf = jnp.dot(flip, xh, preferred_element_type=F32, precision=lax.Precision.HIGHEST)
            rolled = pltpu.roll(xf, 0, 1, stride=1, stride_axis=0)
            g_ref[h:h + 1, :] = jnp.sum(rolled, axis=0, keepdims=True)
        g = pltpu.roll(g_ref[...], ROLL_W - (TQ - 1), 1)
        o_ref[...] = lax.dot_general(g, _rel_onehot(), (((1,), (1,)), ((), ())),
                                     preferred_element_type=F32, precision=lax.Precision.HIGHEST)

    return _call(
        body, name="tile_bias_bwd",
        in_specs=[pl.BlockSpec(memory_space=pltpu.VMEM)],
        out_specs=pl.BlockSpec(memory_space=pltpu.VMEM),
        out_shape=jax.ShapeDtypeStruct((N_HEADS, N_REL_PAD), F32),
        scratch_shapes=[pltpu.VMEM((N_HEADS, ROLL_W), F32)],
        compiler_params=pltpu.CompilerParams(vmem_limit_bytes=VMEM_LIMIT),
    )(dtb)


HB = 4
HBW = HB * HEAD_DIM
ATTN_SCALE = 0.125
assert ATTN_SCALE == 1.0 / math.sqrt(HEAD_DIM)


def _head_masks():
    lane = lax.broadcasted_iota(jnp.int32, (1, HBW), 1) // HEAD_DIM
    return [lane == hh for hh in range(HB)]


def _select_heads(masks, parts):
    out = parts[-1]
    for hh in range(HB - 2, -1, -1):
        out = jnp.where(masks[hh], parts[hh], out)
    return out


def _attn_probs(qm, kcat, tb, valid):
    s = _dot_nt(qm, kcat) + tb
    if valid is not None:
        s = jnp.where(valid, s, NEG)
    m = jnp.max(s, axis=-1, keepdims=True)
    e = jnp.exp(s - m)
    return e * (1.0 / jnp.sum(e, axis=-1, keepdims=True))


def _key_valid(i):
    col = lax.broadcasted_iota(jnp.int32, (TQ, TKEYS), 1)
    return col >= jnp.maximum(2 - i, 0) * TQ


def _kv_specs(col0, nt):
    def spec(back):
        return pl.BlockSpec((TQ, HBW), lambda hp, i: (jnp.clip(i - back, 0, nt - 1), col0 + hp))
    return [spec(2), spec(1), spec(0)]


def _attn_fwd(h, tb):
    s = h.shape[0]
    nt = s // TQ
    nhp = N_HEADS // HB

    def body(q_ref, k0, k1, k2, v0, v1, v2, tb_ref, o_ref):
        i = pl.program_id(1)

        def tile(valid):
            masks = _head_masks()
            qs = q_ref[...].astype(MXU_DTYPE) * ATTN_SCALE
            kcat = jnp.concatenate([k0[...], k1[...], k2[...]], axis=0).astype(MXU_DTYPE)
            vcat = jnp.concatenate([v0[...], v1[...], v2[...]], axis=0).astype(MXU_DTYPE)
            outs = []
            for hh in range(HB):
                qm = jnp.where(masks[hh], qs, jnp.zeros_like(qs))
                p = _attn_probs(qm, kcat, tb_ref[hh], valid)
                outs.append(_dot(p.astype(MXU_DTYPE), vcat))
            o_ref[...] = _select_heads(masks, outs)

        @pl.when(i < 2)
        def _():
            tile(_key_valid(i))

        @pl.when(i >= 2)
        def _():
            tile(None)

    in_specs = [pl.BlockSpec((TQ, HBW), lambda hp, i: (i, hp))]
    in_specs += _kv_specs(nhp, nt) + _kv_specs(2 * nhp, nt)
    in_specs += [pl.BlockSpec((HB, TQ, TKEYS), lambda hp, i: (hp, 0, 0))]
    return _call(
        body, name="attn_fwd", grid=(nhp, nt),
        in_specs=in_specs,
        out_specs=pl.BlockSpec((TQ, HBW), lambda hp, i: (i, hp)),
        out_shape=jax.ShapeDtypeStruct((s, E_MIX), F32),
        compiler_params=_cparams("parallel", "parallel"),
    )(h, h, h, h, h, h, h, tb)


def _attn_bwd(h, tb, d_mix):
    s = h.shape[0]
    nt = s // TQ
    nhp = N_HEADS // HB

    def body(q_ref, k0, k1, k2, v0, v1, v2, tb_ref, do_ref,
             dq_ref, dk_ref, dv_ref, dtb_ref, dk_acc, dv_acc):
        i = pl.program_id(1)

        @pl.when(i == 0)
        def _():
            dk_acc[...] = jnp.zeros_like(dk_acc)
            dv_acc[...] = jnp.zeros_like(dv_acc)
            dtb_ref[...] = jnp.zeros_like(dtb_ref)

        @pl.when((i > 0) & (i < nt))
        def _():
            dk_acc[i % 3] = jnp.zeros((TQ, HBW), F32)
            dv_acc[i % 3] = jnp.zeros((TQ, HBW), F32)

        def tile(valid):
            masks = _head_masks()
            qs = q_ref[...].astype(MXU_DTYPE) * ATTN_SCALE
            do2 = do_ref[...].astype(MXU_DTYPE)
            kcat = jnp.concatenate([k0[...], k1[...], k2[...]], axis=0).astype(MXU_DTYPE)
            vcat = jnp.concatenate([v0[...], v1[...], v2[...]], axis=0).astype(MXU_DTYPE)
            ks = kcat * ATTN_SCALE
            dqs, dks, dvs = [], [], []
            for hh in range(HB):
                qm = jnp.where(masks[hh], qs, jnp.zeros_like(qs))
                dom = jnp.where(masks[hh], do2, jnp.zeros_like(do2))
                p = _attn_probs(qm, kcat, tb_ref[hh], valid)
                dp = _dot_nt(dom, vcat)
                ds = p * (dp - jnp.sum(p * dp, axis=-1, keepdims=True))
                dtb_ref[hh] += ds
                dsb = ds.astype(MXU_DTYPE)
                dqs.append(_dot(dsb, ks))
                dks.append(_dot_tn(dsb, qs))
                dvs.append(_dot_tn(p.astype(MXU_DTYPE), do2))
            dq_ref[...] = _select_heads(masks, dqs).astype(dq_ref.dtype)
            dkc = _select_heads(masks, dks)
            dvc = _select_heads(masks, dvs)
            for jj in range(3):
                slot = (i + 1 + jj) % 3
                dk_acc[slot] += dkc[jj * TQ:(jj + 1) * TQ]
                dv_acc[slot] += dvc[jj * TQ:(jj + 1) * TQ]

        @pl.when(i < 2)
        def _():
            tile(_key_valid(i))

        @pl.when((i >= 2) & (i < nt))
        def _():
            tile(None)

        @pl.when(i >= 2)
        def _():
            slot = (i - 2) % 3
            dk_ref[...] = dk_acc[slot].astype(dk_ref.dtype)
            dv_ref[...] = dv_acc[slot].astype(dv_ref.dtype)

    qmap = lambda hp, i: (jnp.minimum(i, nt - 1), hp)
    kvout = lambda hp, i: (jnp.maximum(i - 2, 0), hp)
    in_specs = [pl.BlockSpec((TQ, HBW), qmap)]
    in_specs += _kv_specs(nhp, nt) + _kv_specs(2 * nhp, nt)
    in_specs += [pl.BlockSpec((HB, TQ, TKEYS), lambda hp, i: (hp, 0, 0)),
                 pl.BlockSpec((TQ, HBW), qmap)]
    blk = (TQ, HBW)
    return _call(
        body, name="attn_bwd", grid=(nhp, nt + 2),
        in_specs=in_specs,
        out_specs=[pl.BlockSpec(blk, qmap), pl.BlockSpec(blk, kvout), pl.BlockSpec(blk, kvout),
                   pl.BlockSpec((HB, TQ, TKEYS), lambda hp, i: (hp, 0, 0))],
        out_shape=[jax.ShapeDtypeStruct((s, E_MIX), BF16)] * 3
        + [jax.ShapeDtypeStruct((N_HEADS, TQ, TKEYS), F32)],
        scratch_shapes=[pltpu.VMEM((3, TQ, HBW), F32)] * 2,
        compiler_params=_cparams("parallel", "arbitrary"),
    )(h, h, h, h, h, h, h, tb, d_mix)


CONV_TS = 256
HALO = 2 * SUBLANES


def _shift_down(prev, cur, k):
    rolled = pltpu.roll(cur, k, 0)
    row = lax.broadcasted_iota(jnp.int32, (HALO, cur.shape[1]), 0)
    top = jnp.where(row < k, pltpu.roll(prev, k, 0), rolled[:HALO])
    return jnp.concatenate([top, rolled[HALO:]], axis=0)


def _shift_up(cur, nxt, k):
    ts = cur.shape[0]
    rolled = pltpu.roll(cur, ts - k, 0)
    row = lax.broadcasted_iota(jnp.int32, (HALO, cur.shape[1]), 0)
    bottom = jnp.where(row >= HALO - k, pltpu.roll(nxt, HALO - k, 0), rolled[ts - HALO:])
    return jnp.concatenate([rolled[:ts - HALO], bottom], axis=0)


def _conv_specs(ts, nb):
    tile = lambda c: pl.BlockSpec((ts, E_MIX), lambda i: (i, c))
    prev = lambda c: pl.BlockSpec((HALO, E_MIX), lambda i: (jnp.maximum(i * (ts // HALO) - 1, 0), c))
    return tile, prev


def _conv_fwd(h, w8):
    s = h.shape[0]
    ts = CONV_TS
    nb = s // ts
    tile, prev = _conv_specs(ts, nb)

    def body(bg, cg, u, cgp, up, w_ref, o_ref):
        i = pl.program_id(0)
        a = cg[...].astype(F32) * u[...].astype(F32)
        ap = jnp.where(i > 0, cgp[...].astype(F32) * up[...].astype(F32), 0.0)
        w = w_ref[...]
        conv = w[0:1] * _shift_down(ap, a, 2) + w[1:2] * _shift_down(ap, a, 1) + w[2:3] * a
        o_ref[...] = bg[...].astype(F32) * conv

    return _call(
        body, name="conv_fwd", grid=(nb,),
        in_specs=[tile(0), tile(1), tile(2), prev(1), prev(2),
                  pl.BlockSpec((SUBLANES, E_MIX), lambda i: (0, 0))],
        out_specs=pl.BlockSpec((ts, E_MIX), lambda i: (i, 0)),
        out_shape=jax.ShapeDtypeStruct((s, E_MIX), F32),
        compiler_params=_cparams("parallel"),
    )(h, h, h, h, h, w8)


def _conv_bwd(h, w8, d_mix):
    s = h.shape[0]
    ts = CONV_TS
    nb = s // ts
    tile, prev = _conv_specs(ts, nb)
    nrow = s // HALO
    nxt = lambda c: pl.BlockSpec((HALO, E_MIX), lambda i: (jnp.minimum((i + 1) * (ts // HALO), nrow - 1), c))

    def body(bg, cg, u, cgp, up, bgn, dmix, dmixn, w_ref, dbg_ref, dcg_ref, du_ref, dw_ref):
        i = pl.program_id(0)

        @pl.when(i == 0)
        def _():
            dw_ref[...] = jnp.zeros_like(dw_ref)

        cgv, uv = cg[...].astype(F32), u[...].astype(F32)
        a = cgv * uv
        ap = jnp.where(i > 0, cgp[...].astype(F32) * up[...].astype(F32), 0.0)
        a1 = _shift_down(ap, a, 1)
        a2 = _shift_down(ap, a, 2)
        w = w_ref[...]
        conv = w[0:1] * a2 + w[1:2] * a1 + w[2:3] * a
        dm = dmix[...]
        dbg_ref[...] = (dm * conv).astype(dbg_ref.dtype)
        dc = dm * bg[...].astype(F32)
        dcn = jnp.where(i < nb - 1, dmixn[...] * bgn[...].astype(F32), 0.0)
        da = w[2:3] * dc + w[1:2] * _shift_up(dc, dcn, 1) + w[0:1] * _shift_up(dc, dcn, 2)
        dcg_ref[...] = (da * uv).astype(dcg_ref.dtype)
        du_ref[...] = (da * cgv).astype(du_ref.dtype)
        dw_ref[0:1, :] += jnp.sum(dc * a2, axis=0, keepdims=True)
        dw_ref[1:2, :] += jnp.sum(dc * a1, axis=0, keepdims=True)
        dw_ref[2:3, :] += jnp.sum(dc * a, axis=0, keepdims=True)

    full = lambda: pl.BlockSpec((ts, E_MIX), lambda i: (i, 0))
    return _call(
        body, name="conv_bwd", grid=(nb,),
        in_specs=[tile(0), tile(1), tile(2), prev(1), prev(2), nxt(0),
                  full(), pl.BlockSpec((HALO, E_MIX), lambda i: (jnp.minimum((i + 1) * (ts // HALO), nrow - 1), 0)),
                  pl.BlockSpec((SUBLANES, E_MIX), lambda i: (0, 0))],
        out_specs=[full(), full(), full(), pl.BlockSpec((SUBLANES, E_MIX), lambda i: (0, 0))],
        out_shape=[jax.ShapeDtypeStruct((s, E_MIX), BF16)] * 3
        + [jax.ShapeDtypeStruct((SUBLANES, E_MIX), F32)],
        compiler_params=_cparams("arbitrary"),
    )(h, h, h, h, h, h, d_mix, d_mix, w8)


def _mem_probs(qh, kh):
    s = _dot_nt(qh, kh) / math.sqrt(MEM_HEAD_DIM)
    m = jnp.max(s, axis=-1, keepdims=True)
    e = jnp.exp(s - m)
    return e / jnp.sum(e, axis=-1, keepdims=True)


def _sigmoid(z):
    return 1.0 / (1.0 + jnp.exp(-z))


def _layer_out_fwd(x, h, mix, kv, w_out, g, b):
    s, d = x.shape
    ts = 2 * TS

    def body(x_ref, mix_ref, qm_ref, z0, z1, z2, kv_ref, wo_ref, g_ref, b_ref,
             xn_ref, r_ref, mem_ref):
        qm = qm_ref[...].astype(MXU_DTYPE)
        kvb = kv_ref[...].astype(MXU_DTYPE)
        mems = []
        for hh in range(MEM_HEADS):
            lo = hh * MEM_HEAD_DIM
            p = _mem_probs(qm[:, lo:lo + MEM_HEAD_DIM], kvb[:, lo:lo + MEM_HEAD_DIM])
            mems.append(_dot(p.astype(MXU_DTYPE), kvb[:, E_MEM + lo:E_MEM + lo + MEM_HEAD_DIM]))
        mem = jnp.concatenate(mems, axis=1)
        mem_ref[...] = mem
        mixv = mix_ref[...]
        half = E_MIX // 2
        parts = [mixv[:, :half], mixv[:, half:], mem]
        out = jnp.zeros((ts, d), F32)
        for c, zr in enumerate((z0, z1, z2)):
            zv = zr[...].astype(F32)
            y = (parts[c] * (zv * _sigmoid(zv))).astype(MXU_DTYPE)
            out += _dot(y, wo_ref[c * half:(c + 1) * half, :])
        r = DN_ALPHA * x_ref[...] + out
        r_ref[...] = r
        mu = jnp.mean(r, axis=-1, keepdims=True)
        rc = r - mu
        var = jnp.mean(rc * rc, axis=-1, keepdims=True)
        xn_ref[...] = rc * lax.rsqrt(var + LN_EPS) * g_ref[...] + b_ref[...]

    row = lambda w, c: pl.BlockSpec((ts, w), lambda i: (i, c))
    const = lambda shp: pl.BlockSpec(shp, lambda i: (0, 0))
    return _call(
        body, name="layer_out_fwd", grid=(s // ts,),
        in_specs=[row(d, 0), row(E_MIX, 0), row(E_MEM, QM_BLK),
                  row(E_MEM, Z_BLK), row(E_MEM, Z_BLK + 1), row(E_MEM, Z_BLK + 2),
                  const((N_MEM, 2 * E_MEM)), const((E_BRANCH, d)), const((1, d)), const((1, d))],
        out_specs=[row(d, 0), row(d, 0), row(E_MEM, 0)],
        out_shape=[jax.ShapeDtypeStruct((s, d), F32), jax.ShapeDtypeStruct((s, d), F32),
                   jax.ShapeDtypeStruct((s, E_MEM), F32)],
        compiler_params=_cparams("parallel"),
    )(x, mix, h, h, h, h, kv, w_out, g, b)


def _layer_out_bwd(dxn, r, g, h, mix, mem, kv, w_out):
    s, d = r.shape
    ts = TS
    nb = s // ts
    half = E_MIX // 2
    inv = 1.0 / math.sqrt(MEM_HEAD_DIM)

    def body(dxn_ref, r_ref, g_ref, mix_ref, mem_ref, qm_ref, z0, z1, z2, kv_ref, wo_ref,
             dxr_ref, dmix_ref, dqz_ref, dwo_ref, dkv_ref, dg_ref, db_ref, dw_acc):
        i = pl.program_id(0)

        @pl.when(i == 0)
        def _():
            dw_acc[...] = jnp.zeros_like(dw_acc)
            dkv_ref[...] = jnp.zeros_like(dkv_ref)
            dg_ref[...] = jnp.zeros_like(dg_ref)
            db_ref[...] = jnp.zeros_like(db_ref)

        dxn_v = dxn_ref[...]
        rv = r_ref[...]
        mu = jnp.mean(rv, axis=-1, keepdims=True)
        rc = rv - mu
        var = jnp.mean(rc * rc, axis=-1, keepdims=True)
        rstd = lax.rsqrt(var + LN_EPS)
        xhat = rc * rstd
        dg_ref[...] += jnp.sum(dxn_v * xhat, axis=0, keepdims=True)
        db_ref[...] += jnp.sum(dxn_v, axis=0, keepdims=True)
        dxh = dxn_v * g_ref[...]
        m1 = jnp.mean(dxh, axis=-1, keepdims=True)
        m2 = jnp.mean(dxh * xhat, axis=-1, keepdims=True)
        dr = rstd * (dxh - m1 - xhat * m2)
        dxr_ref[...] = DN_ALPHA * dr
        dout = dr.astype(MXU_DTYPE)
        mixv = mix_ref[...]
        parts = [mixv[:, :half], mixv[:, half:], mem_ref[...]]
        dcs = []
        for c, zr in enumerate((z0, z1, z2)):
            lo = c * half
            zv = zr[...].astype(F32)
            sg = _sigmoid(zv)
            sl = zv * sg
            dy = _dot_nt(dout, wo_ref[lo:lo + half, :])
            y = (parts[c] * sl).astype(MXU_DTYPE)
            dw_acc[lo:lo + half, :] += _dot_tn(y, dout)
            dcs.append(dy * sl)
            dqz_ref[:, E_MEM + lo:E_MEM + lo + half] = (
                dy * parts[c] * (sg * (1.0 + zv * (1.0 - sg)))).astype(dqz_ref.dtype)
        dmix_ref[...] = jnp.concatenate(dcs[:2], axis=1)

        qm = qm_ref[...].astype(MXU_DTYPE)
        kvb = kv_ref[...].astype(MXU_DTYPE)
        dmb = dcs[2].astype(MXU_DTYPE)
        for hh in range(MEM_HEADS):
            lo = hh * MEM_HEAD_DIM
            qh = qm[:, lo:lo + MEM_HEAD_DIM]
            kh = kvb[:, lo:lo + MEM_HEAD_DIM]
            vh = kvb[:, E_MEM + lo:E_MEM + lo + MEM_HEAD_DIM]
            dmh = dmb[:, lo:lo + MEM_HEAD_DIM]
            p = _mem_probs(qh, kh)
            dp = _dot_nt(dmh, vh)
            ds = p * (dp - jnp.sum(p * dp, axis=-1, keepdims=True))
            dsb = (ds * inv).astype(MXU_DTYPE)
            dqz_ref[:, lo:lo + MEM_HEAD_DIM] = _dot(dsb, kh).astype(dqz_ref.dtype)
            dkv_ref[:, lo:lo + MEM_HEAD_DIM] += _dot_tn(dsb, qh)
            dkv_ref[:, E_MEM + lo:E_MEM + lo + MEM_HEAD_DIM] += _dot_tn(p.astype(MXU_DTYPE), dmh)

        @pl.when(i == nb - 1)
        def _():
            dwo_ref[...] = dw_acc[...].astype(dwo_ref.dtype)

    row = lambda w, c: pl.BlockSpec((ts, w), lambda i: (i, c))
    const = lambda shp: pl.BlockSpec(shp, lambda i: (0, 0))
    return _call(
        body, name="layer_out_bwd", grid=(nb,),
        in_specs=[row(d, 0), row(d, 0), const((1, d)), row(E_MIX, 0), row(E_MEM, 0),
                  row(E_MEM, QM_BLK), row(E_MEM, Z_BLK), row(E_MEM, Z_BLK + 1), row(E_MEM, Z_BLK + 2),
                  const((N_MEM, 2 * E_MEM)), const((E_BRANCH, d))],
        out_specs=[row(d, 0), row(E_MIX, 0), row(E_MEM + E_BRANCH, 0),
                   const((E_BRANCH, d)), const((N_MEM, 2 * E_MEM)), const((1, d)), const((1, d))],
        out_shape=[jax.ShapeDtypeStruct((s, d), F32), jax.ShapeDtypeStruct((s, E_MIX), F32),
                   jax.ShapeDtypeStruct((s, E_MEM + E_BRANCH), BF16),
                   jax.ShapeDtypeStruct((E_BRANCH, d), BF16),
                   jax.ShapeDtypeStruct((N_MEM, 2 * E_MEM), F32),
                   jax.ShapeDtypeStruct((1, d), F32), jax.ShapeDtypeStruct((1, d), F32)],
        scratch_shapes=[pltpu.VMEM((E_BRANCH, d), F32)],
        compiler_params=_cparams("arbitrary"),
    )(dxn, r, g, mix, mem, h, h, h, h, kv, w_out)


def _loss_head(y, target):
    s, d = y.shape
    ts = 512

    def body(y_ref, t_ref, l_ref, dy_ref):
        @pl.when(pl.program_id(0) == 0)
        def _():
            l_ref[...] = jnp.zeros_like(l_ref)

        e = y_ref[...] - t_ref[...]
        dy_ref[...] = e * (1.0 / d)
        l_ref[...] += (0.5 / d) * jnp.sum(jnp.sum(e * e, axis=1, keepdims=True), axis=0, keepdims=True)

    return _call(
        body, name="loss_head", grid=(s // ts,),
        in_specs=[pl.BlockSpec((ts, d), lambda i: (i, 0))] * 2,
        out_specs=[pl.BlockSpec((1, 1), lambda i: (0, 0)), pl.BlockSpec((ts, d), lambda i: (i, 0))],
        out_shape=[jax.ShapeDtypeStruct((1, 1), F32), jax.ShapeDtypeStruct((s, d), F32)],
        compiler_params=_cparams("arbitrary"),
    )(y, target)


def _local_step(x, mem, get_weights, put_grads, rel_bias, conv_w, ln_g, ln_b, target):
    saved = []
    xl = x
    for layer in range(DEPTH):
        w_in_l, rest = get_weights(layer, xl)
        h, xb = _inproj(xl, w_in_l)
        w_kv_l, w_out_l = rest(h)
        if layer % 2 == 0:
            table = jnp.pad(rel_bias[layer // 2], ((0, 0), (0, N_REL_PAD - N_REL)))
            aux = _tile_bias(table)
            mix = _attn_fwd(h, aux)
        else:
            aux = jnp.pad(conv_w[layer // 2], ((0, SUBLANES - 3), (0, 0)))
            mix = _conv_fwd(h, aux)
        kv = _small_matmul(mem, w_kv_l, False, F32, "kv_mem")
        xn, r, mem_out = _layer_out_fwd(xl, h, mix, kv, w_out_l,
                                        ln_g[layer][None], ln_b[layer][None])
        saved.append((xb, h, aux, mix, kv, r, mem_out, w_in_l, w_out_l))
        xl = xn

    loss, dx = _loss_head(xl, target)

    dgs, dbs, d_rel, d_conv = [], [], [], []
    for layer in reversed(range(DEPTH)):
        xb, h, aux, mix, kv, r, mem_out, w_in_l, w_out_l = saved[layer]
        dx_res, d_mix, dqz, dwo, dkv, dg, db = _layer_out_bwd(
            dx, r, ln_g[layer][None], h, mix, mem_out, kv, w_out_l)
        if layer % 2 == 0:
            dq, dk, dv, dtb = _attn_bwd(h, aux, d_mix)
            d_rel.append(_tile_bias_bwd(dtb)[:, :N_REL])
            pieces = [dq, dk, dv, dqz]
        else:
            dbg, dcg, du, dw8 = _conv_bwd(h, aux, d_mix)
            d_conv.append(dw8[:3])
            pieces = [dbg, dcg, du, dqz]
        token = put_grads(layer, _dw_matmul(xb, pieces), _small_matmul(mem, dkv, True, BF16, "dw_kv"), dwo)
        dgs.append(dg[0])
        dbs.append(db[0])
        dx = _dx_matmul(pieces, w_in_l, dx_res, token)

    rev = lambda lst: jnp.stack(lst[::-1])
    return loss, dx, rev(d_rel), rev(d_conv), rev(dgs), rev(dbs)


def _me():
    return lax.axis_index("x"), lax.axis_index("y"), lax.axis_index("c")


def _peer(k):
    x, y, c = _me()
    kx, ky, kc = (k >> 2) & 1, (k >> 1) & 1, k & 1
    return (1 - x if kx else x, 1 - y if ky else y, 1 - c if kc else c)


def _lin(dev):
    return 4 * dev[0] + 2 * dev[1] + dev[2]


ANY = pl.BlockSpec(memory_space=pl.ANY)


def _exchange(srcs, dst_shapes, src_slice, dst_slice, name):
    na = len(srcs)

    def body(*refs):
        src_refs = refs[:na]
        dst_refs = refs[na:2 * na]
        send_sems, recv_sems, local_sems = refs[2 * na:]
        me = _lin(_me())
        copies = []
        for a in range(na):
            loc = pltpu.make_async_copy(src_slice(a, src_refs[a], me), dst_slice(a, dst_refs[a], me),
                                        local_sems.at[a])
            loc.start()
            copies.append(loc)
            for k in range(1, N_DEV):
                peer = _peer(k)
                cp = pltpu.make_async_remote_copy(
                    src_ref=src_slice(a, src_refs[a], _lin(peer)),
                    dst_ref=dst_slice(a, dst_refs[a], me),
                    send_sem=send_sems.at[a, k - 1], recv_sem=recv_sems.at[a, k - 1],
                    device_id=peer, device_id_type=pl.DeviceIdType.MESH)
                cp.start()
                copies.append(cp)
        for cp in copies:
            cp.wait()

    return _call(
        body, name=name,
        in_specs=[ANY] * na, out_specs=[ANY] * na,
        out_shape=[jax.ShapeDtypeStruct(shp, s.dtype) for shp, s in zip(dst_shapes, srcs)],
        scratch_shapes=[pltpu.SemaphoreType.DMA((na, N_DEV - 1)),
                        pltpu.SemaphoreType.DMA((na, N_DEV - 1)),
                        pltpu.SemaphoreType.DMA((na,))],
    )(*srcs)


def _gather_to_all(src, name):
    return _exchange([src], [(N_DEV,) + src.shape], lambda a, ref, p: ref,
                     lambda a, ref, me: ref.at[me], name)[0]


HBM = pl.BlockSpec(memory_space=pltpu.HBM)
SEM = pl.BlockSpec(memory_space=pltpu.SEMAPHORE)
EFFECT = pltpu.SideEffectType.DATAFLOW_SIDE_EFFECTING
N_PEER = N_DEV - 1
N_KIND = 3


def _peer_copies(kind, src_ref, land_ref, send, recv, src_slice, dst_slice):
    me = _lin(_me())
    copies = []
    for k in range(1, N_DEV):
        peer = _peer(k)
        copies.append(pltpu.make_async_remote_copy(
            src_ref=src_slice(kind, src_ref, _lin(peer)),
            dst_ref=dst_slice(kind, land_ref, me, k),
            send_sem=send.at[k - 1], recv_sem=recv.at[k - 1],
            device_id=peer, device_id_type=pl.DeviceIdType.MESH))
    return copies


def _split_start(srcs, kinds, land_shapes, src_slice, dst_slice, name):
    na = len(srcs)

    def body(*refs):
        src_refs, land_refs = refs[:na], refs[na:2 * na]
        sems = refs[2 * na:4 * na]
        token = refs[-1]
        for a in range(na):
            for cp in _peer_copies(kinds[a], src_refs[a], land_refs[a], sems[2 * a], sems[2 * a + 1],
                                   src_slice, dst_slice):
                cp.start()
        token[...] = jnp.zeros_like(token)

    sem_shape = pltpu.SemaphoreType.DMA((N_PEER,))
    lands = [lax.empty(shp, s.dtype) for shp, s in zip(land_shapes, srcs)]
    outs = _call(
        body, name=name,
        in_specs=[HBM] * (2 * na),
        out_specs=[SEM] * (2 * na) + [HBM] * (2 * na) + [pl.BlockSpec(memory_space=pltpu.VMEM)],
        out_shape=[sem_shape] * (2 * na)
        + [pltpu.HBM(s.shape, s.dtype) for s in srcs]
        + [pltpu.HBM(shp, s.dtype) for shp, s in zip(land_shapes, srcs)]
        + [jax.ShapeDtypeStruct((SUBLANES, LANES), F32)],
        input_output_aliases={i: 2 * na + i for i in range(2 * na)},
        compiler_params=pltpu.CompilerParams(has_side_effects=EFFECT),
    )(*[pltpu.with_memory_space_constraint(a, pltpu.HBM) for a in list(srcs) + lands])
    sems = [(outs[2 * a], outs[2 * a + 1]) for a in range(na)]
    thrus = outs[2 * na:3 * na]
    lands = outs[3 * na:4 * na]
    return sems, thrus, lands, outs[-1]


def _split_wait(sems, thrus, lands, kinds, src_slice, dst_slice, after, name):
    na = len(thrus)

    def body(*refs):
        src_refs, land_refs = refs[:na], refs[na:2 * na]
        sem_refs = refs[2 * na:4 * na]
        for a in range(na):
            for cp in _peer_copies(kinds[a], src_refs[a], land_refs[a], sem_refs[2 * a], sem_refs[2 * a + 1],
                                   src_slice, dst_slice):
                cp.wait_send()
                cp.wait_recv()

    outs = _call(
        body, name=name,
        in_specs=[HBM] * (2 * na) + [SEM] * (2 * na) + [ANY],
        out_specs=[HBM] * (2 * na),
        out_shape=[pltpu.HBM(a.shape, a.dtype) for a in list(thrus) + list(lands)],
        input_output_aliases={i: i for i in range(2 * na)},
        compiler_params=pltpu.CompilerParams(has_side_effects=EFFECT),
    )(*thrus, *lands, *[s for pair in sems for s in pair], after)
    return outs[na:]


def _shard_dims(c_in, r_kv, r_out):
    def sl(j, ref, p):
        if j == 0:
            return ref.at[:, pl.ds(pl.multiple_of(p * c_in, LANES), c_in)]
        r = r_kv if j == 1 else r_out
        return ref.at[pl.ds(pl.multiple_of(p * r, 2 * SUBLANES), r), :]
    return sl


def _adamw_math(w, g, m, v):
    m = ADAM_B1 * m + (1.0 - ADAM_B1) * g
    v = ADAM_B2 * v + (1.0 - ADAM_B2) * (g * g)
    m_hat = m / (1.0 - ADAM_B1 ** ADAM_STEP)
    v_hat = v / (1.0 - ADAM_B2 ** ADAM_STEP)
    delta = -ADAM_LR * (m_hat / (jnp.sqrt(v_hat) + ADAM_EPS) + ADAM_WD * w)
    return delta, m, v


def _reduce_adamw(parts, w, m, v, name):
    rows, cols = w.shape
    tr = rows
    for cand in (512, 256, 128, 64, 32, 16):
        if rows % cand == 0 and rows > cand:
            tr = cand
            break

    def body(p_ref, w_ref, m_ref, v_ref, g_out, d_out, m_out, v_out):
        g = p_ref[0].astype(F32)
        for s in range(1, N_DEV):
            g = g + p_ref[s].astype(F32)
        g_out[...] = g
        d_out[...], m_out[...], v_out[...] = _adamw_math(w_ref[...], g, m_ref[...], v_ref[...])

    blk = pl.BlockSpec((tr, cols), lambda i: (i, 0))
    return _call(
        body, name=name, grid=(rows // tr,),
        in_specs=[pl.BlockSpec((N_DEV, tr, cols), lambda i: (0, i, 0)), blk, blk, blk],
        out_specs=[blk] * 4,
        out_shape=[jax.ShapeDtypeStruct((rows, cols), F32)] * 4,
        compiler_params=_cparams("parallel"),
    )(parts, w, m, v)


def _reduce_adamw_layers(lands, owns, w, m, v, name):
    nl, rows, cols = w.shape
    tr = rows
    for cand in (256, 192, 128):
        if rows % cand == 0:
            tr = cand
            break

    def body(*refs):
        land_refs, own_refs = refs[:nl], refs[nl:2 * nl]
        w_ref, m_ref, v_ref, g_out, d_out, m_out, v_out = refs[2 * nl:]
        layer = pl.program_id(0)
        for a in range(nl):
            @pl.when(layer == a)
            def _(a=a):
                g = own_refs[a][...].astype(F32)
                for k in range(N_PEER):
                    g = g + land_refs[a][k].astype(F32)
                g_out[...] = g
                d_out[...], m_out[...], v_out[...] = _adamw_math(w_ref[...], g, m_ref[...], v_ref[...])

    def lmap(a):
        return lambda l, i: (0, jnp.where(l == a, i, 0), 0)

    def omap(a):
        return lambda l, i: (jnp.where(l == a, i, 0), 0)

    blk = pl.BlockSpec((None, tr, cols), lambda l, i: (l, i, 0))
    return _call(
        body, name=name, grid=(nl, rows // tr),
        in_specs=[pl.BlockSpec((N_PEER, tr, cols), lmap(a)) for a in range(nl)]
        + [pl.BlockSpec((tr, cols), omap(a)) for a in range(nl)] + [blk, blk, blk],
        out_specs=[blk] * 4,
        out_shape=[jax.ShapeDtypeStruct((nl, rows, cols), F32)] * 4,
        compiler_params=_cparams("arbitrary", "arbitrary"),
    )(*lands, *owns, w, m, v)


SM_G, SM_B, SM_CONV, SM_REL = 0, 4, 8, 16
SM_ROWS = SM_REL + 2 * N_HEADS
REL_W = 384


def _pack_small(d_rel, d_conv, dg, db):
    buf = jnp.zeros((SM_ROWS, D_MODEL), F32)
    buf = buf.at[SM_G:SM_G + DEPTH].set(dg)
    buf = buf.at[SM_B:SM_B + DEPTH].set(db)
    buf = buf.at[SM_CONV:SM_CONV + 6].set(d_conv.reshape(6, E_MIX))
    buf = buf.at[SM_REL:, :N_REL].set(d_rel.reshape(2 * N_HEADS, N_REL))
    return buf


def kernel(x, mem, w_in, w_mem_kv, w_out, rel_bias, conv_w, ln_g, ln_b, loss_target, m_w_in, m_w_mem_kv, m_w_out, m_rel_bias, m_conv_w, m_ln_g, m_ln_b, v_w_in, v_w_mem_kv, v_w_out, v_rel_bias, v_conv_w, v_ln_g, v_ln_b):
    me = _lin(_me())
    c_in, r_kv, r_out, c_conv = w_in.shape[2], w_mem_kv.shape[1], w_out.shape[1], conv_w.shape[2]

    shard = _shard_dims(c_in, r_kv, r_out)
    own_start = lambda j: (0, me * c_in) if j == 0 else (me * (r_kv if j == 1 else r_out), 0)

    w_sh = [w_in.astype(BF16), w_mem_kv.astype(BF16), w_out.astype(BF16)]
    full_shapes = [(D_MODEL, N_DEV * c_in), (N_DEV * r_kv, w_mem_kv.shape[2]), (N_DEV * r_out, D_MODEL)]
    ag_src = lambda j, ref, p: ref
    ag_dst = lambda j, ref, me_, k: shard(j, ref, me_)
    kinds = list(range(N_KIND))
    ag_sems, ag_thrus, ag_lands, _ = _split_start(
        [w_sh[j][layer] for layer in range(DEPTH) for j in kinds], kinds * DEPTH,
        full_shapes * DEPTH, ag_src, ag_dst, "ag_start")

    def get_weights(layer, x_layer):
        lo = layer * N_KIND

        def wait(js, after, name):
            idx = [lo + j for j in js]
            lands = _split_wait([ag_sems[a] for a in idx], [ag_thrus[a] for a in idx],
                                [ag_lands[a] for a in idx], js, ag_src, ag_dst, after, name)
            return [lax.dynamic_update_slice(land, w_sh[j][layer], own_start(j)) for j, land in zip(js, lands)]

        w_in_l, = wait([0], x_layer, "ag_wait_in_%d" % layer)
        return w_in_l, lambda h: wait([1, 2], h, "ag_wait_kv_out_%d" % layer)

    rs_src = shard
    rs_dst = lambda j, ref, me_, k: ref.at[k - 1]
    rs_shapes = [(N_PEER, D_MODEL, c_in), (N_PEER, r_kv, w_mem_kv.shape[2]), (N_PEER, r_out, D_MODEL)]
    own_sizes = [(D_MODEL, c_in), (r_kv, w_mem_kv.shape[2]), (r_out, D_MODEL)]
    pending = {}

    def put_grads(layer, dwi, dwkv, dwo):
        parts = [dwi, dwkv, dwo]
        owns = [lax.dynamic_slice(parts[j], own_start(j), own_sizes[j]) for j in range(N_KIND)]
        sems, thrus, lands, token = _split_start(parts, kinds, rs_shapes, rs_src, rs_dst, "rs_start_%d" % layer)
        pending[layer] = (sems, thrus, lands, owns)
        return token

    conv_tile = jnp.pad(conv_w.reshape(6, c_conv), ((0, SUBLANES - 6), (0, 0)))
    conv_land = _gather_to_all(conv_tile, "gather_conv")
    conv_f = jnp.transpose(conv_land[:, :6], (1, 0, 2)).reshape(2, 3, N_DEV * c_conv)

    loss, grad_x, d_rel, d_conv, dg, db = _local_step(
        x[0], mem[0], get_weights, put_grads, rel_bias, conv_f, ln_g, ln_b, loss_target[0])

    p_small = _gather_to_all(_pack_small(d_rel, d_conv, dg, db), "gather_small_grads")

    rs_lands, rs_owns = [], []
    for layer in range(DEPTH):
        sems, thrus, lands, owns = pending[layer]
        rs_lands.append(_split_wait(sems, thrus, lands, kinds, rs_src, rs_dst, grad_x, "rs_wait_%d" % layer))
        rs_owns.append(owns)

    def big(j, w, m, v, name):
        return _reduce_adamw_layers([rs_lands[layer][j] for layer in range(DEPTH)],
                                    [rs_owns[layer][j] for layer in range(DEPTH)], w, m, v, name)

    g_in, d_in, nm_in, nv_in = big(0, w_in, m_w_in, v_w_in, "adamw_w_in")
    g_kv, d_kv, nm_kv, nv_kv = big(1, w_mem_kv, m_w_mem_kv, v_w_mem_kv, "adamw_w_kv")
    g_out, d_out, nm_out, nv_out = big(2, w_out, m_w_out, v_w_out, "adamw_w_out")

    def pack_state(rel, conv, g, b):
        conv_full = jnp.zeros((2, 3, E_MIX), F32)
        conv_full = lax.dynamic_update_slice(conv_full, conv, (0, 0, me * c_conv))
        return _pack_small(rel, conv_full, g, b)

    sm_w = pack_state(rel_bias, conv_w, ln_g, ln_b)
    sm_m = pack_state(m_rel_bias, m_conv_w, m_ln_g, m_ln_b)
    sm_v = pack_state(v_rel_bias, v_conv_w, v_ln_g, v_ln_b)
    sm_outs = _reduce_adamw(p_small, sm_w, sm_m, sm_v, "adamw_small")

    def unpack(buf):
        rel = buf[SM_REL:, :N_REL].reshape(2, N_HEADS, N_REL)
        conv = lax.dynamic_slice(buf[SM_CONV:SM_CONV + 6].reshape(2, 3, E_MIX), (0, 0, me * c_conv), (2, 3, c_conv))
        return rel, conv, buf[SM_G:SM_G + DEPTH], buf[SM_B:SM_B + DEPTH]

    g_sm, d_sm, nm_sm, nv_sm = [unpack(b) for b in sm_outs]

    loss = lax.psum(loss[0, 0], ("x", "y", "c"))
    return (loss, grad_x[None],
            g_in, g_kv, g_out, *g_sm,
            d_in, d_kv, d_out, *d_sm,
            nm_in, nm_kv, nm_out, *nm_sm,
            nv_in, nv_kv, nv_out, *nv_sm)
```

```python
import functools
import math

import jax
import jax.numpy as jnp
from jax import lax
from jax.experimental import pallas as pl
from jax.experimental.pallas import tpu as pltpu

F32 = jnp.float32
BF16 = jnp.bfloat16
MXU_DTYPE = jnp.bfloat16

N_DEV = 8
D_MODEL = 1024
DEPTH = 4
CHUNK = 64
N_PREV = 8
N_HEADS = 16
HEAD_DIM = 64
E_MIX = 1024
REL_CLIP = 128
N_REL = 2 * REL_CLIP + 1
N_REL_PAD = 384
N_MEM = 256
MEM_HEADS = 4
MEM_HEAD_DIM = 128
E_MEM = 512
E_BRANCH = E_MIX + E_MEM
N_IN = 3 * E_MIX + E_MEM + E_BRANCH
DN_ALPHA = (2.0 * DEPTH) ** 0.25
LN_EPS = 1e-5
NEG = -1e30

ADAM_LR = 0.001
ADAM_B1 = 0.9
ADAM_B2 = 0.999
ADAM_EPS = 1e-08
ADAM_WD = 0.01
ADAM_STEP = 10

LANES = 128
SUBLANES = 8
VMEM_LIMIT = 56 * 1024 * 1024

TQ = 4 * CHUNK
TKEYS = 3 * TQ
ROLL_W = 1024
TS = 256
QM_BLK = 3 * E_MIX // E_MEM
Z_BLK = QM_BLK + 1


def _call(body, **kw):
    return pl.pallas_call(body, **kw)


def _cparams(*sem):
    return pltpu.CompilerParams(dimension_semantics=sem, vmem_limit_bytes=VMEM_LIMIT)


def _dot(a, b):
    return jnp.dot(a, b, preferred_element_type=F32)


def _dot_nt(a, b):
    return lax.dot_general(a, b, (((1,), (1,)), ((), ())), preferred_element_type=F32)


def _dot_tn(a, b):
    return lax.dot_general(a, b, (((0,), (0,)), ((), ())), preferred_element_type=F32)


def _inproj(x, w):
    s, d = x.shape
    n = w.shape[1]
    tm = min(1024, s)
    tn = 1024

    def body(x_ref, w_ref, o_ref, xb_ref):
        xb = x_ref[...].astype(MXU_DTYPE)

        @pl.when(pl.program_id(1) == 0)
        def _():
            xb_ref[...] = xb.astype(xb_ref.dtype)

        o_ref[...] = _dot(xb, w_ref[...]).astype(o_ref.dtype)

    return _call(
        body, name="inproj", grid=(s // tm, n // tn),
        in_specs=[pl.BlockSpec((tm, d), lambda i, j: (i, 0)),
                  pl.BlockSpec((d, tn), lambda i, j: (0, j))],
        out_specs=[pl.BlockSpec((tm, tn), lambda i, j: (i, j)),
                   pl.BlockSpec((tm, d), lambda i, j: (i, 0))],
        out_shape=[jax.ShapeDtypeStruct((s, n), BF16), jax.ShapeDtypeStruct((s, d), BF16)],
        compiler_params=_cparams("parallel", "arbitrary"),
    )(x, w)


def _small_matmul(a, b, trans_a, out_dtype, name):
    m = a.shape[1] if trans_a else a.shape[0]
    n = b.shape[1]

    def body(a_ref, b_ref, o_ref):
        av = a_ref[...].astype(MXU_DTYPE)
        bv = b_ref[...].astype(MXU_DTYPE)
        r = _dot_tn(av, bv) if trans_a else _dot(av, bv)
        o_ref[...] = r.astype(out_dtype)

    return _call(
        body, name=name,
        in_specs=[pl.BlockSpec(memory_space=pltpu.VMEM)] * 2,
        out_specs=pl.BlockSpec(memory_space=pltpu.VMEM),
        out_shape=jax.ShapeDtypeStruct((m, n), out_dtype),
        compiler_params=pltpu.CompilerParams(vmem_limit_bytes=VMEM_LIMIT),
    )(a, b)


def _piece_blocks(pieces, blk):
    offs, nbs, o = [], [], 0
    for p in pieces:
        nb = p.shape[1] // blk
        offs.append(o)
        nbs.append(nb)
        o += nb
    return offs, nbs, o


def _dx_matmul(pieces, wt, addend, token=None):
    s = pieces[0].shape[0]
    d = wt.shape[1]
    tm = min(1024, s)
    tk = 1024
    offs, nbs, nk = _piece_blocks(pieces, tk)
    np_ = len(pieces)
    extra = [] if token is None else [token]

    def body(*refs):
        a_refs = refs[:np_]
        w_ref, add_ref = refs[np_:np_ + 2]
        o_ref = refs[-1]
        k = pl.program_id(1)

        @pl.when(k == 0)
        def _():
            o_ref[...] = add_ref[...]

        for p in range(np_):
            @pl.when((k >= offs[p]) & (k < offs[p] + nbs[p]))
            def _(p=p):
                o_ref[...] += _dot(a_refs[p][...], w_ref[...])

    def amap(p):
        return lambda i, k: (i, jnp.clip(k - offs[p], 0, nbs[p] - 1))

    in_specs = [pl.BlockSpec((tm, tk), amap(p)) for p in range(np_)]
    in_specs += [pl.BlockSpec((tk, d), lambda i, k: (k, 0)),
                 pl.BlockSpec((tm, d), lambda i, k: (i, 0))]
    in_specs += [pl.BlockSpec((SUBLANES, LANES), lambda i, k: (0, 0)) for _ in extra]
    return _call(
        body, name="dx_matmul", grid=(s // tm, nk),
        in_specs=in_specs,
        out_specs=pl.BlockSpec((tm, d), lambda i, k: (i, 0)),
        out_shape=jax.ShapeDtypeStruct((s, d), F32),
        compiler_params=_cparams("parallel", "arbitrary"),
    )(*pieces, wt, addend, *extra)


def _dw_matmul(x, pieces):
    s, d = x.shape
    tn = 1024
    tk = min(1024, s)
    offs, nbs, nj = _piece_blocks(pieces, tn)
    np_ = len(pieces)
    nk = s // tk

    def body(*refs):
        x_ref = refs[0]
        b_refs = refs[1:1 + np_]
        o_ref, acc = refs[1 + np_:]
        j = pl.program_id(0)
        k = pl.program_id(1)

        @pl.when(k == 0)
        def _():
            acc[...] = jnp.zeros_like(acc)

        for p in range(np_):
            @pl.when((j >= offs[p]) & (j < offs[p] + nbs[p]))
            def _(p=p):
                acc[...] += _dot_tn(x_ref[...], b_refs[p][...])

        @pl.when(k == nk - 1)
        def _():
            o_ref[...] = acc[...].astype(o_ref.dtype)

    def bmap(p):
        def f(j, k):
            inside = (j >= offs[p]) & (j < offs[p] + nbs[p])
            return (jnp.where(inside, k, 0), jnp.clip(j - offs[p], 0, nbs[p] - 1))
        return f

    in_specs = [pl.BlockSpec((tk, d), lambda j, k: (k, 0))]
    in_specs += [pl.BlockSpec((tk, tn), bmap(p)) for p in range(np_)]
    return _call(
        body, name="dw_matmul", grid=(nj, nk),
        in_specs=in_specs,
        out_specs=pl.BlockSpec((d, tn), lambda j, k: (0, j)),
        out_shape=jax.ShapeDtypeStruct((d, nj * tn), BF16),
        scratch_shapes=[pltpu.VMEM((d, tn), F32)],
        compiler_params=_cparams("parallel", "arbitrary"),
    )(x, *pieces)


def _rel_onehot():
    j = lax.broadcasted_iota(jnp.int32, (N_REL_PAD, ROLL_W), 1)
    kk = lax.broadcasted_iota(jnp.int32, (N_REL_PAD, ROLL_W), 0)
    dd = jnp.where(j < TKEYS, j, j - ROLL_W)
    idx = jnp.clip(N_PREV * CHUNK - dd, -REL_CLIP, REL_CLIP) + REL_CLIP
    return jnp.where(idx == kk, 1.0, 0.0).astype(F32)


def _band_mask():
    r = lax.broadcasted_iota(jnp.int32, (TQ, TKEYS), 0) // CHUNK
    m = lax.broadcasted_iota(jnp.int32, (TQ, TKEYS), 1) // CHUNK
    return (m >= r) & (m <= r + N_PREV)


def _tile_bias(table_pad):
    def body(t_ref, o_ref):
        g = jnp.dot(t_ref[...], _rel_onehot(), preferred_element_type=F32,
                    precision=lax.Precision.HIGHEST)
        band = _band_mask()
        for h in range(N_HEADS):
            gh = jnp.broadcast_to(g[h:h + 1, :], (TQ, ROLL_W))
            rolled = pltpu.roll(gh, 0, 1, stride=1, stride_axis=0)
            o_ref[h] = jnp.where(band, rolled[:, :TKEYS], NEG)

    return _call(
        body, name="tile_bias",
        in_specs=[pl.BlockSpec(memory_space=pltpu.VMEM)],
        out_specs=pl.BlockSpec(memory_space=pltpu.VMEM),
        out_shape=jax.ShapeDtypeStruct((N_HEADS, TQ, TKEYS), F32),
        compiler_params=pltpu.CompilerParams(vmem_limit_bytes=VMEM_LIMIT),
    )(table_pad)


def _tile_bias_bwd(dtb):
    def body(d_ref, o_ref, g_ref):
        zpad = jnp.zeros((TQ, ROLL_W - TKEYS), F32)
        rr = lax.broadcasted_iota(jnp.int32, (TQ, TQ), 0)
        cc = lax.broadcasted_iota(jnp.int32, (TQ, TQ), 1)
        flip = jnp.where(rr + cc == TQ - 1, 1.0, 0.0).astype(F32)
        for h in range(N_HEADS):
            xh = jnp.concatenate([d_ref[h], zpad], axis=1)
            xf = jnp.dot(flip, xh, preferred_element_type=F32, precision=lax.Precision.HIGHEST)
            rolled = pltpu.roll(xf, 0, 1, stride=1, stride_axis=0)
            g_ref[h:h + 1, :] = jnp.sum(rolled, axis=0, keepdims=True)
        g = pltpu.roll(g_ref[...], ROLL_W - (TQ - 1), 1)
        o_ref[...] = lax.dot_general(g, _rel_onehot(), (((1,), (1,)), ((), ())),
                                     preferred_element_type=F32, precision=lax.Precision.HIGHEST)

    return _call(
        body, name="tile_bias_bwd",
        in_specs=[pl.BlockSpec(memory_space=pltpu.VMEM)],
        out_specs=pl.BlockSpec(memory_space=pltpu.VMEM),
        out_shape=jax.ShapeDtypeStruct((N_HEADS, N_REL_PAD), F32),
        scratch_shapes=[pltpu.VMEM((N_HEADS, ROLL_W), F32)],
        compiler_params=pltpu.CompilerParams(vmem_limit_bytes=VMEM_LIMIT),
    )(dtb)


HB = 4
HBW = HB * HEAD_DIM
ATTN_SCALE = 0.125
assert ATTN_SCALE == 1.0 / math.sqrt(HEAD_DIM)


def _head_masks():
    lane = lax.broadcasted_iota(jnp.int32, (1, HBW), 1) // HEAD_DIM
    return [lane == hh for hh in range(HB)]


def _select_heads(masks, parts):
    out = parts[-1]
    for hh in range(HB - 2, -1, -1):
        out = jnp.where(masks[hh], parts[hh], out)
    return out


def _attn_probs(qm, kcat, tb, valid):
    s = _dot_nt(qm, kcat) + tb
    if valid is not None:
        s = jnp.where(valid, s, NEG)
    m = jnp.max(s, axis=-1, keepdims=True)
    e = jnp.exp(s - m)
    return e * (1.0 / jnp.sum(e, axis=-1, keepdims=True))


def _key_valid(i):
    col = lax.broadcasted_iota(jnp.int32, (TQ, TKEYS), 1)
    return col >= jnp.maximum(2 - i, 0) * TQ


def _kv_specs(col0, nt):
    def spec(back):
        return pl.BlockSpec((TQ, HBW), lambda hp, i: (jnp.clip(i - back, 0, nt - 1), col0 + hp))
    return [spec(2), spec(1), spec(0)]


def _attn_fwd(h, tb):
    s = h.shape[0]
    nt = s // TQ
    nhp = N_HEADS // HB

    def body(q_ref, k0, k1, k2, v0, v1, v2, tb_ref, o_ref):
        i = pl.program_id(1)

        def tile(valid):
            masks = _head_masks()
            qs = q_ref[...].astype(MXU_DTYPE) * ATTN_SCALE
            kcat = jnp.concatenate([k0[...], k1[...], k2[...]], axis=0).astype(MXU_DTYPE)
            vcat = jnp.concatenate([v0[...], v1[...], v2[...]], axis=0).astype(MXU_DTYPE)
            outs = []
            for hh in range(HB):
                qm = jnp.where(masks[hh], qs, jnp.zeros_like(qs))
                p = _attn_probs(qm, kcat, tb_ref[hh], valid)
                outs.append(_dot(p.astype(MXU_DTYPE), vcat))
            o_ref[...] = _select_heads(masks, outs).astype(o_ref.dtype)

        @pl.when(i < 2)
        def _():
            tile(_key_valid(i))

        @pl.when(i >= 2)
        def _():
            tile(None)

    in_specs = [pl.BlockSpec((TQ, HBW), lambda hp, i: (i, hp))]
    in_specs += _kv_specs(nhp, nt) + _kv_specs(2 * nhp, nt)
    in_specs += [pl.BlockSpec((HB, TQ, TKEYS), lambda hp, i: (hp, 0, 0))]
    return _call(
        body, name="attn_fwd", grid=(nhp, nt),
        in_specs=in_specs,
        out_specs=pl.BlockSpec((TQ, HBW), lambda hp, i: (i, hp)),
        out_shape=jax.ShapeDtypeStruct((s, E_MIX), BF16),
        compiler_params=_cparams("parallel", "parallel"),
    )(h, h, h, h, h, h, h, tb)


def _attn_bwd(h, tb, d_mix):
    s = h.shape[0]
    nt = s // TQ
    nhp = N_HEADS // HB

    def body(q_ref, k0, k1, k2, v0, v1, v2, tb_ref, do_ref,
             dq_ref, dk_ref, dv_ref, dtb_ref, dk_acc, dv_acc):
        i = pl.program_id(1)

        @pl.when(i == 0)
        def _():
            dk_acc[...] = jnp.zeros_like(dk_acc)
            dv_acc[...] = jnp.zeros_like(dv_acc)
            dtb_ref[...] = jnp.zeros_like(dtb_ref)

        @pl.when((i > 0) & (i < nt))
        def _():
            dk_acc[i % 3] = jnp.zeros((TQ, HBW), F32)
            dv_acc[i % 3] = jnp.zeros((TQ, HBW), F32)

        def tile(valid):
            masks = _head_masks()
            qs = q_ref[...].astype(MXU_DTYPE) * ATTN_SCALE
            do2 = do_ref[...].astype(MXU_DTYPE)
            kcat = jnp.concatenate([k0[...], k1[...], k2[...]], axis=0).astype(MXU_DTYPE)
            vcat = jnp.concatenate([v0[...], v1[...], v2[...]], axis=0).astype(MXU_DTYPE)
            ks = kcat * ATTN_SCALE
            dqs, dks, dvs = [], [], []
            for hh in range(HB):
                qm = jnp.where(masks[hh], qs, jnp.zeros_like(qs))
                dom = jnp.where(masks[hh], do2, jnp.zeros_like(do2))
                p = _attn_probs(qm, kcat, tb_ref[hh], valid)
                dp = _dot_nt(dom, vcat)
                ds = p * (dp - jnp.sum(p * dp, axis=-1, keepdims=True))
                dtb_ref[hh] += ds
                dsb = ds.astype(MXU_DTYPE)
                dqs.append(_dot(dsb, ks))
                dks.append(_dot_tn(dsb, qs))
                dvs.append(_dot_tn(p.astype(MXU_DTYPE), do2))
            dq_ref[...] = _select_heads(masks, dqs).astype(dq_ref.dtype)
            dkc = _select_heads(masks, dks)
            dvc = _select_heads(masks, dvs)
            for jj in range(3):
                slot = (i + 1 + jj) % 3
                dk_acc[slot] += dkc[jj * TQ:(jj + 1) * TQ]
                dv_acc[slot] += dvc[jj * TQ:(jj + 1) * TQ]

        @pl.when(i < 2)
        def _():
            tile(_key_valid(i))

        @pl.when((i >= 2) & (i < nt))
        def _():
            tile(None)

        @pl.when(i >= 2)
        def _():
            slot = (i - 2) % 3
            dk_ref[...] = dk_acc[slot].astype(dk_ref.dtype)
            dv_ref[...] = dv_acc[slot].astype(dv_ref.dtype)

    qmap = lambda hp, i: (jnp.minimum(i, nt - 1), hp)
    kvout = lambda hp, i: (jnp.maximum(i - 2, 0), hp)
    in_specs = [pl.BlockSpec((TQ, HBW), qmap)]
    in_specs += _kv_specs(nhp, nt) + _kv_specs(2 * nhp, nt)
    in_specs += [pl.BlockSpec((HB, TQ, TKEYS), lambda hp, i: (hp, 0, 0)),
                 pl.BlockSpec((TQ, HBW), qmap)]
    blk = (TQ, HBW)
    return _call(
        body, name="attn_bwd", grid=(nhp, nt + 2),
        in_specs=in_specs,
        out_specs=[pl.BlockSpec(blk, qmap), pl.BlockSpec(blk, kvout), pl.BlockSpec(blk, kvout),
                   pl.BlockSpec((HB, TQ, TKEYS), lambda hp, i: (hp, 0, 0))],
        out_shape=[jax.ShapeDtypeStruct((s, E_MIX), BF16)] * 3
        + [jax.ShapeDtypeStruct((N_HEADS, TQ, TKEYS), F32)],
        scratch_shapes=[pltpu.VMEM((3, TQ, HBW), F32)] * 2,
        compiler_params=_cparams("parallel", "arbitrary"),
    )(h, h, h, h, h, h, h, tb, d_mix)


CONV_TS = 512
HALO = 2 * SUBLANES


def _shift_down(prev, cur, k):
    rolled = pltpu.roll(cur, k, 0)
    row = lax.broadcasted_iota(jnp.int32, (HALO, cur.shape[1]), 0)
    top = jnp.where(row < k, pltpu.roll(prev, k, 0), rolled[:HALO])
    return jnp.concatenate([top, rolled[HALO:]], axis=0)


def _shift_up(cur, nxt, k):
    ts = cur.shape[0]
    rolled = pltpu.roll(cur, ts - k, 0)
    row = lax.broadcasted_iota(jnp.int32, (HALO, cur.shape[1]), 0)
    bottom = jnp.where(row >= HALO - k, pltpu.roll(nxt, HALO - k, 0), rolled[ts - HALO:])
    return jnp.concatenate([rolled[:ts - HALO], bottom], axis=0)


def _conv_specs(ts, nb):
    tile = lambda c: pl.BlockSpec((ts, E_MIX), lambda i: (i, c))
    prev = lambda c: pl.BlockSpec((HALO, E_MIX), lambda i: (jnp.maximum(i * (ts // HALO) - 1, 0), c))
    return tile, prev


def _conv_fwd(h, w8):
    s = h.shape[0]
    ts = CONV_TS
    nb = s // ts
    tile, prev = _conv_specs(ts, nb)

    def body(bg, cg, u, cgp, up, w_ref, o_ref):
        i = pl.program_id(0)
        a = cg[...].astype(F32) * u[...].astype(F32)
        ap = jnp.where(i > 0, cgp[...].astype(F32) * up[...].astype(F32), 0.0)
        w = w_ref[...]
        conv = w[0:1] * _shift_down(ap, a, 2) + w[1:2] * _shift_down(ap, a, 1) + w[2:3] * a
        o_ref[...] = (bg[...].astype(F32) * conv).astype(o_ref.dtype)

    return _call(
        body, name="conv_fwd", grid=(nb,),
        in_specs=[tile(0), tile(1), tile(2), prev(1), prev(2),
                  pl.BlockSpec((SUBLANES, E_MIX), lambda i: (0, 0))],
        out_specs=pl.BlockSpec((ts, E_MIX), lambda i: (i, 0)),
        out_shape=jax.ShapeDtypeStruct((s, E_MIX), BF16),
        compiler_params=_cparams("parallel"),
    )(h, h, h, h, h, w8)


def _conv_bwd(h, w8, d_mix):
    s = h.shape[0]
    ts = CONV_TS
    nb = s // ts
    tile, prev = _conv_specs(ts, nb)
    nrow = s // HALO
    nxt = lambda c: pl.BlockSpec((HALO, E_MIX), lambda i: (jnp.minimum((i + 1) * (ts // HALO), nrow - 1), c))

    def body(bg, cg, u, cgp, up, bgn, dmix, dmixn, w_ref, dbg_ref, dcg_ref, du_ref, dw_ref):
        i = pl.program_id(0)

        @pl.when(i == 0)
        def _():
            dw_ref[...] = jnp.zeros_like(dw_ref)

        cgv, uv = cg[...].astype(F32), u[...].astype(F32)
        a = cgv * uv
        ap = jnp.where(i > 0, cgp[...].astype(F32) * up[...].astype(F32), 0.0)
        a1 = _shift_down(ap, a, 1)
        a2 = _shift_down(ap, a, 2)
        w = w_ref[...]
        conv = w[0:1] * a2 + w[1:2] * a1 + w[2:3] * a
        dm = dmix[...].astype(F32)
        dbg_ref[...] = (dm * conv).astype(dbg_ref.dtype)
        dc = dm * bg[...].astype(F32)
        dcn = jnp.where(i < nb - 1, dmixn[...].astype(F32) * bgn[...].astype(F32), 0.0)
        da = w[2:3] * dc + w[1:2] * _shift_up(dc, dcn, 1) + w[0:1] * _shift_up(dc, dcn, 2)
        dcg_ref[...] = (da * uv).astype(dcg_ref.dtype)
        du_ref[...] = (da * cgv).astype(du_ref.dtype)
        dw_ref[0:1, :] += jnp.sum(dc * a2, axis=0, keepdims=True)
        dw_ref[1:2, :] += jnp.sum(dc * a1, axis=0, keepdims=True)
        dw_ref[2:3, :] += jnp.sum(dc * a, axis=0, keepdims=True)

    full = lambda: pl.BlockSpec((ts, E_MIX), lambda i: (i, 0))
    return _call(
        body, name="conv_bwd", grid=(nb,),
        in_specs=[tile(0), tile(1), tile(2), prev(1), prev(2), nxt(0),
                  full(), pl.BlockSpec((HALO, E_MIX), lambda i: (jnp.minimum((i + 1) * (ts // HALO), nrow - 1), 0)),
                  pl.BlockSpec((SUBLANES, E_MIX), lambda i: (0, 0))],
        out_specs=[full(), full(), full(), pl.BlockSpec((SUBLANES, E_MIX), lambda i: (0, 0))],
        out_shape=[jax.ShapeDtypeStruct((s, E_MIX), BF16)] * 3
        + [jax.ShapeDtypeStruct((SUBLANES, E_MIX), F32)],
        compiler_params=_cparams("arbitrary"),
    )(h, h, h, h, h, h, d_mix, d_mix, w8)


def _mem_probs(qh, kh):
    s = _dot_nt(qh, kh) / math.sqrt(MEM_HEAD_DIM)
    m = jnp.max(s, axis=-1, keepdims=True)
    e = jnp.exp(s - m)
    return e / jnp.sum(e, axis=-1, keepdims=True)


def _sigmoid(z):
    return 1.0 / (1.0 + jnp.exp(-z))


def _layer_out_fwd(x, h, mix, kv, w_out, g, b):
    s, d = x.shape
    ts = 2 * TS

    def body(x_ref, mix_ref, qm_ref, z0, z1, z2, kv_ref, wo_ref, g_ref, b_ref,
             xn_ref, r_ref, mem_ref):
        qm = qm_ref[...].astype(MXU_DTYPE)
        kvb = kv_ref[...].astype(MXU_DTYPE)
        mems = []
        for hh in range(MEM_HEADS):
            lo = hh * MEM_HEAD_DIM
            p = _mem_probs(qm[:, lo:lo + MEM_HEAD_DIM], kvb[:, lo:lo + MEM_HEAD_DIM])
            mems.append(_dot(p.astype(MXU_DTYPE), kvb[:, E_MEM + lo:E_MEM + lo + MEM_HEAD_DIM]))
        mem = jnp.concatenate(mems, axis=1).astype(mem_ref.dtype)
        mem_ref[...] = mem
        mixv = mix_ref[...].astype(F32)
        half = E_MIX // 2
        parts = [mixv[:, :half], mixv[:, half:], mem.astype(F32)]
        out = jnp.zeros((ts, d), F32)
        for c, zr in enumerate((z0, z1, z2)):
            zv = zr[...].astype(F32)
            y = (parts[c] * (zv * _sigmoid(zv))).astype(MXU_DTYPE)
            out += _dot(y, wo_ref[c * half:(c + 1) * half, :])
        r = DN_ALPHA * x_ref[...] + out
        r_ref[...] = r
        mu = jnp.mean(r, axis=-1, keepdims=True)
        rc = r - mu
        var = jnp.mean(rc * rc, axis=-1, keepdims=True)
        xn_ref[...] = rc * lax.rsqrt(var + LN_EPS) * g_ref[...] + b_ref[...]

    row = lambda w, c: pl.BlockSpec((ts, w), lambda i: (i, c))
    const = lambda shp: pl.BlockSpec(shp, lambda i: (0, 0))
    return _call(
        body, name="layer_out_fwd", grid=(s // ts,),
        in_specs=[row(d, 0), row(E_MIX, 0), row(E_MEM, QM_BLK),
                  row(E_MEM, Z_BLK), row(E_MEM, Z_BLK + 1), row(E_MEM, Z_BLK + 2),
                  const((N_MEM, 2 * E_MEM)), const((E_BRANCH, d)), const((1, d)), const((1, d))],
        out_specs=[row(d, 0), row(d, 0), row(E_MEM, 0)],
        out_shape=[jax.ShapeDtypeStruct((s, d), F32), jax.ShapeDtypeStruct((s, d), F32),
                   jax.ShapeDtypeStruct((s, E_MEM), BF16)],
        compiler_params=_cparams("parallel"),
    )(x, mix, h, h, h, h, kv, w_out, g, b)


def _layer_out_bwd(dxn, r, g, h, mix, mem, kv, w_out):
    s, d = r.shape
    ts = 2 * TS
    nb = s // ts
    half = E_MIX // 2
    inv = 1.0 / math.sqrt(MEM_HEAD_DIM)

    def body(dxn_ref, r_ref, g_ref, mix_ref, mem_ref, qm_ref, z0, z1, z2, kv_ref, wo_ref,
             dxr_ref, dmix_ref, dqz_ref, dwo_ref, dkv_ref, dg_ref, db_ref, dw_acc):
        i = pl.program_id(0)

        @pl.when(i == 0)
        def _():
            dw_acc[...] = jnp.zeros_like(dw_acc)
            dkv_ref[...] = jnp.zeros_like(dkv_ref)
            dg_ref[...] = jnp.zeros_like(dg_ref)
            db_ref[...] = jnp.zeros_like(db_ref)

        dxn_v = dxn_ref[...]
        rv = r_ref[...]
        mu = jnp.mean(rv, axis=-1, keepdims=True)
        rc = rv - mu
        var = jnp.mean(rc * rc, axis=-1, keepdims=True)
        rstd = lax.rsqrt(var + LN_EPS)
        xhat = rc * rstd
        dg_ref[...] += jnp.sum(dxn_v * xhat, axis=0, keepdims=True)
        db_ref[...] += jnp.sum(dxn_v, axis=0, keepdims=True)
        dxh = dxn_v * g_ref[...]
        m1 = jnp.mean(dxh, axis=-1, keepdims=True)
        m2 = jnp.mean(dxh * xhat, axis=-1, keepdims=True)
        dr = rstd * (dxh - m1 - xhat * m2)
        dxr_ref[...] = DN_ALPHA * dr
        dout = dr.astype(MXU_DTYPE)
        mixv = mix_ref[...].astype(F32)
        parts = [mixv[:, :half], mixv[:, half:], mem_ref[...].astype(F32)]
        dcs = []
        for c, zr in enumerate((z0, z1, z2)):
            lo = c * half
            zv = zr[...].astype(F32)
            sg = _sigmoid(zv)
            sl = zv * sg
            dy = _dot_nt(dout, wo_ref[lo:lo + half, :])
            y = (parts[c] * sl).astype(MXU_DTYPE)
            dw_acc[lo:lo + half, :] += _dot_tn(y, dout)
            dcs.append(dy * sl)
            dqz_ref[:, E_MEM + lo:E_MEM + lo + half] = (
                dy * parts[c] * (sg * (1.0 + zv * (1.0 - sg)))).astype(dqz_ref.dtype)
        dmix_ref[...] = jnp.concatenate(dcs[:2], axis=1).astype(dmix_ref.dtype)

        qm = qm_ref[...].astype(MXU_DTYPE)
        kvb = kv_ref[...].astype(MXU_DTYPE)
        dmb = dcs[2].astype(MXU_DTYPE)
        for hh in range(MEM_HEADS):
            lo = hh * MEM_HEAD_DIM
            qh = qm[:, lo:lo + MEM_HEAD_DIM]
            kh = kvb[:, lo:lo + MEM_HEAD_DIM]
            vh = kvb[:, E_MEM + lo:E_MEM + lo + MEM_HEAD_DIM]
            dmh = dmb[:, lo:lo + MEM_HEAD_DIM]
            p = _mem_probs(qh, kh)
            dp = _dot_nt(dmh, vh)
            ds = p * (dp - jnp.sum(p * dp, axis=-1, keepdims=True))
            dsb = (ds * inv).astype(MXU_DTYPE)
            dqz_ref[:, lo:lo + MEM_HEAD_DIM] = _dot(dsb, kh).astype(dqz_ref.dtype)
            dkv_ref[:, lo:lo + MEM_HEAD_DIM] += _dot_tn(dsb, qh)
            dkv_ref[:, E_MEM + lo:E_MEM + lo + MEM_HEAD_DIM] += _dot_tn(p.astype(MXU_DTYPE), dmh)

        @pl.when(i == nb - 1)
        def _():
            dwo_ref[...] = dw_acc[...].astype(dwo_ref.dtype)

    row = lambda w, c: pl.BlockSpec((ts, w), lambda i: (i, c))
    const = lambda shp: pl.BlockSpec(shp, lambda i: (0, 0))
    once = lambda shp: pl.BlockSpec(shp, lambda i: (0, 0), pipeline_mode=pl.Buffered(1))
    return _call(
        body, name="layer_out_bwd", grid=(nb,),
        in_specs=[row(d, 0), row(d, 0), const((1, d)), row(E_MIX, 0), row(E_MEM, 0),
                  row(E_MEM, QM_BLK), row(E_MEM, Z_BLK), row(E_MEM, Z_BLK + 1), row(E_MEM, Z_BLK + 2),
                  once((N_MEM, 2 * E_MEM)), once((E_BRANCH, d))],
        out_specs=[row(d, 0), row(E_MIX, 0), row(E_MEM + E_BRANCH, 0),
                   const((E_BRANCH, d)), const((N_MEM, 2 * E_MEM)), const((1, d)), const((1, d))],
        out_shape=[jax.ShapeDtypeStruct((s, d), F32), jax.ShapeDtypeStruct((s, E_MIX), BF16),
                   jax.ShapeDtypeStruct((s, E_MEM + E_BRANCH), BF16),
                   jax.ShapeDtypeStruct((E_BRANCH, d), BF16),
                   jax.ShapeDtypeStruct((N_MEM, 2 * E_MEM), F32),
                   jax.ShapeDtypeStruct((1, d), F32), jax.ShapeDtypeStruct((1, d), F32)],
        scratch_shapes=[pltpu.VMEM((E_BRANCH, d), F32)],
        compiler_params=_cparams("arbitrary"),
    )(dxn, r, g, mix, mem, h, h, h, h, kv, w_out)


def _loss_head(y, target):
    s, d = y.shape
    ts = 512

    def body(y_ref, t_ref, l_ref, dy_ref):
        @pl.when(pl.program_id(0) == 0)
        def _():
            l_ref[...] = jnp.zeros_like(l_ref)

        e = y_ref[...] - t_ref[...]
        dy_ref[...] = e * (1.0 / d)
        l_ref[...] += (0.5 / d) * jnp.sum(jnp.sum(e * e, axis=1, keepdims=True), axis=0, keepdims=True)

    return _call(
        body, name="loss_head", grid=(s // ts,),
        in_specs=[pl.BlockSpec((ts, d), lambda i: (i, 0))] * 2,
        out_specs=[pl.BlockSpec((1, 1), lambda i: (0, 0)), pl.BlockSpec((ts, d), lambda i: (i, 0))],
        out_shape=[jax.ShapeDtypeStruct((1, 1), F32), jax.ShapeDtypeStruct((s, d), F32)],
        compiler_params=_cparams("arbitrary"),
    )(y, target)


def _local_step(x, mem, get_weights, put_grads, rel_bias, conv_w, ln_g, ln_b, target):
    saved = []
    xl = x
    for layer in range(DEPTH):
        w_in_l, rest = get_weights(layer, xl)
        h, xb = _inproj(xl, w_in_l)
        w_kv_l, w_out_l = rest(h)
        if layer % 2 == 0:
            table = jnp.pad(rel_bias[layer // 2], ((0, 0), (0, N_REL_PAD - N_REL)))
            aux = _tile_bias(table)
            mix = _attn_fwd(h, aux)
        else:
            aux = jnp.pad(conv_w[layer // 2], ((0, SUBLANES - 3), (0, 0)))
            mix = _conv_fwd(h, aux)
        kv = _small_matmul(mem, w_kv_l, False, F32, "kv_mem")
        xn, r, mem_out = _layer_out_fwd(xl, h, mix, kv, w_out_l,
                                        ln_g[layer][None], ln_b[layer][None])
        saved.append((xb, h, aux, mix, kv, r, mem_out, w_in_l, w_out_l))
        xl = xn

    loss, dx = _loss_head(xl, target)

    dgs, dbs, d_rel, d_conv = [], [], [], []
    for layer in reversed(range(DEPTH)):
        xb, h, aux, mix, kv, r, mem_out, w_in_l, w_out_l = saved[layer]
        dx_res, d_mix, dqz, dwo, dkv, dg, db = _layer_out_bwd(
            dx, r, ln_g[layer][None], h, mix, mem_out, kv, w_out_l)
        if layer % 2 == 0:
            dq, dk, dv, dtb = _attn_bwd(h, aux, d_mix)
            d_rel.append(_tile_bias_bwd(dtb)[:, :N_REL])
            pieces = [dq, dk, dv, dqz]
        else:
            dbg, dcg, du, dw8 = _conv_bwd(h, aux, d_mix)
            d_conv.append(dw8[:3])
            pieces = [dbg, dcg, du, dqz]
        token = put_grads(layer, _dw_matmul(xb, pieces), _small_matmul(mem, dkv, True, BF16, "dw_kv"), dwo)
        dgs.append(dg[0])
        dbs.append(db[0])
        dx = _dx_matmul(pieces, w_in_l.T, dx_res, token)

    rev = lambda lst: jnp.stack(lst[::-1])
    return loss, dx, rev(d_rel), rev(d_conv), rev(dgs), rev(dbs)


def _me():
    return lax.axis_index("x"), lax.axis_index("y"), lax.axis_index("c")


def _peer(k):
    x, y, c = _me()
    kx, ky, kc = (k >> 2) & 1, (k >> 1) & 1, k & 1
    return (1 - x if kx else x, 1 - y if ky else y, 1 - c if kc else c)


def _lin(dev):
    return 4 * dev[0] + 2 * dev[1] + dev[2]


ANY = pl.BlockSpec(memory_space=pl.ANY)


def _exchange(srcs, dst_shapes, src_slice, dst_slice, name):
    na = len(srcs)

    def body(*refs):
        src_refs = refs[:na]
        dst_refs = refs[na:2 * na]
        send_sems, recv_sems, local_sems = refs[2 * na:]
        me = _lin(_me())
        copies = []
        for a in range(na):
            loc = pltpu.make_async_copy(src_slice(a, src_refs[a], me), dst_slice(a, dst_refs[a], me),
                                        local_sems.at[a])
            loc.start()
            copies.append(loc)
            for k in range(1, N_DEV):
                peer = _peer(k)
                cp = pltpu.make_async_remote_copy(
                    src_ref=src_slice(a, src_refs[a], _lin(peer)),
                    dst_ref=dst_slice(a, dst_refs[a], me),
                    send_sem=send_sems.at[a, k - 1], recv_sem=recv_sems.at[a, k - 1],
                    device_id=peer, device_id_type=pl.DeviceIdType.MESH)
                cp.start()
                copies.append(cp)
        for cp in copies:
            cp.wait()

    return _call(
        body, name=name,
        in_specs=[ANY] * na, out_specs=[ANY] * na,
        out_shape=[jax.ShapeDtypeStruct(shp, s.dtype) for shp, s in zip(dst_shapes, srcs)],
        scratch_shapes=[pltpu.SemaphoreType.DMA((na, N_DEV - 1)),
                        pltpu.SemaphoreType.DMA((na, N_DEV - 1)),
                        pltpu.SemaphoreType.DMA((na,))],
    )(*srcs)


def _gather_to_all(src, name):
    return _exchange([src], [(N_DEV,) + src.shape], lambda a, ref, p: ref,
                     lambda a, ref, me: ref.at[me], name)[0]


HBM = pl.BlockSpec(memory_space=pltpu.HBM)
SEM = pl.BlockSpec(memory_space=pltpu.SEMAPHORE)
EFFECT = pltpu.SideEffectType.DATAFLOW_SIDE_EFFECTING
N_PEER = N_DEV - 1
N_KIND = 3


def _peer_copies(kind, src_ref, land_ref, send, recv, src_slice, dst_slice):
    me = _lin(_me())
    copies = []
    for k in range(1, N_DEV):
        peer = _peer(k)
        copies.append(pltpu.make_async_remote_copy(
            src_ref=src_slice(kind, src_ref, _lin(peer)),
            dst_ref=dst_slice(kind, land_ref, me, k),
            send_sem=send.at[k - 1], recv_sem=recv.at[k - 1],
            device_id=peer, device_id_type=pl.DeviceIdType.MESH))
    return copies


def _own_copy(kind, src_ref, land_ref, send, src_slice, dst_slice):
    me = _lin(_me())
    return pltpu.make_async_copy(src_slice(kind, src_ref, me), dst_slice(kind, land_ref, me, 0),
                                 send.at[N_PEER])


def _split_start(srcs, kinds, land_shapes, src_slice, dst_slice, name, own=False):
    na = len(srcs)

    def body(*refs):
        src_refs, land_refs = refs[:na], refs[na:2 * na]
        sems = refs[2 * na:4 * na]
        token = refs[-1]
        for a in range(na):
            for cp in _peer_copies(kinds[a], src_refs[a], land_refs[a], sems[2 * a], sems[2 * a + 1],
                                   src_slice, dst_slice):
                cp.start()
            if own:
                _own_copy(kinds[a], src_refs[a], land_refs[a], sems[2 * a], src_slice, dst_slice).start()
        token[...] = jnp.zeros_like(token)

    sem_shape = pltpu.SemaphoreType.DMA((N_DEV,))
    lands = [lax.empty(shp, s.dtype) for shp, s in zip(land_shapes, srcs)]
    outs = _call(
        body, name=name,
        in_specs=[HBM] * (2 * na),
        out_specs=[SEM] * (2 * na) + [HBM] * (2 * na) + [pl.BlockSpec(memory_space=pltpu.VMEM)],
        out_shape=[sem_shape] * (2 * na)
        + [pltpu.HBM(s.shape, s.dtype) for s in srcs]
        + [pltpu.HBM(shp, s.dtype) for shp, s in zip(land_shapes, srcs)]
        + [jax.ShapeDtypeStruct((SUBLANES, LANES), F32)],
        input_output_aliases={i: 2 * na + i for i in range(2 * na)},
        compiler_params=pltpu.CompilerParams(has_side_effects=EFFECT),
    )(*[pltpu.with_memory_space_constraint(a, pltpu.HBM) for a in list(srcs) + lands])
    sems = [(outs[2 * a], outs[2 * a + 1]) for a in range(na)]
    thrus = outs[2 * na:3 * na]
    lands = outs[3 * na:4 * na]
    return sems, thrus, lands, outs[-1]


def _split_wait(sems, thrus, lands, kinds, src_slice, dst_slice, after, name, own=False):
    na = len(thrus)

    def body(*refs):
        src_refs, land_refs = refs[:na], refs[na:2 * na]
        sem_refs = refs[2 * na:4 * na]
        for a in range(na):
            for cp in _peer_copies(kinds[a], src_refs[a], land_refs[a], sem_refs[2 * a], sem_refs[2 * a + 1],
                                   src_slice, dst_slice):
                cp.wait_send()
                cp.wait_recv()
            if own:
                _own_copy(kinds[a], src_refs[a], land_refs[a], sem_refs[2 * a], src_slice, dst_slice).wait()

    outs = _call(
        body, name=name,
        in_specs=[HBM] * (2 * na) + [SEM] * (2 * na) + [ANY],
        out_specs=[HBM] * (2 * na),
        out_shape=[pltpu.HBM(a.shape, a.dtype) for a in list(thrus) + list(lands)],
        input_output_aliases={i: i for i in range(2 * na)},
        compiler_params=pltpu.CompilerParams(has_side_effects=EFFECT),
    )(*thrus, *lands, *[s for pair in sems for s in pair], after)
    return outs[na:]


def _shard_dims(c_in, r_kv, r_out):
    def sl(j, ref, p):
        if j == 0:
            return ref.at[:, pl.ds(pl.multiple_of(p * c_in, LANES), c_in)]
        r = r_kv if j == 1 else r_out
        return ref.at[pl.ds(pl.multiple_of(p * r, 2 * SUBLANES), r), :]
    return sl


def _adamw_math(w, g, m, v):
    m = ADAM_B1 * m + (1.0 - ADAM_B1) * g
    v = ADAM_B2 * v + (1.0 - ADAM_B2) * (g * g)
    m_hat = m / (1.0 - ADAM_B1 ** ADAM_STEP)
    v_hat = v / (1.0 - ADAM_B2 ** ADAM_STEP)
    delta = -ADAM_LR * (m_hat / (jnp.sqrt(v_hat) + ADAM_EPS) + ADAM_WD * w)
    return delta, m, v


def _reduce_adamw(parts, w, m, v, name):
    rows, cols = w.shape
    tr = rows
    for cand in (512, 256, 128, 64, 32, 16):
        if rows % cand == 0 and rows > cand:
            tr = cand
            break

    def body(p_ref, w_ref, m_ref, v_ref, g_out, d_out, m_out, v_out):
        g = p_ref[0].astype(F32)
        for s in range(1, N_DEV):
            g = g + p_ref[s].astype(F32)
        g_out[...] = g
        d_out[...], m_out[...], v_out[...] = _adamw_math(w_ref[...], g, m_ref[...], v_ref[...])

    blk = pl.BlockSpec((tr, cols), lambda i: (i, 0))
    return _call(
        body, name=name, grid=(rows // tr,),
        in_specs=[pl.BlockSpec((N_DEV, tr, cols), lambda i: (0, i, 0)), blk, blk, blk],
        out_specs=[blk] * 4,
        out_shape=[jax.ShapeDtypeStruct((rows, cols), F32)] * 4,
        compiler_params=_cparams("parallel"),
    )(parts, w, m, v)


def _reduce_adamw_layers(lands, owns, w, m, v, name):
    nl, rows, cols = w.shape
    tr = rows
    for cand in (256, 192, 128):
        if rows % cand == 0:
            tr = cand
            break

    def body(*refs):
        land_refs, own_refs = refs[:nl], refs[nl:2 * nl]
        w_ref, m_ref, v_ref, g_out, d_out, m_out, v_out = refs[2 * nl:]
        layer = pl.program_id(0)
        for a in range(nl):
            @pl.when(layer == a)
            def _(a=a):
                g = own_refs[a][...].astype(F32)
                for k in range(N_PEER):
                    g = g + land_refs[a][k].astype(F32)
                g_out[...] = g
                d_out[...], m_out[...], v_out[...] = _adamw_math(w_ref[...], g, m_ref[...], v_ref[...])

    def lmap(a):
        return lambda l, i: (0, jnp.where(l == a, i, 0), 0)

    def omap(a):
        return lambda l, i: (jnp.where(l == a, i, 0), 0)

    blk = pl.BlockSpec((None, tr, cols), lambda l, i: (l, i, 0))
    return _call(
        body, name=name, grid=(nl, rows // tr),
        in_specs=[pl.BlockSpec((N_PEER, tr, cols), lmap(a)) for a in range(nl)]
        + [pl.BlockSpec((tr, cols), omap(a)) for a in range(nl)] + [blk, blk, blk],
        out_specs=[blk] * 4,
        out_shape=[jax.ShapeDtypeStruct((nl, rows, cols), F32)] * 4,
        compiler_params=_cparams("arbitrary", "arbitrary"),
    )(*lands, *owns, w, m, v)


SM_G, SM_B, SM_CONV, SM_REL = 0, 4, 8, 16
SM_ROWS = SM_REL + 2 * N_HEADS
REL_W = 384


def _pack_small(d_rel, d_conv, dg, db):
    buf = jnp.zeros((SM_ROWS, D_MODEL), F32)
    buf = buf.at[SM_G:SM_G + DEPTH].set(dg)
    buf = buf.at[SM_B:SM_B + DEPTH].set(db)
    buf = buf.at[SM_CONV:SM_CONV + 6].set(d_conv.reshape(6, E_MIX))
    buf = buf.at[SM_REL:, :N_REL].set(d_rel.reshape(2 * N_HEADS, N_REL))
    return buf


def kernel(x, mem, w_in, w_mem_kv, w_out, rel_bias, conv_w, ln_g, ln_b, loss_target, m_w_in, m_w_mem_kv, m_w_out, m_rel_bias, m_conv_w, m_ln_g, m_ln_b, v_w_in, v_w_mem_kv, v_w_out, v_rel_bias, v_conv_w, v_ln_g, v_ln_b):
    me = _lin(_me())
    c_in, r_kv, r_out, c_conv = w_in.shape[2], w_mem_kv.shape[1], w_out.shape[1], conv_w.shape[2]

    shard = _shard_dims(c_in, r_kv, r_out)
    own_start = lambda j: (0, me * c_in) if j == 0 else (me * (r_kv if j == 1 else r_out), 0)

    w_sh = [w_in.astype(BF16), w_mem_kv.astype(BF16), w_out.astype(BF16)]
    full_shapes = [(D_MODEL, N_DEV * c_in), (N_DEV * r_kv, w_mem_kv.shape[2]), (N_DEV * r_out, D_MODEL)]
    ag_src = lambda j, ref, p: ref
    ag_dst = lambda j, ref, me_, k: shard(j, ref, me_)
    kinds = list(range(N_KIND))
    ag_sems, ag_thrus, ag_lands, _ = _split_start(
        [w_sh[j][layer] for layer in range(DEPTH) for j in kinds], kinds * DEPTH,
        full_shapes * DEPTH, ag_src, ag_dst, "ag_start", own=True)

    def get_weights(layer, x_layer):
        lo = layer * N_KIND

        def wait(js, after, name):
            idx = [lo + j for j in js]
            return _split_wait([ag_sems[a] for a in idx], [ag_thrus[a] for a in idx],
                               [ag_lands[a] for a in idx], js, ag_src, ag_dst, after, name, own=True)

        w_in_l, = wait([0], x_layer, "ag_wait_in_%d" % layer)
        return w_in_l, lambda h: wait([1, 2], h, "ag_wait_kv_out_%d" % layer)

    rs_src = shard
    rs_dst = lambda j, ref, me_, k: ref.at[k - 1]
    rs_shapes = [(N_PEER, D_MODEL, c_in), (N_PEER, r_kv, w_mem_kv.shape[2]), (N_PEER, r_out, D_MODEL)]
    own_sizes = [(D_MODEL, c_in), (r_kv, w_mem_kv.shape[2]), (r_out, D_MODEL)]
    pending = {}

    def put_grads(layer, dwi, dwkv, dwo):
        parts = [dwi, dwkv, dwo]
        owns = [lax.dynamic_slice(parts[j], own_start(j), own_sizes[j]) for j in range(N_KIND)]
        sems, thrus, lands, token = _split_start(parts, kinds, rs_shapes, rs_src, rs_dst, "rs_start_%d" % layer)
        pending[layer] = (sems, thrus, lands, owns)
        return token

    conv_tile = jnp.pad(conv_w.reshape(6, c_conv), ((0, SUBLANES - 6), (0, 0)))
    conv_land = _gather_to_all(conv_tile, "gather_conv")
    conv_f = jnp.transpose(conv_land[:, :6], (1, 0, 2)).reshape(2, 3, N_DEV * c_conv)

    loss, grad_x, d_rel, d_conv, dg, db = _local_step(
        x[0], mem[0], get_weights, put_grads, rel_bias, conv_f, ln_g, ln_b, loss_target[0])

    p_small = _gather_to_all(_pack_small(d_rel, d_conv, dg, db), "gather_small_grads")

    rs_lands, rs_owns = [], []
    for layer in range(DEPTH):
        sems, thrus, lands, owns = pending[layer]
        rs_lands.append(_split_wait(sems, thrus, lands, kinds, rs_src, rs_dst, grad_x, "rs_wait_%d" % layer))
        rs_owns.append(owns)

    def big(j, w, m, v, name):
        return _reduce_adamw_layers([rs_lands[layer][j] for layer in range(DEPTH)],
                                    [rs_owns[layer][j] for layer in range(DEPTH)], w, m, v, name)

    g_in, d_in, nm_in, nv_in = big(0, w_in, m_w_in, v_w_in, "adamw_w_in")
    g_kv, d_kv, nm_kv, nv_kv = big(1, w_mem_kv, m_w_mem_kv, v_w_mem_kv, "adamw_w_kv")
    g_out, d_out, nm_out, nv_out = big(2, w_out, m_w_out, v_w_out, "adamw_w_out")

    def pack_state(rel, conv, g, b):
        conv_full = jnp.zeros((2, 3, E_MIX), F32)
        conv_full = lax.dynamic_update_slice(conv_full, conv, (0, 0, me * c_conv))
        return _pack_small(rel, conv_full, g, b)

    sm_w = pack_state(rel_bias, conv_w, ln_g, ln_b)
    sm_m = pack_state(m_rel_bias, m_conv_w, m_ln_g, m_ln_b)
    sm_v = pack_state(v_rel_bias, v_conv_w, v_ln_g, v_ln_b)
    sm_outs = _reduce_adamw(p_small, sm_w, sm_m, sm_v, "adamw_small")

    def unpack(buf):
        rel = buf[SM_REL:, :N_REL].reshape(2, N_HEADS, N_REL)
        conv = lax.dynamic_slice(buf[SM_CONV:SM_CONV + 6].reshape(2, 3, E_MIX), (0, 0, me * c_conv), (2, 3, c_conv))
        return rel, conv, buf[SM_G:SM_G + DEPTH], buf[SM_B:SM_B + DEPTH]

    g_sm, d_sm, nm_sm, nv_sm = [unpack(b) for b in sm_outs]

    loss = lax.psum(loss[0, 0], ("x", "y", "c"))
    return (loss, grad_x[None],
            g_in, g_kv, g_out, *g_sm,
            d_in, d_kv, d_out, *d_sm,
            nm_in, nm_kv, nm_out, *nm_sm,
            nv_in, nv_kv, nv_out, *nv_sm)
```

```python
import functools
import math

import jax
import jax.numpy as jnp
from jax import lax
from jax.experimental import pallas as pl
from jax.experimental.pallas import tpu as pltpu

F32 = jnp.float32
BF16 = jnp.bfloat16
MXU_DTYPE = jnp.bfloat16

N_DEV = 8
D_MODEL = 1024
DEPTH = 4
CHUNK = 64
N_PREV = 8
N_HEADS = 16
HEAD_DIM = 64
E_MIX = 1024
REL_CLIP = 128
N_REL = 2 * REL_CLIP + 1
N_REL_PAD = 384
N_MEM = 256
MEM_HEADS = 4
MEM_HEAD_DIM = 128
E_MEM = 512
E_BRANCH = E_MIX + E_MEM
N_IN = 3 * E_MIX + E_MEM + E_BRANCH
DN_ALPHA = (2.0 * DEPTH) ** 0.25
LN_EPS = 1e-5
NEG = -1e30

ADAM_LR = 0.001
ADAM_B1 = 0.9
ADAM_B2 = 0.999
ADAM_EPS = 1e-08
ADAM_WD = 0.01
ADAM_STEP = 10

LANES = 128
SUBLANES = 8
VMEM_LIMIT = 56 * 1024 * 1024

TQ = 4 * CHUNK
TKEYS = 3 * TQ
ROLL_W = 1024
TS = 256
QM_BLK = 3 * E_MIX // E_MEM
Z_BLK = QM_BLK + 1


def _call(body, **kw):
    return pl.pallas_call(body, **kw)


def _cparams(*sem):
    return pltpu.CompilerParams(dimension_semantics=sem, vmem_limit_bytes=VMEM_LIMIT)


def _dot(a, b):
    return jnp.dot(a, b, preferred_element_type=F32)


def _dot_nt(a, b):
    return lax.dot_general(a, b, (((1,), (1,)), ((), ())), preferred_element_type=F32)


def _dot_tn(a, b):
    return lax.dot_general(a, b, (((0,), (0,)), ((), ())), preferred_element_type=F32)


def _inproj(x, w):
    s, d = x.shape
    n = w.shape[1]
    tm = min(1024, s)
    tn = 1024

    def body(x_ref, w_ref, o_ref, xb_ref):
        @pl.when(pl.program_id(1) == 0)
        def _():
            xb_ref[...] = x_ref[...].astype(xb_ref.dtype)

        o_ref[...] = _dot(xb_ref[...], w_ref[...]).astype(o_ref.dtype)

    return _call(
        body, name="inproj", grid=(s // tm, n // tn),
        in_specs=[pl.BlockSpec((tm, d), lambda i, j: (i, 0)),
                  pl.BlockSpec((d, tn), lambda i, j: (0, j))],
        out_specs=[pl.BlockSpec((tm, tn), lambda i, j: (i, j)),
                   pl.BlockSpec((tm, d), lambda i, j: (i, 0))],
        out_shape=[jax.ShapeDtypeStruct((s, n), BF16), jax.ShapeDtypeStruct((s, d), BF16)],
        compiler_params=_cparams("parallel", "arbitrary"),
    )(x, w)


def _small_matmul(a, b, trans_a, out_dtype, name):
    m = a.shape[1] if trans_a else a.shape[0]
    n = b.shape[1]

    def body(a_ref, b_ref, o_ref):
        av = a_ref[...].astype(MXU_DTYPE)
        bv = b_ref[...].astype(MXU_DTYPE)
        r = _dot_tn(av, bv) if trans_a else _dot(av, bv)
        o_ref[...] = r.astype(out_dtype)

    return _call(
        body, name=name,
        in_specs=[pl.BlockSpec(memory_space=pltpu.VMEM)] * 2,
        out_specs=pl.BlockSpec(memory_space=pltpu.VMEM),
        out_shape=jax.ShapeDtypeStruct((m, n), out_dtype),
        compiler_params=pltpu.CompilerParams(vmem_limit_bytes=VMEM_LIMIT),
    )(a, b)


def _piece_blocks(pieces, blk):
    offs, nbs, o = [], [], 0
    for p in pieces:
        nb = p.shape[1] // blk
        offs.append(o)
        nbs.append(nb)
        o += nb
    return offs, nbs, o


def _dx_matmul(pieces, wt, addend, token=None):
    s = pieces[0].shape[0]
    n_in, d = wt.shape
    tm = 512
    np_ = len(pieces)
    extra = [] if token is None else [token]

    def body(*refs):
        a_refs = refs[:np_]
        w_ref, add_ref = refs[np_:np_ + 2]
        o_ref = refs[-1]
        a = jnp.concatenate([r[...] for r in a_refs], axis=1)
        o_ref[...] = add_ref[...] + _dot(a, w_ref[...])

    in_specs = [pl.BlockSpec((tm, p.shape[1]), lambda i: (i, 0)) for p in pieces]
    in_specs += [pl.BlockSpec((n_in, d), lambda i: (0, 0), pipeline_mode=pl.Buffered(1)),
                 pl.BlockSpec((tm, d), lambda i: (i, 0))]
    in_specs += [pl.BlockSpec((SUBLANES, LANES), lambda i: (0, 0)) for _ in extra]
    return _call(
        body, name="dx_matmul", grid=(s // tm,),
        in_specs=in_specs,
        out_specs=pl.BlockSpec((tm, d), lambda i: (i, 0)),
        out_shape=jax.ShapeDtypeStruct((s, d), F32),
        compiler_params=_cparams("parallel"),
    )(*pieces, wt, addend, *extra)


def _dw_matmul(x, pieces):
    s, d = x.shape
    tn = 1024
    tk = min(1024, s)
    offs, nbs, nj = _piece_blocks(pieces, tn)
    np_ = len(pieces)
    nk = s // tk

    def body(*refs):
        x_ref = refs[0]
        b_refs = refs[1:1 + np_]
        o_ref, acc = refs[1 + np_:]
        j = pl.program_id(0)
        k = pl.program_id(1)

        @pl.when(k == 0)
        def _():
            acc[...] = jnp.zeros_like(acc)

        for p in range(np_):
            @pl.when((j >= offs[p]) & (j < offs[p] + nbs[p]))
            def _(p=p):
                acc[...] += _dot_tn(x_ref[...], b_refs[p][...])

        @pl.when(k == nk - 1)
        def _():
            o_ref[...] = acc[...].astype(o_ref.dtype)

    def bmap(p):
        def f(j, k):
            inside = (j >= offs[p]) & (j < offs[p] + nbs[p])
            return (jnp.where(inside, k, 0), jnp.clip(j - offs[p], 0, nbs[p] - 1))
        return f

    in_specs = [pl.BlockSpec((tk, d), lambda j, k: (k, 0))]
    in_specs += [pl.BlockSpec((tk, tn), bmap(p)) for p in range(np_)]
    return _call(
        body, name="dw_matmul", grid=(nj, nk),
        in_specs=in_specs,
        out_specs=pl.BlockSpec((d, tn), lambda j, k: (0, j)),
        out_shape=jax.ShapeDtypeStruct((d, nj * tn), BF16),
        scratch_shapes=[pltpu.VMEM((d, tn), F32)],
        compiler_params=_cparams("parallel", "arbitrary"),
    )(x, *pieces)


def _rel_onehot():
    j = lax.broadcasted_iota(jnp.int32, (N_REL_PAD, ROLL_W), 1)
    kk = lax.broadcasted_iota(jnp.int32, (N_REL_PAD, ROLL_W), 0)
    dd = jnp.where(j < TKEYS, j, j - ROLL_W)
    idx = jnp.clip(N_PREV * CHUNK - dd, -REL_CLIP, REL_CLIP) + REL_CLIP
    return jnp.where(idx == kk, 1.0, 0.0).astype(F32)


def _band_mask():
    r = lax.broadcasted_iota(jnp.int32, (TQ, TKEYS), 0) // CHUNK
    m = lax.broadcasted_iota(jnp.int32, (TQ, TKEYS), 1) // CHUNK
    return (m >= r) & (m <= r + N_PREV)


def _tile_bias(table_pad):
    def body(t_ref, o_ref):
        g = jnp.dot(t_ref[...], _rel_onehot(), preferred_element_type=F32,
                    precision=lax.Precision.HIGHEST)
        band = _band_mask()
        for h in range(N_HEADS):
            gh = jnp.broadcast_to(g[h:h + 1, :], (TQ, ROLL_W))
            rolled = pltpu.roll(gh, 0, 1, stride=1, stride_axis=0)
            o_ref[h] = jnp.where(band, rolled[:, :TKEYS], NEG)

    return _call(
        body, name="tile_bias",
        in_specs=[pl.BlockSpec(memory_space=pltpu.VMEM)],
        out_specs=pl.BlockSpec(memory_space=pltpu.VMEM),
        out_shape=jax.ShapeDtypeStruct((N_HEADS, TQ, TKEYS), F32),
        compiler_params=pltpu.CompilerParams(vmem_limit_bytes=VMEM_LIMIT),
    )(table_pad)


def _tile_bias_bwd(dtb):
    def body(d_ref, o_ref, g_ref):
        zpad = jnp.zeros((TQ, ROLL_W - TKEYS), F32)
        rr = lax.broadcasted_iota(jnp.int32, (TQ, TQ), 0)
        cc = lax.broadcasted_iota(jnp.int32, (TQ, TQ), 1)
        flip = jnp.where(rr + cc == TQ - 1, 1.0, 0.0).astype(F32)
        for h in range(N_HEADS):
            xh = jnp.concatenate([d_ref[h], zpad], axis=1)
            xf = jnp.dot(flip, xh, preferred_element_type=F32, precision=lax.Precision.HIGHEST)
            rolled = pltpu.roll(xf, 0, 1, stride=1, stride_axis=0)
            g_ref[h:h + 1, :] = jnp.sum(rolled, axis=0, keepdims=True)
        g = pltpu.roll(g_ref[...], ROLL_W - (TQ - 1), 1)
        o_ref[...] = lax.dot_general(g, _rel_onehot(), (((1,), (1,)), ((), ())),
                                     preferred_element_type=F32, precision=lax.Precision.HIGHEST)

    return _call(
        body, name="tile_bias_bwd",
        in_specs=[pl.BlockSpec(memory_space=pltpu.VMEM)],
        out_specs=pl.BlockSpec(memory_space=pltpu.VMEM),
        out_shape=jax.ShapeDtypeStruct((N_HEADS, N_REL_PAD), F32),
        scratch_shapes=[pltpu.VMEM((N_HEADS, ROLL_W), F32)],
        compiler_params=pltpu.CompilerParams(vmem_limit_bytes=VMEM_LIMIT),
    )(dtb)


HB = 4
HBW = HB * HEAD_DIM
ATTN_SCALE = 0.125
assert ATTN_SCALE == 1.0 / math.sqrt(HEAD_DIM)


def _head_masks():
    lane = lax.broadcasted_iota(jnp.int32, (1, HBW), 1) // HEAD_DIM
    return [lane == hh for hh in range(HB)]


def _select_heads(masks, parts):
    out = parts[-1]
    for hh in range(HB - 2, -1, -1):
        out = jnp.where(masks[hh], parts[hh], out)
    return out


def _attn_probs(qm, kcat, tb, valid):
    s = _dot_nt(qm, kcat) + tb
    if valid is not None:
        s = jnp.where(valid, s, NEG)
    m = jnp.max(s, axis=-1, keepdims=True)
    e = jnp.exp(s - m)
    return e * (1.0 / jnp.sum(e, axis=-1, keepdims=True))


def _key_valid(i):
    col = lax.broadcasted_iota(jnp.int32, (TQ, TKEYS), 1)
    return col >= jnp.maximum(2 - i, 0) * TQ


def _kv_specs(col0, nt):
    def spec(back):
        return pl.BlockSpec((TQ, HBW), lambda hp, i: (jnp.clip(i - back, 0, nt - 1), col0 + hp))
    return [spec(2), spec(1), spec(0)]


def _attn_fwd(h, tb):
    s = h.shape[0]
    nt = s // TQ
    nhp = N_HEADS // HB

    def body(q_ref, k0, k1, k2, v0, v1, v2, tb_ref, o_ref):
        i = pl.program_id(1)

        def tile(valid):
            masks = _head_masks()
            qs = q_ref[...].astype(MXU_DTYPE) * ATTN_SCALE
            kcat = jnp.concatenate([k0[...], k1[...], k2[...]], axis=0).astype(MXU_DTYPE)
            vcat = jnp.concatenate([v0[...], v1[...], v2[...]], axis=0).astype(MXU_DTYPE)
            outs = []
            for hh in range(HB):
                qm = jnp.where(masks[hh], qs, jnp.zeros_like(qs))
                p = _attn_probs(qm, kcat, tb_ref[hh], valid)
                outs.append(_dot(p.astype(MXU_DTYPE), vcat))
            o_ref[...] = _select_heads(masks, outs).astype(o_ref.dtype)

        @pl.when(i < 2)
        def _():
            tile(_key_valid(i))

        @pl.when(i >= 2)
        def _():
            tile(None)

    in_specs = [pl.BlockSpec((TQ, HBW), lambda hp, i: (i, hp))]
    in_specs += _kv_specs(nhp, nt) + _kv_specs(2 * nhp, nt)
    in_specs += [pl.BlockSpec((HB, TQ, TKEYS), lambda hp, i: (hp, 0, 0))]
    return _call(
        body, name="attn_fwd", grid=(nhp, nt),
        in_specs=in_specs,
        out_specs=pl.BlockSpec((TQ, HBW), lambda hp, i: (i, hp)),
        out_shape=jax.ShapeDtypeStruct((s, E_MIX), BF16),
        compiler_params=_cparams("parallel", "parallel"),
    )(h, h, h, h, h, h, h, tb)


def _attn_bwd(h, tb, d_mix):
    s = h.shape[0]
    nt = s // TQ
    nhp = N_HEADS // HB

    def body(q_ref, k0, k1, k2, v0, v1, v2, tb_ref, do_ref,
             dq_ref, dk_ref, dv_ref, dtb_ref, dk_acc, dv_acc):
        i = pl.program_id(1)

        @pl.when(i == 0)
        def _():
            dk_acc[...] = jnp.zeros_like(dk_acc)
            dv_acc[...] = jnp.zeros_like(dv_acc)
            dtb_ref[...] = jnp.zeros_like(dtb_ref)

        @pl.when((i > 0) & (i < nt))
        def _():
            dk_acc[i % 3] = jnp.zeros((TQ, HBW), F32)
            dv_acc[i % 3] = jnp.zeros((TQ, HBW), F32)

        def tile(valid):
            masks = _head_masks()
            qs = q_ref[...].astype(MXU_DTYPE) * ATTN_SCALE
            do2 = do_ref[...].astype(MXU_DTYPE)
            kcat = jnp.concatenate([k0[...], k1[...], k2[...]], axis=0).astype(MXU_DTYPE)
            vcat = jnp.concatenate([v0[...], v1[...], v2[...]], axis=0).astype(MXU_DTYPE)
            ks = kcat * ATTN_SCALE
            dqs, dks, dvs = [], [], []
            for hh in range(HB):
                qm = jnp.where(masks[hh], qs, jnp.zeros_like(qs))
                dom = jnp.where(masks[hh], do2, jnp.zeros_like(do2))
                p = _attn_probs(qm, kcat, tb_ref[hh], valid)
                dp = _dot_nt(dom, vcat)
                ds = p * (dp - jnp.sum(p * dp, axis=-1, keepdims=True))
                dtb_ref[hh] += ds
                dsb = ds.astype(MXU_DTYPE)
                dqs.append(_dot(dsb, ks))
                dks.append(_dot_tn(dsb, qs))
                dvs.append(_dot_tn(p.astype(MXU_DTYPE), do2))
            dq_ref[...] = _select_heads(masks, dqs).astype(dq_ref.dtype)
            dkc = _select_heads(masks, dks)
            dvc = _select_heads(masks, dvs)
            for jj in range(3):
                slot = (i + 1 + jj) % 3
                dk_acc[slot] += dkc[jj * TQ:(jj + 1) * TQ]
                dv_acc[slot] += dvc[jj * TQ:(jj + 1) * TQ]

        @pl.when(i < 2)
        def _():
            tile(_key_valid(i))

        @pl.when((i >= 2) & (i < nt))
        def _():
            tile(None)

        @pl.when(i >= 2)
        def _():
            slot = (i - 2) % 3
            dk_ref[...] = dk_acc[slot].astype(dk_ref.dtype)
            dv_ref[...] = dv_acc[slot].astype(dv_ref.dtype)

    qmap = lambda hp, i: (jnp.minimum(i, nt - 1), hp)
    kvout = lambda hp, i: (jnp.maximum(i - 2, 0), hp)
    in_specs = [pl.BlockSpec((TQ, HBW), qmap)]
    in_specs += _kv_specs(nhp, nt) + _kv_specs(2 * nhp, nt)
    in_specs += [pl.BlockSpec((HB, TQ, TKEYS), lambda hp, i: (hp, 0, 0)),
                 pl.BlockSpec((TQ, HBW), qmap)]
    blk = (TQ, HBW)
    return _call(
        body, name="attn_bwd", grid=(nhp, nt + 2),
        in_specs=in_specs,
        out_specs=[pl.BlockSpec(blk, qmap), pl.BlockSpec(blk, kvout), pl.BlockSpec(blk, kvout),
                   pl.BlockSpec((HB, TQ, TKEYS), lambda hp, i: (hp, 0, 0))],
        out_shape=[jax.ShapeDtypeStruct((s, E_MIX), BF16)] * 3
        + [jax.ShapeDtypeStruct((N_HEADS, TQ, TKEYS), F32)],
        scratch_shapes=[pltpu.VMEM((3, TQ, HBW), F32)] * 2,
        compiler_params=_cparams("parallel", "arbitrary"),
    )(h, h, h, h, h, h, h, tb, d_mix)


CONV_TS = 512
HALO = 2 * SUBLANES


def _shift_down(prev, cur, k):
    rolled = pltpu.roll(cur, k, 0)
    row = lax.broadcasted_iota(jnp.int32, (HALO, cur.shape[1]), 0)
    top = jnp.where(row < k, pltpu.roll(prev, k, 0), rolled[:HALO])
    return jnp.concatenate([top, rolled[HALO:]], axis=0)


def _shift_up(cur, nxt, k):
    ts = cur.shape[0]
    rolled = pltpu.roll(cur, ts - k, 0)
    row = lax.broadcasted_iota(jnp.int32, (HALO, cur.shape[1]), 0)
    bottom = jnp.where(row >= HALO - k, pltpu.roll(nxt, HALO - k, 0), rolled[ts - HALO:])
    return jnp.concatenate([rolled[:ts - HALO], bottom], axis=0)


def _conv_specs(ts, nb):
    tile = lambda c: pl.BlockSpec((ts, E_MIX), lambda i: (i, c))
    prev = lambda c: pl.BlockSpec((HALO, E_MIX), lambda i: (jnp.maximum(i * (ts // HALO) - 1, 0), c))
    return tile, prev


def _conv_fwd(h, w8):
    s = h.shape[0]
    ts = CONV_TS
    nb = s // ts
    tile, prev = _conv_specs(ts, nb)

    def body(bg, cg, u, cgp, up, w_ref, o_ref):
        i = pl.program_id(0)
        a = cg[...].astype(F32) * u[...].astype(F32)
        ap = jnp.where(i > 0, cgp[...].astype(F32) * up[...].astype(F32), 0.0)
        w = w_ref[...]
        conv = w[0:1] * _shift_down(ap, a, 2) + w[1:2] * _shift_down(ap, a, 1) + w[2:3] * a
        o_ref[...] = (bg[...].astype(F32) * conv).astype(o_ref.dtype)

    return _call(
        body, name="conv_fwd", grid=(nb,),
        in_specs=[tile(0), tile(1), tile(2), prev(1), prev(2),
                  pl.BlockSpec((SUBLANES, E_MIX), lambda i: (0, 0))],
        out_specs=pl.BlockSpec((ts, E_MIX), lambda i: (i, 0)),
        out_shape=jax.ShapeDtypeStruct((s, E_MIX), BF16),
        compiler_params=_cparams("parallel"),
    )(h, h, h, h, h, w8)


def _conv_bwd(h, w8, d_mix):
    s = h.shape[0]
    ts = CONV_TS
    nb = s // ts
    tile, prev = _conv_specs(ts, nb)
    nrow = s // HALO
    nxt = lambda c: pl.BlockSpec((HALO, E_MIX), lambda i: (jnp.minimum((i + 1) * (ts // HALO), nrow - 1), c))

    def body(bg, cg, u, cgp, up, bgn, dmix, dmixn, w_ref, dbg_ref, dcg_ref, du_ref, dw_ref):
        i = pl.program_id(0)

        @pl.when(i == 0)
        def _():
            dw_ref[...] = jnp.zeros_like(dw_ref)

        cgv, uv = cg[...].astype(F32), u[...].astype(F32)
        a = cgv * uv
        ap = jnp.where(i > 0, cgp[...].astype(F32) * up[...].astype(F32), 0.0)
        a1 = _shift_down(ap, a, 1)
        a2 = _shift_down(ap, a, 2)
        w = w_ref[...]
        conv = w[0:1] * a2 + w[1:2] * a1 + w[2:3] * a
        dm = dmix[...].astype(F32)
        dbg_ref[...] = (dm * conv).astype(dbg_ref.dtype)
        dc = dm * bg[...].astype(F32)
        dcn = jnp.where(i < nb - 1, dmixn[...].astype(F32) * bgn[...].astype(F32), 0.0)
        da = w[2:3] * dc + w[1:2] * _shift_up(dc, dcn, 1) + w[0:1] * _shift_up(dc, dcn, 2)
        dcg_ref[...] = (da * uv).astype(dcg_ref.dtype)
        du_ref[...] = (da * cgv).astype(du_ref.dtype)
        dw_ref[0:1, :] += jnp.sum(dc * a2, axis=0, keepdims=True)
        dw_ref[1:2, :] += jnp.sum(dc * a1, axis=0, keepdims=True)
        dw_ref[2:3, :] += jnp.sum(dc * a, axis=0, keepdims=True)

    full = lambda: pl.BlockSpec((ts, E_MIX), lambda i: (i, 0))
    return _call(
        body, name="conv_bwd", grid=(nb,),
        in_specs=[tile(0), tile(1), tile(2), prev(1), prev(2), nxt(0),
                  full(), pl.BlockSpec((HALO, E_MIX), lambda i: (jnp.minimum((i + 1) * (ts // HALO), nrow - 1), 0)),
                  pl.BlockSpec((SUBLANES, E_MIX), lambda i: (0, 0))],
        out_specs=[full(), full(), full(), pl.BlockSpec((SUBLANES, E_MIX), lambda i: (0, 0))],
        out_shape=[jax.ShapeDtypeStruct((s, E_MIX), BF16)] * 3
        + [jax.ShapeDtypeStruct((SUBLANES, E_MIX), F32)],
        compiler_params=_cparams("arbitrary"),
    )(h, h, h, h, h, h, d_mix, d_mix, w8)


def _mem_probs(qh, kh):
    s = _dot_nt(qh, kh) / math.sqrt(MEM_HEAD_DIM)
    m = jnp.max(s, axis=-1, keepdims=True)
    e = jnp.exp(s - m)
    return e / jnp.sum(e, axis=-1, keepdims=True)


def _sigmoid(z):
    return 1.0 / (1.0 + jnp.exp(-z))


def _layer_out_fwd(x, h, mix, kv, w_out, g, b):
    s, d = x.shape
    ts = 2 * TS

    def body(x_ref, mix_ref, qm_ref, z0, z1, z2, kv_ref, wo_ref, g_ref, b_ref,
             xn_ref, r_ref, mem_ref):
        qm = qm_ref[...].astype(MXU_DTYPE)
        kvb = kv_ref[...].astype(MXU_DTYPE)
        mems = []
        for hh in range(MEM_HEADS):
            lo = hh * MEM_HEAD_DIM
            p = _mem_probs(qm[:, lo:lo + MEM_HEAD_DIM], kvb[:, lo:lo + MEM_HEAD_DIM])
            mems.append(_dot(p.astype(MXU_DTYPE), kvb[:, E_MEM + lo:E_MEM + lo + MEM_HEAD_DIM]))
        mem = jnp.concatenate(mems, axis=1).astype(mem_ref.dtype)
        mem_ref[...] = mem
        mixv = mix_ref[...].astype(F32)
        half = E_MIX // 2
        parts = [mixv[:, :half], mixv[:, half:], mem.astype(F32)]
        out = jnp.zeros((ts, d), F32)
        for c, zr in enumerate((z0, z1, z2)):
            zv = zr[...].astype(F32)
            y = (parts[c] * (zv * _sigmoid(zv))).astype(MXU_DTYPE)
            out += _dot(y, wo_ref[c * half:(c + 1) * half, :])
        r = DN_ALPHA * x_ref[...] + out
        r_ref[...] = r
        mu = jnp.mean(r, axis=-1, keepdims=True)
        rc = r - mu
        var = jnp.mean(rc * rc, axis=-1, keepdims=True)
        xn_ref[...] = rc * lax.rsqrt(var + LN_EPS) * g_ref[...] + b_ref[...]

    row = lambda w, c: pl.BlockSpec((ts, w), lambda i: (i, c))
    const = lambda shp: pl.BlockSpec(shp, lambda i: (0, 0))
    return _call(
        body, name="layer_out_fwd", grid=(s // ts,),
        in_specs=[row(d, 0), row(E_MIX, 0), row(E_MEM, QM_BLK),
                  row(E_MEM, Z_BLK), row(E_MEM, Z_BLK + 1), row(E_MEM, Z_BLK + 2),
                  const((N_MEM, 2 * E_MEM)), const((E_BRANCH, d)), const((1, d)), const((1, d))],
        out_specs=[row(d, 0), row(d, 0), row(E_MEM, 0)],
        out_shape=[jax.ShapeDtypeStruct((s, d), F32), jax.ShapeDtypeStruct((s, d), F32),
                   jax.ShapeDtypeStruct((s, E_MEM), BF16)],
        compiler_params=_cparams("parallel"),
    )(x, mix, h, h, h, h, kv, w_out, g, b)


def _layer_out_bwd(dxn, r, g, h, mix, mem, kv, w_out):
    s, d = r.shape
    ts = 2 * TS
    nb = s // ts
    half = E_MIX // 2
    inv = 1.0 / math.sqrt(MEM_HEAD_DIM)

    def body(dxn_ref, r_ref, g_ref, mix_ref, mem_ref, qm_ref, z0, z1, z2, kv_ref, wo_ref,
             dxr_ref, dmix_ref, dqz_ref, dwo_ref, dkv_ref, dg_ref, db_ref, dw_acc):
        i = pl.program_id(0)

        @pl.when(i == 0)
        def _():
            dw_acc[...] = jnp.zeros_like(dw_acc)
            dkv_ref[...] = jnp.zeros_like(dkv_ref)
            dg_ref[...] = jnp.zeros_like(dg_ref)
            db_ref[...] = jnp.zeros_like(db_ref)

        dxn_v = dxn_ref[...]
        rv = r_ref[...]
        mu = jnp.mean(rv, axis=-1, keepdims=True)
        rc = rv - mu
        var = jnp.mean(rc * rc, axis=-1, keepdims=True)
        rstd = lax.rsqrt(var + LN_EPS)
        xhat = rc * rstd
        dg_ref[...] += jnp.sum(dxn_v * xhat, axis=0, keepdims=True)
        db_ref[...] += jnp.sum(dxn_v, axis=0, keepdims=True)
        dxh = dxn_v * g_ref[...]
        m1 = jnp.mean(dxh, axis=-1, keepdims=True)
        m2 = jnp.mean(dxh * xhat, axis=-1, keepdims=True)
        dr = rstd * (dxh - m1 - xhat * m2)
        dxr_ref[...] = DN_ALPHA * dr
        dout = dr.astype(MXU_DTYPE)
        mixv = mix_ref[...].astype(F32)
        parts = [mixv[:, :half], mixv[:, half:], mem_ref[...].astype(F32)]
        dcs = []
        for c, zr in enumerate((z0, z1, z2)):
            lo = c * half
            zv = zr[...].astype(F32)
            sg = _sigmoid(zv)
            sl = zv * sg
            dy = _dot_nt(dout, wo_ref[lo:lo + half, :])
            y = (parts[c] * sl).astype(MXU_DTYPE)
            dw_acc[lo:lo + half, :] += _dot_tn(y, dout)
            dcs.append(dy * sl)
            dqz_ref[:, E_MEM + lo:E_MEM + lo + half] = (
                dy * parts[c] * (sg * (1.0 + zv * (1.0 - sg)))).astype(dqz_ref.dtype)
        dmix_ref[...] = jnp.concatenate(dcs[:2], axis=1).astype(dmix_ref.dtype)

        qm = qm_ref[...].astype(MXU_DTYPE)
        kvb = kv_ref[...].astype(MXU_DTYPE)
        dmb = dcs[2].astype(MXU_DTYPE)
        for hh in range(MEM_HEADS):
            lo = hh * MEM_HEAD_DIM
            qh = qm[:, lo:lo + MEM_HEAD_DIM]
            kh = kvb[:, lo:lo + MEM_HEAD_DIM]
            vh = kvb[:, E_MEM + lo:E_MEM + lo + MEM_HEAD_DIM]
            dmh = dmb[:, lo:lo + MEM_HEAD_DIM]
            p = _mem_probs(qh, kh)
            dp = _dot_nt(dmh, vh)
            ds = p * (dp - jnp.sum(p * dp, axis=-1, keepdims=True))
            dsb = (ds * inv).astype(MXU_DTYPE)
            dqz_ref[:, lo:lo + MEM_HEAD_DIM] = _dot(dsb, kh).astype(dqz_ref.dtype)
            dkv_ref[:, lo:lo + MEM_HEAD_DIM] += _dot_tn(dsb, qh)
            dkv_ref[:, E_MEM + lo:E_MEM + lo + MEM_HEAD_DIM] += _dot_tn(p.astype(MXU_DTYPE), dmh)

        @pl.when(i == nb - 1)
        def _():
            dwo_ref[...] = dw_acc[...].astype(dwo_ref.dtype)

    row = lambda w, c: pl.BlockSpec((ts, w), lambda i: (i, c))
    const = lambda shp: pl.BlockSpec(shp, lambda i: (0, 0))
    once = lambda shp: pl.BlockSpec(shp, lambda i: (0, 0), pipeline_mode=pl.Buffered(1))
    return _call(
        body, name="layer_out_bwd", grid=(nb,),
        in_specs=[row(d, 0), row(d, 0), const((1, d)), row(E_MIX, 0), row(E_MEM, 0),
                  row(E_MEM, QM_BLK), row(E_MEM, Z_BLK), row(E_MEM, Z_BLK + 1), row(E_MEM, Z_BLK + 2),
                  once((N_MEM, 2 * E_MEM)), once((E_BRANCH, d))],
        out_specs=[row(d, 0), row(E_MIX, 0), row(E_MEM + E_BRANCH, 0),
                   const((E_BRANCH, d)), const((N_MEM, 2 * E_MEM)), const((1, d)), const((1, d))],
        out_shape=[jax.ShapeDtypeStruct((s, d), F32), jax.ShapeDtypeStruct((s, E_MIX), BF16),
                   jax.ShapeDtypeStruct((s, E_MEM + E_BRANCH), BF16),
                   jax.ShapeDtypeStruct((E_BRANCH, d), BF16),
                   jax.ShapeDtypeStruct((N_MEM, 2 * E_MEM), F32),
                   jax.ShapeDtypeStruct((1, d), F32), jax.ShapeDtypeStruct((1, d), F32)],
        scratch_shapes=[pltpu.VMEM((E_BRANCH, d), F32)],
        compiler_params=_cparams("arbitrary"),
    )(dxn, r, g, mix, mem, h, h, h, h, kv, w_out)


def _loss_head(y, target):
    s, d = y.shape
    ts = 512

    def body(y_ref, t_ref, l_ref, dy_ref):
        @pl.when(pl.program_id(0) == 0)
        def _():
            l_ref[...] = jnp.zeros_like(l_ref)

        e = y_ref[...] - t_ref[...]
        dy_ref[...] = e * (1.0 / d)
        l_ref[...] += (0.5 / d) * jnp.sum(jnp.sum(e * e, axis=1, keepdims=True), axis=0, keepdims=True)

    return _call(
        body, name="loss_head", grid=(s // ts,),
        in_specs=[pl.BlockSpec((ts, d), lambda i: (i, 0))] * 2,
        out_specs=[pl.BlockSpec((1, 1), lambda i: (0, 0)), pl.BlockSpec((ts, d), lambda i: (i, 0))],
        out_shape=[jax.ShapeDtypeStruct((1, 1), F32), jax.ShapeDtypeStruct((s, d), F32)],
        compiler_params=_cparams("arbitrary"),
    )(y, target)


def _local_step(x, mem, get_weights, put_grads, rel_bias, conv_w, ln_g, ln_b, target):
    saved = []
    xl = x
    for layer in range(DEPTH):
        w_in_l, rest = get_weights(layer, xl)
        h, xb = _inproj(xl, w_in_l)
        w_kv_l, w_out_l = rest(h)
        if layer % 2 == 0:
            table = jnp.pad(rel_bias[layer // 2], ((0, 0), (0, N_REL_PAD - N_REL)))
            aux = _tile_bias(table)
            mix = _attn_fwd(h, aux)
        else:
            aux = jnp.pad(conv_w[layer // 2], ((0, SUBLANES - 3), (0, 0)))
            mix = _conv_fwd(h, aux)
        kv = _small_matmul(mem, w_kv_l, False, F32, "kv_mem")
        xn, r, mem_out = _layer_out_fwd(xl, h, mix, kv, w_out_l,
                                        ln_g[layer][None], ln_b[layer][None])
        saved.append((xb, h, aux, mix, kv, r, mem_out, w_in_l, w_out_l))
        xl = xn

    loss, dx = _loss_head(xl, target)

    dgs, dbs, d_rel, d_conv = [], [], [], []
    for layer in reversed(range(DEPTH)):
        xb, h, aux, mix, kv, r, mem_out, w_in_l, w_out_l = saved[layer]
        dx_res, d_mix, dqz, dwo, dkv, dg, db = _layer_out_bwd(
            dx, r, ln_g[layer][None], h, mix, mem_out, kv, w_out_l)
        if layer % 2 == 0:
            dq, dk, dv, dtb = _attn_bwd(h, aux, d_mix)
            d_rel.append(_tile_bias_bwd(dtb)[:, :N_REL])
            pieces = [dq, dk, dv, dqz]
        else:
            dbg, dcg, du, dw8 = _conv_bwd(h, aux, d_mix)
            d_conv.append(dw8[:3])
            pieces = [dbg, dcg, du, dqz]
        token = put_grads(layer, _dw_matmul(xb, pieces), _small_matmul(mem, dkv, True, BF16, "dw_kv"), dwo)
        dgs.append(dg[0])
        dbs.append(db[0])
        dx = _dx_matmul(pieces, w_in_l.T, dx_res, token)

    rev = lambda lst: jnp.stack(lst[::-1])
    return loss, dx, rev(d_rel), rev(d_conv), rev(dgs), rev(dbs)


def _me():
    return lax.axis_index("x"), lax.axis_index("y"), lax.axis_index("c")


def _peer(k):
    x, y, c = _me()
    kx, ky, kc = (k >> 2) & 1, (k >> 1) & 1, k & 1
    return (1 - x if kx else x, 1 - y if ky else y, 1 - c if kc else c)


def _lin(dev):
    return 4 * dev[0] + 2 * dev[1] + dev[2]


ANY = pl.BlockSpec(memory_space=pl.ANY)


def _exchange(srcs, dst_shapes, src_slice, dst_slice, name):
    na = len(srcs)

    def body(*refs):
        src_refs = refs[:na]
        dst_refs = refs[na:2 * na]
        send_sems, recv_sems, local_sems = refs[2 * na:]
        me = _lin(_me())
        copies = []
        for a in range(na):
            loc = pltpu.make_async_copy(src_slice(a, src_refs[a], me), dst_slice(a, dst_refs[a], me),
                                        local_sems.at[a])
            loc.start()
            copies.append(loc)
            for k in range(1, N_DEV):
                peer = _peer(k)
                cp = pltpu.make_async_remote_copy(
                    src_ref=src_slice(a, src_refs[a], _lin(peer)),
                    dst_ref=dst_slice(a, dst_refs[a], me),
                    send_sem=send_sems.at[a, k - 1], recv_sem=recv_sems.at[a, k - 1],
                    device_id=peer, device_id_type=pl.DeviceIdType.MESH)
                cp.start()
                copies.append(cp)
        for cp in copies:
            cp.wait()

    return _call(
        body, name=name,
        in_specs=[ANY] * na, out_specs=[ANY] * na,
        out_shape=[jax.ShapeDtypeStruct(shp, s.dtype) for shp, s in zip(dst_shapes, srcs)],
        scratch_shapes=[pltpu.SemaphoreType.DMA((na, N_DEV - 1)),
                        pltpu.SemaphoreType.DMA((na, N_DEV - 1)),
                        pltpu.SemaphoreType.DMA((na,))],
    )(*srcs)


def _gather_to_all(src, name):
    return _exchange([src], [(N_DEV,) + src.shape], lambda a, ref, p: ref,
                     lambda a, ref, me: ref.at[me], name)[0]


HBM = pl.BlockSpec(memory_space=pltpu.HBM)
SEM = pl.BlockSpec(memory_space=pltpu.SEMAPHORE)
EFFECT = pltpu.SideEffectType.DATAFLOW_SIDE_EFFECTING
N_PEER = N_DEV - 1
N_KIND = 3


def _peer_copies(kind, src_ref, land_ref, send, recv, src_slice, dst_slice):
    me = _lin(_me())
    copies = []
    for k in range(1, N_DEV):
        peer = _peer(k)
        copies.append(pltpu.make_async_remote_copy(
            src_ref=src_slice(kind, src_ref, _lin(peer)),
            dst_ref=dst_slice(kind, land_ref, me, k),
            send_sem=send.at[k - 1], recv_sem=recv.at[k - 1],
            device_id=peer, device_id_type=pl.DeviceIdType.MESH))
    return copies


def _own_copy(kind, src_ref, land_ref, send, src_slice, dst_slice):
    me = _lin(_me())
    return pltpu.make_async_copy(src_slice(kind, src_ref, me), dst_slice(kind, land_ref, me, 0),
                                 send.at[N_PEER])


def _split_start(srcs, kinds, land_shapes, src_slice, dst_slice, name, own=False):
    na = len(srcs)

    def body(*refs):
        src_refs, land_refs = refs[:na], refs[na:2 * na]
        sems = refs[2 * na:4 * na]
        token = refs[-1]
        for a in range(na):
            for cp in _peer_copies(kinds[a], src_refs[a], land_refs[a], sems[2 * a], sems[2 * a + 1],
                                   src_slice, dst_slice):
                cp.start()
            if own:
                _own_copy(kinds[a], src_refs[a], land_refs[a], sems[2 * a], src_slice, dst_slice).start()
        token[...] = jnp.zeros_like(token)

    sem_shape = pltpu.SemaphoreType.DMA((N_DEV,))
    lands = [lax.empty(shp, s.dtype) for shp, s in zip(land_shapes, srcs)]
    outs = _call(
        body, name=name,
        in_specs=[HBM] * (2 * na),
        out_specs=[SEM] * (2 * na) + [HBM] * (2 * na) + [pl.BlockSpec(memory_space=pltpu.VMEM)],
        out_shape=[sem_shape] * (2 * na)
        + [pltpu.HBM(s.shape, s.dtype) for s in srcs]
        + [pltpu.HBM(shp, s.dtype) for shp, s in zip(land_shapes, srcs)]
        + [jax.ShapeDtypeStruct((SUBLANES, LANES), F32)],
        input_output_aliases={i: 2 * na + i for i in range(2 * na)},
        compiler_params=pltpu.CompilerParams(has_side_effects=EFFECT),
    )(*[pltpu.with_memory_space_constraint(a, pltpu.HBM) for a in list(srcs) + lands])
    sems = [(outs[2 * a], outs[2 * a + 1]) for a in range(na)]
    thrus = outs[2 * na:3 * na]
    lands = outs[3 * na:4 * na]
    return sems, thrus, lands, outs[-1]


def _split_wait(sems, thrus, lands, kinds, src_slice, dst_slice, after, name, own=False):
    na = len(thrus)

    def body(*refs):
        src_refs, land_refs = refs[:na], refs[na:2 * na]
        sem_refs = refs[2 * na:4 * na]
        for a in range(na):
            for cp in _peer_copies(kinds[a], src_refs[a], land_refs[a], sem_refs[2 * a], sem_refs[2 * a + 1],
                                   src_slice, dst_slice):
                cp.wait_send()
                cp.wait_recv()
            if own:
                _own_copy(kinds[a], src_refs[a], land_refs[a], sem_refs[2 * a], src_slice, dst_slice).wait()

    outs = _call(
        body, name=name,
        in_specs=[HBM] * (2 * na) + [SEM] * (2 * na) + [ANY],
        out_specs=[HBM] * (2 * na),
        out_shape=[pltpu.HBM(a.shape, a.dtype) for a in list(thrus) + list(lands)],
        input_output_aliases={i: i for i in range(2 * na)},
        compiler_params=pltpu.CompilerParams(has_side_effects=EFFECT),
    )(*thrus, *lands, *[s for pair in sems for s in pair], after)
    return outs[na:]


def _shard_dims(c_in, r_kv, r_out):
    def sl(j, ref, p):
        if j == 0:
            return ref.at[:, pl.ds(pl.multiple_of(p * c_in, LANES), c_in)]
        r = r_kv if j == 1 else r_out
        return ref.at[pl.ds(pl.multiple_of(p * r, 2 * SUBLANES), r), :]
    return sl


def _adamw_math(w, g, m, v):
    m = ADAM_B1 * m + (1.0 - ADAM_B1) * g
    v = ADAM_B2 * v + (1.0 - ADAM_B2) * (g * g)
    m_hat = m / (1.0 - ADAM_B1 ** ADAM_STEP)
    v_hat = v / (1.0 - ADAM_B2 ** ADAM_STEP)
    delta = -ADAM_LR * (m_hat / (jnp.sqrt(v_hat) + ADAM_EPS) + ADAM_WD * w)
    return delta, m, v


def _reduce_adamw(parts, w, m, v, name):
    rows, cols = w.shape
    tr = rows
    for cand in (512, 256, 128, 64, 32, 16):
        if rows % cand == 0 and rows > cand:
            tr = cand
            break

    def body(p_ref, w_ref, m_ref, v_ref, g_out, d_out, m_out, v_out):
        g = p_ref[0].astype(F32)
        for s in range(1, N_DEV):
            g = g + p_ref[s].astype(F32)
        g_out[...] = g
        d_out[...], m_out[...], v_out[...] = _adamw_math(w_ref[...], g, m_ref[...], v_ref[...])

    blk = pl.BlockSpec((tr, cols), lambda i: (i, 0))
    return _call(
        body, name=name, grid=(rows // tr,),
        in_specs=[pl.BlockSpec((N_DEV, tr, cols), lambda i: (0, i, 0)), blk, blk, blk],
        out_specs=[blk] * 4,
        out_shape=[jax.ShapeDtypeStruct((rows, cols), F32)] * 4,
        compiler_params=_cparams("parallel"),
    )(parts, w, m, v)


def _reduce_adamw_layers(lands, owns, w, m, v, name):
    nl, rows, cols = w.shape
    tr = rows
    for cand in (256, 192, 128):
        if rows % cand == 0:
            tr = cand
            break

    def body(*refs):
        land_refs, own_refs = refs[:nl], refs[nl:2 * nl]
        w_ref, m_ref, v_ref, g_out, d_out, m_out, v_out = refs[2 * nl:]
        layer = pl.program_id(0)
        for a in range(nl):
            @pl.when(layer == a)
            def _(a=a):
                g = own_refs[a][...].astype(F32)
                for k in range(N_PEER):
                    g = g + land_refs[a][k].astype(F32)
                g_out[...] = g
                d_out[...], m_out[...], v_out[...] = _adamw_math(w_ref[...], g, m_ref[...], v_ref[...])

    def lmap(a):
        return lambda l, i: (0, jnp.where(l == a, i, 0), 0)

    def omap(a):
        return lambda l, i: (jnp.where(l == a, i, 0), 0)

    blk = pl.BlockSpec((None, tr, cols), lambda l, i: (l, i, 0))
    return _call(
        body, name=name, grid=(nl, rows // tr),
        in_specs=[pl.BlockSpec((N_PEER, tr, cols), lmap(a)) for a in range(nl)]
        + [pl.BlockSpec((tr, cols), omap(a)) for a in range(nl)] + [blk, blk, blk],
        out_specs=[blk] * 4,
        out_shape=[jax.ShapeDtypeStruct((nl, rows, cols), F32)] * 4,
        compiler_params=_cparams("arbitrary", "arbitrary"),
    )(*lands, *owns, w, m, v)


SM_G, SM_B, SM_CONV, SM_REL = 0, 4, 8, 16
SM_ROWS = SM_REL + 2 * N_HEADS
REL_W = 384


def _pack_small(d_rel, d_conv, dg, db):
    buf = jnp.zeros((SM_ROWS, D_MODEL), F32)
    buf = buf.at[SM_G:SM_G + DEPTH].set(dg)
    buf = buf.at[SM_B:SM_B + DEPTH].set(db)
    buf = buf.at[SM_CONV:SM_CONV + 6].set(d_conv.reshape(6, E_MIX))
    buf = buf.at[SM_REL:, :N_REL].set(d_rel.reshape(2 * N_HEADS, N_REL))
    return buf


def kernel(x, mem, w_in, w_mem_kv, w_out, rel_bias, conv_w, ln_g, ln_b, loss_target, m_w_in, m_w_mem_kv, m_w_out, m_rel_bias, m_conv_w, m_ln_g, m_ln_b, v_w_in, v_w_mem_kv, v_w_out, v_rel_bias, v_conv_w, v_ln_g, v_ln_b):
    me = _lin(_me())
    c_in, r_kv, r_out, c_conv = w_in.shape[2], w_mem_kv.shape[1], w_out.shape[1], conv_w.shape[2]

    shard = _shard_dims(c_in, r_kv, r_out)
    own_start = lambda j: (0, me * c_in) if j == 0 else (me * (r_kv if j == 1 else r_out), 0)

    w_sh = [w_in.astype(BF16), w_mem_kv.astype(BF16), w_out.astype(BF16)]
    full_shapes = [(D_MODEL, N_DEV * c_in), (N_DEV * r_kv, w_mem_kv.shape[2]), (N_DEV * r_out, D_MODEL)]
    ag_src = lambda j, ref, p: ref
    ag_dst = lambda j, ref, me_, k: shard(j, ref, me_)
    kinds = list(range(N_KIND))
    ag_sems, ag_thrus, ag_lands, _ = _split_start(
        [w_sh[j][layer] for layer in range(DEPTH) for j in kinds], kinds * DEPTH,
        full_shapes * DEPTH, ag_src, ag_dst, "ag_start", own=True)

    def get_weights(layer, x_layer):
        lo = layer * N_KIND

        def wait(js, after, name):
            idx = [lo + j for j in js]
            return _split_wait([ag_sems[a] for a in idx], [ag_thrus[a] for a in idx],
                               [ag_lands[a] for a in idx], js, ag_src, ag_dst, after, name, own=True)

        w_in_l, = wait([0], x_layer, "ag_wait_in_%d" % layer)
        return w_in_l, lambda h: wait([1, 2], h, "ag_wait_kv_out_%d" % layer)

    rs_src = shard
    rs_dst = lambda j, ref, me_, k: ref.at[k - 1]
    rs_shapes = [(N_PEER, D_MODEL, c_in), (N_PEER, r_kv, w_mem_kv.shape[2]), (N_PEER, r_out, D_MODEL)]
    own_sizes = [(D_MODEL, c_in), (r_kv, w_mem_kv.shape[2]), (r_out, D_MODEL)]
    pending = {}

    def put_grads(layer, dwi, dwkv, dwo):
        parts = [dwi, dwkv, dwo]
        owns = [lax.dynamic_slice(parts[j], own_start(j), own_sizes[j]) for j in range(N_KIND)]
        sems, thrus, lands, token = _split_start(parts, kinds, rs_shapes, rs_src, rs_dst, "rs_start_%d" % layer)
        pending[layer] = (sems, thrus, lands, owns)
        return token

    conv_tile = jnp.pad(conv_w.reshape(6, c_conv), ((0, SUBLANES - 6), (0, 0)))
    conv_land = _gather_to_all(conv_tile, "gather_conv")
    conv_f = jnp.transpose(conv_land[:, :6], (1, 0, 2)).reshape(2, 3, N_DEV * c_conv)

    loss, grad_x, d_rel, d_conv, dg, db = _local_step(
        x[0], mem[0], get_weights, put_grads, rel_bias, conv_f, ln_g, ln_b, loss_target[0])

    p_small = _gather_to_all(_pack_small(d_rel, d_conv, dg, db), "gather_small_grads")

    rs_lands, rs_owns = [], []
    for layer in range(DEPTH):
        sems, thrus, lands, owns = pending[layer]
        rs_lands.append(_split_wait(sems, thrus, lands, kinds, rs_src, rs_dst, grad_x, "rs_wait_%d" % layer))
        rs_owns.append(owns)

    def big(j, w, m, v, name):
        return _reduce_adamw_layers([rs_lands[layer][j] for layer in range(DEPTH)],
                                    [rs_owns[layer][j] for layer in range(DEPTH)], w, m, v, name)

    g_in, d_in, nm_in, nv_in = big(0, w_in, m_w_in, v_w_in, "adamw_w_in")
    g_kv, d_kv, nm_kv, nv_kv = big(1, w_mem_kv, m_w_mem_kv, v_w_mem_kv, "adamw_w_kv")
    g_out, d_out, nm_out, nv_out = big(2, w_out, m_w_out, v_w_out, "adamw_w_out")

    def pack_state(rel, conv, g, b):
        conv_full = jnp.zeros((2, 3, E_MIX), F32)
        conv_full = lax.dynamic_update_slice(conv_full, conv, (0, 0, me * c_conv))
        return _pack_small(rel, conv_full, g, b)

    sm_w = pack_state(rel_bias, conv_w, ln_g, ln_b)
    sm_m = pack_state(m_rel_bias, m_conv_w, m_ln_g, m_ln_b)
    sm_v = pack_state(v_rel_bias, v_conv_w, v_ln_g, v_ln_b)
    sm_outs = _reduce_adamw(p_small, sm_w, sm_m, sm_v, "adamw_small")

    def unpack(buf):
        rel = buf[SM_REL:, :N_REL].reshape(2, N_HEADS, N_REL)
        conv = lax.dynamic_slice(buf[SM_CONV:SM_CONV + 6].reshape(2, 3, E_MIX), (0, 0, me * c_conv), (2, 3, c_conv))
        return rel, conv, buf[SM_G:SM_G + DEPTH], buf[SM_B:SM_B + DEPTH]

    g_sm, d_sm, nm_sm, nv_sm = [unpack(b) for b in sm_outs]

    loss = lax.psum(loss[0, 0], ("x", "y", "c"))
    return (loss, grad_x[None],
            g_in, g_kv, g_out, *g_sm,
            d_in, d_kv, d_out, *d_sm,
            nm_in, nm_kv, nm_out, *nm_sm,
            nv_in, nv_kv, nv_out, *nv_sm)
```

```python
import functools
import math

import jax
import jax.numpy as jnp
from jax import lax
from jax.experimental import pallas as pl
from jax.experimental.pallas import tpu as pltpu

F32 = jnp.float32
BF16 = jnp.bfloat16
MXU_DTYPE = jnp.bfloat16

N_DEV = 8
D_MODEL = 1024
DEPTH = 4
CHUNK = 64
N_PREV = 8
N_HEADS = 16
HEAD_DIM = 64
E_MIX = 1024
REL_CLIP = 128
N_REL = 2 * REL_CLIP + 1
N_REL_PAD = 384
N_MEM = 256
MEM_HEADS = 4
MEM_HEAD_DIM = 128
E_MEM = 512
E_BRANCH = E_MIX + E_MEM
N_IN = 3 * E_MIX + E_MEM + E_BRANCH
DN_ALPHA = (2.0 * DEPTH) ** 0.25
LN_EPS = 1e-5
NEG = -1e30

ADAM_LR = 0.001
ADAM_B1 = 0.9
ADAM_B2 = 0.999
ADAM_EPS = 1e-08
ADAM_WD = 0.01
ADAM_STEP = 10

LANES = 128
SUBLANES = 8
VMEM_LIMIT = 56 * 1024 * 1024

TQ = 4 * CHUNK
TKEYS = 3 * TQ
ROLL_W = 1024
TS = 256
QM_BLK = 3 * E_MIX // E_MEM
Z_BLK = QM_BLK + 1


def _call(body, **kw):
    return pl.pallas_call(body, **kw)


def _cparams(*sem):
    return pltpu.CompilerParams(dimension_semantics=sem, vmem_limit_bytes=VMEM_LIMIT)


def _dot(a, b):
    return jnp.dot(a, b, preferred_element_type=F32)


def _dot_nt(a, b):
    return lax.dot_general(a, b, (((1,), (1,)), ((), ())), preferred_element_type=F32)


def _dot_tn(a, b):
    return lax.dot_general(a, b, (((0,), (0,)), ((), ())), preferred_element_type=F32)


def _inproj(x, w):
    s, d = x.shape
    n = w.shape[1]
    tm = 512
    tn = 1024

    def body(x_ref, w_ref, o_ref, xb_ref):
        xb = x_ref[...].astype(xb_ref.dtype)
        xb_ref[...] = xb
        for j in range(n // tn):
            o_ref[:, j * tn:(j + 1) * tn] = _dot(xb, w_ref[:, j * tn:(j + 1) * tn]).astype(o_ref.dtype)

    return _call(
        body, name="inproj", grid=(s // tm,),
        in_specs=[pl.BlockSpec((tm, d), lambda i: (i, 0)),
                  pl.BlockSpec((d, n), lambda i: (0, 0), pipeline_mode=pl.Buffered(1))],
        out_specs=[pl.BlockSpec((tm, n), lambda i: (i, 0)),
                   pl.BlockSpec((tm, d), lambda i: (i, 0))],
        out_shape=[jax.ShapeDtypeStruct((s, n), BF16), jax.ShapeDtypeStruct((s, d), BF16)],
        compiler_params=_cparams("parallel"),
    )(x, w)


def _small_matmul(a, b, trans_a, out_dtype, name):
    m = a.shape[1] if trans_a else a.shape[0]
    n = b.shape[1]

    def body(a_ref, b_ref, o_ref):
        av = a_ref[...].astype(MXU_DTYPE)
        bv = b_ref[...].astype(MXU_DTYPE)
        r = _dot_tn(av, bv) if trans_a else _dot(av, bv)
        o_ref[...] = r.astype(out_dtype)

    return _call(
        body, name=name,
        in_specs=[pl.BlockSpec(memory_space=pltpu.VMEM)] * 2,
        out_specs=pl.BlockSpec(memory_space=pltpu.VMEM),
        out_shape=jax.ShapeDtypeStruct((m, n), out_dtype),
        compiler_params=pltpu.CompilerParams(vmem_limit_bytes=VMEM_LIMIT),
    )(a, b)


def _piece_blocks(pieces, blk):
    offs, nbs, o = [], [], 0
    for p in pieces:
        nb = p.shape[1] // blk
        offs.append(o)
        nbs.append(nb)
        o += nb
    return offs, nbs, o


def _dx_matmul(pieces, wt, addend, token=None):
    s = pieces[0].shape[0]
    n_in, d = wt.shape
    tm = 512
    np_ = len(pieces)
    extra = [] if token is None else [token]

    def body(*refs):
        a_refs = refs[:np_]
        w_ref, add_ref = refs[np_:np_ + 2]
        o_ref = refs[-1]
        a = jnp.concatenate([r[...] for r in a_refs], axis=1)
        o_ref[...] = add_ref[...] + _dot(a, w_ref[...])

    in_specs = [pl.BlockSpec((tm, p.shape[1]), lambda i: (i, 0)) for p in pieces]
    in_specs += [pl.BlockSpec((n_in, d), lambda i: (0, 0), pipeline_mode=pl.Buffered(1)),
                 pl.BlockSpec((tm, d), lambda i: (i, 0))]
    in_specs += [pl.BlockSpec((SUBLANES, LANES), lambda i: (0, 0)) for _ in extra]
    return _call(
        body, name="dx_matmul", grid=(s // tm,),
        in_specs=in_specs,
        out_specs=pl.BlockSpec((tm, d), lambda i: (i, 0)),
        out_shape=jax.ShapeDtypeStruct((s, d), F32),
        compiler_params=_cparams("parallel"),
    )(*pieces, wt, addend, *extra)


def _dw_matmul(x, pieces):
    s, d = x.shape
    tn = 1024
    tk = min(1024, s)
    offs, nbs, nj = _piece_blocks(pieces, tn)
    np_ = len(pieces)
    nk = s // tk

    def body(*refs):
        x_ref = refs[0]
        b_refs = refs[1:1 + np_]
        o_ref, acc = refs[1 + np_:]
        j = pl.program_id(0)
        k = pl.program_id(1)

        @pl.when(k == 0)
        def _():
            acc[...] = jnp.zeros_like(acc)

        for p in range(np_):
            @pl.when((j >= offs[p]) & (j < offs[p] + nbs[p]))
            def _(p=p):
                acc[...] += _dot_tn(x_ref[...], b_refs[p][...])

        @pl.when(k == nk - 1)
        def _():
            o_ref[...] = acc[...].astype(o_ref.dtype)

    def bmap(p):
        def f(j, k):
            inside = (j >= offs[p]) & (j < offs[p] + nbs[p])
            return (jnp.where(inside, k, 0), jnp.clip(j - offs[p], 0, nbs[p] - 1))
        return f

    in_specs = [pl.BlockSpec((tk, d), lambda j, k: (k, 0))]
    in_specs += [pl.BlockSpec((tk, tn), bmap(p)) for p in range(np_)]
    return _call(
        body, name="dw_matmul", grid=(nj, nk),
        in_specs=in_specs,
        out_specs=pl.BlockSpec((d, tn), lambda j, k: (0, j)),
        out_shape=jax.ShapeDtypeStruct((d, nj * tn), BF16),
        scratch_shapes=[pltpu.VMEM((d, tn), F32)],
        compiler_params=_cparams("parallel", "arbitrary"),
    )(x, *pieces)


def _rel_onehot():
    j = lax.broadcasted_iota(jnp.int32, (N_REL_PAD, ROLL_W), 1)
    kk = lax.broadcasted_iota(jnp.int32, (N_REL_PAD, ROLL_W), 0)
    dd = jnp.where(j < TKEYS, j, j - ROLL_W)
    idx = jnp.clip(N_PREV * CHUNK - dd, -REL_CLIP, REL_CLIP) + REL_CLIP
    return jnp.where(idx == kk, 1.0, 0.0).astype(F32)


def _band_mask():
    r = lax.broadcasted_iota(jnp.int32, (TQ, TKEYS), 0) // CHUNK
    m = lax.broadcasted_iota(jnp.int32, (TQ, TKEYS), 1) // CHUNK
    return (m >= r) & (m <= r + N_PREV)


def _tile_bias(table_pad):
    def body(t_ref, o_ref):
        g = jnp.dot(t_ref[...], _rel_onehot(), preferred_element_type=F32,
                    precision=lax.Precision.HIGHEST)
        band = _band_mask()
        for h in range(N_HEADS):
            gh = jnp.broadcast_to(g[h:h + 1, :], (TQ, ROLL_W))
            rolled = pltpu.roll(gh, 0, 1, stride=1, stride_axis=0)
            o_ref[h] = jnp.where(band, rolled[:, :TKEYS], NEG)

    return _call(
        body, name="tile_bias",
        in_specs=[pl.BlockSpec(memory_space=pltpu.VMEM)],
        out_specs=pl.BlockSpec(memory_space=pltpu.VMEM),
        out_shape=jax.ShapeDtypeStruct((N_HEADS, TQ, TKEYS), F32),
        compiler_params=pltpu.CompilerParams(vmem_limit_bytes=VMEM_LIMIT),
    )(table_pad)


def _tile_bias_bwd(dtb):
    def body(d_ref, o_ref, g_ref):
        zpad = jnp.zeros((TQ, ROLL_W - TKEYS), F32)
        rr = lax.broadcasted_iota(jnp.int32, (TQ, TQ), 0)
        cc = lax.broadcasted_iota(jnp.int32, (TQ, TQ), 1)
        flip = jnp.where(rr + cc == TQ - 1, 1.0, 0.0).astype(F32)
        for h in range(N_HEADS):
            xh = jnp.concatenate([d_ref[h], zpad], axis=1)
            xf = jnp.dot(flip, xh, preferred_element_type=F32, precision=lax.Precision.HIGHEST)
            rolled = pltpu.roll(xf, 0, 1, stride=1, stride_axis=0)
            g_ref[h:h + 1, :] = jnp.sum(rolled, axis=0, keepdims=True)
        g = pltpu.roll(g_ref[...], ROLL_W - (TQ - 1), 1)
        o_ref[...] = lax.dot_general(g, _rel_onehot(), (((1,), (1,)), ((), ())),
                                     preferred_element_type=F32, precision=lax.Precision.HIGHEST)

    return _call(
        body, name="tile_bias_bwd",
        in_specs=[pl.BlockSpec(memory_space=pltpu.VMEM)],
        out_specs=pl.BlockSpec(memory_space=pltpu.VMEM),
        out_shape=jax.ShapeDtypeStruct((N_HEADS, N_REL_PAD), F32),
        scratch_shapes=[pltpu.VMEM((N_HEADS, ROLL_W), F32)],
        compiler_params=pltpu.CompilerParams(vmem_limit_bytes=VMEM_LIMIT),
    )(dtb)


HB = 4
HBW = HB * HEAD_DIM
ATTN_SCALE = 0.125
assert ATTN_SCALE == 1.0 / math.sqrt(HEAD_DIM)


def _head_masks():
    lane = lax.broadcasted_iota(jnp.int32, (1, HBW), 1) // HEAD_DIM
    return [lane == hh for hh in range(HB)]


def _select_heads(masks, parts):
    out = parts[-1]
    for hh in range(HB - 2, -1, -1):
        out = jnp.where(masks[hh], parts[hh], out)
    return out


def _attn_probs(qm, kcat, tb, valid):
    s = _dot_nt(qm, kcat) + tb
    if valid is not None:
        s = jnp.where(valid, s, NEG)
    m = jnp.max(s, axis=-1, keepdims=True)
    e = jnp.exp(s - m)
    return e * (1.0 / jnp.sum(e, axis=-1, keepdims=True))


def _key_valid(i):
    col = lax.broadcasted_iota(jnp.int32, (TQ, TKEYS), 1)
    return col >= jnp.maximum(2 - i, 0) * TQ


def _kv_specs(col0, nt):
    def spec(back):
        return pl.BlockSpec((TQ, HBW), lambda hp, i: (jnp.clip(i - back, 0, nt - 1), col0 + hp))
    return [spec(2), spec(1), spec(0)]


def _attn_fwd(h, tb):
    s = h.shape[0]
    nt = s // TQ
    nhp = N_HEADS // HB

    def body(q_ref, k0, k1, k2, v0, v1, v2, tb_ref, o_ref):
        i = pl.program_id(1)

        def tile(valid):
            masks = _head_masks()
            qs = q_ref[...].astype(MXU_DTYPE) * ATTN_SCALE
            kcat = jnp.concatenate([k0[...], k1[...], k2[...]], axis=0).astype(MXU_DTYPE)
            vcat = jnp.concatenate([v0[...], v1[...], v2[...]], axis=0).astype(MXU_DTYPE)
            outs = []
            for hh in range(HB):
                qm = jnp.where(masks[hh], qs, jnp.zeros_like(qs))
                p = _attn_probs(qm, kcat, tb_ref[hh], valid)
                outs.append(_dot(p.astype(MXU_DTYPE), vcat))
            o_ref[...] = _select_heads(masks, outs).astype(o_ref.dtype)

        @pl.when(i < 2)
        def _():
            tile(_key_valid(i))

        @pl.when(i >= 2)
        def _():
            tile(None)

    in_specs = [pl.BlockSpec((TQ, HBW), lambda hp, i: (i, hp))]
    in_specs += _kv_specs(nhp, nt) + _kv_specs(2 * nhp, nt)
    in_specs += [pl.BlockSpec((HB, TQ, TKEYS), lambda hp, i: (hp, 0, 0))]
    return _call(
        body, name="attn_fwd", grid=(nhp, nt),
        in_specs=in_specs,
        out_specs=pl.BlockSpec((TQ, HBW), lambda hp, i: (i, hp)),
        out_shape=jax.ShapeDtypeStruct((s, E_MIX), BF16),
        compiler_params=_cparams("parallel", "parallel"),
    )(h, h, h, h, h, h, h, tb)


def _attn_bwd(h, tb, d_mix):
    s = h.shape[0]
    nt = s // TQ
    nhp = N_HEADS // HB

    def body(q_ref, k0, k1, k2, v0, v1, v2, tb_ref, do_ref,
             dq_ref, dk_ref, dv_ref, dtb_ref, dk_acc, dv_acc):
        i = pl.program_id(1)

        @pl.when(i == 0)
        def _():
            dk_acc[...] = jnp.zeros_like(dk_acc)
            dv_acc[...] = jnp.zeros_like(dv_acc)
            dtb_ref[...] = jnp.zeros_like(dtb_ref)

        @pl.when((i > 0) & (i < nt))
        def _():
            dk_acc[i % 3] = jnp.zeros((TQ, HBW), F32)
            dv_acc[i % 3] = jnp.zeros((TQ, HBW), F32)

        def tile(valid):
            masks = _head_masks()
            qs = q_ref[...].astype(MXU_DTYPE) * ATTN_SCALE
            do2 = do_ref[...].astype(MXU_DTYPE)
            kcat = jnp.concatenate([k0[...], k1[...], k2[...]], axis=0).astype(MXU_DTYPE)
            vcat = jnp.concatenate([v0[...], v1[...], v2[...]], axis=0).astype(MXU_DTYPE)
            ks = kcat * ATTN_SCALE
            dqs, dks, dvs = [], [], []
            for hh in range(HB):
                qm = jnp.where(masks[hh], qs, jnp.zeros_like(qs))
                dom = jnp.where(masks[hh], do2, jnp.zeros_like(do2))
                p = _attn_probs(qm, kcat, tb_ref[hh], valid)
                dp = _dot_nt(dom, vcat)
                ds = p * (dp - jnp.sum(p * dp, axis=-1, keepdims=True))
                dtb_ref[hh] += ds
                dsb = ds.astype(MXU_DTYPE)
                dqs.append(_dot(dsb, ks))
                dks.append(_dot_tn(dsb, qs))
                dvs.append(_dot_tn(p.astype(MXU_DTYPE), do2))
            dq_ref[...] = _select_heads(masks, dqs).astype(dq_ref.dtype)
            dkc = _select_heads(masks, dks)
            dvc = _select_heads(masks, dvs)
            for jj in range(3):
                slot = (i + 1 + jj) % 3
                dk_acc[slot] += dkc[jj * TQ:(jj + 1) * TQ]
                dv_acc[slot] += dvc[jj * TQ:(jj + 1) * TQ]

        @pl.when(i < 2)
        def _():
            tile(_key_valid(i))

        @pl.when((i >= 2) & (i < nt))
        def _():
            tile(None)

        @pl.when(i >= 2)
        def _():
            slot = (i - 2) % 3
            dk_ref[...] = dk_acc[slot].astype(dk_ref.dtype)
            dv_ref[...] = dv_acc[slot].astype(dv_ref.dtype)

    qmap = lambda hp, i: (jnp.minimum(i, nt - 1), hp)
    kvout = lambda hp, i: (jnp.maximum(i - 2, 0), hp)
    in_specs = [pl.BlockSpec((TQ, HBW), qmap)]
    in_specs += _kv_specs(nhp, nt) + _kv_specs(2 * nhp, nt)
    in_specs += [pl.BlockSpec((HB, TQ, TKEYS), lambda hp, i: (hp, 0, 0)),
                 pl.BlockSpec((TQ, HBW), qmap)]
    blk = (TQ, HBW)
    return _call(
        body, name="attn_bwd", grid=(nhp, nt + 2),
        in_specs=in_specs,
        out_specs=[pl.BlockSpec(blk, qmap), pl.BlockSpec(blk, kvout), pl.BlockSpec(blk, kvout),
                   pl.BlockSpec((HB, TQ, TKEYS), lambda hp, i: (hp, 0, 0))],
        out_shape=[jax.ShapeDtypeStruct((s, E_MIX), BF16)] * 3
        + [jax.ShapeDtypeStruct((N_HEADS, TQ, TKEYS), F32)],
        scratch_shapes=[pltpu.VMEM((3, TQ, HBW), F32)] * 2,
        compiler_params=_cparams("parallel", "arbitrary"),
    )(h, h, h, h, h, h, h, tb, d_mix)


CONV_TS = 512
HALO = 2 * SUBLANES


def _shift_down(prev, cur, k):
    rolled = pltpu.roll(cur, k, 0)
    row = lax.broadcasted_iota(jnp.int32, (HALO, cur.shape[1]), 0)
    top = jnp.where(row < k, pltpu.roll(prev, k, 0), rolled[:HALO])
    return jnp.concatenate([top, rolled[HALO:]], axis=0)


def _shift_up(cur, nxt, k):
    ts = cur.shape[0]
    rolled = pltpu.roll(cur, ts - k, 0)
    row = lax.broadcasted_iota(jnp.int32, (HALO, cur.shape[1]), 0)
    bottom = jnp.where(row >= HALO - k, pltpu.roll(nxt, HALO - k, 0), rolled[ts - HALO:])
    return jnp.concatenate([rolled[:ts - HALO], bottom], axis=0)


def _conv_specs(ts, nb):
    tile = lambda c: pl.BlockSpec((ts, E_MIX), lambda i: (i, c))
    prev = lambda c: pl.BlockSpec((HALO, E_MIX), lambda i: (jnp.maximum(i * (ts // HALO) - 1, 0), c))
    return tile, prev


def _conv_fwd(h, w8):
    s = h.shape[0]
    ts = CONV_TS
    nb = s // ts
    tile, prev = _conv_specs(ts, nb)

    def body(bg, cg, u, cgp, up, w_ref, o_ref):
        i = pl.program_id(0)
        a = cg[...].astype(F32) * u[...].astype(F32)
        ap = jnp.where(i > 0, cgp[...].astype(F32) * up[...].astype(F32), 0.0)
        w = w_ref[...]
        conv = w[0:1] * _shift_down(ap, a, 2) + w[1:2] * _shift_down(ap, a, 1) + w[2:3] * a
        o_ref[...] = (bg[...].astype(F32) * conv).astype(o_ref.dtype)

    return _call(
        body, name="conv_fwd", grid=(nb,),
        in_specs=[tile(0), tile(1), tile(2), prev(1), prev(2),
                  pl.BlockSpec((SUBLANES, E_MIX), lambda i: (0, 0))],
        out_specs=pl.BlockSpec((ts, E_MIX), lambda i: (i, 0)),
        out_shape=jax.ShapeDtypeStruct((s, E_MIX), BF16),
        compiler_params=_cparams("parallel"),
    )(h, h, h, h, h, w8)


def _conv_bwd(h, w8, d_mix):
    s = h.shape[0]
    ts = CONV_TS
    nb = s // ts
    tile, prev = _conv_specs(ts, nb)
    nrow = s // HALO
    nxt = lambda c: pl.BlockSpec((HALO, E_MIX), lambda i: (jnp.minimum((i + 1) * (ts // HALO), nrow - 1), c))

    def body(bg, cg, u, cgp, up, bgn, dmix, dmixn, w_ref, dbg_ref, dcg_ref, du_ref, dw_ref):
        i = pl.program_id(0)

        @pl.when(i == 0)
        def _():
            dw_ref[...] = jnp.zeros_like(dw_ref)

        cgv, uv = cg[...].astype(F32), u[...].astype(F32)
        a = cgv * uv
        ap = jnp.where(i > 0, cgp[...].astype(F32) * up[...].astype(F32), 0.0)
        a1 = _shift_down(ap, a, 1)
        a2 = _shift_down(ap, a, 2)
        w = w_ref[...]
        conv = w[0:1] * a2 + w[1:2] * a1 + w[2:3] * a
        dm = dmix[...].astype(F32)
        dbg_ref[...] = (dm * conv).astype(dbg_ref.dtype)
        dc = dm * bg[...].astype(F32)
        dcn = jnp.where(i < nb - 1, dmixn[...].astype(F32) * bgn[...].astype(F32), 0.0)
        da = w[2:3] * dc + w[1:2] * _shift_up(dc, dcn, 1) + w[0:1] * _shift_up(dc, dcn, 2)
        dcg_ref[...] = (da * uv).astype(dcg_ref.dtype)
        du_ref[...] = (da * cgv).astype(du_ref.dtype)
        dw_ref[0:1, :] += jnp.sum(dc * a2, axis=0, keepdims=True)
        dw_ref[1:2, :] += jnp.sum(dc * a1, axis=0, keepdims=True)
        dw_ref[2:3, :] += jnp.sum(dc * a, axis=0, keepdims=True)

    full = lambda: pl.BlockSpec((ts, E_MIX), lambda i: (i, 0))
    return _call(
        body, name="conv_bwd", grid=(nb,),
        in_specs=[tile(0), tile(1), tile(2), prev(1), prev(2), nxt(0),
                  full(), pl.BlockSpec((HALO, E_MIX), lambda i: (jnp.minimum((i + 1) * (ts // HALO), nrow - 1), 0)),
                  pl.BlockSpec((SUBLANES, E_MIX), lambda i: (0, 0))],
        out_specs=[full(), full(), full(), pl.BlockSpec((SUBLANES, E_MIX), lambda i: (0, 0))],
        out_shape=[jax.ShapeDtypeStruct((s, E_MIX), BF16)] * 3
        + [jax.ShapeDtypeStruct((SUBLANES, E_MIX), F32)],
        compiler_params=_cparams("arbitrary"),
    )(h, h, h, h, h, h, d_mix, d_mix, w8)


def _mem_probs(qh, kh):
    s = _dot_nt(qh, kh) / math.sqrt(MEM_HEAD_DIM)
    m = jnp.max(s, axis=-1, keepdims=True)
    e = jnp.exp(s - m)
    return e / jnp.sum(e, axis=-1, keepdims=True)


def _sigmoid(z):
    return 1.0 / (1.0 + jnp.exp(-z))


def _layer_out_fwd(x, h, mix, kv, w_out, g, b):
    s, d = x.shape
    ts = 2 * TS

    def body(x_ref, mix_ref, qm_ref, z0, z1, z2, kv_ref, wo_ref, g_ref, b_ref,
             xn_ref, r_ref, mem_ref):
        qm = qm_ref[...].astype(MXU_DTYPE)
        kvb = kv_ref[...].astype(MXU_DTYPE)
        mems = []
        for hh in range(MEM_HEADS):
            lo = hh * MEM_HEAD_DIM
            p = _mem_probs(qm[:, lo:lo + MEM_HEAD_DIM], kvb[:, lo:lo + MEM_HEAD_DIM])
            mems.append(_dot(p.astype(MXU_DTYPE), kvb[:, E_MEM + lo:E_MEM + lo + MEM_HEAD_DIM]))
        mem = jnp.concatenate(mems, axis=1).astype(mem_ref.dtype)
        mem_ref[...] = mem
        mixv = mix_ref[...].astype(F32)
        half = E_MIX // 2
        parts = [mixv[:, :half], mixv[:, half:], mem.astype(F32)]
        out = jnp.zeros((ts, d), F32)
        for c, zr in enumerate((z0, z1, z2)):
            zv = zr[...].astype(F32)
            y = (parts[c] * (zv * _sigmoid(zv))).astype(MXU_DTYPE)
            out += _dot(y, wo_ref[c * half:(c + 1) * half, :])
        r = DN_ALPHA * x_ref[...] + out
        r_ref[...] = r
        mu = jnp.mean(r, axis=-1, keepdims=True)
        rc = r - mu
        var = jnp.mean(rc * rc, axis=-1, keepdims=True)
        xn_ref[...] = rc * lax.rsqrt(var + LN_EPS) * g_ref[...] + b_ref[...]

    row = lambda w, c: pl.BlockSpec((ts, w), lambda i: (i, c))
    const = lambda shp: pl.BlockSpec(shp, lambda i: (0, 0))
    return _call(
        body, name="layer_out_fwd", grid=(s // ts,),
        in_specs=[row(d, 0), row(E_MIX, 0), row(E_MEM, QM_BLK),
                  row(E_MEM, Z_BLK), row(E_MEM, Z_BLK + 1), row(E_MEM, Z_BLK + 2),
                  const((N_MEM, 2 * E_MEM)), const((E_BRANCH, d)), const((1, d)), const((1, d))],
        out_specs=[row(d, 0), row(d, 0), row(E_MEM, 0)],
        out_shape=[jax.ShapeDtypeStruct((s, d), F32), jax.ShapeDtypeStruct((s, d), F32),
                   jax.ShapeDtypeStruct((s, E_MEM), BF16)],
        compiler_params=_cparams("parallel"),
    )(x, mix, h, h, h, h, kv, w_out, g, b)


def _layer_out_bwd(dxn, r, g, h, mix, mem, kv, w_out):
    s, d = r.shape
    ts = 2 * TS
    nb = s // ts
    half = E_MIX // 2
    inv = 1.0 / math.sqrt(MEM_HEAD_DIM)

    def body(dxn_ref, r_ref, g_ref, mix_ref, mem_ref, qm_ref, z0, z1, z2, kv_ref, wo_ref,
             dxr_ref, dmix_ref, dqz_ref, dwo_ref, dkv_ref, dg_ref, db_ref, dw_acc):
        i = pl.program_id(0)

        @pl.when(i == 0)
        def _():
            dw_acc[...] = jnp.zeros_like(dw_acc)
            dkv_ref[...] = jnp.zeros_like(dkv_ref)
            dg_ref[...] = jnp.zeros_like(dg_ref)
            db_ref[...] = jnp.zeros_like(db_ref)

        dxn_v = dxn_ref[...]
        rv = r_ref[...]
        mu = jnp.mean(rv, axis=-1, keepdims=True)
        rc = rv - mu
        var = jnp.mean(rc * rc, axis=-1, keepdims=True)
        rstd = lax.rsqrt(var + LN_EPS)
        xhat = rc * rstd
        dg_ref[...] += jnp.sum(dxn_v * xhat, axis=0, keepdims=True)
        db_ref[...] += jnp.sum(dxn_v, axis=0, keepdims=True)
        dxh = dxn_v * g_ref[...]
        m1 = jnp.mean(dxh, axis=-1, keepdims=True)
        m2 = jnp.mean(dxh * xhat, axis=-1, keepdims=True)
        dr = rstd * (dxh - m1 - xhat * m2)
        dxr_ref[...] = DN_ALPHA * dr
        dout = dr.astype(MXU_DTYPE)
        mixv = mix_ref[...].astype(F32)
        parts = [mixv[:, :half], mixv[:, half:], mem_ref[...].astype(F32)]
        dcs = []
        for c, zr in enumerate((z0, z1, z2)):
            lo = c * half
            zv = zr[...].astype(F32)
            sg = _sigmoid(zv)
            sl = zv * sg
            dy = _dot_nt(dout, wo_ref[lo:lo + half, :])
            y = (parts[c] * sl).astype(MXU_DTYPE)
            dw_acc[lo:lo + half, :] += _dot_tn(y, dout)
            dcs.append(dy * sl)
            dqz_ref[:, E_MEM + lo:E_MEM + lo + half] = (
                dy * parts[c] * (sg * (1.0 + zv * (1.0 - sg)))).astype(dqz_ref.dtype)
        dmix_ref[...] = jnp.concatenate(dcs[:2], axis=1).astype(dmix_ref.dtype)

        qm = qm_ref[...].astype(MXU_DTYPE)
        kvb = kv_ref[...].astype(MXU_DTYPE)
        dmb = dcs[2].astype(MXU_DTYPE)
        for hh in range(MEM_HEADS):
            lo = hh * MEM_HEAD_DIM
            qh = qm[:, lo:lo + MEM_HEAD_DIM]
            kh = kvb[:, lo:lo + MEM_HEAD_DIM]
            vh = kvb[:, E_MEM + lo:E_MEM + lo + MEM_HEAD_DIM]
            dmh = dmb[:, lo:lo + MEM_HEAD_DIM]
            p = _mem_probs(qh, kh)
            dp = _dot_nt(dmh, vh)
            ds = p * (dp - jnp.sum(p * dp, axis=-1, keepdims=True))
            dsb = (ds * inv).astype(MXU_DTYPE)
            dqz_ref[:, lo:lo + MEM_HEAD_DIM] = _dot(dsb, kh).astype(dqz_ref.dtype)
            dkv_ref[:, lo:lo + MEM_HEAD_DIM] += _dot_tn(dsb, qh)
            dkv_ref[:, E_MEM + lo:E_MEM + lo + MEM_HEAD_DIM] += _dot_tn(p.astype(MXU_DTYPE), dmh)

        @pl.when(i == nb - 1)
        def _():
            dwo_ref[...] = dw_acc[...].astype(dwo_ref.dtype)

    row = lambda w, c: pl.BlockSpec((ts, w), lambda i: (i, c))
    const = lambda shp: pl.BlockSpec(shp, lambda i: (0, 0))
    once = lambda shp: pl.BlockSpec(shp, lambda i: (0, 0), pipeline_mode=pl.Buffered(1))
    return _call(
        body, name="layer_out_bwd", grid=(nb,),
        in_specs=[row(d, 0), row(d, 0), const((1, d)), row(E_MIX, 0), row(E_MEM, 0),
                  row(E_MEM, QM_BLK), row(E_MEM, Z_BLK), row(E_MEM, Z_BLK + 1), row(E_MEM, Z_BLK + 2),
                  once((N_MEM, 2 * E_MEM)), once((E_BRANCH, d))],
        out_specs=[row(d, 0), row(E_MIX, 0), row(E_MEM + E_BRANCH, 0),
                   const((E_BRANCH, d)), const((N_MEM, 2 * E_MEM)), const((1, d)), const((1, d))],
        out_shape=[jax.ShapeDtypeStruct((s, d), F32), jax.ShapeDtypeStruct((s, E_MIX), BF16),
                   jax.ShapeDtypeStruct((s, E_MEM + E_BRANCH), BF16),
                   jax.ShapeDtypeStruct((E_BRANCH, d), BF16),
                   jax.ShapeDtypeStruct((N_MEM, 2 * E_MEM), F32),
                   jax.ShapeDtypeStruct((1, d), F32), jax.ShapeDtypeStruct((1, d), F32)],
        scratch_shapes=[pltpu.VMEM((E_BRANCH, d), F32)],
        compiler_params=_cparams("arbitrary"),
    )(dxn, r, g, mix, mem, h, h, h, h, kv, w_out)


def _loss_head(y, target):
    s, d = y.shape
    ts = 512

    def body(y_ref, t_ref, l_ref, dy_ref):
        @pl.when(pl.program_id(0) == 0)
        def _():
            l_ref[...] = jnp.zeros_like(l_ref)

        e = y_ref[...] - t_ref[...]
        dy_ref[...] = e * (1.0 / d)
        l_ref[...] += (0.5 / d) * jnp.sum(jnp.sum(e * e, axis=1, keepdims=True), axis=0, keepdims=True)

    return _call(
        body, name="loss_head", grid=(s // ts,),
        in_specs=[pl.BlockSpec((ts, d), lambda i: (i, 0))] * 2,
        out_specs=[pl.BlockSpec((1, 1), lambda i: (0, 0)), pl.BlockSpec((ts, d), lambda i: (i, 0))],
        out_shape=[jax.ShapeDtypeStruct((1, 1), F32), jax.ShapeDtypeStruct((s, d), F32)],
        compiler_params=_cparams("arbitrary"),
    )(y, target)


def _local_step(x, mem, get_weights, put_grads, rel_bias, conv_w, ln_g, ln_b, target):
    saved = []
    xl = x
    for layer in range(DEPTH):
        w_in_l, rest = get_weights(layer, xl)
        h, xb = _inproj(xl, w_in_l)
        w_kv_l, w_out_l = rest(h)
        if layer % 2 == 0:
            table = jnp.pad(rel_bias[layer // 2], ((0, 0), (0, N_REL_PAD - N_REL)))
            aux = _tile_bias(table)
            mix = _attn_fwd(h, aux)
        else:
            aux = jnp.pad(conv_w[layer // 2], ((0, SUBLANES - 3), (0, 0)))
            mix = _conv_fwd(h, aux)
        kv = _small_matmul(mem, w_kv_l, False, F32, "kv_mem")
        xn, r, mem_out = _layer_out_fwd(xl, h, mix, kv, w_out_l,
                                        ln_g[layer][None], ln_b[layer][None])
        saved.append((xb, h, aux, mix, kv, r, mem_out, w_in_l, w_out_l))
        xl = xn

    loss, dx = _loss_head(xl, target)

    dgs, dbs, d_rel, d_conv = [], [], [], []
    for layer in reversed(range(DEPTH)):
        xb, h, aux, mix, kv, r, mem_out, w_in_l, w_out_l = saved[layer]
        dx_res, d_mix, dqz, dwo, dkv, dg, db = _layer_out_bwd(
            dx, r, ln_g[layer][None], h, mix, mem_out, kv, w_out_l)
        put_grads(layer, [1, 2], [_small_matmul(mem, dkv, True, BF16, "dw_kv"), dwo])
        if layer % 2 == 0:
            dq, dk, dv, dtb = _attn_bwd(h, aux, d_mix)
            d_rel.append(_tile_bias_bwd(dtb)[:, :N_REL])
            pieces = [dq, dk, dv, dqz]
        else:
            dbg, dcg, du, dw8 = _conv_bwd(h, aux, d_mix)
            d_conv.append(dw8[:3])
            pieces = [dbg, dcg, du, dqz]
        token = put_grads(layer, [0], [_dw_matmul(xb, pieces)])
        dgs.append(dg[0])
        dbs.append(db[0])
        dx = _dx_matmul(pieces, w_in_l.T, dx_res, token)

    rev = lambda lst: jnp.stack(lst[::-1])
    return loss, dx, rev(d_rel), rev(d_conv), rev(dgs), rev(dbs)


def _me():
    return lax.axis_index("x"), lax.axis_index("y"), lax.axis_index("c")


def _peer(k):
    x, y, c = _me()
    kx, ky, kc = (k >> 2) & 1, (k >> 1) & 1, k & 1
    return (1 - x if kx else x, 1 - y if ky else y, 1 - c if kc else c)


def _lin(dev):
    return 4 * dev[0] + 2 * dev[1] + dev[2]


ANY = pl.BlockSpec(memory_space=pl.ANY)


def _exchange(srcs, dst_shapes, src_slice, dst_slice, name):
    na = len(srcs)

    def body(*refs):
        src_refs = refs[:na]
        dst_refs = refs[na:2 * na]
        send_sems, recv_sems, local_sems = refs[2 * na:]
        me = _lin(_me())
        copies = []
        for a in range(na):
            loc = pltpu.make_async_copy(src_slice(a, src_refs[a], me), dst_slice(a, dst_refs[a], me),
                                        local_sems.at[a])
            loc.start()
            copies.append(loc)
            for k in range(1, N_DEV):
                peer = _peer(k)
                cp = pltpu.make_async_remote_copy(
                    src_ref=src_slice(a, src_refs[a], _lin(peer)),
                    dst_ref=dst_slice(a, dst_refs[a], me),
                    send_sem=send_sems.at[a, k - 1], recv_sem=recv_sems.at[a, k - 1],
                    device_id=peer, device_id_type=pl.DeviceIdType.MESH)
                cp.start()
                copies.append(cp)
        for cp in copies:
            cp.wait()

    return _call(
        body, name=name,
        in_specs=[ANY] * na, out_specs=[ANY] * na,
        out_shape=[jax.ShapeDtypeStruct(shp, s.dtype) for shp, s in zip(dst_shapes, srcs)],
        scratch_shapes=[pltpu.SemaphoreType.DMA((na, N_DEV - 1)),
                        pltpu.SemaphoreType.DMA((na, N_DEV - 1)),
                        pltpu.SemaphoreType.DMA((na,))],
    )(*srcs)


def _gather_to_all(src, name):
    return _exchange([src], [(N_DEV,) + src.shape], lambda a, ref, p: ref,
                     lambda a, ref, me: ref.at[me], name)[0]


HBM = pl.BlockSpec(memory_space=pltpu.HBM)
SEM = pl.BlockSpec(memory_space=pltpu.SEMAPHORE)
EFFECT = pltpu.SideEffectType.DATAFLOW_SIDE_EFFECTING
N_PEER = N_DEV - 1
N_KIND = 3


def _peer_copies(kind, src_ref, land_ref, send, recv, src_slice, dst_slice):
    me = _lin(_me())
    copies = []
    for k in range(1, N_DEV):
        peer = _peer(k)
        copies.append(pltpu.make_async_remote_copy(
            src_ref=src_slice(kind, src_ref, _lin(peer)),
            dst_ref=dst_slice(kind, land_ref, me, k),
            send_sem=send.at[k - 1], recv_sem=recv.at[k - 1],
            device_id=peer, device_id_type=pl.DeviceIdType.MESH))
    return copies


def _own_copy(kind, src_ref, land_ref, send, src_slice, dst_slice):
    me = _lin(_me())
    return pltpu.make_async_copy(src_slice(kind, src_ref, me), dst_slice(kind, land_ref, me, 0),
                                 send.at[N_PEER])


def _split_start(srcs, kinds, land_shapes, src_slice, dst_slice, name, own=False):
    na = len(srcs)

    def body(*refs):
        src_refs, land_refs = refs[:na], refs[na:2 * na]
        sems = refs[2 * na:4 * na]
        token = refs[-1]
        for a in range(na):
            for cp in _peer_copies(kinds[a], src_refs[a], land_refs[a], sems[2 * a], sems[2 * a + 1],
                                   src_slice, dst_slice):
                cp.start()
            if own:
                _own_copy(kinds[a], src_refs[a], land_refs[a], sems[2 * a], src_slice, dst_slice).start()
        token[...] = jnp.zeros_like(token)

    sem_shape = pltpu.SemaphoreType.DMA((N_DEV,))
    lands = [lax.empty(shp, s.dtype) for shp, s in zip(land_shapes, srcs)]
    outs = _call(
        body, name=name,
        in_specs=[HBM] * (2 * na),
        out_specs=[SEM] * (2 * na) + [HBM] * (2 * na) + [pl.BlockSpec(memory_space=pltpu.VMEM)],
        out_shape=[sem_shape] * (2 * na)
        + [pltpu.HBM(s.shape, s.dtype) for s in srcs]
        + [pltpu.HBM(shp, s.dtype) for shp, s in zip(land_shapes, srcs)]
        + [jax.ShapeDtypeStruct((SUBLANES, LANES), F32)],
        input_output_aliases={i: 2 * na + i for i in range(2 * na)},
        compiler_params=pltpu.CompilerParams(has_side_effects=EFFECT),
    )(*[pltpu.with_memory_space_constraint(a, pltpu.HBM) for a in list(srcs) + lands])
    sems = [(outs[2 * a], outs[2 * a + 1]) for a in range(na)]
    thrus = outs[2 * na:3 * na]
    lands = outs[3 * na:4 * na]
    return sems, thrus, lands, outs[-1]


def _split_wait(sems, thrus, lands, kinds, src_slice, dst_slice, after, name, own=False):
    na = len(thrus)

    def body(*refs):
        src_refs, land_refs = refs[:na], refs[na:2 * na]
        sem_refs = refs[2 * na:4 * na]
        for a in range(na):
            for cp in _peer_copies(kinds[a], src_refs[a], land_refs[a], sem_refs[2 * a], sem_refs[2 * a + 1],
                                   src_slice, dst_slice):
                cp.wait_send()
                cp.wait_recv()
            if own:
                _own_copy(kinds[a], src_refs[a], land_refs[a], sem_refs[2 * a], src_slice, dst_slice).wait()

    outs = _call(
        body, name=name,
        in_specs=[HBM] * (2 * na) + [SEM] * (2 * na) + [ANY],
        out_specs=[HBM] * (2 * na),
        out_shape=[pltpu.HBM(a.shape, a.dtype) for a in list(thrus) + list(lands)],
        input_output_aliases={i: i for i in range(2 * na)},
        compiler_params=pltpu.CompilerParams(has_side_effects=EFFECT),
    )(*thrus, *lands, *[s for pair in sems for s in pair], after)
    return outs[na:]


def _shard_dims(c_in, r_kv, r_out):
    def sl(j, ref, p):
        if j == 0:
            return ref.at[:, pl.ds(pl.multiple_of(p * c_in, LANES), c_in)]
        r = r_kv if j == 1 else r_out
        return ref.at[pl.ds(pl.multiple_of(p * r, 2 * SUBLANES), r), :]
    return sl


def _adamw_math(w, g, m, v):
    m = ADAM_B1 * m + (1.0 - ADAM_B1) * g
    v = ADAM_B2 * v + (1.0 - ADAM_B2) * (g * g)
    m_hat = m / (1.0 - ADAM_B1 ** ADAM_STEP)
    v_hat = v / (1.0 - ADAM_B2 ** ADAM_STEP)
    delta = -ADAM_LR * (m_hat / (jnp.sqrt(v_hat) + ADAM_EPS) + ADAM_WD * w)
    return delta, m, v


def _reduce_adamw(parts, w, m, v, name):
    rows, cols = w.shape
    tr = rows
    for cand in (512, 256, 128, 64, 32, 16):
        if rows % cand == 0 and rows > cand:
            tr = cand
            break

    def body(p_ref, w_ref, m_ref, v_ref, g_out, d_out, m_out, v_out):
        g = p_ref[0].astype(F32)
        for s in range(1, N_DEV):
            g = g + p_ref[s].astype(F32)
        g_out[...] = g
        d_out[...], m_out[...], v_out[...] = _adamw_math(w_ref[...], g, m_ref[...], v_ref[...])

    blk = pl.BlockSpec((tr, cols), lambda i: (i, 0))
    return _call(
        body, name=name, grid=(rows // tr,),
        in_specs=[pl.BlockSpec((N_DEV, tr, cols), lambda i: (0, i, 0)), blk, blk, blk],
        out_specs=[blk] * 4,
        out_shape=[jax.ShapeDtypeStruct((rows, cols), F32)] * 4,
        compiler_params=_cparams("parallel"),
    )(parts, w, m, v)


def _reduce_adamw_layers(lands, owns, w, m, v, name):
    nl, rows, cols = w.shape
    tr = rows
    for cand in (256, 192, 128):
        if rows % cand == 0:
            tr = cand
            break

    def body(*refs):
        land_refs, own_refs = refs[:nl], refs[nl:2 * nl]
        w_ref, m_ref, v_ref, g_out, d_out, m_out, v_out = refs[2 * nl:]
        layer = pl.program_id(0)
        for a in range(nl):
            @pl.when(layer == a)
            def _(a=a):
                g = own_refs[a][...].astype(F32)
                for k in range(N_PEER):
                    g = g + land_refs[a][k].astype(F32)
                g_out[...] = g
                d_out[...], m_out[...], v_out[...] = _adamw_math(w_ref[...], g, m_ref[...], v_ref[...])

    def lmap(a):
        return lambda l, i: (0, jnp.where(l == a, i, 0), 0)

    def omap(a):
        return lambda l, i: (jnp.where(l == a, i, 0), 0)

    blk = pl.BlockSpec((None, tr, cols), lambda l, i: (l, i, 0))
    return _call(
        body, name=name, grid=(nl, rows // tr),
        in_specs=[pl.BlockSpec((N_PEER, tr, cols), lmap(a)) for a in range(nl)]
        + [pl.BlockSpec((tr, cols), omap(a)) for a in range(nl)] + [blk, blk, blk],
        out_specs=[blk] * 4,
        out_shape=[jax.ShapeDtypeStruct((nl, rows, cols), F32)] * 4,
        compiler_params=_cparams("arbitrary", "arbitrary"),
    )(*lands, *owns, w, m, v)


SM_G, SM_B, SM_CONV, SM_REL = 0, 4, 8, 16
SM_ROWS = SM_REL + 2 * N_HEADS
REL_W = 384


def _pack_small(d_rel, d_conv, dg, db):
    buf = jnp.zeros((SM_ROWS, D_MODEL), F32)
    buf = buf.at[SM_G:SM_G + DEPTH].set(dg)
    buf = buf.at[SM_B:SM_B + DEPTH].set(db)
    buf = buf.at[SM_CONV:SM_CONV + 6].set(d_conv.reshape(6, E_MIX))
    buf = buf.at[SM_REL:, :N_REL].set(d_rel.reshape(2 * N_HEADS, N_REL))
    return buf


def kernel(x, mem, w_in, w_mem_kv, w_out, rel_bias, conv_w, ln_g, ln_b, loss_target, m_w_in, m_w_mem_kv, m_w_out, m_rel_bias, m_conv_w, m_ln_g, m_ln_b, v_w_in, v_w_mem_kv, v_w_out, v_rel_bias, v_conv_w, v_ln_g, v_ln_b):
    me = _lin(_me())
    c_in, r_kv, r_out, c_conv = w_in.shape[2], w_mem_kv.shape[1], w_out.shape[1], conv_w.shape[2]

    shard = _shard_dims(c_in, r_kv, r_out)
    own_start = lambda j: (0, me * c_in) if j == 0 else (me * (r_kv if j == 1 else r_out), 0)

    w_sh = [w_in.astype(BF16), w_mem_kv.astype(BF16), w_out.astype(BF16)]
    full_shapes = [(D_MODEL, N_DEV * c_in), (N_DEV * r_kv, w_mem_kv.shape[2]), (N_DEV * r_out, D_MODEL)]
    ag_src = lambda j, ref, p: ref
    ag_dst = lambda j, ref, me_, k: shard(j, ref, me_)
    kinds = list(range(N_KIND))
    ag_sems, ag_thrus, ag_lands, _ = _split_start(
        [w_sh[j][layer] for layer in range(DEPTH) for j in kinds], kinds * DEPTH,
        full_shapes * DEPTH, ag_src, ag_dst, "ag_start", own=True)

    def get_weights(layer, x_layer):
        lo = layer * N_KIND

        def wait(js, after, name):
            idx = [lo + j for j in js]
            return _split_wait([ag_sems[a] for a in idx], [ag_thrus[a] for a in idx],
                               [ag_lands[a] for a in idx], js, ag_src, ag_dst, after, name, own=True)

        w_in_l, = wait([0], x_layer, "ag_wait_in_%d" % layer)
        return w_in_l, lambda h: wait([1, 2], h, "ag_wait_kv_out_%d" % layer)

    rs_src = shard
    rs_dst = lambda j, ref, me_, k: ref.at[k - 1]
    rs_shapes = [(N_PEER, D_MODEL, c_in), (N_PEER, r_kv, w_mem_kv.shape[2]), (N_PEER, r_out, D_MODEL)]
    own_sizes = [(D_MODEL, c_in), (r_kv, w_mem_kv.shape[2]), (r_out, D_MODEL)]
    pending = {}

    held = {}

    def put_grads(layer, js, arrays):
        if layer > 0 and js != [0]:
            held[layer] = (js, arrays)
            return None
        if layer > 0:
            js, arrays = held[layer][0] + js, held[layer][1] + arrays
        owns = [lax.dynamic_slice(a, own_start(j), own_sizes[j]) for j, a in zip(js, arrays)]
        sems, thrus, lands, token = _split_start(
            arrays, js, [rs_shapes[j] for j in js], rs_src, rs_dst,
            "rs_start_%d_%s" % (layer, "".join(str(j) for j in js)))
        entry = pending.setdefault(layer, ([], [], [], [], []))
        for lst, new in zip(entry, (js, sems, thrus, lands, owns)):
            lst.extend(new)
        return token

    conv_tile = jnp.pad(conv_w.reshape(6, c_conv), ((0, SUBLANES - 6), (0, 0)))
    conv_land = _gather_to_all(conv_tile, "gather_conv")
    conv_f = jnp.transpose(conv_land[:, :6], (1, 0, 2)).reshape(2, 3, N_DEV * c_conv)

    loss, grad_x, d_rel, d_conv, dg, db = _local_step(
        x[0], mem[0], get_weights, put_grads, rel_bias, conv_f, ln_g, ln_b, loss_target[0])

    p_small = _gather_to_all(_pack_small(d_rel, d_conv, dg, db), "gather_small_grads")

    rs_lands, rs_owns = [], []
    for layer in range(DEPTH):
        js, sems, thrus, lands, owns = pending[layer]
        lands = _split_wait(sems, thrus, lands, js, rs_src, rs_dst, grad_x, "rs_wait_%d" % layer)
        by_kind = lambda vals: [vals[js.index(j)] for j in kinds]
        rs_lands.append(by_kind(lands))
        rs_owns.append(by_kind(owns))

    def big(j, w, m, v, name):
        return _reduce_adamw_layers([rs_lands[layer][j] for layer in range(DEPTH)],
                                    [rs_owns[layer][j] for layer in range(DEPTH)], w, m, v, name)

    g_in, d_in, nm_in, nv_in = big(0, w_in, m_w_in, v_w_in, "adamw_w_in")
    g_kv, d_kv, nm_kv, nv_kv = big(1, w_mem_kv, m_w_mem_kv, v_w_mem_kv, "adamw_w_kv")
    g_out, d_out, nm_out, nv_out = big(2, w_out, m_w_out, v_w_out, "adamw_w_out")

    def pack_state(rel, conv, g, b):
        conv_full = jnp.zeros((2, 3, E_MIX), F32)
        conv_full = lax.dynamic_update_slice(conv_full, conv, (0, 0, me * c_conv))
        return _pack_small(rel, conv_full, g, b)

    sm_w = pack_state(rel_bias, conv_w, ln_g, ln_b)
    sm_m = pack_state(m_rel_bias, m_conv_w, m_ln_g, m_ln_b)
    sm_v = pack_state(v_rel_bias, v_conv_w, v_ln_g, v_ln_b)
    sm_outs = _reduce_adamw(p_small, sm_w, sm_m, sm_v, "adamw_small")

    def unpack(buf):
        rel = buf[SM_REL:, :N_REL].reshape(2, N_HEADS, N_REL)
        conv = lax.dynamic_slice(buf[SM_CONV:SM_CONV + 6].reshape(2, 3, E_MIX), (0, 0, me * c_conv), (2, 3, c_conv))
        return rel, conv, buf[SM_G:SM_G + DEPTH], buf[SM_B:SM_B + DEPTH]

    g_sm, d_sm, nm_sm, nv_sm = [unpack(b) for b in sm_outs]

    loss = lax.psum(loss[0, 0], ("x", "y", "c"))
    return (loss, grad_x[None],
            g_in, g_kv, g_out, *g_sm,
            d_in, d_kv, d_out, *d_sm,
            nm_in, nm_kv, nm_out, *nm_sm,
            nv_in, nv_kv, nv_out, *nv_sm)
```

```python
import functools
import math

import jax
import jax.numpy as jnp
from jax import lax
from jax.experimental import pallas as pl
from jax.experimental.pallas import tpu as pltpu

F32 = jnp.float32
BF16 = jnp.bfloat16
MXU_DTYPE = jnp.bfloat16

N_DEV = 8
D_MODEL = 1024
DEPTH = 4
CHUNK = 64
N_PREV = 8
N_HEADS = 16
HEAD_DIM = 64
E_MIX = 1024
REL_CLIP = 128
N_REL = 2 * REL_CLIP + 1
N_REL_PAD = 384
N_MEM = 256
MEM_HEADS = 4
MEM_HEAD_DIM = 128
E_MEM = 512
E_BRANCH = E_MIX + E_MEM
N_IN = 3 * E_MIX + E_MEM + E_BRANCH
DN_ALPHA = (2.0 * DEPTH) ** 0.25
LN_EPS = 1e-5
NEG = -1e30

ADAM_LR = 0.001
ADAM_B1 = 0.9
ADAM_B2 = 0.999
ADAM_EPS = 1e-08
ADAM_WD = 0.01
ADAM_STEP = 10

LANES = 128
SUBLANES = 8
VMEM_LIMIT = 56 * 1024 * 1024

TQ = 4 * CHUNK
TKEYS = 3 * TQ
ROLL_W = 1024
TS = 256
QM_BLK = 3 * E_MIX // E_MEM
Z_BLK = QM_BLK + 1


def _call(body, **kw):
    return pl.pallas_call(body, **kw)


def _cparams(*sem):
    return pltpu.CompilerParams(dimension_semantics=sem, vmem_limit_bytes=VMEM_LIMIT)


def _dot(a, b):
    return jnp.dot(a, b, preferred_element_type=F32)


def _dot_nt(a, b):
    return lax.dot_general(a, b, (((1,), (1,)), ((), ())), preferred_element_type=F32)


def _dot_tn(a, b):
    return lax.dot_general(a, b, (((0,), (0,)), ((), ())), preferred_element_type=F32)


def _inproj(x, w):
    s, d = x.shape
    n = w.shape[1]
    tm = 512
    tn = 1024

    def body(x_ref, w_ref, o_ref, xb_ref):
        xb = x_ref[...].astype(xb_ref.dtype)
        xb_ref[...] = xb
        for j in range(n // tn):
            o_ref[:, j * tn:(j + 1) * tn] = _dot(xb, w_ref[:, j * tn:(j + 1) * tn]).astype(o_ref.dtype)

    return _call(
        body, name="inproj", grid=(s // tm,),
        in_specs=[pl.BlockSpec((tm, d), lambda i: (i, 0)),
                  pl.BlockSpec((d, n), lambda i: (0, 0), pipeline_mode=pl.Buffered(1))],
        out_specs=[pl.BlockSpec((tm, n), lambda i: (i, 0)),
                   pl.BlockSpec((tm, d), lambda i: (i, 0))],
        out_shape=[jax.ShapeDtypeStruct((s, n), BF16), jax.ShapeDtypeStruct((s, d), BF16)],
        compiler_params=_cparams("parallel"),
    )(x, w)


def _small_matmul(a, b, trans_a, out_dtype, name):
    m = a.shape[1] if trans_a else a.shape[0]
    n = b.shape[1]

    def body(a_ref, b_ref, o_ref):
        av = a_ref[...].astype(MXU_DTYPE)
        bv = b_ref[...].astype(MXU_DTYPE)
        r = _dot_tn(av, bv) if trans_a else _dot(av, bv)
        o_ref[...] = r.astype(out_dtype)

    return _call(
        body, name=name,
        in_specs=[pl.BlockSpec(memory_space=pltpu.VMEM)] * 2,
        out_specs=pl.BlockSpec(memory_space=pltpu.VMEM),
        out_shape=jax.ShapeDtypeStruct((m, n), out_dtype),
        compiler_params=pltpu.CompilerParams(vmem_limit_bytes=VMEM_LIMIT),
    )(a, b)


def _piece_blocks(pieces, blk):
    offs, nbs, o = [], [], 0
    for p in pieces:
        nb = p.shape[1] // blk
        offs.append(o)
        nbs.append(nb)
        o += nb
    return offs, nbs, o


def _dx_matmul(pieces, wt, addend, token=None):
    s = pieces[0].shape[0]
    n_in, d = wt.shape
    tm = 512
    np_ = len(pieces)
    extra = [] if token is None else [token]

    def body(*refs):
        a_refs = refs[:np_]
        w_ref, add_ref = refs[np_:np_ + 2]
        o_ref = refs[-1]
        a = jnp.concatenate([r[...] for r in a_refs], axis=1)
        o_ref[...] = add_ref[...] + _dot(a, w_ref[...])

    in_specs = [pl.BlockSpec((tm, p.shape[1]), lambda i: (i, 0)) for p in pieces]
    in_specs += [pl.BlockSpec((n_in, d), lambda i: (0, 0), pipeline_mode=pl.Buffered(1)),
                 pl.BlockSpec((tm, d), lambda i: (i, 0))]
    in_specs += [pl.BlockSpec((SUBLANES, LANES), lambda i: (0, 0)) for _ in extra]
    return _call(
        body, name="dx_matmul", grid=(s // tm,),
        in_specs=in_specs,
        out_specs=pl.BlockSpec((tm, d), lambda i: (i, 0)),
        out_shape=jax.ShapeDtypeStruct((s, d), F32),
        compiler_params=_cparams("parallel"),
    )(*pieces, wt, addend, *extra)


def _dw_matmul(x, pieces):
    s, d = x.shape
    tn = 1024
    tk = min(1024, s)
    offs, nbs, nj = _piece_blocks(pieces, tn)
    np_ = len(pieces)
    nk = s // tk

    def body(*refs):
        x_ref = refs[0]
        b_refs = refs[1:1 + np_]
        o_ref, acc = refs[1 + np_:]
        j = pl.program_id(0)
        k = pl.program_id(1)

        @pl.when(k == 0)
        def _():
            acc[...] = jnp.zeros_like(acc)

        for p in range(np_):
            @pl.when((j >= offs[p]) & (j < offs[p] + nbs[p]))
            def _(p=p):
                acc[...] += _dot_tn(x_ref[...], b_refs[p][...])

        @pl.when(k == nk - 1)
        def _():
            o_ref[...] = acc[...].astype(o_ref.dtype)

    def bmap(p):
        def f(j, k):
            inside = (j >= offs[p]) & (j < offs[p] + nbs[p])
            return (jnp.where(inside, k, 0), jnp.clip(j - offs[p], 0, nbs[p] - 1))
        return f

    in_specs = [pl.BlockSpec((tk, d), lambda j, k: (k, 0))]
    in_specs += [pl.BlockSpec((tk, tn), bmap(p)) for p in range(np_)]
    return _call(
        body, name="dw_matmul", grid=(nj, nk),
        in_specs=in_specs,
        out_specs=pl.BlockSpec((d, tn), lambda j, k: (0, j)),
        out_shape=jax.ShapeDtypeStruct((d, nj * tn), BF16),
        scratch_shapes=[pltpu.VMEM((d, tn), F32)],
        compiler_params=_cparams("parallel", "arbitrary"),
    )(x, *pieces)


def _rel_onehot():
    j = lax.broadcasted_iota(jnp.int32, (N_REL_PAD, ROLL_W), 1)
    kk = lax.broadcasted_iota(jnp.int32, (N_REL_PAD, ROLL_W), 0)
    dd = jnp.where(j < TKEYS, j, j - ROLL_W)
    idx = jnp.clip(N_PREV * CHUNK - dd, -REL_CLIP, REL_CLIP) + REL_CLIP
    return jnp.where(idx == kk, 1.0, 0.0).astype(F32)


def _band_mask():
    r = lax.broadcasted_iota(jnp.int32, (TQ, TKEYS), 0) // CHUNK
    m = lax.broadcasted_iota(jnp.int32, (TQ, TKEYS), 1) // CHUNK
    return (m >= r) & (m <= r + N_PREV)


def _tile_bias(table_pad):
    def body(t_ref, o_ref):
        g = jnp.dot(t_ref[...], _rel_onehot(), preferred_element_type=F32,
                    precision=lax.Precision.HIGHEST)
        band = _band_mask()
        for h in range(N_HEADS):
            gh = jnp.broadcast_to(g[h:h + 1, :], (TQ, ROLL_W))
            rolled = pltpu.roll(gh, 0, 1, stride=1, stride_axis=0)
            o_ref[h] = jnp.where(band, rolled[:, :TKEYS], NEG)

    return _call(
        body, name="tile_bias",
        in_specs=[pl.BlockSpec(memory_space=pltpu.VMEM)],
        out_specs=pl.BlockSpec(memory_space=pltpu.VMEM),
        out_shape=jax.ShapeDtypeStruct((N_HEADS, TQ, TKEYS), F32),
        compiler_params=pltpu.CompilerParams(vmem_limit_bytes=VMEM_LIMIT),
    )(table_pad)


def _tile_bias_bwd(dtb):
    def body(d_ref, o_ref, g_ref):
        zpad = jnp.zeros((TQ, ROLL_W - TKEYS), F32)
        rr = lax.broadcasted_iota(jnp.int32, (TQ, TQ), 0)
        cc = lax.broadcasted_iota(jnp.int32, (TQ, TQ), 1)
        flip = jnp.where(rr + cc == TQ - 1, 1.0, 0.0).astype(F32)
        for h in range(N_HEADS):
            xh = jnp.concatenate([d_ref[h], zpad], axis=1)
            xf = jnp.dot(flip, xh, preferred_element_type=F32, precision=lax.Precision.HIGHEST)
            rolled = pltpu.roll(xf, 0, 1, stride=1, stride_axis=0)
            g_ref[h:h + 1, :] = jnp.sum(rolled, axis=0, keepdims=True)
        g = pltpu.roll(g_ref[...], ROLL_W - (TQ - 1), 1)
        o_ref[...] = lax.dot_general(g, _rel_onehot(), (((1,), (1,)), ((), ())),
                                     preferred_element_type=F32, precision=lax.Precision.HIGHEST)

    return _call(
        body, name="tile_bias_bwd",
        in_specs=[pl.BlockSpec(memory_space=pltpu.VMEM)],
        out_specs=pl.BlockSpec(memory_space=pltpu.VMEM),
        out_shape=jax.ShapeDtypeStruct((N_HEADS, N_REL_PAD), F32),
        scratch_shapes=[pltpu.VMEM((N_HEADS, ROLL_W), F32)],
        compiler_params=pltpu.CompilerParams(vmem_limit_bytes=VMEM_LIMIT),
    )(dtb)


HB = 4
HBW = HB * HEAD_DIM
ATTN_SCALE = 0.125
assert ATTN_SCALE == 1.0 / math.sqrt(HEAD_DIM)


def _head_masks():
    lane = lax.broadcasted_iota(jnp.int32, (1, HBW), 1) // HEAD_DIM
    return [lane == hh for hh in range(HB)]


def _select_heads(masks, parts):
    out = parts[-1]
    for hh in range(HB - 2, -1, -1):
        out = jnp.where(masks[hh], parts[hh], out)
    return out


def _attn_probs(qm, kcat, tb, valid):
    s = _dot_nt(qm, kcat) + tb
    if valid is not None:
        s = jnp.where(valid, s, NEG)
    m = jnp.max(s, axis=-1, keepdims=True)
    e = jnp.exp(s - m)
    return e * (1.0 / jnp.sum(e, axis=-1, keepdims=True))


def _key_valid(i):
    col = lax.broadcasted_iota(jnp.int32, (TQ, TKEYS), 1)
    return col >= jnp.maximum(2 - i, 0) * TQ


def _kv_specs(col0, nt):
    def spec(back):
        return pl.BlockSpec((TQ, HBW), lambda hp, i: (jnp.clip(i - back, 0, nt - 1), col0 + hp))
    return [spec(2), spec(1), spec(0)]


def _attn_fwd(h, tb):
    s = h.shape[0]
    nt = s // TQ
    nhp = N_HEADS // HB

    def body(q_ref, k0, k1, k2, v0, v1, v2, tb_ref, o_ref):
        i = pl.program_id(1)

        def tile(valid):
            masks = _head_masks()
            qs = q_ref[...].astype(MXU_DTYPE) * ATTN_SCALE
            kcat = jnp.concatenate([k0[...], k1[...], k2[...]], axis=0).astype(MXU_DTYPE)
            vcat = jnp.concatenate([v0[...], v1[...], v2[...]], axis=0).astype(MXU_DTYPE)
            outs = []
            for hh in range(HB):
                qm = jnp.where(masks[hh], qs, jnp.zeros_like(qs))
                p = _attn_probs(qm, kcat, tb_ref[hh], valid)
                outs.append(_dot(p.astype(MXU_DTYPE), vcat))
            o_ref[...] = _select_heads(masks, outs).astype(o_ref.dtype)

        @pl.when(i < 2)
        def _():
            tile(_key_valid(i))

        @pl.when(i >= 2)
        def _():
            tile(None)

    in_specs = [pl.BlockSpec((TQ, HBW), lambda hp, i: (i, hp))]
    in_specs += _kv_specs(nhp, nt) + _kv_specs(2 * nhp, nt)
    in_specs += [pl.BlockSpec((HB, TQ, TKEYS), lambda hp, i: (hp, 0, 0))]
    return _call(
        body, name="attn_fwd", grid=(nhp, nt),
        in_specs=in_specs,
        out_specs=pl.BlockSpec((TQ, HBW), lambda hp, i: (i, hp)),
        out_shape=jax.ShapeDtypeStruct((s, E_MIX), BF16),
        compiler_params=_cparams("parallel", "parallel"),
    )(h, h, h, h, h, h, h, tb)


def _attn_bwd(h, tb, d_mix):
    s = h.shape[0]
    nt = s // TQ
    nhp = N_HEADS // HB

    def body(q_ref, k0, k1, k2, v0, v1, v2, tb_ref, do_ref,
             dq_ref, dk_ref, dv_ref, dtb_ref, dk_acc, dv_acc):
        i = pl.program_id(1)

        @pl.when(i == 0)
        def _():
            dk_acc[...] = jnp.zeros_like(dk_acc)
            dv_acc[...] = jnp.zeros_like(dv_acc)
            dtb_ref[...] = jnp.zeros_like(dtb_ref)

        @pl.when((i > 0) & (i < nt))
        def _():
            dk_acc[i % 3] = jnp.zeros((TQ, HBW), F32)
            dv_acc[i % 3] = jnp.zeros((TQ, HBW), F32)

        def tile(valid):
            masks = _head_masks()
            qs = q_ref[...].astype(MXU_DTYPE) * ATTN_SCALE
            do2 = do_ref[...].astype(MXU_DTYPE)
            kcat = jnp.concatenate([k0[...], k1[...], k2[...]], axis=0).astype(MXU_DTYPE)
            vcat = jnp.concatenate([v0[...], v1[...], v2[...]], axis=0).astype(MXU_DTYPE)
            ks = kcat * ATTN_SCALE
            dqs, dks, dvs = [], [], []
            for hh in range(HB):
                qm = jnp.where(masks[hh], qs, jnp.zeros_like(qs))
                dom = jnp.where(masks[hh], do2, jnp.zeros_like(do2))
                p = _attn_probs(qm, kcat, tb_ref[hh], valid)
                dp = _dot_nt(dom, vcat)
                ds = p * (dp - jnp.sum(p * dp, axis=-1, keepdims=True))
                dtb_ref[hh] += ds
                dsb = ds.astype(MXU_DTYPE)
                dqs.append(_dot(dsb, ks))
                dks.append(_dot_tn(dsb, qs))
                dvs.append(_dot_tn(p.astype(MXU_DTYPE), do2))
            dq_ref[...] = _select_heads(masks, dqs).astype(dq_ref.dtype)
            dkc = _select_heads(masks, dks)
            dvc = _select_heads(masks, dvs)
            for jj in range(3):
                slot = (i + 1 + jj) % 3
                dk_acc[slot] += dkc[jj * TQ:(jj + 1) * TQ]
                dv_acc[slot] += dvc[jj * TQ:(jj + 1) * TQ]

        @pl.when(i < 2)
        def _():
            tile(_key_valid(i))

        @pl.when((i >= 2) & (i < nt))
        def _():
            tile(None)

        @pl.when(i >= 2)
        def _():
            slot = (i - 2) % 3
            dk_ref[...] = dk_acc[slot].astype(dk_ref.dtype)
            dv_ref[...] = dv_acc[slot].astype(dv_ref.dtype)

    qmap = lambda hp, i: (jnp.minimum(i, nt - 1), hp)
    kvout = lambda hp, i: (jnp.maximum(i - 2, 0), hp)
    in_specs = [pl.BlockSpec((TQ, HBW), qmap)]
    in_specs += _kv_specs(nhp, nt) + _kv_specs(2 * nhp, nt)
    in_specs += [pl.BlockSpec((HB, TQ, TKEYS), lambda hp, i: (hp, 0, 0)),
                 pl.BlockSpec((TQ, HBW), qmap)]
    blk = (TQ, HBW)
    return _call(
        body, name="attn_bwd", grid=(nhp, nt + 2),
        in_specs=in_specs,
        out_specs=[pl.BlockSpec(blk, qmap), pl.BlockSpec(blk, kvout), pl.BlockSpec(blk, kvout),
                   pl.BlockSpec((HB, TQ, TKEYS), lambda hp, i: (hp, 0, 0))],
        out_shape=[jax.ShapeDtypeStruct((s, E_MIX), BF16)] * 3
        + [jax.ShapeDtypeStruct((N_HEADS, TQ, TKEYS), F32)],
        scratch_shapes=[pltpu.VMEM((3, TQ, HBW), F32)] * 2,
        compiler_params=_cparams("parallel", "arbitrary"),
    )(h, h, h, h, h, h, h, tb, d_mix)


CONV_TS = 512
HALO = 2 * SUBLANES


def _shift_down(prev, cur, k):
    rolled = pltpu.roll(cur, k, 0)
    row = lax.broadcasted_iota(jnp.int32, (HALO, cur.shape[1]), 0)
    top = jnp.where(row < k, pltpu.roll(prev, k, 0), rolled[:HALO])
    return jnp.concatenate([top, rolled[HALO:]], axis=0)


def _shift_up(cur, nxt, k):
    ts = cur.shape[0]
    rolled = pltpu.roll(cur, ts - k, 0)
    row = lax.broadcasted_iota(jnp.int32, (HALO, cur.shape[1]), 0)
    bottom = jnp.where(row >= HALO - k, pltpu.roll(nxt, HALO - k, 0), rolled[ts - HALO:])
    return jnp.concatenate([rolled[:ts - HALO], bottom], axis=0)


def _conv_specs(ts, nb):
    tile = lambda c: pl.BlockSpec((ts, E_MIX), lambda i: (i, c))
    prev = lambda c: pl.BlockSpec((HALO, E_MIX), lambda i: (jnp.maximum(i * (ts // HALO) - 1, 0), c))
    return tile, prev


def _conv_fwd(h, w8):
    s = h.shape[0]
    ts = CONV_TS
    nb = s // ts
    tile, prev = _conv_specs(ts, nb)

    def body(bg, cg, u, cgp, up, w_ref, o_ref):
        i = pl.program_id(0)
        a = cg[...].astype(F32) * u[...].astype(F32)
        ap = jnp.where(i > 0, cgp[...].astype(F32) * up[...].astype(F32), 0.0)
        w = w_ref[...]
        conv = w[0:1] * _shift_down(ap, a, 2) + w[1:2] * _shift_down(ap, a, 1) + w[2:3] * a
        o_ref[...] = (bg[...].astype(F32) * conv).astype(o_ref.dtype)

    return _call(
        body, name="conv_fwd", grid=(nb,),
        in_specs=[tile(0), tile(1), tile(2), prev(1), prev(2),
                  pl.BlockSpec((SUBLANES, E_MIX), lambda i: (0, 0))],
        out_specs=pl.BlockSpec((ts, E_MIX), lambda i: (i, 0)),
        out_shape=jax.ShapeDtypeStruct((s, E_MIX), BF16),
        compiler_params=_cparams("parallel"),
    )(h, h, h, h, h, w8)


def _conv_bwd(h, w8, d_mix):
    s = h.shape[0]
    ts = CONV_TS
    nb = s // ts
    tile, prev = _conv_specs(ts, nb)
    nrow = s // HALO
    nxt = lambda c: pl.BlockSpec((HALO, E_MIX), lambda i: (jnp.minimum((i + 1) * (ts // HALO), nrow - 1), c))

    def body(bg, cg, u, cgp, up, bgn, dmix, dmixn, w_ref, dbg_ref, dcg_ref, du_ref, dw_ref):
        i = pl.program_id(0)

        @pl.when(i == 0)
        def _():
            dw_ref[...] = jnp.zeros_like(dw_ref)

        cgv, uv = cg[...].astype(F32), u[...].astype(F32)
        a = cgv * uv
        ap = jnp.where(i > 0, cgp[...].astype(F32) * up[...].astype(F32), 0.0)
        a1 = _shift_down(ap, a, 1)
        a2 = _shift_down(ap, a, 2)
        w = w_ref[...]
        conv = w[0:1] * a2 + w[1:2] * a1 + w[2:3] * a
        dm = dmix[...].astype(F32)
        dbg_ref[...] = (dm * conv).astype(dbg_ref.dtype)
        dc = dm * bg[...].astype(F32)
        dcn = jnp.where(i < nb - 1, dmixn[...].astype(F32) * bgn[...].astype(F32), 0.0)
        da = w[2:3] * dc + w[1:2] * _shift_up(dc, dcn, 1) + w[0:1] * _shift_up(dc, dcn, 2)
        dcg_ref[...] = (da * uv).astype(dcg_ref.dtype)
        du_ref[...] = (da * cgv).astype(du_ref.dtype)
        dw_ref[0:1, :] += jnp.sum(dc * a2, axis=0, keepdims=True)
        dw_ref[1:2, :] += jnp.sum(dc * a1, axis=0, keepdims=True)
        dw_ref[2:3, :] += jnp.sum(dc * a, axis=0, keepdims=True)

    full = lambda: pl.BlockSpec((ts, E_MIX), lambda i: (i, 0))
    return _call(
        body, name="conv_bwd", grid=(nb,),
        in_specs=[tile(0), tile(1), tile(2), prev(1), prev(2), nxt(0),
                  full(), pl.BlockSpec((HALO, E_MIX), lambda i: (jnp.minimum((i + 1) * (ts // HALO), nrow - 1), 0)),
                  pl.BlockSpec((SUBLANES, E_MIX), lambda i: (0, 0))],
        out_specs=[full(), full(), full(), pl.BlockSpec((SUBLANES, E_MIX), lambda i: (0, 0))],
        out_shape=[jax.ShapeDtypeStruct((s, E_MIX), BF16)] * 3
        + [jax.ShapeDtypeStruct((SUBLANES, E_MIX), F32)],
        compiler_params=_cparams("arbitrary"),
    )(h, h, h, h, h, h, d_mix, d_mix, w8)


def _mem_probs(qh, kh):
    s = _dot_nt(qh, kh) / math.sqrt(MEM_HEAD_DIM)
    m = jnp.max(s, axis=-1, keepdims=True)
    e = jnp.exp(s - m)
    return e / jnp.sum(e, axis=-1, keepdims=True)


def _sigmoid(z):
    return 1.0 / (1.0 + jnp.exp(-z))


def _layer_out_fwd(x, h, mix, kv, w_out, g, b):
    s, d = x.shape
    ts = 2 * TS

    def body(x_ref, mix_ref, qm_ref, z0, z1, z2, kv_ref, wo_ref, g_ref, b_ref,
             xn_ref, r_ref, mem_ref):
        qm = qm_ref[...].astype(MXU_DTYPE)
        kvb = kv_ref[...].astype(MXU_DTYPE)
        mems = []
        for hh in range(MEM_HEADS):
            lo = hh * MEM_HEAD_DIM
            p = _mem_probs(qm[:, lo:lo + MEM_HEAD_DIM], kvb[:, lo:lo + MEM_HEAD_DIM])
            mems.append(_dot(p.astype(MXU_DTYPE), kvb[:, E_MEM + lo:E_MEM + lo + MEM_HEAD_DIM]))
        mem = jnp.concatenate(mems, axis=1).astype(mem_ref.dtype)
        mem_ref[...] = mem
        mixv = mix_ref[...].astype(F32)
        half = E_MIX // 2
        parts = [mixv[:, :half], mixv[:, half:], mem.astype(F32)]
        out = jnp.zeros((ts, d), F32)
        for c, zr in enumerate((z0, z1, z2)):
            zv = zr[...].astype(F32)
            y = (parts[c] * (zv * _sigmoid(zv))).astype(MXU_DTYPE)
            out += _dot(y, wo_ref[c * half:(c + 1) * half, :])
        r = DN_ALPHA * x_ref[...] + out
        r_ref[...] = r
        mu = jnp.mean(r, axis=-1, keepdims=True)
        rc = r - mu
        var = jnp.mean(rc * rc, axis=-1, keepdims=True)
        xn_ref[...] = rc * lax.rsqrt(var + LN_EPS) * g_ref[...] + b_ref[...]

    row = lambda w, c: pl.BlockSpec((ts, w), lambda i: (i, c))
    const = lambda shp: pl.BlockSpec(shp, lambda i: (0, 0))
    return _call(
        body, name="layer_out_fwd", grid=(s // ts,),
        in_specs=[row(d, 0), row(E_MIX, 0), row(E_MEM, QM_BLK),
                  row(E_MEM, Z_BLK), row(E_MEM, Z_BLK + 1), row(E_MEM, Z_BLK + 2),
                  const((N_MEM, 2 * E_MEM)), const((E_BRANCH, d)), const((1, d)), const((1, d))],
        out_specs=[row(d, 0), row(d, 0), row(E_MEM, 0)],
        out_shape=[jax.ShapeDtypeStruct((s, d), F32), jax.ShapeDtypeStruct((s, d), F32),
                   jax.ShapeDtypeStruct((s, E_MEM), BF16)],
        compiler_params=_cparams("parallel"),
    )(x, mix, h, h, h, h, kv, w_out, g, b)


def _layer_out_bwd(dxn, r, g, h, mix, mem, kv, w_out_t):
    s, d = r.shape
    ts = 2 * TS
    nb = s // ts
    half = E_MIX // 2
    inv = 1.0 / math.sqrt(MEM_HEAD_DIM)

    def body(dxn_ref, r_ref, g_ref, mix_ref, mem_ref, qm_ref, z0, z1, z2, kv_ref, wo_ref,
             dxr_ref, dmix_ref, dqz_ref, dwo_ref, dkv_ref, dg_ref, db_ref, dw_acc):
        i = pl.program_id(0)

        @pl.when(i == 0)
        def _():
            dw_acc[...] = jnp.zeros_like(dw_acc)
            dkv_ref[...] = jnp.zeros_like(dkv_ref)
            dg_ref[...] = jnp.zeros_like(dg_ref)
            db_ref[...] = jnp.zeros_like(db_ref)

        dxn_v = dxn_ref[...]
        rv = r_ref[...]
        mu = jnp.mean(rv, axis=-1, keepdims=True)
        rc = rv - mu
        var = jnp.mean(rc * rc, axis=-1, keepdims=True)
        rstd = lax.rsqrt(var + LN_EPS)
        xhat = rc * rstd
        dg_ref[...] += jnp.sum(dxn_v * xhat, axis=0, keepdims=True)
        db_ref[...] += jnp.sum(dxn_v, axis=0, keepdims=True)
        dxh = dxn_v * g_ref[...]
        m1 = jnp.mean(dxh, axis=-1, keepdims=True)
        m2 = jnp.mean(dxh * xhat, axis=-1, keepdims=True)
        dr = rstd * (dxh - m1 - xhat * m2)
        dxr_ref[...] = DN_ALPHA * dr
        dout = dr.astype(MXU_DTYPE)
        mixv = mix_ref[...].astype(F32)
        parts = [mixv[:, :half], mixv[:, half:], mem_ref[...].astype(F32)]
        dcs = []
        for c, zr in enumerate((z0, z1, z2)):
            lo = c * half
            zv = zr[...].astype(F32)
            sg = _sigmoid(zv)
            sl = zv * sg
            dy = _dot(dout, wo_ref[:, lo:lo + half])
            y = (parts[c] * sl).astype(MXU_DTYPE)
            dw_acc[lo:lo + half, :] += _dot_tn(y, dout)
            dcs.append(dy * sl)
            dqz_ref[:, E_MEM + lo:E_MEM + lo + half] = (
                dy * parts[c] * (sg * (1.0 + zv * (1.0 - sg)))).astype(dqz_ref.dtype)
        dmix_ref[...] = jnp.concatenate(dcs[:2], axis=1).astype(dmix_ref.dtype)

        qm = qm_ref[...].astype(MXU_DTYPE)
        kvb = kv_ref[...].astype(MXU_DTYPE)
        dmb = dcs[2].astype(MXU_DTYPE)
        for hh in range(MEM_HEADS):
            lo = hh * MEM_HEAD_DIM
            qh = qm[:, lo:lo + MEM_HEAD_DIM]
            kh = kvb[:, lo:lo + MEM_HEAD_DIM]
            vh = kvb[:, E_MEM + lo:E_MEM + lo + MEM_HEAD_DIM]
            dmh = dmb[:, lo:lo + MEM_HEAD_DIM]
            p = _mem_probs(qh, kh)
            dp = _dot_nt(dmh, vh)
            ds = p * (dp - jnp.sum(p * dp, axis=-1, keepdims=True))
            dsb = (ds * inv).astype(MXU_DTYPE)
            dqz_ref[:, lo:lo + MEM_HEAD_DIM] = _dot(dsb, kh).astype(dqz_ref.dtype)
            dkv_ref[:, lo:lo + MEM_HEAD_DIM] += _dot_tn(dsb, qh)
            dkv_ref[:, E_MEM + lo:E_MEM + lo + MEM_HEAD_DIM] += _dot_tn(p.astype(MXU_DTYPE), dmh)

        @pl.when(i == nb - 1)
        def _():
            dwo_ref[...] = dw_acc[...].astype(dwo_ref.dtype)

    row = lambda w, c: pl.BlockSpec((ts, w), lambda i: (i, c))
    const = lambda shp: pl.BlockSpec(shp, lambda i: (0, 0))
    once = lambda shp: pl.BlockSpec(shp, lambda i: (0, 0), pipeline_mode=pl.Buffered(1))
    return _call(
        body, name="layer_out_bwd", grid=(nb,),
        in_specs=[row(d, 0), row(d, 0), const((1, d)), row(E_MIX, 0), row(E_MEM, 0),
                  row(E_MEM, QM_BLK), row(E_MEM, Z_BLK), row(E_MEM, Z_BLK + 1), row(E_MEM, Z_BLK + 2),
                  once((N_MEM, 2 * E_MEM)), once((d, E_BRANCH))],
        out_specs=[row(d, 0), row(E_MIX, 0), row(E_MEM + E_BRANCH, 0),
                   const((E_BRANCH, d)), const((N_MEM, 2 * E_MEM)), const((1, d)), const((1, d))],
        out_shape=[jax.ShapeDtypeStruct((s, d), F32), jax.ShapeDtypeStruct((s, E_MIX), BF16),
                   jax.ShapeDtypeStruct((s, E_MEM + E_BRANCH), BF16),
                   jax.ShapeDtypeStruct((E_BRANCH, d), BF16),
                   jax.ShapeDtypeStruct((N_MEM, 2 * E_MEM), F32),
                   jax.ShapeDtypeStruct((1, d), F32), jax.ShapeDtypeStruct((1, d), F32)],
        scratch_shapes=[pltpu.VMEM((E_BRANCH, d), F32)],
        compiler_params=_cparams("arbitrary"),
    )(dxn, r, g, mix, mem, h, h, h, h, kv, w_out_t)


def _loss_head(y, target):
    s, d = y.shape
    ts = 512

    def body(y_ref, t_ref, l_ref, dy_ref):
        @pl.when(pl.program_id(0) == 0)
        def _():
            l_ref[...] = jnp.zeros_like(l_ref)

        e = y_ref[...] - t_ref[...]
        dy_ref[...] = e * (1.0 / d)
        l_ref[...] += (0.5 / d) * jnp.sum(jnp.sum(e * e, axis=1, keepdims=True), axis=0, keepdims=True)

    return _call(
        body, name="loss_head", grid=(s // ts,),
        in_specs=[pl.BlockSpec((ts, d), lambda i: (i, 0))] * 2,
        out_specs=[pl.BlockSpec((1, 1), lambda i: (0, 0)), pl.BlockSpec((ts, d), lambda i: (i, 0))],
        out_shape=[jax.ShapeDtypeStruct((1, 1), F32), jax.ShapeDtypeStruct((s, d), F32)],
        compiler_params=_cparams("arbitrary"),
    )(y, target)


def _local_step(x, mem, get_weights, put_grads, rel_bias, conv_w, ln_g, ln_b, target):
    auxes = []
    for layer in range(DEPTH):
        if layer % 2 == 0:
            auxes.append(_tile_bias(jnp.pad(rel_bias[layer // 2], ((0, 0), (0, N_REL_PAD - N_REL)))))
        else:
            auxes.append(jnp.pad(conv_w[layer // 2], ((0, SUBLANES - 3), (0, 0))))

    saved = []
    xl = x
    for layer in range(DEPTH):
        w_in_l, rest = get_weights(layer, xl if layer else auxes)
        h, xb = _inproj(xl, w_in_l)
        w_kv_l, w_out_l = rest(h)
        aux = auxes[layer]
        if layer % 2 == 0:
            mix = _attn_fwd(h, aux)
        else:
            mix = _conv_fwd(h, aux)
        kv = _small_matmul(mem, w_kv_l, False, F32, "kv_mem")
        xn, r, mem_out = _layer_out_fwd(xl, h, mix, kv, w_out_l,
                                        ln_g[layer][None], ln_b[layer][None])
        saved.append((xb, h, aux, mix, kv, r, mem_out, w_in_l, w_out_l))
        xl = xn

    loss, dx = _loss_head(xl, target)

    dgs, dbs, d_rel, d_conv = [], [], [], []
    for layer in reversed(range(DEPTH)):
        xb, h, aux, mix, kv, r, mem_out, w_in_l, w_out_l = saved[layer]
        dx_res, d_mix, dqz, dwo, dkv, dg, db = _layer_out_bwd(
            dx, r, ln_g[layer][None], h, mix, mem_out, kv, w_out_l.T)
        put_grads(layer, [1, 2], [_small_matmul(mem, dkv, True, BF16, "dw_kv"), dwo])
        if layer % 2 == 0:
            dq, dk, dv, dtb = _attn_bwd(h, aux, d_mix)
            d_rel.append(_tile_bias_bwd(dtb)[:, :N_REL])
            pieces = [dq, dk, dv, dqz]
        else:
            dbg, dcg, du, dw8 = _conv_bwd(h, aux, d_mix)
            d_conv.append(dw8[:3])
            pieces = [dbg, dcg, du, dqz]
        token = put_grads(layer, [0], [_dw_matmul(xb, pieces)])
        dgs.append(dg[0])
        dbs.append(db[0])
        dx = _dx_matmul(pieces, w_in_l.T, dx_res, token)

    rev = lambda lst: jnp.stack(lst[::-1])
    return loss, dx, rev(d_rel), rev(d_conv), rev(dgs), rev(dbs)


def _me():
    return lax.axis_index("x"), lax.axis_index("y"), lax.axis_index("c")


def _peer(k):
    x, y, c = _me()
    kx, ky, kc = (k >> 2) & 1, (k >> 1) & 1, k & 1
    return (1 - x if kx else x, 1 - y if ky else y, 1 - c if kc else c)


def _lin(dev):
    return 4 * dev[0] + 2 * dev[1] + dev[2]


ANY = pl.BlockSpec(memory_space=pl.ANY)


def _exchange(srcs, dst_shapes, src_slice, dst_slice, name):
    na = len(srcs)

    def body(*refs):
        src_refs = refs[:na]
        dst_refs = refs[na:2 * na]
        send_sems, recv_sems, local_sems = refs[2 * na:]
        me = _lin(_me())
        copies = []
        for a in range(na):
            loc = pltpu.make_async_copy(src_slice(a, src_refs[a], me), dst_slice(a, dst_refs[a], me),
                                        local_sems.at[a])
            loc.start()
            copies.append(loc)
            for k in range(1, N_DEV):
                peer = _peer(k)
                cp = pltpu.make_async_remote_copy(
                    src_ref=src_slice(a, src_refs[a], _lin(peer)),
                    dst_ref=dst_slice(a, dst_refs[a], me),
                    send_sem=send_sems.at[a, k - 1], recv_sem=recv_sems.at[a, k - 1],
                    device_id=peer, device_id_type=pl.DeviceIdType.MESH)
                cp.start()
                copies.append(cp)
        for cp in copies:
            cp.wait()

    return _call(
        body, name=name,
        in_specs=[ANY] * na, out_specs=[ANY] * na,
        out_shape=[jax.ShapeDtypeStruct(shp, s.dtype) for shp, s in zip(dst_shapes, srcs)],
        scratch_shapes=[pltpu.SemaphoreType.DMA((na, N_DEV - 1)),
                        pltpu.SemaphoreType.DMA((na, N_DEV - 1)),
                        pltpu.SemaphoreType.DMA((na,))],
    )(*srcs)


def _gather_to_all(src, name):
    return _exchange([src], [(N_DEV,) + src.shape], lambda a, ref, p: ref,
                     lambda a, ref, me: ref.at[me], name)[0]


HBM = pl.BlockSpec(memory_space=pltpu.HBM)
SEM = pl.BlockSpec(memory_space=pltpu.SEMAPHORE)
EFFECT = pltpu.SideEffectType.DATAFLOW_SIDE_EFFECTING
N_PEER = N_DEV - 1
N_KIND = 3


def _peer_copies(kind, src_ref, land_ref, send, recv, src_slice, dst_slice):
    me = _lin(_me())
    copies = []
    for k in range(1, N_DEV):
        peer = _peer(k)
        copies.append(pltpu.make_async_remote_copy(
            src_ref=src_slice(kind, src_ref, _lin(peer)),
            dst_ref=dst_slice(kind, land_ref, me, k),
            send_sem=send.at[k - 1], recv_sem=recv.at[k - 1],
            device_id=peer, device_id_type=pl.DeviceIdType.MESH))
    return copies


def _own_copy(kind, src_ref, land_ref, send, src_slice, dst_slice):
    me = _lin(_me())
    return pltpu.make_async_copy(src_slice(kind, src_ref, me), dst_slice(kind, land_ref, me, 0),
                                 send.at[N_PEER])


def _split_start(srcs, kinds, land_shapes, src_slice, dst_slice, name, own=False):
    na = len(srcs)

    def body(*refs):
        src_refs, land_refs = refs[:na], refs[na:2 * na]
        sems = refs[2 * na:4 * na]
        token = refs[-1]
        for a in range(na):
            for cp in _peer_copies(kinds[a], src_refs[a], land_refs[a], sems[2 * a], sems[2 * a + 1],
                                   src_slice, dst_slice):
                cp.start()
            if own:
                _own_copy(kinds[a], src_refs[a], land_refs[a], sems[2 * a], src_slice, dst_slice).start()
        token[...] = jnp.zeros_like(token)

    sem_shape = pltpu.SemaphoreType.DMA((N_DEV,))
    lands = [lax.empty(shp, s.dtype) for shp, s in zip(land_shapes, srcs)]
    outs = _call(
        body, name=name,
        in_specs=[HBM] * (2 * na),
        out_specs=[SEM] * (2 * na) + [HBM] * (2 * na) + [pl.BlockSpec(memory_space=pltpu.VMEM)],
        out_shape=[sem_shape] * (2 * na)
        + [pltpu.HBM(s.shape, s.dtype) for s in srcs]
        + [pltpu.HBM(shp, s.dtype) for shp, s in zip(land_shapes, srcs)]
        + [jax.ShapeDtypeStruct((SUBLANES, LANES), F32)],
        input_output_aliases={i: 2 * na + i for i in range(2 * na)},
        compiler_params=pltpu.CompilerParams(has_side_effects=EFFECT),
    )(*[pltpu.with_memory_space_constraint(a, pltpu.HBM) for a in list(srcs) + lands])
    sems = [(outs[2 * a], outs[2 * a + 1]) for a in range(na)]
    thrus = outs[2 * na:3 * na]
    lands = outs[3 * na:4 * na]
    return sems, thrus, lands, outs[-1]


def _split_wait(sems, thrus, lands, kinds, src_slice, dst_slice, after, name, own=False):
    na = len(thrus)
    after = list(after) if isinstance(after, (list, tuple)) else [after]

    def body(*refs):
        src_refs, land_refs = refs[:na], refs[na:2 * na]
        sem_refs = refs[2 * na:4 * na]
        for a in range(na):
            for cp in _peer_copies(kinds[a], src_refs[a], land_refs[a], sem_refs[2 * a], sem_refs[2 * a + 1],
                                   src_slice, dst_slice):
                cp.wait_send()
                cp.wait_recv()
            if own:
                _own_copy(kinds[a], src_refs[a], land_refs[a], sem_refs[2 * a], src_slice, dst_slice).wait()

    outs = _call(
        body, name=name,
        in_specs=[HBM] * (2 * na) + [SEM] * (2 * na) + [ANY] * len(after),
        out_specs=[HBM] * (2 * na),
        out_shape=[pltpu.HBM(a.shape, a.dtype) for a in list(thrus) + list(lands)],
        input_output_aliases={i: i for i in range(2 * na)},
        compiler_params=pltpu.CompilerParams(has_side_effects=EFFECT),
    )(*thrus, *lands, *[s for pair in sems for s in pair], *after)
    return outs[na:]


def _shard_dims(c_in, r_kv, r_out):
    def sl(j, ref, p):
        if j == 0:
            return ref.at[:, pl.ds(pl.multiple_of(p * c_in, LANES), c_in)]
        r = r_kv if j == 1 else r_out
        return ref.at[pl.ds(pl.multiple_of(p * r, 2 * SUBLANES), r), :]
    return sl


def _adamw_math(w, g, m, v):
    m = ADAM_B1 * m + (1.0 - ADAM_B1) * g
    v = ADAM_B2 * v + (1.0 - ADAM_B2) * (g * g)
    m_hat = m / (1.0 - ADAM_B1 ** ADAM_STEP)
    v_hat = v / (1.0 - ADAM_B2 ** ADAM_STEP)
    delta = -ADAM_LR * (m_hat / (jnp.sqrt(v_hat) + ADAM_EPS) + ADAM_WD * w)
    return delta, m, v


def _reduce_adamw(parts, w, m, v, name):
    rows, cols = w.shape
    tr = rows
    for cand in (512, 256, 128, 64, 32, 16):
        if rows % cand == 0 and rows > cand:
            tr = cand
            break

    def body(p_ref, w_ref, m_ref, v_ref, g_out, d_out, m_out, v_out):
        g = p_ref[0].astype(F32)
        for s in range(1, N_DEV):
            g = g + p_ref[s].astype(F32)
        g_out[...] = g
        d_out[...], m_out[...], v_out[...] = _adamw_math(w_ref[...], g, m_ref[...], v_ref[...])

    blk = pl.BlockSpec((tr, cols), lambda i: (i, 0))
    return _call(
        body, name=name, grid=(rows // tr,),
        in_specs=[pl.BlockSpec((N_DEV, tr, cols), lambda i: (0, i, 0)), blk, blk, blk],
        out_specs=[blk] * 4,
        out_shape=[jax.ShapeDtypeStruct((rows, cols), F32)] * 4,
        compiler_params=_cparams("parallel"),
    )(parts, w, m, v)


def _reduce_adamw_layers(lands, owns, w, m, v, name, first=0, prev=None):
    depth, rows, cols = w.shape
    nl = len(lands)
    tr = rows
    for cand in (256, 192, 128):
        if rows % cand == 0:
            tr = cand
            break

    kept = [] if prev is None else list(prev)

    def body(*refs):
        land_refs, own_refs = refs[:nl], refs[nl:2 * nl]
        w_ref, m_ref, v_ref = refs[2 * nl:2 * nl + 3]
        g_out, d_out, m_out, v_out = refs[2 * nl + 3 + len(kept):]
        layer = pl.program_id(0)
        for a in range(nl):
            @pl.when(layer == a)
            def _(a=a):
                g = own_refs[a][...].astype(F32)
                for k in range(N_PEER):
                    g = g + land_refs[a][k].astype(F32)
                g_out[...] = g
                d_out[...], m_out[...], v_out[...] = _adamw_math(w_ref[...], g, m_ref[...], v_ref[...])

    def lmap(a):
        return lambda l, i: (0, jnp.where(l == a, i, 0), 0)

    def omap(a):
        return lambda l, i: (jnp.where(l == a, i, 0), 0)

    blk = pl.BlockSpec((None, tr, cols), lambda l, i: (l + first, i, 0))
    n_in = 2 * nl + 3
    return _call(
        body, name=name, grid=(nl, rows // tr),
        in_specs=[pl.BlockSpec((N_PEER, tr, cols), lmap(a)) for a in range(nl)]
        + [pl.BlockSpec((tr, cols), omap(a)) for a in range(nl)] + [blk, blk, blk] + [ANY] * len(kept),
        out_specs=[blk] * 4,
        out_shape=[jax.ShapeDtypeStruct((depth, rows, cols), F32)] * 4,
        input_output_aliases={n_in + i: i for i in range(len(kept))},
        compiler_params=_cparams("arbitrary", "arbitrary"),
    )(*lands, *owns, w, m, v, *kept)


SM_G, SM_B, SM_CONV, SM_REL = 0, 4, 8, 16
SM_ROWS = SM_REL + 2 * N_HEADS
REL_W = 384


def _pack_small(d_rel, d_conv, dg, db):
    buf = jnp.zeros((SM_ROWS, D_MODEL), F32)
    buf = buf.at[SM_G:SM_G + DEPTH].set(dg)
    buf = buf.at[SM_B:SM_B + DEPTH].set(db)
    buf = buf.at[SM_CONV:SM_CONV + 6].set(d_conv.reshape(6, E_MIX))
    buf = buf.at[SM_REL:, :N_REL].set(d_rel.reshape(2 * N_HEADS, N_REL))
    return buf


def kernel(x, mem, w_in, w_mem_kv, w_out, rel_bias, conv_w, ln_g, ln_b, loss_target, m_w_in, m_w_mem_kv, m_w_out, m_rel_bias, m_conv_w, m_ln_g, m_ln_b, v_w_in, v_w_mem_kv, v_w_out, v_rel_bias, v_conv_w, v_ln_g, v_ln_b):
    me = _lin(_me())
    c_in, r_kv, r_out, c_conv = w_in.shape[2], w_mem_kv.shape[1], w_out.shape[1], conv_w.shape[2]

    shard = _shard_dims(c_in, r_kv, r_out)
    own_start = lambda j: (0, me * c_in) if j == 0 else (me * (r_kv if j == 1 else r_out), 0)

    w_sh = [w_in.astype(BF16), w_mem_kv.astype(BF16), w_out.astype(BF16)]
    full_shapes = [(D_MODEL, N_DEV * c_in), (N_DEV * r_kv, w_mem_kv.shape[2]), (N_DEV * r_out, D_MODEL)]
    ag_src = lambda j, ref, p: ref
    ag_dst = lambda j, ref, me_, k: shard(j, ref, me_)
    kinds = list(range(N_KIND))
    ag_sems, ag_thrus, ag_lands, _ = _split_start(
        [w_sh[j][layer] for layer in range(DEPTH) for j in kinds], kinds * DEPTH,
        full_shapes * DEPTH, ag_src, ag_dst, "ag_start", own=True)

    def get_weights(layer, x_layer):
        lo = layer * N_KIND

        def wait(js, after, name):
            idx = [lo + j for j in js]
            return _split_wait([ag_sems[a] for a in idx], [ag_thrus[a] for a in idx],
                               [ag_lands[a] for a in idx], js, ag_src, ag_dst, after, name, own=True)

        w_in_l, = wait([0], x_layer, "ag_wait_in_%d" % layer)
        return w_in_l, lambda h: wait([1, 2], h, "ag_wait_kv_out_%d" % layer)

    rs_src = shard
    rs_dst = lambda j, ref, me_, k: ref.at[k - 1]
    rs_shapes = [(N_PEER, D_MODEL, c_in), (N_PEER, r_kv, w_mem_kv.shape[2]), (N_PEER, r_out, D_MODEL)]
    own_sizes = [(D_MODEL, c_in), (r_kv, w_mem_kv.shape[2]), (r_out, D_MODEL)]
    pending = {}

    held = {}

    def put_grads(layer, js, arrays):
        if layer > 0 and js != [0]:
            held[layer] = (js, arrays)
            return None
        if layer > 0:
            js, arrays = held[layer][0] + js, held[layer][1] + arrays
        owns = [lax.dynamic_slice(a, own_start(j), own_sizes[j]) for j, a in zip(js, arrays)]
        sems, thrus, lands, token = _split_start(
            arrays, js, [rs_shapes[j] for j in js], rs_src, rs_dst,
            "rs_start_%d_%s" % (layer, "".join(str(j) for j in js)))
        entry = pending.setdefault(layer, ([], [], [], [], []))
        for lst, new in zip(entry, (js, sems, thrus, lands, owns)):
            lst.extend(new)
        return token

    conv_tile = jnp.pad(conv_w.reshape(6, c_conv), ((0, SUBLANES - 6), (0, 0)))
    conv_land = _gather_to_all(conv_tile, "gather_conv")
    conv_f = jnp.transpose(conv_land[:, :6], (1, 0, 2)).reshape(2, 3, N_DEV * c_conv)

    loss, grad_x, d_rel, d_conv, dg, db = _local_step(
        x[0], mem[0], get_weights, put_grads, rel_bias, conv_f, ln_g, ln_b, loss_target[0])

    p_small = _gather_to_all(_pack_small(d_rel, d_conv, dg, db), "gather_small_grads")

    rs_lands, rs_owns = {}, {}

    def rs_wait(layer, want, after, name):
        js, sems, thrus, lands, owns = pending[layer]
        pos = [js.index(j) for j in want]
        got = _split_wait([sems[p] for p in pos], [thrus[p] for p in pos], [lands[p] for p in pos],
                          want, rs_src, rs_dst, after, name)
        for j, p, land in zip(want, pos, got):
            rs_lands[layer, j], rs_owns[layer, j] = land, owns[p]

    for layer in range(1, DEPTH):
        rs_wait(layer, kinds, grad_x, "rs_wait_%d" % layer)
    rs_wait(0, [1, 2], grad_x, "rs_wait_0_kv_out")

    def big(j, w, m, v, name, layers, prev=None):
        return _reduce_adamw_layers([rs_lands[layer, j] for layer in layers], [rs_owns[layer, j] for layer in layers],
                                    w, m, v, name, first=layers[0], prev=prev)

    every = list(range(DEPTH))
    g_kv, d_kv, nm_kv, nv_kv = big(1, w_mem_kv, m_w_mem_kv, v_w_mem_kv, "adamw_w_kv", every)
    g_out, d_out, nm_out, nv_out = big(2, w_out, m_w_out, v_w_out, "adamw_w_out", every)
    later = big(0, w_in, m_w_in, v_w_in, "adamw_w_in_later_layers", every[1:])

    def pack_state(rel, conv, g, b):
        conv_full = jnp.zeros((2, 3, E_MIX), F32)
        conv_full = lax.dynamic_update_slice(conv_full, conv, (0, 0, me * c_conv))
        return _pack_small(rel, conv_full, g, b)

    sm_w = pack_state(rel_bias, conv_w, ln_g, ln_b)
    sm_m = pack_state(m_rel_bias, m_conv_w, m_ln_g, m_ln_b)
    sm_v = pack_state(v_rel_bias, v_conv_w, v_ln_g, v_ln_b)
    sm_outs = _reduce_adamw(p_small, sm_w, sm_m, sm_v, "adamw_small")

    rs_wait(0, [0], [later[0], sm_outs[0], g_kv, g_out], "rs_wait_0_in")
    g_in, d_in, nm_in, nv_in = big(0, w_in, m_w_in, v_w_in, "adamw_w_in_layer_0", [0], prev=later)

    def unpack(buf):
        rel = buf[SM_REL:, :N_REL].reshape(2, N_HEADS, N_REL)
        conv = lax.dynamic_slice(buf[SM_CONV:SM_CONV + 6].reshape(2, 3, E_MIX), (0, 0, me * c_conv), (2, 3, c_conv))
        return rel, conv, buf[SM_G:SM_G + DEPTH], buf[SM_B:SM_B + DEPTH]

    g_sm, d_sm, nm_sm, nv_sm = [unpack(b) for b in sm_outs]

    loss = lax.psum(loss[0, 0], ("x", "y", "c"))
    return (loss, grad_x[None],
            g_in, g_kv, g_out, *g_sm,
            d_in, d_kv, d_out, *d_sm,
            nm_in, nm_kv, nm_out, *nm_sm,
            nv_in, nv_kv, nv_out, *nv_sm)
```

```python
import functools
import math

import jax
import jax.numpy as jnp
from jax import lax
from jax.experimental import pallas as pl
from jax.experimental.pallas import tpu as pltpu

F32 = jnp.float32
BF16 = jnp.bfloat16
MXU_DTYPE = jnp.bfloat16

N_DEV = 8
D_MODEL = 1024
DEPTH = 4
CHUNK = 64
N_PREV = 8
N_HEADS = 16
HEAD_DIM = 64
E_MIX = 1024
REL_CLIP = 128
N_REL = 2 * REL_CLIP + 1
N_REL_PAD = 384
N_MEM = 256
MEM_HEADS = 4
MEM_HEAD_DIM = 128
E_MEM = 512
E_BRANCH = E_MIX + E_MEM
N_IN = 3 * E_MIX + E_MEM + E_BRANCH
DN_ALPHA = (2.0 * DEPTH) ** 0.25
LN_EPS = 1e-5
NEG = -1e30

ADAM_LR = 0.001
ADAM_B1 = 0.9
ADAM_B2 = 0.999
ADAM_EPS = 1e-08
ADAM_WD = 0.01
ADAM_STEP = 10

LANES = 128
SUBLANES = 8
VMEM_LIMIT = 56 * 1024 * 1024

TQ = 4 * CHUNK
TKEYS = 3 * TQ
ROLL_W = 1024
TS = 256
QM_BLK = 3 * E_MIX // E_MEM
Z_BLK = QM_BLK + 1


def _call(body, **kw):
    return pl.pallas_call(body, **kw)


def _cparams(*sem):
    return pltpu.CompilerParams(dimension_semantics=sem, vmem_limit_bytes=VMEM_LIMIT)


def _dot(a, b):
    return jnp.dot(a, b, preferred_element_type=F32)


def _dot_nt(a, b):
    return lax.dot_general(a, b, (((1,), (1,)), ((), ())), preferred_element_type=F32)


def _dot_tn(a, b):
    return lax.dot_general(a, b, (((0,), (0,)), ((), ())), preferred_element_type=F32)


def _inproj(x, w):
    s, d = x.shape
    n = w.shape[1]
    tm = 512
    tn = 1024

    def body(x_ref, w_ref, o_ref, xb_ref):
        xb = x_ref[...].astype(xb_ref.dtype)
        xb_ref[...] = xb
        for j in range(n // tn):
            o_ref[:, j * tn:(j + 1) * tn] = _dot(xb, w_ref[:, j * tn:(j + 1) * tn]).astype(o_ref.dtype)

    return _call(
        body, name="inproj", grid=(s // tm,),
        in_specs=[pl.BlockSpec((tm, d), lambda i: (i, 0)),
                  pl.BlockSpec((d, n), lambda i: (0, 0), pipeline_mode=pl.Buffered(1))],
        out_specs=[pl.BlockSpec((tm, n), lambda i: (i, 0)),
                   pl.BlockSpec((tm, d), lambda i: (i, 0))],
        out_shape=[jax.ShapeDtypeStruct((s, n), BF16), jax.ShapeDtypeStruct((s, d), BF16)],
        compiler_params=_cparams("parallel"),
    )(x, w)


def _small_matmul(a, b, trans_a, out_dtype, name):
    m = a.shape[1] if trans_a else a.shape[0]
    n = b.shape[1]

    def body(a_ref, b_ref, o_ref):
        av = a_ref[...].astype(MXU_DTYPE)
        bv = b_ref[...].astype(MXU_DTYPE)
        r = _dot_tn(av, bv) if trans_a else _dot(av, bv)
        o_ref[...] = r.astype(out_dtype)

    return _call(
        body, name=name,
        in_specs=[pl.BlockSpec(memory_space=pltpu.VMEM)] * 2,
        out_specs=pl.BlockSpec(memory_space=pltpu.VMEM),
        out_shape=jax.ShapeDtypeStruct((m, n), out_dtype),
        compiler_params=pltpu.CompilerParams(vmem_limit_bytes=VMEM_LIMIT),
    )(a, b)


def _piece_blocks(pieces, blk):
    offs, nbs, o = [], [], 0
    for p in pieces:
        nb = p.shape[1] // blk
        offs.append(o)
        nbs.append(nb)
        o += nb
    return offs, nbs, o


def _dx_matmul(pieces, wt, addend, token=None):
    s = pieces[0].shape[0]
    n_in, d = wt.shape
    tm = 512
    np_ = len(pieces)
    extra = [] if token is None else [token]

    def body(*refs):
        a_refs = refs[:np_]
        w_ref, add_ref = refs[np_:np_ + 2]
        o_ref = refs[-1]
        a = jnp.concatenate([r[...] for r in a_refs], axis=1)
        o_ref[...] = add_ref[...] + _dot(a, w_ref[...])

    in_specs = [pl.BlockSpec((tm, p.shape[1]), lambda i: (i, 0)) for p in pieces]
    in_specs += [pl.BlockSpec((n_in, d), lambda i: (0, 0), pipeline_mode=pl.Buffered(1)),
                 pl.BlockSpec((tm, d), lambda i: (i, 0))]
    in_specs += [pl.BlockSpec((SUBLANES, LANES), lambda i: (0, 0)) for _ in extra]
    return _call(
        body, name="dx_matmul", grid=(s // tm,),
        in_specs=in_specs,
        out_specs=pl.BlockSpec((tm, d), lambda i: (i, 0)),
        out_shape=jax.ShapeDtypeStruct((s, d), F32),
        compiler_params=_cparams("parallel"),
    )(*pieces, wt, addend, *extra)


def _dw_matmul(x, pieces):
    s, d = x.shape
    tn = 1024
    tk = min(1024, s)
    offs, nbs, nj = _piece_blocks(pieces, tn)
    np_ = len(pieces)
    nk = s // tk

    def body(*refs):
        x_ref = refs[0]
        b_refs = refs[1:1 + np_]
        o_ref, acc = refs[1 + np_:]
        j = pl.program_id(0)
        k = pl.program_id(1)

        @pl.when(k == 0)
        def _():
            acc[...] = jnp.zeros_like(acc)

        for p in range(np_):
            @pl.when((j >= offs[p]) & (j < offs[p] + nbs[p]))
            def _(p=p):
                acc[...] += _dot_tn(x_ref[...], b_refs[p][...])

        @pl.when(k == nk - 1)
        def _():
            o_ref[...] = acc[...].astype(o_ref.dtype)

    def bmap(p):
        def f(j, k):
            inside = (j >= offs[p]) & (j < offs[p] + nbs[p])
            return (jnp.where(inside, k, 0), jnp.clip(j - offs[p], 0, nbs[p] - 1))
        return f

    in_specs = [pl.BlockSpec((tk, d), lambda j, k: (k, 0))]
    in_specs += [pl.BlockSpec((tk, tn), bmap(p)) for p in range(np_)]
    return _call(
        body, name="dw_matmul", grid=(nj, nk),
        in_specs=in_specs,
        out_specs=pl.BlockSpec((d, tn), lambda j, k: (0, j)),
        out_shape=jax.ShapeDtypeStruct((d, nj * tn), BF16),
        scratch_shapes=[pltpu.VMEM((d, tn), F32)],
        compiler_params=_cparams("parallel", "arbitrary"),
    )(x, *pieces)


def _rel_onehot():
    j = lax.broadcasted_iota(jnp.int32, (N_REL_PAD, ROLL_W), 1)
    kk = lax.broadcasted_iota(jnp.int32, (N_REL_PAD, ROLL_W), 0)
    dd = jnp.where(j < TKEYS, j, j - ROLL_W)
    idx = jnp.clip(N_PREV * CHUNK - dd, -REL_CLIP, REL_CLIP) + REL_CLIP
    return jnp.where(idx == kk, 1.0, 0.0).astype(F32)


def _band_mask():
    r = lax.broadcasted_iota(jnp.int32, (TQ, TKEYS), 0) // CHUNK
    m = lax.broadcasted_iota(jnp.int32, (TQ, TKEYS), 1) // CHUNK
    return (m >= r) & (m <= r + N_PREV)


def _tile_bias(table_pad):
    def body(t_ref, o_ref):
        g = jnp.dot(t_ref[...], _rel_onehot(), preferred_element_type=F32,
                    precision=lax.Precision.HIGHEST)
        band = _band_mask()
        for h in range(N_HEADS):
            gh = jnp.broadcast_to(g[h:h + 1, :], (TQ, ROLL_W))
            rolled = pltpu.roll(gh, 0, 1, stride=1, stride_axis=0)
            o_ref[h] = jnp.where(band, rolled[:, :TKEYS], NEG)

    return _call(
        body, name="tile_bias",
        in_specs=[pl.BlockSpec(memory_space=pltpu.VMEM)],
        out_specs=pl.BlockSpec(memory_space=pltpu.VMEM),
        out_shape=jax.ShapeDtypeStruct((N_HEADS, TQ, TKEYS), F32),
        compiler_params=pltpu.CompilerParams(vmem_limit_bytes=VMEM_LIMIT),
    )(table_pad)


def _tile_bias_bwd(dtb):
    def body(d_ref, o_ref, g_ref):
        zpad = jnp.zeros((TQ, ROLL_W - TKEYS), F32)
        rr = lax.broadcasted_iota(jnp.int32, (TQ, TQ), 0)
        cc = lax.broadcasted_iota(jnp.int32, (TQ, TQ), 1)
        flip = jnp.where(rr + cc == TQ - 1, 1.0, 0.0).astype(F32)
        for h in range(N_HEADS):
            xh = jnp.concatenate([d_ref[h], zpad], axis=1)
            xf = jnp.dot(flip, xh, preferred_element_type=F32, precision=lax.Precision.HIGHEST)
            rolled = pltpu.roll(xf, 0, 1, stride=1, stride_axis=0)
            g_ref[h:h + 1, :] = jnp.sum(rolled, axis=0, keepdims=True)
        g = pltpu.roll(g_ref[...], ROLL_W - (TQ - 1), 1)
        o_ref[...] = lax.dot_general(g, _rel_onehot(), (((1,), (1,)), ((), ())),
                                     preferred_element_type=F32, precision=lax.Precision.HIGHEST)

    return _call(
        body, name="tile_bias_bwd",
        in_specs=[pl.BlockSpec(memory_space=pltpu.VMEM)],
        out_specs=pl.BlockSpec(memory_space=pltpu.VMEM),
        out_shape=jax.ShapeDtypeStruct((N_HEADS, N_REL_PAD), F32),
        scratch_shapes=[pltpu.VMEM((N_HEADS, ROLL_W), F32)],
        compiler_params=pltpu.CompilerParams(vmem_limit_bytes=VMEM_LIMIT),
    )(dtb)


HB = 4
HBW = HB * HEAD_DIM
ATTN_SCALE = 0.125
assert ATTN_SCALE == 1.0 / math.sqrt(HEAD_DIM)


def _head_masks():
    lane = lax.broadcasted_iota(jnp.int32, (1, HBW), 1) // HEAD_DIM
    return [lane == hh for hh in range(HB)]


def _select_heads(masks, parts):
    out = parts[-1]
    for hh in range(HB - 2, -1, -1):
        out = jnp.where(masks[hh], parts[hh], out)
    return out


def _attn_probs(qm, kcat, tb, valid):
    s = _dot_nt(qm, kcat) + tb
    if valid is not None:
        s = jnp.where(valid, s, NEG)
    m = jnp.max(s, axis=-1, keepdims=True)
    e = jnp.exp(s - m)
    return e * (1.0 / jnp.sum(e, axis=-1, keepdims=True))


def _key_valid(i):
    col = lax.broadcasted_iota(jnp.int32, (TQ, TKEYS), 1)
    return col >= jnp.maximum(2 - i, 0) * TQ


def _kv_specs(col0, nt):
    def spec(back):
        return pl.BlockSpec((TQ, HBW), lambda hp, i: (jnp.clip(i - back, 0, nt - 1), col0 + hp))
    return [spec(2), spec(1), spec(0)]


def _attn_fwd(h, tb):
    s = h.shape[0]
    nt = s // TQ
    nhp = N_HEADS // HB

    def body(q_ref, k0, k1, k2, v0, v1, v2, tb_ref, o_ref):
        i = pl.program_id(1)

        def tile(valid):
            masks = _head_masks()
            qs = q_ref[...].astype(MXU_DTYPE) * ATTN_SCALE
            kcat = jnp.concatenate([k0[...], k1[...], k2[...]], axis=0).astype(MXU_DTYPE)
            vcat = jnp.concatenate([v0[...], v1[...], v2[...]], axis=0).astype(MXU_DTYPE)
            outs = []
            for hh in range(HB):
                qm = jnp.where(masks[hh], qs, jnp.zeros_like(qs))
                p = _attn_probs(qm, kcat, tb_ref[hh], valid)
                outs.append(_dot(p.astype(MXU_DTYPE), vcat))
            o_ref[...] = _select_heads(masks, outs).astype(o_ref.dtype)

        @pl.when(i < 2)
        def _():
            tile(_key_valid(i))

        @pl.when(i >= 2)
        def _():
            tile(None)

    in_specs = [pl.BlockSpec((TQ, HBW), lambda hp, i: (i, hp))]
    in_specs += _kv_specs(nhp, nt) + _kv_specs(2 * nhp, nt)
    in_specs += [pl.BlockSpec((HB, TQ, TKEYS), lambda hp, i: (hp, 0, 0))]
    return _call(
        body, name="attn_fwd", grid=(nhp, nt),
        in_specs=in_specs,
        out_specs=pl.BlockSpec((TQ, HBW), lambda hp, i: (i, hp)),
        out_shape=jax.ShapeDtypeStruct((s, E_MIX), BF16),
        compiler_params=_cparams("parallel", "parallel"),
    )(h, h, h, h, h, h, h, tb)


def _attn_bwd(h, tb, d_mix, token=None):
    s = h.shape[0]
    nt = s // TQ
    nhp = N_HEADS // HB
    extra = [] if token is None else [token]

    def body(q_ref, k0, k1, k2, v0, v1, v2, tb_ref, do_ref, *rest):
        dq_ref, dk_ref, dv_ref, dtb_ref, dk_acc, dv_acc = rest[len(extra):]
        i = pl.program_id(1)

        @pl.when(i == 0)
        def _():
            dk_acc[...] = jnp.zeros_like(dk_acc)
            dv_acc[...] = jnp.zeros_like(dv_acc)
            dtb_ref[...] = jnp.zeros_like(dtb_ref)

        @pl.when((i > 0) & (i < nt))
        def _():
            dk_acc[i % 3] = jnp.zeros((TQ, HBW), F32)
            dv_acc[i % 3] = jnp.zeros((TQ, HBW), F32)

        def tile(valid):
            masks = _head_masks()
            qs = q_ref[...].astype(MXU_DTYPE) * ATTN_SCALE
            do2 = do_ref[...].astype(MXU_DTYPE)
            kcat = jnp.concatenate([k0[...], k1[...], k2[...]], axis=0).astype(MXU_DTYPE)
            vcat = jnp.concatenate([v0[...], v1[...], v2[...]], axis=0).astype(MXU_DTYPE)
            ks = kcat * ATTN_SCALE
            dqs, dks, dvs = [], [], []
            for hh in range(HB):
                qm = jnp.where(masks[hh], qs, jnp.zeros_like(qs))
                dom = jnp.where(masks[hh], do2, jnp.zeros_like(do2))
                p = _attn_probs(qm, kcat, tb_ref[hh], valid)
                dp = _dot_nt(dom, vcat)
                ds = p * (dp - jnp.sum(p * dp, axis=-1, keepdims=True))
                dtb_ref[hh] += ds
                dsb = ds.astype(MXU_DTYPE)
                dqs.append(_dot(dsb, ks))
                dks.append(_dot_tn(dsb, qs))
                dvs.append(_dot_tn(p.astype(MXU_DTYPE), do2))
            dq_ref[...] = _select_heads(masks, dqs).astype(dq_ref.dtype)
            dkc = _select_heads(masks, dks)
            dvc = _select_heads(masks, dvs)
            for jj in range(3):
                slot = (i + 1 + jj) % 3
                dk_acc[slot] += dkc[jj * TQ:(jj + 1) * TQ]
                dv_acc[slot] += dvc[jj * TQ:(jj + 1) * TQ]

        @pl.when(i < 2)
        def _():
            tile(_key_valid(i))

        @pl.when((i >= 2) & (i < nt))
        def _():
            tile(None)

        @pl.when(i >= 2)
        def _():
            slot = (i - 2) % 3
            dk_ref[...] = dk_acc[slot].astype(dk_ref.dtype)
            dv_ref[...] = dv_acc[slot].astype(dv_ref.dtype)

    qmap = lambda hp, i: (jnp.minimum(i, nt - 1), hp)
    kvout = lambda hp, i: (jnp.maximum(i - 2, 0), hp)
    in_specs = [pl.BlockSpec((TQ, HBW), qmap)]
    in_specs += _kv_specs(nhp, nt) + _kv_specs(2 * nhp, nt)
    in_specs += [pl.BlockSpec((HB, TQ, TKEYS), lambda hp, i: (hp, 0, 0)),
                 pl.BlockSpec((TQ, HBW), qmap)]
    in_specs += [pl.BlockSpec((SUBLANES, LANES), lambda hp, i: (0, 0)) for _ in extra]
    blk = (TQ, HBW)
    return _call(
        body, name="attn_bwd", grid=(nhp, nt + 2),
        in_specs=in_specs,
        out_specs=[pl.BlockSpec(blk, qmap), pl.BlockSpec(blk, kvout), pl.BlockSpec(blk, kvout),
                   pl.BlockSpec((HB, TQ, TKEYS), lambda hp, i: (hp, 0, 0))],
        out_shape=[jax.ShapeDtypeStruct((s, E_MIX), BF16)] * 3
        + [jax.ShapeDtypeStruct((N_HEADS, TQ, TKEYS), F32)],
        scratch_shapes=[pltpu.VMEM((3, TQ, HBW), F32)] * 2,
        compiler_params=_cparams("parallel", "arbitrary"),
    )(h, h, h, h, h, h, h, tb, d_mix, *extra)


CONV_TS = 512
HALO = 2 * SUBLANES


def _shift_down(prev, cur, k):
    rolled = pltpu.roll(cur, k, 0)
    row = lax.broadcasted_iota(jnp.int32, (HALO, cur.shape[1]), 0)
    top = jnp.where(row < k, pltpu.roll(prev, k, 0), rolled[:HALO])
    return jnp.concatenate([top, rolled[HALO:]], axis=0)


def _shift_up(cur, nxt, k):
    ts = cur.shape[0]
    rolled = pltpu.roll(cur, ts - k, 0)
    row = lax.broadcasted_iota(jnp.int32, (HALO, cur.shape[1]), 0)
    bottom = jnp.where(row >= HALO - k, pltpu.roll(nxt, HALO - k, 0), rolled[ts - HALO:])
    return jnp.concatenate([rolled[:ts - HALO], bottom], axis=0)


def _conv_specs(ts, nb):
    tile = lambda c: pl.BlockSpec((ts, E_MIX), lambda i: (i, c))
    prev = lambda c: pl.BlockSpec((HALO, E_MIX), lambda i: (jnp.maximum(i * (ts // HALO) - 1, 0), c))
    return tile, prev


def _conv_fwd(h, w8):
    s = h.shape[0]
    ts = CONV_TS
    nb = s // ts
    tile, prev = _conv_specs(ts, nb)

    def body(bg, cg, u, cgp, up, w_ref, o_ref):
        i = pl.program_id(0)
        a = cg[...].astype(F32) * u[...].astype(F32)
        ap = jnp.where(i > 0, cgp[...].astype(F32) * up[...].astype(F32), 0.0)
        w = w_ref[...]
        conv = w[0:1] * _shift_down(ap, a, 2) + w[1:2] * _shift_down(ap, a, 1) + w[2:3] * a
        o_ref[...] = (bg[...].astype(F32) * conv).astype(o_ref.dtype)

    return _call(
        body, name="conv_fwd", grid=(nb,),
        in_specs=[tile(0), tile(1), tile(2), prev(1), prev(2),
                  pl.BlockSpec((SUBLANES, E_MIX), lambda i: (0, 0))],
        out_specs=pl.BlockSpec((ts, E_MIX), lambda i: (i, 0)),
        out_shape=jax.ShapeDtypeStruct((s, E_MIX), BF16),
        compiler_params=_cparams("parallel"),
    )(h, h, h, h, h, w8)


def _conv_bwd(h, w8, d_mix):
    s = h.shape[0]
    ts = CONV_TS
    nb = s // ts
    tile, prev = _conv_specs(ts, nb)
    nrow = s // HALO
    nxt = lambda c: pl.BlockSpec((HALO, E_MIX), lambda i: (jnp.minimum((i + 1) * (ts // HALO), nrow - 1), c))

    def body(bg, cg, u, cgp, up, bgn, dmix, dmixn, w_ref, dbg_ref, dcg_ref, du_ref, dw_ref):
        i = pl.program_id(0)

        @pl.when(i == 0)
        def _():
            dw_ref[...] = jnp.zeros_like(dw_ref)

        cgv, uv = cg[...].astype(F32), u[...].astype(F32)
        a = cgv * uv
        ap = jnp.where(i > 0, cgp[...].astype(F32) * up[...].astype(F32), 0.0)
        a1 = _shift_down(ap, a, 1)
        a2 = _shift_down(ap, a, 2)
        w = w_ref[...]
        conv = w[0:1] * a2 + w[1:2] * a1 + w[2:3] * a
        dm = dmix[...].astype(F32)
        dbg_ref[...] = (dm * conv).astype(dbg_ref.dtype)
        dc = dm * bg[...].astype(F32)
        dcn = jnp.where(i < nb - 1, dmixn[...].astype(F32) * bgn[...].astype(F32), 0.0)
        da = w[2:3] * dc + w[1:2] * _shift_up(dc, dcn, 1) + w[0:1] * _shift_up(dc, dcn, 2)
        dcg_ref[...] = (da * uv).astype(dcg_ref.dtype)
        du_ref[...] = (da * cgv).astype(du_ref.dtype)
        dw_ref[0:1, :] += jnp.sum(dc * a2, axis=0, keepdims=True)
        dw_ref[1:2, :] += jnp.sum(dc * a1, axis=0, keepdims=True)
        dw_ref[2:3, :] += jnp.sum(dc * a, axis=0, keepdims=True)

    full = lambda: pl.BlockSpec((ts, E_MIX), lambda i: (i, 0))
    return _call(
        body, name="conv_bwd", grid=(nb,),
        in_specs=[tile(0), tile(1), tile(2), prev(1), prev(2), nxt(0),
                  full(), pl.BlockSpec((HALO, E_MIX), lambda i: (jnp.minimum((i + 1) * (ts // HALO), nrow - 1), 0)),
                  pl.BlockSpec((SUBLANES, E_MIX), lambda i: (0, 0))],
        out_specs=[full(), full(), full(), pl.BlockSpec((SUBLANES, E_MIX), lambda i: (0, 0))],
        out_shape=[jax.ShapeDtypeStruct((s, E_MIX), BF16)] * 3
        + [jax.ShapeDtypeStruct((SUBLANES, E_MIX), F32)],
        compiler_params=_cparams("arbitrary"),
    )(h, h, h, h, h, h, d_mix, d_mix, w8)


def _mem_probs(qh, kh):
    s = _dot_nt(qh, kh) / math.sqrt(MEM_HEAD_DIM)
    m = jnp.max(s, axis=-1, keepdims=True)
    e = jnp.exp(s - m)
    return e / jnp.sum(e, axis=-1, keepdims=True)


def _sigmoid(z):
    return 1.0 / (1.0 + jnp.exp(-z))


def _layer_out_fwd(x, h, mix, kv, w_out, g, b):
    s, d = x.shape
    ts = 2 * TS

    def body(x_ref, mix_ref, qm_ref, z0, z1, z2, kv_ref, wo_ref, g_ref, b_ref,
             xn_ref, r_ref, mem_ref):
        qm = qm_ref[...].astype(MXU_DTYPE)
        kvb = kv_ref[...].astype(MXU_DTYPE)
        mems = []
        for hh in range(MEM_HEADS):
            lo = hh * MEM_HEAD_DIM
            p = _mem_probs(qm[:, lo:lo + MEM_HEAD_DIM], kvb[:, lo:lo + MEM_HEAD_DIM])
            mems.append(_dot(p.astype(MXU_DTYPE), kvb[:, E_MEM + lo:E_MEM + lo + MEM_HEAD_DIM]))
        mem = jnp.concatenate(mems, axis=1).astype(mem_ref.dtype)
        mem_ref[...] = mem
        mixv = mix_ref[...].astype(F32)
        half = E_MIX // 2
        parts = [mixv[:, :half], mixv[:, half:], mem.astype(F32)]
        out = jnp.zeros((ts, d), F32)
        for c, zr in enumerate((z0, z1, z2)):
            zv = zr[...].astype(F32)
            y = (parts[c] * (zv * _sigmoid(zv))).astype(MXU_DTYPE)
            out += _dot(y, wo_ref[c * half:(c + 1) * half, :])
        r = DN_ALPHA * x_ref[...] + out
        r_ref[...] = r
        mu = jnp.mean(r, axis=-1, keepdims=True)
        rc = r - mu
        var = jnp.mean(rc * rc, axis=-1, keepdims=True)
        xn_ref[...] = rc * lax.rsqrt(var + LN_EPS) * g_ref[...] + b_ref[...]

    row = lambda w, c: pl.BlockSpec((ts, w), lambda i: (i, c))
    const = lambda shp: pl.BlockSpec(shp, lambda i: (0, 0))
    return _call(
        body, name="layer_out_fwd", grid=(s // ts,),
        in_specs=[row(d, 0), row(E_MIX, 0), row(E_MEM, QM_BLK),
                  row(E_MEM, Z_BLK), row(E_MEM, Z_BLK + 1), row(E_MEM, Z_BLK + 2),
                  const((N_MEM, 2 * E_MEM)), const((E_BRANCH, d)), const((1, d)), const((1, d))],
        out_specs=[row(d, 0), row(d, 0), row(E_MEM, 0)],
        out_shape=[jax.ShapeDtypeStruct((s, d), F32), jax.ShapeDtypeStruct((s, d), F32),
                   jax.ShapeDtypeStruct((s, E_MEM), BF16)],
        compiler_params=_cparams("parallel"),
    )(x, mix, h, h, h, h, kv, w_out, g, b)


def _layer_out_bwd(dxn, r, g, h, mix, mem, kv, w_out_t):
    s, d = r.shape
    ts = 2 * TS
    nb = s // ts
    half = E_MIX // 2
    inv = 1.0 / math.sqrt(MEM_HEAD_DIM)

    def body(dxn_ref, r_ref, g_ref, mix_ref, mem_ref, qm_ref, z0, z1, z2, kv_ref, wo_ref,
             dxr_ref, dmix_ref, dqz_ref, dwo_ref, dkv_ref, dg_ref, db_ref, dw_acc):
        i = pl.program_id(0)

        @pl.when(i == 0)
        def _():
            dw_acc[...] = jnp.zeros_like(dw_acc)
            dkv_ref[...] = jnp.zeros_like(dkv_ref)
            dg_ref[...] = jnp.zeros_like(dg_ref)
            db_ref[...] = jnp.zeros_like(db_ref)

        dxn_v = dxn_ref[...]
        rv = r_ref[...]
        mu = jnp.mean(rv, axis=-1, keepdims=True)
        rc = rv - mu
        var = jnp.mean(rc * rc, axis=-1, keepdims=True)
        rstd = lax.rsqrt(var + LN_EPS)
        xhat = rc * rstd
        dg_ref[...] += jnp.sum(dxn_v * xhat, axis=0, keepdims=True)
        db_ref[...] += jnp.sum(dxn_v, axis=0, keepdims=True)
        dxh = dxn_v * g_ref[...]
        m1 = jnp.mean(dxh, axis=-1, keepdims=True)
        m2 = jnp.mean(dxh * xhat, axis=-1, keepdims=True)
        dr = rstd * (dxh - m1 - xhat * m2)
        dxr_ref[...] = DN_ALPHA * dr
        dout = dr.astype(MXU_DTYPE)
        mixv = mix_ref[...].astype(F32)
        parts = [mixv[:, :half], mixv[:, half:], mem_ref[...].astype(F32)]
        dcs = []
        for c, zr in enumerate((z0, z1, z2)):
            lo = c * half
            zv = zr[...].astype(F32)
            sg = _sigmoid(zv)
            sl = zv * sg
            dy = _dot(dout, wo_ref[:, lo:lo + half])
            y = (parts[c] * sl).astype(MXU_DTYPE)
            dw_acc[lo:lo + half, :] += _dot_tn(y, dout)
            dcs.append(dy * sl)
            dqz_ref[:, E_MEM + lo:E_MEM + lo + half] = (
                dy * parts[c] * (sg * (1.0 + zv * (1.0 - sg)))).astype(dqz_ref.dtype)
        dmix_ref[...] = jnp.concatenate(dcs[:2], axis=1).astype(dmix_ref.dtype)

        qm = qm_ref[...].astype(MXU_DTYPE)
        kvb = kv_ref[...].astype(MXU_DTYPE)
        dmb = dcs[2].astype(MXU_DTYPE)
        for hh in range(MEM_HEADS):
            lo = hh * MEM_HEAD_DIM
            qh = qm[:, lo:lo + MEM_HEAD_DIM]
            kh = kvb[:, lo:lo + MEM_HEAD_DIM]
            vh = kvb[:, E_MEM + lo:E_MEM + lo + MEM_HEAD_DIM]
            dmh = dmb[:, lo:lo + MEM_HEAD_DIM]
            p = _mem_probs(qh, kh)
            dp = _dot_nt(dmh, vh)
            ds = p * (dp - jnp.sum(p * dp, axis=-1, keepdims=True))
            dsb = (ds * inv).astype(MXU_DTYPE)
            dqz_ref[:, lo:lo + MEM_HEAD_DIM] = _dot(dsb, kh).astype(dqz_ref.dtype)
            dkv_ref[:, lo:lo + MEM_HEAD_DIM] += _dot_tn(dsb, qh)
            dkv_ref[:, E_MEM + lo:E_MEM + lo + MEM_HEAD_DIM] += _dot_tn(p.astype(MXU_DTYPE), dmh)

        @pl.when(i == nb - 1)
        def _():
            dwo_ref[...] = dw_acc[...].astype(dwo_ref.dtype)

    row = lambda w, c: pl.BlockSpec((ts, w), lambda i: (i, c))
    const = lambda shp: pl.BlockSpec(shp, lambda i: (0, 0))
    once = lambda shp: pl.BlockSpec(shp, lambda i: (0, 0), pipeline_mode=pl.Buffered(1))
    return _call(
        body, name="layer_out_bwd", grid=(nb,),
        in_specs=[row(d, 0), row(d, 0), const((1, d)), row(E_MIX, 0), row(E_MEM, 0),
                  row(E_MEM, QM_BLK), row(E_MEM, Z_BLK), row(E_MEM, Z_BLK + 1), row(E_MEM, Z_BLK + 2),
                  once((N_MEM, 2 * E_MEM)), once((d, E_BRANCH))],
        out_specs=[row(d, 0), row(E_MIX, 0), row(E_MEM + E_BRANCH, 0),
                   const((E_BRANCH, d)), const((N_MEM, 2 * E_MEM)), const((1, d)), const((1, d))],
        out_shape=[jax.ShapeDtypeStruct((s, d), F32), jax.ShapeDtypeStruct((s, E_MIX), BF16),
                   jax.ShapeDtypeStruct((s, E_MEM + E_BRANCH), BF16),
                   jax.ShapeDtypeStruct((E_BRANCH, d), BF16),
                   jax.ShapeDtypeStruct((N_MEM, 2 * E_MEM), F32),
                   jax.ShapeDtypeStruct((1, d), F32), jax.ShapeDtypeStruct((1, d), F32)],
        scratch_shapes=[pltpu.VMEM((E_BRANCH, d), F32)],
        compiler_params=_cparams("arbitrary"),
    )(dxn, r, g, mix, mem, h, h, h, h, kv, w_out_t)


def _loss_head(y, target):
    s, d = y.shape
    ts = 512

    def body(y_ref, t_ref, l_ref, dy_ref):
        @pl.when(pl.program_id(0) == 0)
        def _():
            l_ref[...] = jnp.zeros_like(l_ref)

        e = y_ref[...] - t_ref[...]
        dy_ref[...] = e * (1.0 / d)
        l_ref[...] += (0.5 / d) * jnp.sum(jnp.sum(e * e, axis=1, keepdims=True), axis=0, keepdims=True)

    return _call(
        body, name="loss_head", grid=(s // ts,),
        in_specs=[pl.BlockSpec((ts, d), lambda i: (i, 0))] * 2,
        out_specs=[pl.BlockSpec((1, 1), lambda i: (0, 0)), pl.BlockSpec((ts, d), lambda i: (i, 0))],
        out_shape=[jax.ShapeDtypeStruct((1, 1), F32), jax.ShapeDtypeStruct((s, d), F32)],
        compiler_params=_cparams("arbitrary"),
    )(y, target)


def _local_step(x, mem, get_weights, put_grads, rel_bias, conv_w, ln_g, ln_b, target):
    biases = [_tile_bias(jnp.pad(rel_bias[a], ((0, 0), (0, N_REL_PAD - N_REL)))) for a in range(DEPTH // 2)]

    saved = []
    xl = x
    for layer in range(DEPTH):
        w_in_l, rest = get_weights(layer, xl if layer else biases)
        h, xb = _inproj(xl, w_in_l)
        w_kv_l, w_out_l = rest(h)
        if layer % 2 == 0:
            aux = biases[layer // 2]
            mix = _attn_fwd(h, aux)
        else:
            aux = jnp.pad(conv_w()[layer // 2], ((0, SUBLANES - 3), (0, 0)))
            mix = _conv_fwd(h, aux)
        kv = _small_matmul(mem, w_kv_l, False, F32, "kv_mem")
        xn, r, mem_out = _layer_out_fwd(xl, h, mix, kv, w_out_l,
                                        ln_g[layer][None], ln_b[layer][None])
        saved.append((xb, h, aux, mix, kv, r, mem_out, w_in_l, w_out_l))
        xl = xn

    loss, dx = _loss_head(xl, target)

    dgs, dbs, d_rel, d_conv = [], [], [], []
    for layer in reversed(range(DEPTH)):
        xb, h, aux, mix, kv, r, mem_out, w_in_l, w_out_l = saved[layer]
        dx_res, d_mix, dqz, dwo, dkv, dg, db = _layer_out_bwd(
            dx, r, ln_g[layer][None], h, mix, mem_out, kv, w_out_l.T)
        early = put_grads(layer, [1, 2], [_small_matmul(mem, dkv, True, BF16, "dw_kv"), dwo])
        if layer % 2 == 0:
            dq, dk, dv, dtb = _attn_bwd(h, aux, d_mix, early)
            d_rel.append(_tile_bias_bwd(dtb)[:, :N_REL])
            pieces = [dq, dk, dv, dqz]
        else:
            dbg, dcg, du, dw8 = _conv_bwd(h, aux, d_mix)
            d_conv.append(dw8[:3])
            pieces = [dbg, dcg, du, dqz]
        token = put_grads(layer, [0], [_dw_matmul(xb, pieces)])
        dgs.append(dg[0])
        dbs.append(db[0])
        dx = _dx_matmul(pieces, w_in_l.T, dx_res, token)

    rev = lambda lst: jnp.stack(lst[::-1])
    return loss, dx, rev(d_rel), rev(d_conv), rev(dgs), rev(dbs)


def _me():
    return lax.axis_index("x"), lax.axis_index("y"), lax.axis_index("c")


def _peer(k):
    x, y, c = _me()
    kx, ky, kc = (k >> 2) & 1, (k >> 1) & 1, k & 1
    return (1 - x if kx else x, 1 - y if ky else y, 1 - c if kc else c)


def _lin(dev):
    return 4 * dev[0] + 2 * dev[1] + dev[2]


ANY = pl.BlockSpec(memory_space=pl.ANY)


def _exchange(srcs, dst_shapes, src_slice, dst_slice, name):
    na = len(srcs)

    def body(*refs):
        src_refs = refs[:na]
        dst_refs = refs[na:2 * na]
        send_sems, recv_sems, local_sems = refs[2 * na:]
        me = _lin(_me())
        copies = []
        for a in range(na):
            loc = pltpu.make_async_copy(src_slice(a, src_refs[a], me), dst_slice(a, dst_refs[a], me),
                                        local_sems.at[a])
            loc.start()
            copies.append(loc)
            for k in range(1, N_DEV):
                peer = _peer(k)
                cp = pltpu.make_async_remote_copy(
                    src_ref=src_slice(a, src_refs[a], _lin(peer)),
                    dst_ref=dst_slice(a, dst_refs[a], me),
                    send_sem=send_sems.at[a, k - 1], recv_sem=recv_sems.at[a, k - 1],
                    device_id=peer, device_id_type=pl.DeviceIdType.MESH)
                cp.start()
                copies.append(cp)
        for cp in copies:
            cp.wait()

    return _call(
        body, name=name,
        in_specs=[ANY] * na, out_specs=[ANY] * na,
        out_shape=[jax.ShapeDtypeStruct(shp, s.dtype) for shp, s in zip(dst_shapes, srcs)],
        scratch_shapes=[pltpu.SemaphoreType.DMA((na, N_DEV - 1)),
                        pltpu.SemaphoreType.DMA((na, N_DEV - 1)),
                        pltpu.SemaphoreType.DMA((na,))],
    )(*srcs)


def _gather_to_all(src, name):
    return _exchange([src], [(N_DEV,) + src.shape], lambda a, ref, p: ref,
                     lambda a, ref, me: ref.at[me], name)[0]


HBM = pl.BlockSpec(memory_space=pltpu.HBM)
SEM = pl.BlockSpec(memory_space=pltpu.SEMAPHORE)
EFFECT = pltpu.SideEffectType.DATAFLOW_SIDE_EFFECTING
N_PEER = N_DEV - 1
N_KIND = 3


def _peer_copies(kind, src_ref, land_ref, send, recv, src_slice, dst_slice):
    me = _lin(_me())
    copies = []
    for k in range(1, N_DEV):
        peer = _peer(k)
        copies.append(pltpu.make_async_remote_copy(
            src_ref=src_slice(kind, src_ref, _lin(peer)),
            dst_ref=dst_slice(kind, land_ref, me, k),
            send_sem=send.at[k - 1], recv_sem=recv.at[k - 1],
            device_id=peer, device_id_type=pl.DeviceIdType.MESH))
    return copies


def _own_copy(kind, src_ref, land_ref, send, src_slice, dst_slice):
    me = _lin(_me())
    return pltpu.make_async_copy(src_slice(kind, src_ref, me), dst_slice(kind, land_ref, me, 0),
                                 send.at[N_PEER])


def _split_start(srcs, kinds, land_shapes, src_slice, dst_slice, name, own=False):
    na = len(srcs)

    def body(*refs):
        src_refs, land_refs = refs[:na], refs[na:2 * na]
        sems = refs[2 * na:4 * na]
        token = refs[-1]
        for a in range(na):
            for cp in _peer_copies(kinds[a], src_refs[a], land_refs[a], sems[2 * a], sems[2 * a + 1],
                                   src_slice, dst_slice):
                cp.start()
            if own:
                _own_copy(kinds[a], src_refs[a], land_refs[a], sems[2 * a], src_slice, dst_slice).start()
        token[...] = jnp.zeros_like(token)

    sem_shape = pltpu.SemaphoreType.DMA((N_DEV,))
    lands = [lax.empty(shp, s.dtype) for shp, s in zip(land_shapes, srcs)]
    outs = _call(
        body, name=name,
        in_specs=[HBM] * (2 * na),
        out_specs=[SEM] * (2 * na) + [HBM] * (2 * na) + [pl.BlockSpec(memory_space=pltpu.VMEM)],
        out_shape=[sem_shape] * (2 * na)
        + [pltpu.HBM(s.shape, s.dtype) for s in srcs]
        + [pltpu.HBM(shp, s.dtype) for shp, s in zip(land_shapes, srcs)]
        + [jax.ShapeDtypeStruct((SUBLANES, LANES), F32)],
        input_output_aliases={i: 2 * na + i for i in range(2 * na)},
        compiler_params=pltpu.CompilerParams(has_side_effects=EFFECT),
    )(*[pltpu.with_memory_space_constraint(a, pltpu.HBM) for a in list(srcs) + lands])
    sems = [(outs[2 * a], outs[2 * a + 1]) for a in range(na)]
    thrus = outs[2 * na:3 * na]
    lands = outs[3 * na:4 * na]
    return sems, thrus, lands, outs[-1]


def _split_wait(sems, thrus, lands, kinds, src_slice, dst_slice, after, name, own=False):
    na = len(thrus)
    after = list(after) if isinstance(after, (list, tuple)) else [after]

    def body(*refs):
        src_refs, land_refs = refs[:na], refs[na:2 * na]
        sem_refs = refs[2 * na:4 * na]
        for a in range(na):
            for cp in _peer_copies(kinds[a], src_refs[a], land_refs[a], sem_refs[2 * a], sem_refs[2 * a + 1],
                                   src_slice, dst_slice):
                cp.wait_send()
                cp.wait_recv()
            if own:
                _own_copy(kinds[a], src_refs[a], land_refs[a], sem_refs[2 * a], src_slice, dst_slice).wait()

    outs = _call(
        body, name=name,
        in_specs=[HBM] * (2 * na) + [SEM] * (2 * na) + [ANY] * len(after),
        out_specs=[HBM] * (2 * na),
        out_shape=[pltpu.HBM(a.shape, a.dtype) for a in list(thrus) + list(lands)],
        input_output_aliases={i: i for i in range(2 * na)},
        compiler_params=pltpu.CompilerParams(has_side_effects=EFFECT),
    )(*thrus, *lands, *[s for pair in sems for s in pair], *after)
    return outs[na:]


def _shard_dims(c_in, r_kv, r_out):
    def sl(j, ref, p):
        if j == 0:
            return ref.at[:, pl.ds(pl.multiple_of(p * c_in, LANES), c_in)]
        r = r_kv if j == 1 else r_out
        return ref.at[pl.ds(pl.multiple_of(p * r, 2 * SUBLANES), r), :]
    return sl


def _adamw_math(w, g, m, v):
    m = ADAM_B1 * m + (1.0 - ADAM_B1) * g
    v = ADAM_B2 * v + (1.0 - ADAM_B2) * (g * g)
    m_hat = m / (1.0 - ADAM_B1 ** ADAM_STEP)
    v_hat = v / (1.0 - ADAM_B2 ** ADAM_STEP)
    delta = -ADAM_LR * (m_hat / (jnp.sqrt(v_hat) + ADAM_EPS) + ADAM_WD * w)
    return delta, m, v


def _reduce_adamw(parts, w, m, v, name):
    rows, cols = w.shape
    tr = rows
    for cand in (512, 256, 128, 64, 32, 16):
        if rows % cand == 0 and rows > cand:
            tr = cand
            break

    def body(p_ref, w_ref, m_ref, v_ref, g_out, d_out, m_out, v_out):
        g = p_ref[0].astype(F32)
        for s in range(1, N_DEV):
            g = g + p_ref[s].astype(F32)
        g_out[...] = g
        d_out[...], m_out[...], v_out[...] = _adamw_math(w_ref[...], g, m_ref[...], v_ref[...])

    blk = pl.BlockSpec((tr, cols), lambda i: (i, 0))
    return _call(
        body, name=name, grid=(rows // tr,),
        in_specs=[pl.BlockSpec((N_DEV, tr, cols), lambda i: (0, i, 0)), blk, blk, blk],
        out_specs=[blk] * 4,
        out_shape=[jax.ShapeDtypeStruct((rows, cols), F32)] * 4,
        compiler_params=_cparams("parallel"),
    )(parts, w, m, v)


def _reduce_adamw_layers(lands, owns, w, m, v, name, first=0, prev=None):
    depth, rows, cols = w.shape
    nl = len(lands)
    tr = rows
    for cand in (256, 192, 128):
        if rows % cand == 0:
            tr = cand
            break

    kept = [] if prev is None else list(prev)

    def body(*refs):
        land_refs, own_refs = refs[:nl], refs[nl:2 * nl]
        w_ref, m_ref, v_ref = refs[2 * nl:2 * nl + 3]
        g_out, d_out, m_out, v_out = refs[2 * nl + 3 + len(kept):]
        layer = pl.program_id(0)
        for a in range(nl):
            @pl.when(layer == a)
            def _(a=a):
                g = own_refs[a][...].astype(F32)
                for k in range(N_PEER):
                    g = g + land_refs[a][k].astype(F32)
                g_out[...] = g
                d_out[...], m_out[...], v_out[...] = _adamw_math(w_ref[...], g, m_ref[...], v_ref[...])

    def lmap(a):
        return lambda l, i: (0, jnp.where(l == a, i, 0), 0)

    def omap(a):
        return lambda l, i: (jnp.where(l == a, i, 0), 0)

    blk = pl.BlockSpec((None, tr, cols), lambda l, i: (l + first, i, 0))
    n_in = 2 * nl + 3
    return _call(
        body, name=name, grid=(nl, rows // tr),
        in_specs=[pl.BlockSpec((N_PEER, tr, cols), lmap(a)) for a in range(nl)]
        + [pl.BlockSpec((tr, cols), omap(a)) for a in range(nl)] + [blk, blk, blk] + [ANY] * len(kept),
        out_specs=[blk] * 4,
        out_shape=[jax.ShapeDtypeStruct((depth, rows, cols), F32)] * 4,
        input_output_aliases={n_in + i: i for i in range(len(kept))},
        compiler_params=_cparams("arbitrary", "arbitrary"),
    )(*lands, *owns, w, m, v, *kept)


SM_G, SM_B, SM_CONV, SM_REL = 0, 4, 8, 16
SM_ROWS = SM_REL + 2 * N_HEADS
REL_W = 384


def _pack_small(d_rel, d_conv, dg, db):
    buf = jnp.zeros((SM_ROWS, D_MODEL), F32)
    buf = buf.at[SM_G:SM_G + DEPTH].set(dg)
    buf = buf.at[SM_B:SM_B + DEPTH].set(db)
    buf = buf.at[SM_CONV:SM_CONV + 6].set(d_conv.reshape(6, E_MIX))
    buf = buf.at[SM_REL:, :N_REL].set(d_rel.reshape(2 * N_HEADS, N_REL))
    return buf


def kernel(x, mem, w_in, w_mem_kv, w_out, rel_bias, conv_w, ln_g, ln_b, loss_target, m_w_in, m_w_mem_kv, m_w_out, m_rel_bias, m_conv_w, m_ln_g, m_ln_b, v_w_in, v_w_mem_kv, v_w_out, v_rel_bias, v_conv_w, v_ln_g, v_ln_b):
    me = _lin(_me())
    c_in, r_kv, r_out, c_conv = w_in.shape[2], w_mem_kv.shape[1], w_out.shape[1], conv_w.shape[2]

    shard = _shard_dims(c_in, r_kv, r_out)
    own_start = lambda j: (0, me * c_in) if j == 0 else (me * (r_kv if j == 1 else r_out), 0)

    w_sh = [w_in.astype(BF16), w_mem_kv.astype(BF16), w_out.astype(BF16)]
    full_shapes = [(D_MODEL, N_DEV * c_in), (N_DEV * r_kv, w_mem_kv.shape[2]), (N_DEV * r_out, D_MODEL)]
    conv_kind = N_KIND
    conv_tile = jnp.pad(conv_w.reshape(6, c_conv), ((0, SUBLANES - 6), (0, 0)))
    ag_src = lambda j, ref, p: ref
    ag_dst = lambda j, ref, me_, k: ref.at[me_] if j == conv_kind else shard(j, ref, me_)
    kinds = list(range(N_KIND))
    ag_sems, ag_thrus, ag_lands, _ = _split_start(
        [conv_tile] + [w_sh[j][layer] for layer in range(DEPTH) for j in kinds], [conv_kind] + kinds * DEPTH,
        [(N_DEV,) + conv_tile.shape] + full_shapes * DEPTH, ag_src, ag_dst, "ag_start", own=True)
    conv_landed = []

    def get_weights(layer, x_layer):
        lo = 1 + layer * N_KIND

        def wait(idx, js, after, name):
            return _split_wait([ag_sems[a] for a in idx], [ag_thrus[a] for a in idx],
                               [ag_lands[a] for a in idx], js, ag_src, ag_dst, after, name, own=True)

        if layer == 0:
            conv_land, w_in_l = wait([0, lo], [conv_kind, 0], x_layer, "ag_wait_in_0")
            conv_landed.append(conv_land)
        else:
            w_in_l, = wait([lo], [0], x_layer, "ag_wait_in_%d" % layer)
        return w_in_l, lambda h: wait([lo + 1, lo + 2], [1, 2], h, "ag_wait_kv_out_%d" % layer)

    def conv_full():
        return jnp.transpose(conv_landed[0][:, :6], (1, 0, 2)).reshape(2, 3, N_DEV * c_conv)

    rs_src = shard
    rs_dst = lambda j, ref, me_, k: ref.at[k - 1]
    rs_shapes = [(N_PEER, D_MODEL, c_in), (N_PEER, r_kv, w_mem_kv.shape[2]), (N_PEER, r_out, D_MODEL)]
    own_sizes = [(D_MODEL, c_in), (r_kv, w_mem_kv.shape[2]), (r_out, D_MODEL)]
    pending = {}

    held = {}

    def put_grads(layer, js, arrays):
        if layer > 0 and js != [0]:
            held[layer] = (js, arrays)
            return None
        if layer > 0:
            js, arrays = held[layer][0] + js, held[layer][1] + arrays
        owns = [lax.dynamic_slice(a, own_start(j), own_sizes[j]) for j, a in zip(js, arrays)]
        sems, thrus, lands, token = _split_start(
            arrays, js, [rs_shapes[j] for j in js], rs_src, rs_dst,
            "rs_start_%d_%s" % (layer, "".join(str(j) for j in js)))
        entry = pending.setdefault(layer, ([], [], [], [], []))
        for lst, new in zip(entry, (js, sems, thrus, lands, owns)):
            lst.extend(new)
        return token


    loss, grad_x, d_rel, d_conv, dg, db = _local_step(
        x[0], mem[0], get_weights, put_grads, rel_bias, conv_full, ln_g, ln_b, loss_target[0])

    p_small = _gather_to_all(_pack_small(d_rel, d_conv, dg, db), "gather_small_grads")

    rs_lands, rs_owns = {}, {}

    def rs_wait(layer, want, after, name):
        js, sems, thrus, lands, owns = pending[layer]
        pos = [js.index(j) for j in want]
        got = _split_wait([sems[p] for p in pos], [thrus[p] for p in pos], [lands[p] for p in pos],
                          want, rs_src, rs_dst, after, name)
        for j, p, land in zip(want, pos, got):
            rs_lands[layer, j], rs_owns[layer, j] = land, owns[p]

    for layer in range(1, DEPTH):
        rs_wait(layer, kinds, grad_x, "rs_wait_%d" % layer)
    rs_wait(0, [1, 2], grad_x, "rs_wait_0_kv_out")

    def big(j, w, m, v, name, layers, prev=None):
        return _reduce_adamw_layers([rs_lands[layer, j] for layer in layers], [rs_owns[layer, j] for layer in layers],
                                    w, m, v, name, first=layers[0], prev=prev)

    every = list(range(DEPTH))
    g_kv, d_kv, nm_kv, nv_kv = big(1, w_mem_kv, m_w_mem_kv, v_w_mem_kv, "adamw_w_kv", every)
    g_out, d_out, nm_out, nv_out = big(2, w_out, m_w_out, v_w_out, "adamw_w_out", every)
    later = big(0, w_in, m_w_in, v_w_in, "adamw_w_in_later_layers", every[1:])

    def pack_state(rel, conv, g, b):
        conv_full = jnp.zeros((2, 3, E_MIX), F32)
        conv_full = lax.dynamic_update_slice(conv_full, conv, (0, 0, me * c_conv))
        return _pack_small(rel, conv_full, g, b)

    sm_w = pack_state(rel_bias, conv_w, ln_g, ln_b)
    sm_m = pack_state(m_rel_bias, m_conv_w, m_ln_g, m_ln_b)
    sm_v = pack_state(v_rel_bias, v_conv_w, v_ln_g, v_ln_b)
    sm_outs = _reduce_adamw(p_small, sm_w, sm_m, sm_v, "adamw_small")

    rs_wait(0, [0], [later[0], sm_outs[0], g_kv, g_out], "rs_wait_0_in")
    g_in, d_in, nm_in, nv_in = big(0, w_in, m_w_in, v_w_in, "adamw_w_in_layer_0", [0], prev=later)

    def unpack(buf):
        rel = buf[SM_REL:, :N_REL].reshape(2, N_HEADS, N_REL)
        conv = lax.dynamic_slice(buf[SM_CONV:SM_CONV + 6].reshape(2, 3, E_MIX), (0, 0, me * c_conv), (2, 3, c_conv))
        return rel, conv, buf[SM_G:SM_G + DEPTH], buf[SM_B:SM_B + DEPTH]

    g_sm, d_sm, nm_sm, nv_sm = [unpack(b) for b in sm_outs]

    loss = lax.psum(loss[0, 0], ("x", "y", "c"))
    return (loss, grad_x[None],
            g_in, g_kv, g_out, *g_sm,
            d_in, d_kv, d_out, *d_sm,
            nm_in, nm_kv, nm_out, *nm_sm,
            nv_in, nv_kv, nv_out, *nv_sm)
```

```python
import functools
import math

import jax
import jax.numpy as jnp
from jax import lax
from jax.experimental import pallas as pl
from jax.experimental.pallas import tpu as pltpu

F32 = jnp.float32
BF16 = jnp.bfloat16
MXU_DTYPE = jnp.bfloat16

N_DEV = 8
D_MODEL = 1024
DEPTH = 4
CHUNK = 64
N_PREV = 8
N_HEADS = 16
HEAD_DIM = 64
E_MIX = 1024
REL_CLIP = 128
N_REL = 2 * REL_CLIP + 1
N_REL_PAD = 384
N_MEM = 256
MEM_HEADS = 4
MEM_HEAD_DIM = 128
E_MEM = 512
E_BRANCH = E_MIX + E_MEM
N_IN = 3 * E_MIX + E_MEM + E_BRANCH
DN_ALPHA = (2.0 * DEPTH) ** 0.25
LN_EPS = 1e-5
NEG = -1e30

ADAM_LR = 0.001
ADAM_B1 = 0.9
ADAM_B2 = 0.999
ADAM_EPS = 1e-08
ADAM_WD = 0.01
ADAM_STEP = 10

LANES = 128
SUBLANES = 8
VMEM_LIMIT = 56 * 1024 * 1024

TQ = 4 * CHUNK
TKEYS = 3 * TQ
ROLL_W = 1024
TS = 256
QM_BLK = 3 * E_MIX // E_MEM
Z_BLK = QM_BLK + 1


def _call(body, **kw):
    return pl.pallas_call(body, **kw)


def _cparams(*sem):
    return pltpu.CompilerParams(dimension_semantics=sem, vmem_limit_bytes=VMEM_LIMIT)


def _dot(a, b):
    return jnp.dot(a, b, preferred_element_type=F32)


def _dot_nt(a, b):
    return lax.dot_general(a, b, (((1,), (1,)), ((), ())), preferred_element_type=F32)


def _dot_tn(a, b):
    return lax.dot_general(a, b, (((0,), (0,)), ((), ())), preferred_element_type=F32)


def _inproj(x, w):
    s, d = x.shape
    n = w.shape[1]
    tm = 512
    tn = 1024

    def body(x_ref, w_ref, o_ref, xb_ref):
        xb = x_ref[...].astype(xb_ref.dtype)
        xb_ref[...] = xb
        for j in range(n // tn):
            o_ref[:, j * tn:(j + 1) * tn] = _dot(xb, w_ref[:, j * tn:(j + 1) * tn]).astype(o_ref.dtype)

    return _call(
        body, name="inproj", grid=(s // tm,),
        in_specs=[pl.BlockSpec((tm, d), lambda i: (i, 0)),
                  pl.BlockSpec((d, n), lambda i: (0, 0), pipeline_mode=pl.Buffered(1))],
        out_specs=[pl.BlockSpec((tm, n), lambda i: (i, 0)),
                   pl.BlockSpec((tm, d), lambda i: (i, 0))],
        out_shape=[jax.ShapeDtypeStruct((s, n), BF16), jax.ShapeDtypeStruct((s, d), BF16)],
        compiler_params=_cparams("parallel"),
    )(x, w)


def _small_matmul(a, b, trans_a, out_dtype, name):
    m = a.shape[1] if trans_a else a.shape[0]
    n = b.shape[1]

    def body(a_ref, b_ref, o_ref):
        av = a_ref[...].astype(MXU_DTYPE)
        bv = b_ref[...].astype(MXU_DTYPE)
        r = _dot_tn(av, bv) if trans_a else _dot(av, bv)
        o_ref[...] = r.astype(out_dtype)

    return _call(
        body, name=name,
        in_specs=[pl.BlockSpec(memory_space=pltpu.VMEM)] * 2,
        out_specs=pl.BlockSpec(memory_space=pltpu.VMEM),
        out_shape=jax.ShapeDtypeStruct((m, n), out_dtype),
        compiler_params=pltpu.CompilerParams(vmem_limit_bytes=VMEM_LIMIT),
    )(a, b)


def _piece_blocks(pieces, blk):
    offs, nbs, o = [], [], 0
    for p in pieces:
        nb = p.shape[1] // blk
        offs.append(o)
        nbs.append(nb)
        o += nb
    return offs, nbs, o


def _dx_matmul(pieces, wt, addend, token=None):
    s = pieces[0].shape[0]
    n_in, d = wt.shape
    tm = 512
    np_ = len(pieces)
    extra = [] if token is None else [token]

    def body(*refs):
        a_refs = refs[:np_]
        w_ref, add_ref = refs[np_:np_ + 2]
        o_ref = refs[-1]
        a = jnp.concatenate([r[...] for r in a_refs], axis=1)
        o_ref[...] = add_ref[...] + _dot(a, w_ref[...])

    in_specs = [pl.BlockSpec((tm, p.shape[1]), lambda i: (i, 0)) for p in pieces]
    in_specs += [pl.BlockSpec((n_in, d), lambda i: (0, 0), pipeline_mode=pl.Buffered(1)),
                 pl.BlockSpec((tm, d), lambda i: (i, 0))]
    in_specs += [pl.BlockSpec((SUBLANES, LANES), lambda i: (0, 0)) for _ in extra]
    return _call(
        body, name="dx_matmul", grid=(s // tm,),
        in_specs=in_specs,
        out_specs=pl.BlockSpec((tm, d), lambda i: (i, 0)),
        out_shape=jax.ShapeDtypeStruct((s, d), F32),
        compiler_params=_cparams("parallel"),
    )(*pieces, wt, addend, *extra)


def _dw_matmul(x, pieces):
    s, d = x.shape
    tn = 1024
    tk = min(1024, s)
    offs, nbs, nj = _piece_blocks(pieces, tn)
    np_ = len(pieces)
    nk = s // tk

    def body(*refs):
        x_ref = refs[0]
        b_refs = refs[1:1 + np_]
        o_ref, acc = refs[1 + np_:]
        j = pl.program_id(0)
        k = pl.program_id(1)

        @pl.when(k == 0)
        def _():
            acc[...] = jnp.zeros_like(acc)

        for p in range(np_):
            @pl.when((j >= offs[p]) & (j < offs[p] + nbs[p]))
            def _(p=p):
                acc[...] += _dot_tn(x_ref[...], b_refs[p][...])

        @pl.when(k == nk - 1)
        def _():
            o_ref[...] = acc[...].astype(o_ref.dtype)

    def bmap(p):
        def f(j, k):
            inside = (j >= offs[p]) & (j < offs[p] + nbs[p])
            return (jnp.where(inside, k, 0), jnp.clip(j - offs[p], 0, nbs[p] - 1))
        return f

    in_specs = [pl.BlockSpec((tk, d), lambda j, k: (k, 0))]
    in_specs += [pl.BlockSpec((tk, tn), bmap(p)) for p in range(np_)]
    return _call(
        body, name="dw_matmul", grid=(nj, nk),
        in_specs=in_specs,
        out_specs=pl.BlockSpec((d, tn), lambda j, k: (0, j)),
        out_shape=jax.ShapeDtypeStruct((d, nj * tn), BF16),
        scratch_shapes=[pltpu.VMEM((d, tn), F32)],
        compiler_params=_cparams("parallel", "arbitrary"),
    )(x, *pieces)


def _rel_onehot():
    j = lax.broadcasted_iota(jnp.int32, (N_REL_PAD, ROLL_W), 1)
    kk = lax.broadcasted_iota(jnp.int32, (N_REL_PAD, ROLL_W), 0)
    dd = jnp.where(j < TKEYS, j, j - ROLL_W)
    idx = jnp.clip(N_PREV * CHUNK - dd, -REL_CLIP, REL_CLIP) + REL_CLIP
    return jnp.where(idx == kk, 1.0, 0.0).astype(F32)


def _band_mask():
    r = lax.broadcasted_iota(jnp.int32, (TQ, TKEYS), 0) // CHUNK
    m = lax.broadcasted_iota(jnp.int32, (TQ, TKEYS), 1) // CHUNK
    return (m >= r) & (m <= r + N_PREV)


def _tile_bias(table_pad):
    def body(t_ref, o_ref):
        g = jnp.dot(t_ref[...], _rel_onehot(), preferred_element_type=F32,
                    precision=lax.Precision.HIGHEST)
        band = _band_mask()
        for h in range(N_HEADS):
            gh = jnp.broadcast_to(g[h:h + 1, :], (TQ, ROLL_W))
            rolled = pltpu.roll(gh, 0, 1, stride=1, stride_axis=0)
            o_ref[h] = jnp.where(band, rolled[:, :TKEYS], NEG)

    return _call(
        body, name="tile_bias",
        in_specs=[pl.BlockSpec(memory_space=pltpu.VMEM)],
        out_specs=pl.BlockSpec(memory_space=pltpu.VMEM),
        out_shape=jax.ShapeDtypeStruct((N_HEADS, TQ, TKEYS), F32),
        compiler_params=pltpu.CompilerParams(vmem_limit_bytes=VMEM_LIMIT),
    )(table_pad)


def _tile_bias_bwd(dtb):
    def body(d_ref, o_ref, g_ref):
        zpad = jnp.zeros((TQ, ROLL_W - TKEYS), F32)
        rr = lax.broadcasted_iota(jnp.int32, (TQ, TQ), 0)
        cc = lax.broadcasted_iota(jnp.int32, (TQ, TQ), 1)
        flip = jnp.where(rr + cc == TQ - 1, 1.0, 0.0).astype(F32)
        for h in range(N_HEADS):
            xh = jnp.concatenate([d_ref[h], zpad], axis=1)
            xf = jnp.dot(flip, xh, preferred_element_type=F32, precision=lax.Precision.HIGHEST)
            rolled = pltpu.roll(xf, 0, 1, stride=1, stride_axis=0)
            g_ref[h:h + 1, :] = jnp.sum(rolled, axis=0, keepdims=True)
        g = pltpu.roll(g_ref[...], ROLL_W - (TQ - 1), 1)
        o_ref[...] = lax.dot_general(g, _rel_onehot(), (((1,), (1,)), ((), ())),
                                     preferred_element_type=F32, precision=lax.Precision.HIGHEST)

    return _call(
        body, name="tile_bias_bwd",
        in_specs=[pl.BlockSpec(memory_space=pltpu.VMEM)],
        out_specs=pl.BlockSpec(memory_space=pltpu.VMEM),
        out_shape=jax.ShapeDtypeStruct((N_HEADS, N_REL_PAD), F32),
        scratch_shapes=[pltpu.VMEM((N_HEADS, ROLL_W), F32)],
        compiler_params=pltpu.CompilerParams(vmem_limit_bytes=VMEM_LIMIT),
    )(dtb)


HB = 4
HBW = HB * HEAD_DIM
ATTN_SCALE = 0.125
assert ATTN_SCALE == 1.0 / math.sqrt(HEAD_DIM)


def _head_masks():
    lane = lax.broadcasted_iota(jnp.int32, (1, HBW), 1) // HEAD_DIM
    return [lane == hh for hh in range(HB)]


def _select_heads(masks, parts):
    out = parts[-1]
    for hh in range(HB - 2, -1, -1):
        out = jnp.where(masks[hh], parts[hh], out)
    return out


def _attn_probs(qm, kcat, tb, valid):
    s = _dot_nt(qm, kcat) + tb
    if valid is not None:
        s = jnp.where(valid, s, NEG)
    m = jnp.max(s, axis=-1, keepdims=True)
    e = jnp.exp(s - m)
    return e * (1.0 / jnp.sum(e, axis=-1, keepdims=True))


def _key_valid(i):
    col = lax.broadcasted_iota(jnp.int32, (TQ, TKEYS), 1)
    return col >= jnp.maximum(2 - i, 0) * TQ


def _kv_specs(col0, nt):
    def spec(back):
        return pl.BlockSpec((TQ, HBW), lambda hp, i: (jnp.clip(i - back, 0, nt - 1), col0 + hp))
    return [spec(2), spec(1), spec(0)]


def _attn_fwd(h, tb):
    s = h.shape[0]
    nt = s // TQ
    nhp = N_HEADS // HB

    def body(q_ref, k0, k1, k2, v0, v1, v2, tb_ref, o_ref, p_ref):
        i = pl.program_id(1)

        def tile(valid):
            masks = _head_masks()
            qs = q_ref[...].astype(MXU_DTYPE) * ATTN_SCALE
            kcat = jnp.concatenate([k0[...], k1[...], k2[...]], axis=0).astype(MXU_DTYPE)
            vcat = jnp.concatenate([v0[...], v1[...], v2[...]], axis=0).astype(MXU_DTYPE)
            outs = []
            for hh in range(HB):
                qm = jnp.where(masks[hh], qs, jnp.zeros_like(qs))
                pb = _attn_probs(qm, kcat, tb_ref[hh], valid).astype(MXU_DTYPE)
                p_ref[hh] = pb.astype(p_ref.dtype)
                outs.append(_dot(pb, vcat))
            o_ref[...] = _select_heads(masks, outs).astype(o_ref.dtype)

        @pl.when(i < 2)
        def _():
            tile(_key_valid(i))

        @pl.when(i >= 2)
        def _():
            tile(None)

    in_specs = [pl.BlockSpec((TQ, HBW), lambda hp, i: (i, hp))]
    in_specs += _kv_specs(nhp, nt) + _kv_specs(2 * nhp, nt)
    in_specs += [pl.BlockSpec((HB, TQ, TKEYS), lambda hp, i: (hp, 0, 0))]
    return _call(
        body, name="attn_fwd", grid=(nhp, nt),
        in_specs=in_specs,
        out_specs=[pl.BlockSpec((TQ, HBW), lambda hp, i: (i, hp)),
                   pl.BlockSpec((HB, TQ, TKEYS), lambda hp, i: (hp, i, 0))],
        out_shape=[jax.ShapeDtypeStruct((s, E_MIX), BF16), jax.ShapeDtypeStruct((N_HEADS, s, TKEYS), BF16)],
        compiler_params=_cparams("parallel", "parallel"),
    )(h, h, h, h, h, h, h, tb)


def _attn_bwd(h, probs, d_mix, token=None):
    s = h.shape[0]
    nt = s // TQ
    nhp = N_HEADS // HB
    extra = [] if token is None else [token]

    def body(q_ref, k0, k1, k2, v0, v1, v2, p_ref, do_ref, *rest):
        dq_ref, dk_ref, dv_ref, dtb_ref, dk_acc, dv_acc = rest[len(extra):]
        i = pl.program_id(1)

        @pl.when(i == 0)
        def _():
            dk_acc[...] = jnp.zeros_like(dk_acc)
            dv_acc[...] = jnp.zeros_like(dv_acc)
            dtb_ref[...] = jnp.zeros_like(dtb_ref)

        @pl.when((i > 0) & (i < nt))
        def _():
            dk_acc[i % 3] = jnp.zeros((TQ, HBW), F32)
            dv_acc[i % 3] = jnp.zeros((TQ, HBW), F32)

        @pl.when(i < nt)
        def _():
            masks = _head_masks()
            qs = q_ref[...].astype(MXU_DTYPE) * ATTN_SCALE
            do2 = do_ref[...].astype(MXU_DTYPE)
            kcat = jnp.concatenate([k0[...], k1[...], k2[...]], axis=0).astype(MXU_DTYPE)
            vcat = jnp.concatenate([v0[...], v1[...], v2[...]], axis=0).astype(MXU_DTYPE)
            ks = kcat * ATTN_SCALE
            dqs, dks, dvs = [], [], []
            for hh in range(HB):
                dom = jnp.where(masks[hh], do2, jnp.zeros_like(do2))
                pb = p_ref[hh]
                p = pb.astype(F32)
                dp = _dot_nt(dom, vcat)
                ds = p * (dp - jnp.sum(p * dp, axis=-1, keepdims=True))
                dtb_ref[hh] += ds
                dsb = ds.astype(MXU_DTYPE)
                dqs.append(_dot(dsb, ks))
                dks.append(_dot_tn(dsb, qs))
                dvs.append(_dot_tn(pb.astype(MXU_DTYPE), do2))
            dq_ref[...] = _select_heads(masks, dqs).astype(dq_ref.dtype)
            dkc = _select_heads(masks, dks)
            dvc = _select_heads(masks, dvs)
            for jj in range(3):
                slot = (i + 1 + jj) % 3
                dk_acc[slot] += dkc[jj * TQ:(jj + 1) * TQ]
                dv_acc[slot] += dvc[jj * TQ:(jj + 1) * TQ]

        @pl.when(i >= 2)
        def _():
            slot = (i - 2) % 3
            dk_ref[...] = dk_acc[slot].astype(dk_ref.dtype)
            dv_ref[...] = dv_acc[slot].astype(dv_ref.dtype)

    qmap = lambda hp, i: (jnp.minimum(i, nt - 1), hp)
    kvout = lambda hp, i: (jnp.maximum(i - 2, 0), hp)
    in_specs = [pl.BlockSpec((TQ, HBW), qmap)]
    in_specs += _kv_specs(nhp, nt) + _kv_specs(2 * nhp, nt)
    in_specs += [pl.BlockSpec((HB, TQ, TKEYS), lambda hp, i: (hp, jnp.minimum(i, nt - 1), 0)),
                 pl.BlockSpec((TQ, HBW), qmap)]
    in_specs += [pl.BlockSpec((SUBLANES, LANES), lambda hp, i: (0, 0)) for _ in extra]
    blk = (TQ, HBW)
    return _call(
        body, name="attn_bwd", grid=(nhp, nt + 2),
        in_specs=in_specs,
        out_specs=[pl.BlockSpec(blk, qmap), pl.BlockSpec(blk, kvout), pl.BlockSpec(blk, kvout),
                   pl.BlockSpec((HB, TQ, TKEYS), lambda hp, i: (hp, 0, 0))],
        out_shape=[jax.ShapeDtypeStruct((s, E_MIX), BF16)] * 3
        + [jax.ShapeDtypeStruct((N_HEADS, TQ, TKEYS), F32)],
        scratch_shapes=[pltpu.VMEM((3, TQ, HBW), F32)] * 2,
        compiler_params=_cparams("parallel", "arbitrary"),
    )(h, h, h, h, h, h, h, probs, d_mix, *extra)


CONV_TS = 512
HALO = 2 * SUBLANES


def _shift_down(prev, cur, k):
    rolled = pltpu.roll(cur, k, 0)
    row = lax.broadcasted_iota(jnp.int32, (HALO, cur.shape[1]), 0)
    top = jnp.where(row < k, pltpu.roll(prev, k, 0), rolled[:HALO])
    return jnp.concatenate([top, rolled[HALO:]], axis=0)


def _shift_up(cur, nxt, k):
    ts = cur.shape[0]
    rolled = pltpu.roll(cur, ts - k, 0)
    row = lax.broadcasted_iota(jnp.int32, (HALO, cur.shape[1]), 0)
    bottom = jnp.where(row >= HALO - k, pltpu.roll(nxt, HALO - k, 0), rolled[ts - HALO:])
    return jnp.concatenate([rolled[:ts - HALO], bottom], axis=0)


def _conv_specs(ts, nb):
    tile = lambda c: pl.BlockSpec((ts, E_MIX), lambda i: (i, c))
    prev = lambda c: pl.BlockSpec((HALO, E_MIX), lambda i: (jnp.maximum(i * (ts // HALO) - 1, 0), c))
    return tile, prev


def _conv_fwd(h, w8):
    s = h.shape[0]
    ts = CONV_TS
    nb = s // ts
    tile, prev = _conv_specs(ts, nb)

    def body(bg, cg, u, cgp, up, w_ref, o_ref):
        i = pl.program_id(0)
        a = cg[...].astype(F32) * u[...].astype(F32)
        ap = jnp.where(i > 0, cgp[...].astype(F32) * up[...].astype(F32), 0.0)
        w = w_ref[...]
        conv = w[0:1] * _shift_down(ap, a, 2) + w[1:2] * _shift_down(ap, a, 1) + w[2:3] * a
        o_ref[...] = (bg[...].astype(F32) * conv).astype(o_ref.dtype)

    return _call(
        body, name="conv_fwd", grid=(nb,),
        in_specs=[tile(0), tile(1), tile(2), prev(1), prev(2),
                  pl.BlockSpec((SUBLANES, E_MIX), lambda i: (0, 0))],
        out_specs=pl.BlockSpec((ts, E_MIX), lambda i: (i, 0)),
        out_shape=jax.ShapeDtypeStruct((s, E_MIX), BF16),
        compiler_params=_cparams("parallel"),
    )(h, h, h, h, h, w8)


def _conv_bwd(h, w8, d_mix):
    s = h.shape[0]
    ts = CONV_TS
    nb = s // ts
    tile, prev = _conv_specs(ts, nb)
    nrow = s // HALO
    nxt = lambda c: pl.BlockSpec((HALO, E_MIX), lambda i: (jnp.minimum((i + 1) * (ts // HALO), nrow - 1), c))

    def body(bg, cg, u, cgp, up, bgn, dmix, dmixn, w_ref, dbg_ref, dcg_ref, du_ref, dw_ref):
        i = pl.program_id(0)

        @pl.when(i == 0)
        def _():
            dw_ref[...] = jnp.zeros_like(dw_ref)

        cgv, uv = cg[...].astype(F32), u[...].astype(F32)
        a = cgv * uv
        ap = jnp.where(i > 0, cgp[...].astype(F32) * up[...].astype(F32), 0.0)
        a1 = _shift_down(ap, a, 1)
        a2 = _shift_down(ap, a, 2)
        w = w_ref[...]
        conv = w[0:1] * a2 + w[1:2] * a1 + w[2:3] * a
        dm = dmix[...].astype(F32)
        dbg_ref[...] = (dm * conv).astype(dbg_ref.dtype)
        dc = dm * bg[...].astype(F32)
        dcn = jnp.where(i < nb - 1, dmixn[...].astype(F32) * bgn[...].astype(F32), 0.0)
        da = w[2:3] * dc + w[1:2] * _shift_up(dc, dcn, 1) + w[0:1] * _shift_up(dc, dcn, 2)
        dcg_ref[...] = (da * uv).astype(dcg_ref.dtype)
        du_ref[...] = (da * cgv).astype(du_ref.dtype)
        dw_ref[0:1, :] += jnp.sum(dc * a2, axis=0, keepdims=True)
        dw_ref[1:2, :] += jnp.sum(dc * a1, axis=0, keepdims=True)
        dw_ref[2:3, :] += jnp.sum(dc * a, axis=0, keepdims=True)

    full = lambda: pl.BlockSpec((ts, E_MIX), lambda i: (i, 0))
    return _call(
        body, name="conv_bwd", grid=(nb,),
        in_specs=[tile(0), tile(1), tile(2), prev(1), prev(2), nxt(0),
                  full(), pl.BlockSpec((HALO, E_MIX), lambda i: (jnp.minimum((i + 1) * (ts // HALO), nrow - 1), 0)),
                  pl.BlockSpec((SUBLANES, E_MIX), lambda i: (0, 0))],
        out_specs=[full(), full(), full(), pl.BlockSpec((SUBLANES, E_MIX), lambda i: (0, 0))],
        out_shape=[jax.ShapeDtypeStruct((s, E_MIX), BF16)] * 3
        + [jax.ShapeDtypeStruct((SUBLANES, E_MIX), F32)],
        compiler_params=_cparams("arbitrary"),
    )(h, h, h, h, h, h, d_mix, d_mix, w8)


def _mem_probs(qh, kh):
    s = _dot_nt(qh, kh) / math.sqrt(MEM_HEAD_DIM)
    m = jnp.max(s, axis=-1, keepdims=True)
    e = jnp.exp(s - m)
    return e / jnp.sum(e, axis=-1, keepdims=True)


def _sigmoid(z):
    return 1.0 / (1.0 + jnp.exp(-z))


def _layer_out_fwd(x, h, mix, kv, w_out, g, b):
    s, d = x.shape
    ts = 2 * TS

    def body(x_ref, mix_ref, qm_ref, z0, z1, z2, kv_ref, wo_ref, g_ref, b_ref,
             xn_ref, r_ref, mem_ref):
        qm = qm_ref[...].astype(MXU_DTYPE)
        kvb = kv_ref[...].astype(MXU_DTYPE)
        mems = []
        for hh in range(MEM_HEADS):
            lo = hh * MEM_HEAD_DIM
            p = _mem_probs(qm[:, lo:lo + MEM_HEAD_DIM], kvb[:, lo:lo + MEM_HEAD_DIM])
            mems.append(_dot(p.astype(MXU_DTYPE), kvb[:, E_MEM + lo:E_MEM + lo + MEM_HEAD_DIM]))
        mem = jnp.concatenate(mems, axis=1).astype(mem_ref.dtype)
        mem_ref[...] = mem
        mixv = mix_ref[...].astype(F32)
        half = E_MIX // 2
        parts = [mixv[:, :half], mixv[:, half:], mem.astype(F32)]
        out = jnp.zeros((ts, d), F32)
        for c, zr in enumerate((z0, z1, z2)):
            zv = zr[...].astype(F32)
            y = (parts[c] * (zv * _sigmoid(zv))).astype(MXU_DTYPE)
            out += _dot(y, wo_ref[c * half:(c + 1) * half, :])
        r = DN_ALPHA * x_ref[...] + out
        r_ref[...] = r
        mu = jnp.mean(r, axis=-1, keepdims=True)
        rc = r - mu
        var = jnp.mean(rc * rc, axis=-1, keepdims=True)
        xn_ref[...] = rc * lax.rsqrt(var + LN_EPS) * g_ref[...] + b_ref[...]

    row = lambda w, c: pl.BlockSpec((ts, w), lambda i: (i, c))
    const = lambda shp: pl.BlockSpec(shp, lambda i: (0, 0))
    return _call(
        body, name="layer_out_fwd", grid=(s // ts,),
        in_specs=[row(d, 0), row(E_MIX, 0), row(E_MEM, QM_BLK),
                  row(E_MEM, Z_BLK), row(E_MEM, Z_BLK + 1), row(E_MEM, Z_BLK + 2),
                  const((N_MEM, 2 * E_MEM)), const((E_BRANCH, d)), const((1, d)), const((1, d))],
        out_specs=[row(d, 0), row(d, 0), row(E_MEM, 0)],
        out_shape=[jax.ShapeDtypeStruct((s, d), F32), jax.ShapeDtypeStruct((s, d), F32),
                   jax.ShapeDtypeStruct((s, E_MEM), BF16)],
        compiler_params=_cparams("parallel"),
    )(x, mix, h, h, h, h, kv, w_out, g, b)


def _layer_out_bwd(dxn, r, g, h, mix, mem, kv, w_out_t):
    s, d = r.shape
    ts = 2 * TS
    nb = s // ts
    half = E_MIX // 2
    inv = 1.0 / math.sqrt(MEM_HEAD_DIM)

    def body(dxn_ref, r_ref, g_ref, mix_ref, mem_ref, qm_ref, z0, z1, z2, kv_ref, wo_ref,
             dxr_ref, dmix_ref, dqz_ref, dwo_ref, dkv_ref, dg_ref, db_ref, dw_acc):
        i = pl.program_id(0)

        @pl.when(i == 0)
        def _():
            dw_acc[...] = jnp.zeros_like(dw_acc)
            dkv_ref[...] = jnp.zeros_like(dkv_ref)
            dg_ref[...] = jnp.zeros_like(dg_ref)
            db_ref[...] = jnp.zeros_like(db_ref)

        dxn_v = dxn_ref[...]
        rv = r_ref[...]
        mu = jnp.mean(rv, axis=-1, keepdims=True)
        rc = rv - mu
        var = jnp.mean(rc * rc, axis=-1, keepdims=True)
        rstd = lax.rsqrt(var + LN_EPS)
        xhat = rc * rstd
        dg_ref[...] += jnp.sum(dxn_v * xhat, axis=0, keepdims=True)
        db_ref[...] += jnp.sum(dxn_v, axis=0, keepdims=True)
        dxh = dxn_v * g_ref[...]
        m1 = jnp.mean(dxh, axis=-1, keepdims=True)
        m2 = jnp.mean(dxh * xhat, axis=-1, keepdims=True)
        dr = rstd * (dxh - m1 - xhat * m2)
        dxr_ref[...] = DN_ALPHA * dr
        dout = dr.astype(MXU_DTYPE)
        mixv = mix_ref[...].astype(F32)
        parts = [mixv[:, :half], mixv[:, half:], mem_ref[...].astype(F32)]
        dcs = []
        for c, zr in enumerate((z0, z1, z2)):
            lo = c * half
            zv = zr[...].astype(F32)
            sg = _sigmoid(zv)
            sl = zv * sg
            dy = _dot(dout, wo_ref[:, lo:lo + half])
            y = (parts[c] * sl).astype(MXU_DTYPE)
            dw_acc[lo:lo + half, :] += _dot_tn(y, dout)
            dcs.append(dy * sl)
            dqz_ref[:, E_MEM + lo:E_MEM + lo + half] = (
                dy * parts[c] * (sg * (1.0 + zv * (1.0 - sg)))).astype(dqz_ref.dtype)
        dmix_ref[...] = jnp.concatenate(dcs[:2], axis=1).astype(dmix_ref.dtype)

        qm = qm_ref[...].astype(MXU_DTYPE)
        kvb = kv_ref[...].astype(MXU_DTYPE)
        dmb = dcs[2].astype(MXU_DTYPE)
        for hh in range(MEM_HEADS):
            lo = hh * MEM_HEAD_DIM
            qh = qm[:, lo:lo + MEM_HEAD_DIM]
            kh = kvb[:, lo:lo + MEM_HEAD_DIM]
            vh = kvb[:, E_MEM + lo:E_MEM + lo + MEM_HEAD_DIM]
            dmh = dmb[:, lo:lo + MEM_HEAD_DIM]
            p = _mem_probs(qh, kh)
            dp = _dot_nt(dmh, vh)
            ds = p * (dp - jnp.sum(p * dp, axis=-1, keepdims=True))
            dsb = (ds * inv).astype(MXU_DTYPE)
            dqz_ref[:, lo:lo + MEM_HEAD_DIM] = _dot(dsb, kh).astype(dqz_ref.dtype)
            dkv_ref[:, lo:lo + MEM_HEAD_DIM] += _dot_tn(dsb, qh)
            dkv_ref[:, E_MEM + lo:E_MEM + lo + MEM_HEAD_DIM] += _dot_tn(p.astype(MXU_DTYPE), dmh)

        @pl.when(i == nb - 1)
        def _():
            dwo_ref[...] = dw_acc[...].astype(dwo_ref.dtype)

    row = lambda w, c: pl.BlockSpec((ts, w), lambda i: (i, c))
    const = lambda shp: pl.BlockSpec(shp, lambda i: (0, 0))
    once = lambda shp: pl.BlockSpec(shp, lambda i: (0, 0), pipeline_mode=pl.Buffered(1))
    return _call(
        body, name="layer_out_bwd", grid=(nb,),
        in_specs=[row(d, 0), row(d, 0), const((1, d)), row(E_MIX, 0), row(E_MEM, 0),
                  row(E_MEM, QM_BLK), row(E_MEM, Z_BLK), row(E_MEM, Z_BLK + 1), row(E_MEM, Z_BLK + 2),
                  once((N_MEM, 2 * E_MEM)), once((d, E_BRANCH))],
        out_specs=[row(d, 0), row(E_MIX, 0), row(E_MEM + E_BRANCH, 0),
                   const((E_BRANCH, d)), const((N_MEM, 2 * E_MEM)), const((1, d)), const((1, d))],
        out_shape=[jax.ShapeDtypeStruct((s, d), F32), jax.ShapeDtypeStruct((s, E_MIX), BF16),
                   jax.ShapeDtypeStruct((s, E_MEM + E_BRANCH), BF16),
                   jax.ShapeDtypeStruct((E_BRANCH, d), BF16),
                   jax.ShapeDtypeStruct((N_MEM, 2 * E_MEM), F32),
                   jax.ShapeDtypeStruct((1, d), F32), jax.ShapeDtypeStruct((1, d), F32)],
        scratch_shapes=[pltpu.VMEM((E_BRANCH, d), F32)],
        compiler_params=_cparams("arbitrary"),
    )(dxn, r, g, mix, mem, h, h, h, h, kv, w_out_t)


def _loss_head(y, target):
    s, d = y.shape
    ts = 512

    def body(y_ref, t_ref, l_ref, dy_ref):
        @pl.when(pl.program_id(0) == 0)
        def _():
            l_ref[...] = jnp.zeros_like(l_ref)

        e = y_ref[...] - t_ref[...]
        dy_ref[...] = e * (1.0 / d)
        l_ref[...] += (0.5 / d) * jnp.sum(jnp.sum(e * e, axis=1, keepdims=True), axis=0, keepdims=True)

    return _call(
        body, name="loss_head", grid=(s // ts,),
        in_specs=[pl.BlockSpec((ts, d), lambda i: (i, 0))] * 2,
        out_specs=[pl.BlockSpec((1, 1), lambda i: (0, 0)), pl.BlockSpec((ts, d), lambda i: (i, 0))],
        out_shape=[jax.ShapeDtypeStruct((1, 1), F32), jax.ShapeDtypeStruct((s, d), F32)],
        compiler_params=_cparams("arbitrary"),
    )(y, target)


def _local_step(x, mem, get_weights, put_grads, rel_bias, conv_w, ln_g, ln_b, target):
    biases = [_tile_bias(jnp.pad(rel_bias[a], ((0, 0), (0, N_REL_PAD - N_REL)))) for a in range(DEPTH // 2)]

    saved = []
    xl = x
    for layer in range(DEPTH):
        w_in_l, rest = get_weights(layer, xl if layer else biases)
        h, xb = _inproj(xl, w_in_l)
        w_kv_l, w_out_l = rest(h)
        if layer % 2 == 0:
            mix, aux = _attn_fwd(h, biases[layer // 2])
        else:
            aux = jnp.pad(conv_w()[layer // 2], ((0, SUBLANES - 3), (0, 0)))
            mix = _conv_fwd(h, aux)
        kv = _small_matmul(mem, w_kv_l, False, F32, "kv_mem")
        xn, r, mem_out = _layer_out_fwd(xl, h, mix, kv, w_out_l,
                                        ln_g[layer][None], ln_b[layer][None])
        saved.append((xb, h, aux, mix, kv, r, mem_out, w_in_l, w_out_l))
        xl = xn

    loss, dx = _loss_head(xl, target)

    dgs, dbs, d_rel, d_conv = [], [], [], []
    for layer in reversed(range(DEPTH)):
        xb, h, aux, mix, kv, r, mem_out, w_in_l, w_out_l = saved[layer]
        dx_res, d_mix, dqz, dwo, dkv, dg, db = _layer_out_bwd(
            dx, r, ln_g[layer][None], h, mix, mem_out, kv, w_out_l.T)
        early = put_grads(layer, [1, 2], [_small_matmul(mem, dkv, True, BF16, "dw_kv"), dwo])
        if layer % 2 == 0:
            dq, dk, dv, dtb = _attn_bwd(h, aux, d_mix, early)
            d_rel.append(_tile_bias_bwd(dtb)[:, :N_REL])
            pieces = [dq, dk, dv, dqz]
        else:
            dbg, dcg, du, dw8 = _conv_bwd(h, aux, d_mix)
            d_conv.append(dw8[:3])
            pieces = [dbg, dcg, du, dqz]
        token = put_grads(layer, [0], [_dw_matmul(xb, pieces)])
        dgs.append(dg[0])
        dbs.append(db[0])
        dx = _dx_matmul(pieces, w_in_l.T, dx_res, token)

    rev = lambda lst: jnp.stack(lst[::-1])
    return loss, dx, rev(d_rel), rev(d_conv), rev(dgs), rev(dbs)


def _me():
    return lax.axis_index("x"), lax.axis_index("y"), lax.axis_index("c")


def _peer(k):
    x, y, c = _me()
    kx, ky, kc = (k >> 2) & 1, (k >> 1) & 1, k & 1
    return (1 - x if kx else x, 1 - y if ky else y, 1 - c if kc else c)


def _lin(dev):
    return 4 * dev[0] + 2 * dev[1] + dev[2]


ANY = pl.BlockSpec(memory_space=pl.ANY)


def _exchange(srcs, dst_shapes, src_slice, dst_slice, name):
    na = len(srcs)

    def body(*refs):
        src_refs = refs[:na]
        dst_refs = refs[na:2 * na]
        send_sems, recv_sems, local_sems = refs[2 * na:]
        me = _lin(_me())
        copies = []
        for a in range(na):
            loc = pltpu.make_async_copy(src_slice(a, src_refs[a], me), dst_slice(a, dst_refs[a], me),
                                        local_sems.at[a])
            loc.start()
            copies.append(loc)
            for k in range(1, N_DEV):
                peer = _peer(k)
                cp = pltpu.make_async_remote_copy(
                    src_ref=src_slice(a, src_refs[a], _lin(peer)),
                    dst_ref=dst_slice(a, dst_refs[a], me),
                    send_sem=send_sems.at[a, k - 1], recv_sem=recv_sems.at[a, k - 1],
                    device_id=peer, device_id_type=pl.DeviceIdType.MESH)
                cp.start()
                copies.append(cp)
        for cp in copies:
            cp.wait()

    return _call(
        body, name=name,
        in_specs=[ANY] * na, out_specs=[ANY] * na,
        out_shape=[jax.ShapeDtypeStruct(shp, s.dtype) for shp, s in zip(dst_shapes, srcs)],
        scratch_shapes=[pltpu.SemaphoreType.DMA((na, N_DEV - 1)),
                        pltpu.SemaphoreType.DMA((na, N_DEV - 1)),
                        pltpu.SemaphoreType.DMA((na,))],
    )(*srcs)


def _gather_to_all(src, name):
    return _exchange([src], [(N_DEV,) + src.shape], lambda a, ref, p: ref,
                     lambda a, ref, me: ref.at[me], name)[0]


HBM = pl.BlockSpec(memory_space=pltpu.HBM)
SEM = pl.BlockSpec(memory_space=pltpu.SEMAPHORE)
EFFECT = pltpu.SideEffectType.DATAFLOW_SIDE_EFFECTING
N_PEER = N_DEV - 1
N_KIND = 3


def _peer_copies(kind, src_ref, land_ref, send, recv, src_slice, dst_slice):
    me = _lin(_me())
    copies = []
    for k in range(1, N_DEV):
        peer = _peer(k)
        copies.append(pltpu.make_async_remote_copy(
            src_ref=src_slice(kind, src_ref, _lin(peer)),
            dst_ref=dst_slice(kind, land_ref, me, k),
            send_sem=send.at[k - 1], recv_sem=recv.at[k - 1],
            device_id=peer, device_id_type=pl.DeviceIdType.MESH))
    return copies


def _own_copy(kind, src_ref, land_ref, send, src_slice, dst_slice):
    me = _lin(_me())
    return pltpu.make_async_copy(src_slice(kind, src_ref, me), dst_slice(kind, land_ref, me, 0),
                                 send.at[N_PEER])


def _split_start(srcs, kinds, land_shapes, src_slice, dst_slice, name, own=False):
    na = len(srcs)

    def body(*refs):
        src_refs, land_refs = refs[:na], refs[na:2 * na]
        sems = refs[2 * na:4 * na]
        token = refs[-1]
        for a in range(na):
            for cp in _peer_copies(kinds[a], src_refs[a], land_refs[a], sems[2 * a], sems[2 * a + 1],
                                   src_slice, dst_slice):
                cp.start()
            if own:
                _own_copy(kinds[a], src_refs[a], land_refs[a], sems[2 * a], src_slice, dst_slice).start()
        token[...] = jnp.zeros_like(token)

    sem_shape = pltpu.SemaphoreType.DMA((N_DEV,))
    lands = [lax.empty(shp, s.dtype) for shp, s in zip(land_shapes, srcs)]
    outs = _call(
        body, name=name,
        in_specs=[HBM] * (2 * na),
        out_specs=[SEM] * (2 * na) + [HBM] * (2 * na) + [pl.BlockSpec(memory_space=pltpu.VMEM)],
        out_shape=[sem_shape] * (2 * na)
        + [pltpu.HBM(s.shape, s.dtype) for s in srcs]
        + [pltpu.HBM(shp, s.dtype) for shp, s in zip(land_shapes, srcs)]
        + [jax.ShapeDtypeStruct((SUBLANES, LANES), F32)],
        input_output_aliases={i: 2 * na + i for i in range(2 * na)},
        compiler_params=pltpu.CompilerParams(has_side_effects=EFFECT),
    )(*[pltpu.with_memory_space_constraint(a, pltpu.HBM) for a in list(srcs) + lands])
    sems = [(outs[2 * a], outs[2 * a + 1]) for a in range(na)]
    thrus = outs[2 * na:3 * na]
    lands = outs[3 * na:4 * na]
    return sems, thrus, lands, outs[-1]


def _split_wait(sems, thrus, lands, kinds, src_slice, dst_slice, after, name, own=False):
    na = len(thrus)
    after = list(after) if isinstance(after, (list, tuple)) else [after]

    def body(*refs):
        src_refs, land_refs = refs[:na], refs[na:2 * na]
        sem_refs = refs[2 * na:4 * na]
        for a in range(na):
            for cp in _peer_copies(kinds[a], src_refs[a], land_refs[a], sem_refs[2 * a], sem_refs[2 * a + 1],
                                   src_slice, dst_slice):
                cp.wait_send()
                cp.wait_recv()
            if own:
                _own_copy(kinds[a], src_refs[a], land_refs[a], sem_refs[2 * a], src_slice, dst_slice).wait()

    outs = _call(
        body, name=name,
        in_specs=[HBM] * (2 * na) + [SEM] * (2 * na) + [ANY] * len(after),
        out_specs=[HBM] * (2 * na),
        out_shape=[pltpu.HBM(a.shape, a.dtype) for a in list(thrus) + list(lands)],
        input_output_aliases={i: i for i in range(2 * na)},
        compiler_params=pltpu.CompilerParams(has_side_effects=EFFECT),
    )(*thrus, *lands, *[s for pair in sems for s in pair], *after)
    return outs[na:]


def _shard_dims(c_in, r_kv, r_out):
    def sl(j, ref, p):
        if j == 0:
            return ref.at[:, pl.ds(pl.multiple_of(p * c_in, LANES), c_in)]
        r = r_kv if j == 1 else r_out
        return ref.at[pl.ds(pl.multiple_of(p * r, 2 * SUBLANES), r), :]
    return sl


def _adamw_math(w, g, m, v):
    m = ADAM_B1 * m + (1.0 - ADAM_B1) * g
    v = ADAM_B2 * v + (1.0 - ADAM_B2) * (g * g)
    m_hat = m / (1.0 - ADAM_B1 ** ADAM_STEP)
    v_hat = v / (1.0 - ADAM_B2 ** ADAM_STEP)
    delta = -ADAM_LR * (m_hat / (jnp.sqrt(v_hat) + ADAM_EPS) + ADAM_WD * w)
    return delta, m, v


def _reduce_adamw(parts, w, m, v, name):
    rows, cols = w.shape
    tr = rows
    for cand in (512, 256, 128, 64, 32, 16):
        if rows % cand == 0 and rows > cand:
            tr = cand
            break

    def body(p_ref, w_ref, m_ref, v_ref, g_out, d_out, m_out, v_out):
        g = p_ref[0].astype(F32)
        for s in range(1, N_DEV):
            g = g + p_ref[s].astype(F32)
        g_out[...] = g
        d_out[...], m_out[...], v_out[...] = _adamw_math(w_ref[...], g, m_ref[...], v_ref[...])

    blk = pl.BlockSpec((tr, cols), lambda i: (i, 0))
    return _call(
        body, name=name, grid=(rows // tr,),
        in_specs=[pl.BlockSpec((N_DEV, tr, cols), lambda i: (0, i, 0)), blk, blk, blk],
        out_specs=[blk] * 4,
        out_shape=[jax.ShapeDtypeStruct((rows, cols), F32)] * 4,
        compiler_params=_cparams("parallel"),
    )(parts, w, m, v)


def _reduce_adamw_layers(lands, owns, w, m, v, name, first=0, prev=None):
    depth, rows, cols = w.shape
    nl = len(lands)
    tr = rows
    for cand in (256, 192, 128):
        if rows % cand == 0:
            tr = cand
            break

    kept = [] if prev is None else list(prev)

    def body(*refs):
        land_refs, own_refs = refs[:nl], refs[nl:2 * nl]
        w_ref, m_ref, v_ref = refs[2 * nl:2 * nl + 3]
        g_out, d_out, m_out, v_out = refs[2 * nl + 3 + len(kept):]
        layer = pl.program_id(0)
        for a in range(nl):
            @pl.when(layer == a)
            def _(a=a):
                g = own_refs[a][...].astype(F32)
                for k in range(N_PEER):
                    g = g + land_refs[a][k].astype(F32)
                g_out[...] = g
                d_out[...], m_out[...], v_out[...] = _adamw_math(w_ref[...], g, m_ref[...], v_ref[...])

    def lmap(a):
        return lambda l, i: (0, jnp.where(l == a, i, 0), 0)

    def omap(a):
        return lambda l, i: (jnp.where(l == a, i, 0), 0)

    blk = pl.BlockSpec((None, tr, cols), lambda l, i: (l + first, i, 0))
    n_in = 2 * nl + 3
    return _call(
        body, name=name, grid=(nl, rows // tr),
        in_specs=[pl.BlockSpec((N_PEER, tr, cols), lmap(a)) for a in range(nl)]
        + [pl.BlockSpec((tr, cols), omap(a)) for a in range(nl)] + [blk, blk, blk] + [ANY] * len(kept),
        out_specs=[blk] * 4,
        out_shape=[jax.ShapeDtypeStruct((depth, rows, cols), F32)] * 4,
        input_output_aliases={n_in + i: i for i in range(len(kept))},
        compiler_params=_cparams("arbitrary", "arbitrary"),
    )(*lands, *owns, w, m, v, *kept)


SM_G, SM_B, SM_CONV, SM_REL = 0, 4, 8, 16
SM_ROWS = SM_REL + 2 * N_HEADS
REL_W = 384


def _pack_small(d_rel, d_conv, dg, db):
    buf = jnp.zeros((SM_ROWS, D_MODEL), F32)
    buf = buf.at[SM_G:SM_G + DEPTH].set(dg)
    buf = buf.at[SM_B:SM_B + DEPTH].set(db)
    buf = buf.at[SM_CONV:SM_CONV + 6].set(d_conv.reshape(6, E_MIX))
    buf = buf.at[SM_REL:, :N_REL].set(d_rel.reshape(2 * N_HEADS, N_REL))
    return buf


def kernel(x, mem, w_in, w_mem_kv, w_out, rel_bias, conv_w, ln_g, ln_b, loss_target, m_w_in, m_w_mem_kv, m_w_out, m_rel_bias, m_conv_w, m_ln_g, m_ln_b, v_w_in, v_w_mem_kv, v_w_out, v_rel_bias, v_conv_w, v_ln_g, v_ln_b):
    me = _lin(_me())
    c_in, r_kv, r_out, c_conv = w_in.shape[2], w_mem_kv.shape[1], w_out.shape[1], conv_w.shape[2]

    shard = _shard_dims(c_in, r_kv, r_out)
    own_start = lambda j: (0, me * c_in) if j == 0 else (me * (r_kv if j == 1 else r_out), 0)

    w_sh = [w_in.astype(BF16), w_mem_kv.astype(BF16), w_out.astype(BF16)]
    full_shapes = [(D_MODEL, N_DEV * c_in), (N_DEV * r_kv, w_mem_kv.shape[2]), (N_DEV * r_out, D_MODEL)]
    conv_kind = N_KIND
    conv_tile = jnp.pad(conv_w.reshape(6, c_conv), ((0, SUBLANES - 6), (0, 0)))
    ag_src = lambda j, ref, p: ref
    ag_dst = lambda j, ref, me_, k: ref.at[me_] if j == conv_kind else shard(j, ref, me_)
    kinds = list(range(N_KIND))
    ag_sems, ag_thrus, ag_lands, _ = _split_start(
        [conv_tile] + [w_sh[j][layer] for layer in range(DEPTH) for j in kinds], [conv_kind] + kinds * DEPTH,
        [(N_DEV,) + conv_tile.shape] + full_shapes * DEPTH, ag_src, ag_dst, "ag_start", own=True)
    conv_landed = []

    def get_weights(layer, x_layer):
        lo = 1 + layer * N_KIND

        def wait(idx, js, after, name):
            return _split_wait([ag_sems[a] for a in idx], [ag_thrus[a] for a in idx],
                               [ag_lands[a] for a in idx], js, ag_src, ag_dst, after, name, own=True)

        if layer == 0:
            conv_land, w_in_l = wait([0, lo], [conv_kind, 0], x_layer, "ag_wait_in_0")
            conv_landed.append(conv_land)
        else:
            w_in_l, = wait([lo], [0], x_layer, "ag_wait_in_%d" % layer)
        return w_in_l, lambda h: wait([lo + 1, lo + 2], [1, 2], h, "ag_wait_kv_out_%d" % layer)

    def conv_full():
        return jnp.transpose(conv_landed[0][:, :6], (1, 0, 2)).reshape(2, 3, N_DEV * c_conv)

    rs_src = shard
    rs_dst = lambda j, ref, me_, k: ref.at[k - 1]
    rs_shapes = [(N_PEER, D_MODEL, c_in), (N_PEER, r_kv, w_mem_kv.shape[2]), (N_PEER, r_out, D_MODEL)]
    own_sizes = [(D_MODEL, c_in), (r_kv, w_mem_kv.shape[2]), (r_out, D_MODEL)]
    pending = {}

    held = {}

    def put_grads(layer, js, arrays):
        if layer > 0 and js != [0]:
            held[layer] = (js, arrays)
            return None
        if layer > 0:
            js, arrays = held[layer][0] + js, held[layer][1] + arrays
        owns = [lax.dynamic_slice(a, own_start(j), own_sizes[j]) for j, a in zip(js, arrays)]
        sems, thrus, lands, token = _split_start(
            arrays, js, [rs_shapes[j] for j in js], rs_src, rs_dst,
            "rs_start_%d_%s" % (layer, "".join(str(j) for j in js)))
        entry = pending.setdefault(layer, ([], [], [], [], []))
        for lst, new in zip(entry, (js, sems, thrus, lands, owns)):
            lst.extend(new)
        return token


    loss, grad_x, d_rel, d_conv, dg, db = _local_step(
        x[0], mem[0], get_weights, put_grads, rel_bias, conv_full, ln_g, ln_b, loss_target[0])

    p_small = _gather_to_all(_pack_small(d_rel, d_conv, dg, db), "gather_small_grads")

    rs_lands, rs_owns = {}, {}

    def rs_wait(layer, want, after, name):
        js, sems, thrus, lands, owns = pending[layer]
        pos = [js.index(j) for j in want]
        got = _split_wait([sems[p] for p in pos], [thrus[p] for p in pos], [lands[p] for p in pos],
                          want, rs_src, rs_dst, after, name)
        for j, p, land in zip(want, pos, got):
            rs_lands[layer, j], rs_owns[layer, j] = land, owns[p]

    for layer in range(1, DEPTH):
        rs_wait(layer, kinds, grad_x, "rs_wait_%d" % layer)
    rs_wait(0, [1, 2], grad_x, "rs_wait_0_kv_out")

    def big(j, w, m, v, name, layers, prev=None):
        return _reduce_adamw_layers([rs_lands[layer, j] for layer in layers], [rs_owns[layer, j] for layer in layers],
                                    w, m, v, name, first=layers[0], prev=prev)

    every = list(range(DEPTH))
    g_kv, d_kv, nm_kv, nv_kv = big(1, w_mem_kv, m_w_mem_kv, v_w_mem_kv, "adamw_w_kv", every)
    g_out, d_out, nm_out, nv_out = big(2, w_out, m_w_out, v_w_out, "adamw_w_out", every)
    later = big(0, w_in, m_w_in, v_w_in, "adamw_w_in_later_layers", every[1:])

    def pack_state(rel, conv, g, b):
        conv_full = jnp.zeros((2, 3, E_MIX), F32)
        conv_full = lax.dynamic_update_slice(conv_full, conv, (0, 0, me * c_conv))
        return _pack_small(rel, conv_full, g, b)

    sm_w = pack_state(rel_bias, conv_w, ln_g, ln_b)
    sm_m = pack_state(m_rel_bias, m_conv_w, m_ln_g, m_ln_b)
    sm_v = pack_state(v_rel_bias, v_conv_w, v_ln_g, v_ln_b)
    sm_outs = _reduce_adamw(p_small, sm_w, sm_m, sm_v, "adamw_small")

    rs_wait(0, [0], [later[0], sm_outs[0], g_kv, g_out], "rs_wait_0_in")
    g_in, d_in, nm_in, nv_in = big(0, w_in, m_w_in, v_w_in, "adamw_w_in_layer_0", [0], prev=later)

    def unpack(buf):
        rel = buf[SM_REL:, :N_REL].reshape(2, N_HEADS, N_REL)
        conv = lax.dynamic_slice(buf[SM_CONV:SM_CONV + 6].reshape(2, 3, E_MIX), (0, 0, me * c_conv), (2, 3, c_conv))
        return rel, conv, buf[SM_G:SM_G + DEPTH], buf[SM_B:SM_B + DEPTH]

    g_sm, d_sm, nm_sm, nv_sm = [unpack(b) for b in sm_outs]

    loss = lax.psum(loss[0, 0], ("x", "y", "c"))
    return (loss, grad_x[None],
            g_in, g_kv, g_out, *g_sm,
            d_in, d_kv, d_out, *d_sm,
            nm_in, nm_kv, nm_out, *nm_sm,
            nv_in, nv_kv, nv_out, *nv_sm)
```

```python
import functools
import math

import jax
import jax.numpy as jnp
from jax import lax
from jax.experimental import pallas as pl
from jax.experimental.pallas import tpu as pltpu

F32 = jnp.float32
BF16 = jnp.bfloat16
MXU_DTYPE = jnp.bfloat16

N_DEV = 8
D_MODEL = 1024
DEPTH = 4
CHUNK = 64
N_PREV = 8
N_HEADS = 16
HEAD_DIM = 64
E_MIX = 1024
REL_CLIP = 128
N_REL = 2 * REL_CLIP + 1
N_REL_PAD = 384
N_MEM = 256
MEM_HEADS = 4
MEM_HEAD_DIM = 128
E_MEM = 512
E_BRANCH = E_MIX + E_MEM
N_IN = 3 * E_MIX + E_MEM + E_BRANCH
DN_ALPHA = (2.0 * DEPTH) ** 0.25
LN_EPS = 1e-5
NEG = -1e30

ADAM_LR = 0.001
ADAM_B1 = 0.9
ADAM_B2 = 0.999
ADAM_EPS = 1e-08
ADAM_WD = 0.01
ADAM_STEP = 10

LANES = 128
SUBLANES = 8
VMEM_LIMIT = 56 * 1024 * 1024

TQ = 4 * CHUNK
TKEYS = 3 * TQ
ROLL_W = 1024
TS = 256
QM_BLK = 3 * E_MIX // E_MEM
Z_BLK = QM_BLK + 1


def _call(body, **kw):
    return pl.pallas_call(body, **kw)


def _cparams(*sem):
    return pltpu.CompilerParams(dimension_semantics=sem, vmem_limit_bytes=VMEM_LIMIT)


def _dot(a, b):
    return jnp.dot(a, b, preferred_element_type=F32)


def _dot_nt(a, b):
    return lax.dot_general(a, b, (((1,), (1,)), ((), ())), preferred_element_type=F32)


def _dot_tn(a, b):
    return lax.dot_general(a, b, (((0,), (0,)), ((), ())), preferred_element_type=F32)


def _inproj(x, w):
    s, d = x.shape
    n = w.shape[1]
    tm = 512
    tn = 1024

    def body(x_ref, w_ref, o_ref, xb_ref):
        xb = x_ref[...].astype(xb_ref.dtype)
        xb_ref[...] = xb
        for j in range(n // tn):
            o_ref[:, j * tn:(j + 1) * tn] = _dot(xb, w_ref[:, j * tn:(j + 1) * tn]).astype(o_ref.dtype)

    return _call(
        body, name="inproj", grid=(s // tm,),
        in_specs=[pl.BlockSpec((tm, d), lambda i: (i, 0)),
                  pl.BlockSpec((d, n), lambda i: (0, 0), pipeline_mode=pl.Buffered(1))],
        out_specs=[pl.BlockSpec((tm, n), lambda i: (i, 0)),
                   pl.BlockSpec((tm, d), lambda i: (i, 0))],
        out_shape=[jax.ShapeDtypeStruct((s, n), BF16), jax.ShapeDtypeStruct((s, d), BF16)],
        compiler_params=_cparams("parallel"),
    )(x, w)


def _small_matmul(a, b, trans_a, out_dtype, name):
    m = a.shape[1] if trans_a else a.shape[0]
    n = b.shape[1]

    def body(a_ref, b_ref, o_ref):
        av = a_ref[...].astype(MXU_DTYPE)
        bv = b_ref[...].astype(MXU_DTYPE)
        r = _dot_tn(av, bv) if trans_a else _dot(av, bv)
        o_ref[...] = r.astype(out_dtype)

    return _call(
        body, name=name,
        in_specs=[pl.BlockSpec(memory_space=pltpu.VMEM)] * 2,
        out_specs=pl.BlockSpec(memory_space=pltpu.VMEM),
        out_shape=jax.ShapeDtypeStruct((m, n), out_dtype),
        compiler_params=pltpu.CompilerParams(vmem_limit_bytes=VMEM_LIMIT),
    )(a, b)


def _piece_blocks(pieces, blk):
    offs, nbs, o = [], [], 0
    for p in pieces:
        nb = p.shape[1] // blk
        offs.append(o)
        nbs.append(nb)
        o += nb
    return offs, nbs, o


def _dx_matmul(pieces, w, addend, token=None):
    s = pieces[0].shape[0]
    d, n_in = w.shape
    tm = 512
    tw = 1024
    np_ = len(pieces)
    extra = [] if token is None else [token]

    def body(*refs):
        a_refs = refs[:np_]
        w_ref, add_ref = refs[np_:np_ + 2]
        o_ref, wt = refs[-2:]

        @pl.when(pl.program_id(0) == 0)
        def _():
            for j in range(n_in // tw):
                wt[j * tw:(j + 1) * tw, :] = w_ref[:, j * tw:(j + 1) * tw].T

        a = jnp.concatenate([r[...] for r in a_refs], axis=1)
        o_ref[...] = add_ref[...] + _dot(a, wt[...])

    in_specs = [pl.BlockSpec((tm, p.shape[1]), lambda i: (i, 0)) for p in pieces]
    in_specs += [pl.BlockSpec((d, n_in), lambda i: (0, 0), pipeline_mode=pl.Buffered(1)),
                 pl.BlockSpec((tm, d), lambda i: (i, 0))]
    in_specs += [pl.BlockSpec((SUBLANES, LANES), lambda i: (0, 0)) for _ in extra]
    return _call(
        body, name="dx_matmul", grid=(s // tm,),
        in_specs=in_specs,
        out_specs=pl.BlockSpec((tm, d), lambda i: (i, 0)),
        out_shape=jax.ShapeDtypeStruct((s, d), F32),
        scratch_shapes=[pltpu.VMEM((n_in, d), w.dtype)],
        compiler_params=_cparams("arbitrary"),
    )(*pieces, w, addend, *extra)


def _dw_matmul(x, pieces):
    s, d = x.shape
    tn = 1024
    tk = min(1024, s)
    offs, nbs, nj = _piece_blocks(pieces, tn)
    np_ = len(pieces)
    nk = s // tk

    def body(*refs):
        x_ref = refs[0]
        b_refs = refs[1:1 + np_]
        o_ref, acc = refs[1 + np_:]
        j = pl.program_id(0)
        k = pl.program_id(1)

        @pl.when(k == 0)
        def _():
            acc[...] = jnp.zeros_like(acc)

        for p in range(np_):
            @pl.when((j >= offs[p]) & (j < offs[p] + nbs[p]))
            def _(p=p):
                acc[...] += _dot_tn(x_ref[...], b_refs[p][...])

        @pl.when(k == nk - 1)
        def _():
            o_ref[...] = acc[...].astype(o_ref.dtype)

    def bmap(p):
        def f(j, k):
            inside = (j >= offs[p]) & (j < offs[p] + nbs[p])
            return (jnp.where(inside, k, 0), jnp.clip(j - offs[p], 0, nbs[p] - 1))
        return f

    in_specs = [pl.BlockSpec((tk, d), lambda j, k: (k, 0))]
    in_specs += [pl.BlockSpec((tk, tn), bmap(p)) for p in range(np_)]
    return _call(
        body, name="dw_matmul", grid=(nj, nk),
        in_specs=in_specs,
        out_specs=pl.BlockSpec((d, tn), lambda j, k: (0, j)),
        out_shape=jax.ShapeDtypeStruct((d, nj * tn), BF16),
        scratch_shapes=[pltpu.VMEM((d, tn), F32)],
        compiler_params=_cparams("parallel", "arbitrary"),
    )(x, *pieces)


def _rel_onehot():
    j = lax.broadcasted_iota(jnp.int32, (N_REL_PAD, ROLL_W), 1)
    kk = lax.broadcasted_iota(jnp.int32, (N_REL_PAD, ROLL_W), 0)
    dd = jnp.where(j < TKEYS, j, j - ROLL_W)
    idx = jnp.clip(N_PREV * CHUNK - dd, -REL_CLIP, REL_CLIP) + REL_CLIP
    return jnp.where(idx == kk, 1.0, 0.0).astype(F32)


def _band_mask():
    r = lax.broadcasted_iota(jnp.int32, (TQ, TKEYS), 0) // CHUNK
    m = lax.broadcasted_iota(jnp.int32, (TQ, TKEYS), 1) // CHUNK
    return (m >= r) & (m <= r + N_PREV)


def _tile_bias(table_pad):
    def body(t_ref, o_ref):
        g = jnp.dot(t_ref[...], _rel_onehot(), preferred_element_type=F32,
                    precision=lax.Precision.HIGHEST)
        band = _band_mask()
        for h in range(N_HEADS):
            gh = jnp.broadcast_to(g[h:h + 1, :], (TQ, ROLL_W))
            rolled = pltpu.roll(gh, 0, 1, stride=1, stride_axis=0)
            o_ref[h] = jnp.where(band, rolled[:, :TKEYS], NEG)

    return _call(
        body, name="tile_bias",
        in_specs=[pl.BlockSpec(memory_space=pltpu.VMEM)],
        out_specs=pl.BlockSpec(memory_space=pltpu.VMEM),
        out_shape=jax.ShapeDtypeStruct((N_HEADS, TQ, TKEYS), F32),
        compiler_params=pltpu.CompilerParams(vmem_limit_bytes=VMEM_LIMIT),
    )(table_pad)


def _tile_bias_bwd(dtb):
    def body(d_ref, o_ref, g_ref):
        zpad = jnp.zeros((TQ, ROLL_W - TKEYS), F32)
        rr = lax.broadcasted_iota(jnp.int32, (TQ, TQ), 0)
        cc = lax.broadcasted_iota(jnp.int32, (TQ, TQ), 1)
        flip = jnp.where(rr + cc == TQ - 1, 1.0, 0.0).astype(F32)
        for h in range(N_HEADS):
            xh = jnp.concatenate([d_ref[h], zpad], axis=1)
            xf = jnp.dot(flip, xh, preferred_element_type=F32, precision=lax.Precision.HIGHEST)
            rolled = pltpu.roll(xf, 0, 1, stride=1, stride_axis=0)
            g_ref[h:h + 1, :] = jnp.sum(rolled, axis=0, keepdims=True)
        g = pltpu.roll(g_ref[...], ROLL_W - (TQ - 1), 1)
        o_ref[...] = lax.dot_general(g, _rel_onehot(), (((1,), (1,)), ((), ())),
                                     preferred_element_type=F32, precision=lax.Precision.HIGHEST)

    return _call(
        body, name="tile_bias_bwd",
        in_specs=[pl.BlockSpec(memory_space=pltpu.VMEM)],
        out_specs=pl.BlockSpec(memory_space=pltpu.VMEM),
        out_shape=jax.ShapeDtypeStruct((N_HEADS, N_REL_PAD), F32),
        scratch_shapes=[pltpu.VMEM((N_HEADS, ROLL_W), F32)],
        compiler_params=pltpu.CompilerParams(vmem_limit_bytes=VMEM_LIMIT),
    )(dtb)


HB = 4
HBW = HB * HEAD_DIM
ATTN_SCALE = 0.125
assert ATTN_SCALE == 1.0 / math.sqrt(HEAD_DIM)


def _head_masks():
    lane = lax.broadcasted_iota(jnp.int32, (1, HBW), 1) // HEAD_DIM
    return [lane == hh for hh in range(HB)]


def _select_heads(masks, parts):
    out = parts[-1]
    for hh in range(HB - 2, -1, -1):
        out = jnp.where(masks[hh], parts[hh], out)
    return out


def _attn_probs(qm, kcat, tb, valid):
    s = _dot_nt(qm, kcat) + tb
    if valid is not None:
        s = jnp.where(valid, s, NEG)
    m = jnp.max(s, axis=-1, keepdims=True)
    e = jnp.exp(s - m)
    return e * (1.0 / jnp.sum(e, axis=-1, keepdims=True))


def _key_valid(i):
    col = lax.broadcasted_iota(jnp.int32, (TQ, TKEYS), 1)
    return col >= jnp.maximum(2 - i, 0) * TQ


def _kv_specs(col0, nt):
    def spec(back):
        return pl.BlockSpec((TQ, HBW), lambda hp, i: (jnp.clip(i - back, 0, nt - 1), col0 + hp))
    return [spec(2), spec(1), spec(0)]


def _attn_fwd(h, tb):
    s = h.shape[0]
    nt = s // TQ
    nhp = N_HEADS // HB

    def body(q_ref, k0, k1, k2, v0, v1, v2, tb_ref, o_ref, p_ref):
        i = pl.program_id(1)

        def tile(valid):
            masks = _head_masks()
            qs = q_ref[...].astype(MXU_DTYPE) * ATTN_SCALE
            kcat = jnp.concatenate([k0[...], k1[...], k2[...]], axis=0).astype(MXU_DTYPE)
            vcat = jnp.concatenate([v0[...], v1[...], v2[...]], axis=0).astype(MXU_DTYPE)
            outs = []
            for hh in range(HB):
                qm = jnp.where(masks[hh], qs, jnp.zeros_like(qs))
                pb = _attn_probs(qm, kcat, tb_ref[hh], valid).astype(MXU_DTYPE)
                p_ref[hh] = pb.astype(p_ref.dtype)
                outs.append(_dot(pb, vcat))
            o_ref[...] = _select_heads(masks, outs).astype(o_ref.dtype)

        @pl.when(i < 2)
        def _():
            tile(_key_valid(i))

        @pl.when(i >= 2)
        def _():
            tile(None)

    in_specs = [pl.BlockSpec((TQ, HBW), lambda hp, i: (i, hp))]
    in_specs += _kv_specs(nhp, nt) + _kv_specs(2 * nhp, nt)
    in_specs += [pl.BlockSpec((HB, TQ, TKEYS), lambda hp, i: (hp, 0, 0))]
    return _call(
        body, name="attn_fwd", grid=(nhp, nt),
        in_specs=in_specs,
        out_specs=[pl.BlockSpec((TQ, HBW), lambda hp, i: (i, hp)),
                   pl.BlockSpec((HB, TQ, TKEYS), lambda hp, i: (hp, i, 0))],
        out_shape=[jax.ShapeDtypeStruct((s, E_MIX), BF16), jax.ShapeDtypeStruct((N_HEADS, s, TKEYS), BF16)],
        compiler_params=_cparams("parallel", "parallel"),
    )(h, h, h, h, h, h, h, tb)


def _attn_bwd(h, probs, d_mix, token=None):
    s = h.shape[0]
    nt = s // TQ
    nhp = N_HEADS // HB
    extra = [] if token is None else [token]

    def body(q_ref, k0, k1, k2, v0, v1, v2, p_ref, do_ref, *rest):
        dq_ref, dk_ref, dv_ref, dtb_ref, dk_acc, dv_acc = rest[len(extra):]
        i = pl.program_id(1)

        @pl.when(i == 0)
        def _():
            dk_acc[...] = jnp.zeros_like(dk_acc)
            dv_acc[...] = jnp.zeros_like(dv_acc)
            dtb_ref[...] = jnp.zeros_like(dtb_ref)

        @pl.when((i > 0) & (i < nt))
        def _():
            dk_acc[i % 3] = jnp.zeros((TQ, HBW), F32)
            dv_acc[i % 3] = jnp.zeros((TQ, HBW), F32)

        @pl.when(i < nt)
        def _():
            masks = _head_masks()
            qs = q_ref[...].astype(MXU_DTYPE) * ATTN_SCALE
            do2 = do_ref[...].astype(MXU_DTYPE)
            kcat = jnp.concatenate([k0[...], k1[...], k2[...]], axis=0).astype(MXU_DTYPE)
            vcat = jnp.concatenate([v0[...], v1[...], v2[...]], axis=0).astype(MXU_DTYPE)
            ks = kcat * ATTN_SCALE
            dqs, dks, dvs = [], [], []
            for hh in range(HB):
                dom = jnp.where(masks[hh], do2, jnp.zeros_like(do2))
                pb = p_ref[hh]
                p = pb.astype(F32)
                dp = _dot_nt(dom, vcat)
                ds = p * (dp - jnp.sum(p * dp, axis=-1, keepdims=True))
                dtb_ref[hh] += ds
                dsb = ds.astype(MXU_DTYPE)
                dqs.append(_dot(dsb, ks))
                dks.append(_dot_tn(dsb, qs))
                dvs.append(_dot_tn(pb.astype(MXU_DTYPE), do2))
            dq_ref[...] = _select_heads(masks, dqs).astype(dq_ref.dtype)
            dkc = _select_heads(masks, dks)
            dvc = _select_heads(masks, dvs)
            for jj in range(3):
                slot = (i + 1 + jj) % 3
                dk_acc[slot] += dkc[jj * TQ:(jj + 1) * TQ]
                dv_acc[slot] += dvc[jj * TQ:(jj + 1) * TQ]

        @pl.when(i >= 2)
        def _():
            slot = (i - 2) % 3
            dk_ref[...] = dk_acc[slot].astype(dk_ref.dtype)
            dv_ref[...] = dv_acc[slot].astype(dv_ref.dtype)

    qmap = lambda hp, i: (jnp.minimum(i, nt - 1), hp)
    kvout = lambda hp, i: (jnp.maximum(i - 2, 0), hp)
    in_specs = [pl.BlockSpec((TQ, HBW), qmap)]
    in_specs += _kv_specs(nhp, nt) + _kv_specs(2 * nhp, nt)
    in_specs += [pl.BlockSpec((HB, TQ, TKEYS), lambda hp, i: (hp, jnp.minimum(i, nt - 1), 0)),
                 pl.BlockSpec((TQ, HBW), qmap)]
    in_specs += [pl.BlockSpec((SUBLANES, LANES), lambda hp, i: (0, 0)) for _ in extra]
    blk = (TQ, HBW)
    return _call(
        body, name="attn_bwd", grid=(nhp, nt + 2),
        in_specs=in_specs,
        out_specs=[pl.BlockSpec(blk, qmap), pl.BlockSpec(blk, kvout), pl.BlockSpec(blk, kvout),
                   pl.BlockSpec((HB, TQ, TKEYS), lambda hp, i: (hp, 0, 0))],
        out_shape=[jax.ShapeDtypeStruct((s, E_MIX), BF16)] * 3
        + [jax.ShapeDtypeStruct((N_HEADS, TQ, TKEYS), F32)],
        scratch_shapes=[pltpu.VMEM((3, TQ, HBW), F32)] * 2,
        compiler_params=_cparams("parallel", "arbitrary"),
    )(h, h, h, h, h, h, h, probs, d_mix, *extra)


CONV_TS = 512
HALO = 2 * SUBLANES


def _shift_down(prev, cur, k):
    rolled = pltpu.roll(cur, k, 0)
    row = lax.broadcasted_iota(jnp.int32, (HALO, cur.shape[1]), 0)
    top = jnp.where(row < k, pltpu.roll(prev, k, 0), rolled[:HALO])
    return jnp.concatenate([top, rolled[HALO:]], axis=0)


def _shift_up(cur, nxt, k):
    ts = cur.shape[0]
    rolled = pltpu.roll(cur, ts - k, 0)
    row = lax.broadcasted_iota(jnp.int32, (HALO, cur.shape[1]), 0)
    bottom = jnp.where(row >= HALO - k, pltpu.roll(nxt, HALO - k, 0), rolled[ts - HALO:])
    return jnp.concatenate([rolled[:ts - HALO], bottom], axis=0)


def _conv_specs(ts, nb):
    tile = lambda c: pl.BlockSpec((ts, E_MIX), lambda i: (i, c))
    prev = lambda c: pl.BlockSpec((HALO, E_MIX), lambda i: (jnp.maximum(i * (ts // HALO) - 1, 0), c))
    return tile, prev


def _conv_fwd(h, w8):
    s = h.shape[0]
    ts = CONV_TS
    nb = s // ts
    tile, prev = _conv_specs(ts, nb)

    def body(bg, cg, u, cgp, up, w_ref, o_ref):
        i = pl.program_id(0)
        a = cg[...].astype(F32) * u[...].astype(F32)
        ap = jnp.where(i > 0, cgp[...].astype(F32) * up[...].astype(F32), 0.0)
        w = w_ref[...]
        conv = w[0:1] * _shift_down(ap, a, 2) + w[1:2] * _shift_down(ap, a, 1) + w[2:3] * a
        o_ref[...] = (bg[...].astype(F32) * conv).astype(o_ref.dtype)

    return _call(
        body, name="conv_fwd", grid=(nb,),
        in_specs=[tile(0), tile(1), tile(2), prev(1), prev(2),
                  pl.BlockSpec((SUBLANES, E_MIX), lambda i: (0, 0))],
        out_specs=pl.BlockSpec((ts, E_MIX), lambda i: (i, 0)),
        out_shape=jax.ShapeDtypeStruct((s, E_MIX), BF16),
        compiler_params=_cparams("parallel"),
    )(h, h, h, h, h, w8)


def _conv_bwd(h, w8, d_mix):
    s = h.shape[0]
    ts = CONV_TS
    nb = s // ts
    tile, prev = _conv_specs(ts, nb)
    nrow = s // HALO
    nxt = lambda c: pl.BlockSpec((HALO, E_MIX), lambda i: (jnp.minimum((i + 1) * (ts // HALO), nrow - 1), c))

    def body(bg, cg, u, cgp, up, bgn, dmix, dmixn, w_ref, dbg_ref, dcg_ref, du_ref, dw_ref):
        i = pl.program_id(0)

        @pl.when(i == 0)
        def _():
            dw_ref[...] = jnp.zeros_like(dw_ref)

        cgv, uv = cg[...].astype(F32), u[...].astype(F32)
        a = cgv * uv
        ap = jnp.where(i > 0, cgp[...].astype(F32) * up[...].astype(F32), 0.0)
        a1 = _shift_down(ap, a, 1)
        a2 = _shift_down(ap, a, 2)
        w = w_ref[...]
        conv = w[0:1] * a2 + w[1:2] * a1 + w[2:3] * a
        dm = dmix[...].astype(F32)
        dbg_ref[...] = (dm * conv).astype(dbg_ref.dtype)
        dc = dm * bg[...].astype(F32)
        dcn = jnp.where(i < nb - 1, dmixn[...].astype(F32) * bgn[...].astype(F32), 0.0)
        da = w[2:3] * dc + w[1:2] * _shift_up(dc, dcn, 1) + w[0:1] * _shift_up(dc, dcn, 2)
        dcg_ref[...] = (da * uv).astype(dcg_ref.dtype)
        du_ref[...] = (da * cgv).astype(du_ref.dtype)
        dw_ref[0:1, :] += jnp.sum(dc * a2, axis=0, keepdims=True)
        dw_ref[1:2, :] += jnp.sum(dc * a1, axis=0, keepdims=True)
        dw_ref[2:3, :] += jnp.sum(dc * a, axis=0, keepdims=True)

    full = lambda: pl.BlockSpec((ts, E_MIX), lambda i: (i, 0))
    return _call(
        body, name="conv_bwd", grid=(nb,),
        in_specs=[tile(0), tile(1), tile(2), prev(1), prev(2), nxt(0),
                  full(), pl.BlockSpec((HALO, E_MIX), lambda i: (jnp.minimum((i + 1) * (ts // HALO), nrow - 1), 0)),
                  pl.BlockSpec((SUBLANES, E_MIX), lambda i: (0, 0))],
        out_specs=[full(), full(), full(), pl.BlockSpec((SUBLANES, E_MIX), lambda i: (0, 0))],
        out_shape=[jax.ShapeDtypeStruct((s, E_MIX), BF16)] * 3
        + [jax.ShapeDtypeStruct((SUBLANES, E_MIX), F32)],
        compiler_params=_cparams("arbitrary"),
    )(h, h, h, h, h, h, d_mix, d_mix, w8)


def _mem_probs(qh, kh):
    s = _dot_nt(qh, kh) / math.sqrt(MEM_HEAD_DIM)
    m = jnp.max(s, axis=-1, keepdims=True)
    e = jnp.exp(s - m)
    return e / jnp.sum(e, axis=-1, keepdims=True)


def _sigmoid(z):
    return 1.0 / (1.0 + jnp.exp(-z))


def _layer_out_fwd(x, h, mix, kv, w_out, g, b, target=None):
    s, d = x.shape
    ts = 2 * TS
    last = target is not None

    def body(x_ref, mix_ref, qm_ref, z0, z1, z2, kv_ref, wo_ref, g_ref, b_ref, *rest):
        xn_ref, r_ref, mem_ref = rest[last:last + 3]
        qm = qm_ref[...].astype(MXU_DTYPE)
        kvb = kv_ref[...].astype(MXU_DTYPE)
        mems = []
        for hh in range(MEM_HEADS):
            lo = hh * MEM_HEAD_DIM
            p = _mem_probs(qm[:, lo:lo + MEM_HEAD_DIM], kvb[:, lo:lo + MEM_HEAD_DIM])
            mems.append(_dot(p.astype(MXU_DTYPE), kvb[:, E_MEM + lo:E_MEM + lo + MEM_HEAD_DIM]))
        mem = jnp.concatenate(mems, axis=1).astype(mem_ref.dtype)
        mem_ref[...] = mem
        mixv = mix_ref[...].astype(F32)
        half = E_MIX // 2
        parts = [mixv[:, :half], mixv[:, half:], mem.astype(F32)]
        out = jnp.zeros((ts, d), F32)
        for c, zr in enumerate((z0, z1, z2)):
            zv = zr[...].astype(F32)
            y = (parts[c] * (zv * _sigmoid(zv))).astype(MXU_DTYPE)
            out += _dot(y, wo_ref[c * half:(c + 1) * half, :])
        r = DN_ALPHA * x_ref[...] + out
        r_ref[...] = r
        mu = jnp.mean(r, axis=-1, keepdims=True)
        rc = r - mu
        var = jnp.mean(rc * rc, axis=-1, keepdims=True)
        xn = rc * lax.rsqrt(var + LN_EPS) * g_ref[...] + b_ref[...]
        if not last:
            xn_ref[...] = xn
        else:
            t_ref, l_ref = rest[0], rest[4]

            @pl.when(pl.program_id(0) == 0)
            def _():
                l_ref[...] = jnp.zeros_like(l_ref)

            e = xn - t_ref[...]
            xn_ref[...] = e * (1.0 / d)
            l_ref[...] += (0.5 / d) * jnp.sum(jnp.sum(e * e, axis=1, keepdims=True), axis=0, keepdims=True)

    row = lambda w, c: pl.BlockSpec((ts, w), lambda i: (i, c))
    const = lambda shp: pl.BlockSpec(shp, lambda i: (0, 0))
    tail = [target] if last else []
    return _call(
        body, name="layer_out_fwd_loss" if last else "layer_out_fwd", grid=(s // ts,),
        in_specs=[row(d, 0), row(E_MIX, 0), row(E_MEM, QM_BLK),
                  row(E_MEM, Z_BLK), row(E_MEM, Z_BLK + 1), row(E_MEM, Z_BLK + 2),
                  const((N_MEM, 2 * E_MEM)), const((E_BRANCH, d)), const((1, d)), const((1, d))]
        + [row(d, 0) for _ in tail],
        out_specs=[row(d, 0), row(d, 0), row(E_MEM, 0)] + [const((1, 1)) for _ in tail],
        out_shape=[jax.ShapeDtypeStruct((s, d), F32), jax.ShapeDtypeStruct((s, d), F32),
                   jax.ShapeDtypeStruct((s, E_MEM), BF16)] + [jax.ShapeDtypeStruct((1, 1), F32) for _ in tail],
        compiler_params=_cparams("arbitrary" if last else "parallel"),
    )(x, mix, h, h, h, h, kv, w_out, g, b, *tail)


def _layer_out_bwd(dxn, r, g, h, mix, mem, kv, w_out_t):
    s, d = r.shape
    ts = 2 * TS
    nb = s // ts
    half = E_MIX // 2
    inv = 1.0 / math.sqrt(MEM_HEAD_DIM)

    def body(dxn_ref, r_ref, g_ref, mix_ref, mem_ref, qm_ref, z0, z1, z2, kv_ref, wo_ref,
             dxr_ref, dmix_ref, dqz_ref, dwo_ref, dkv_ref, dg_ref, db_ref, dw_acc):
        i = pl.program_id(0)

        @pl.when(i == 0)
        def _():
            dw_acc[...] = jnp.zeros_like(dw_acc)
            dkv_ref[...] = jnp.zeros_like(dkv_ref)
            dg_ref[...] = jnp.zeros_like(dg_ref)
            db_ref[...] = jnp.zeros_like(db_ref)

        dxn_v = dxn_ref[...]
        rv = r_ref[...]
        mu = jnp.mean(rv, axis=-1, keepdims=True)
        rc = rv - mu
        var = jnp.mean(rc * rc, axis=-1, keepdims=True)
        rstd = lax.rsqrt(var + LN_EPS)
        xhat = rc * rstd
        dg_ref[...] += jnp.sum(dxn_v * xhat, axis=0, keepdims=True)
        db_ref[...] += jnp.sum(dxn_v, axis=0, keepdims=True)
        dxh = dxn_v * g_ref[...]
        m1 = jnp.mean(dxh, axis=-1, keepdims=True)
        m2 = jnp.mean(dxh * xhat, axis=-1, keepdims=True)
        dr = rstd * (dxh - m1 - xhat * m2)
        dxr_ref[...] = DN_ALPHA * dr
        dout = dr.astype(MXU_DTYPE)
        mixv = mix_ref[...].astype(F32)
        parts = [mixv[:, :half], mixv[:, half:], mem_ref[...].astype(F32)]
        dcs = []
        for c, zr in enumerate((z0, z1, z2)):
            lo = c * half
            zv = zr[...].astype(F32)
            sg = _sigmoid(zv)
            sl = zv * sg
            dy = _dot(dout, wo_ref[:, lo:lo + half])
            y = (parts[c] * sl).astype(MXU_DTYPE)
            dw_acc[lo:lo + half, :] += _dot_tn(y, dout)
            dcs.append(dy * sl)
            dqz_ref[:, E_MEM + lo:E_MEM + lo + half] = (
                dy * parts[c] * (sg * (1.0 + zv * (1.0 - sg)))).astype(dqz_ref.dtype)
        dmix_ref[...] = jnp.concatenate(dcs[:2], axis=1).astype(dmix_ref.dtype)

        qm = qm_ref[...].astype(MXU_DTYPE)
        kvb = kv_ref[...].astype(MXU_DTYPE)
        dmb = dcs[2].astype(MXU_DTYPE)
        for hh in range(MEM_HEADS):
            lo = hh * MEM_HEAD_DIM
            qh = qm[:, lo:lo + MEM_HEAD_DIM]
            kh = kvb[:, lo:lo + MEM_HEAD_DIM]
            vh = kvb[:, E_MEM + lo:E_MEM + lo + MEM_HEAD_DIM]
            dmh = dmb[:, lo:lo + MEM_HEAD_DIM]
            p = _mem_probs(qh, kh)
            dp = _dot_nt(dmh, vh)
            ds = p * (dp - jnp.sum(p * dp, axis=-1, keepdims=True))
            dsb = (ds * inv).astype(MXU_DTYPE)
            dqz_ref[:, lo:lo + MEM_HEAD_DIM] = _dot(dsb, kh).astype(dqz_ref.dtype)
            dkv_ref[:, lo:lo + MEM_HEAD_DIM] += _dot_tn(dsb, qh)
            dkv_ref[:, E_MEM + lo:E_MEM + lo + MEM_HEAD_DIM] += _dot_tn(p.astype(MXU_DTYPE), dmh)

        @pl.when(i == nb - 1)
        def _():
            dwo_ref[...] = dw_acc[...].astype(dwo_ref.dtype)

    row = lambda w, c: pl.BlockSpec((ts, w), lambda i: (i, c))
    const = lambda shp: pl.BlockSpec(shp, lambda i: (0, 0))
    once = lambda shp: pl.BlockSpec(shp, lambda i: (0, 0), pipeline_mode=pl.Buffered(1))
    return _call(
        body, name="layer_out_bwd", grid=(nb,),
        in_specs=[row(d, 0), row(d, 0), const((1, d)), row(E_MIX, 0), row(E_MEM, 0),
                  row(E_MEM, QM_BLK), row(E_MEM, Z_BLK), row(E_MEM, Z_BLK + 1), row(E_MEM, Z_BLK + 2),
                  once((N_MEM, 2 * E_MEM)), once((d, E_BRANCH))],
        out_specs=[row(d, 0), row(E_MIX, 0), row(E_MEM + E_BRANCH, 0),
                   const((E_BRANCH, d)), const((N_MEM, 2 * E_MEM)), const((1, d)), const((1, d))],
        out_shape=[jax.ShapeDtypeStruct((s, d), F32), jax.ShapeDtypeStruct((s, E_MIX), BF16),
                   jax.ShapeDtypeStruct((s, E_MEM + E_BRANCH), BF16),
                   jax.ShapeDtypeStruct((E_BRANCH, d), BF16),
                   jax.ShapeDtypeStruct((N_MEM, 2 * E_MEM), F32),
                   jax.ShapeDtypeStruct((1, d), F32), jax.ShapeDtypeStruct((1, d), F32)],
        scratch_shapes=[pltpu.VMEM((E_BRANCH, d), F32)],
        compiler_params=_cparams("arbitrary"),
    )(dxn, r, g, mix, mem, h, h, h, h, kv, w_out_t)


def _local_step(x, mem, get_weights, put_grads, rel_bias, conv_w, ln_g, ln_b, target):
    biases = [_tile_bias(jnp.pad(rel_bias[a], ((0, 0), (0, N_REL_PAD - N_REL)))) for a in range(DEPTH // 2)]

    saved = []
    xl = x
    for layer in range(DEPTH):
        w_in_l, rest = get_weights(layer, xl if layer else biases)
        h, xb = _inproj(xl, w_in_l)
        w_kv_l, w_out_l = rest(h)
        if layer % 2 == 0:
            mix, aux = _attn_fwd(h, biases[layer // 2])
        else:
            aux = jnp.pad(conv_w()[layer // 2], ((0, SUBLANES - 3), (0, 0)))
            mix = _conv_fwd(h, aux)
        kv = _small_matmul(mem, w_kv_l, False, F32, "kv_mem")
        if layer < DEPTH - 1:
            x_next, r, mem_out = _layer_out_fwd(xl, h, mix, kv, w_out_l, ln_g[layer][None], ln_b[layer][None])
        else:
            dx, r, mem_out, loss = _layer_out_fwd(xl, h, mix, kv, w_out_l, ln_g[layer][None], ln_b[layer][None],
                                                  target)
            x_next = None
        saved.append((xb, h, aux, mix, kv, r, mem_out, w_in_l, w_out_l))
        xl = x_next

    dgs, dbs, d_rel, d_conv = [], [], [], []
    for layer in reversed(range(DEPTH)):
        xb, h, aux, mix, kv, r, mem_out, w_in_l, w_out_l = saved[layer]
        dx_res, d_mix, dqz, dwo, dkv, dg, db = _layer_out_bwd(
            dx, r, ln_g[layer][None], h, mix, mem_out, kv, w_out_l.T)
        early = put_grads(layer, [1, 2], [_small_matmul(mem, dkv, True, BF16, "dw_kv"), dwo])
        if layer % 2 == 0:
            dq, dk, dv, dtb = _attn_bwd(h, aux, d_mix, early)
            d_rel.append(_tile_bias_bwd(dtb)[:, :N_REL])
            pieces = [dq, dk, dv, dqz]
        else:
            dbg, dcg, du, dw8 = _conv_bwd(h, aux, d_mix)
            d_conv.append(dw8[:3])
            pieces = [dbg, dcg, du, dqz]
        token = put_grads(layer, [0], [_dw_matmul(xb, pieces)])
        dgs.append(dg[0])
        dbs.append(db[0])
        dx = _dx_matmul(pieces, w_in_l, dx_res, token)

    rev = lambda lst: jnp.stack(lst[::-1])
    return loss, dx, rev(d_rel), rev(d_conv), rev(dgs), rev(dbs)


def _me():
    return lax.axis_index("x"), lax.axis_index("y"), lax.axis_index("c")


def _peer(k):
    x, y, c = _me()
    kx, ky, kc = (k >> 2) & 1, (k >> 1) & 1, k & 1
    return (1 - x if kx else x, 1 - y if ky else y, 1 - c if kc else c)


def _lin(dev):
    return 4 * dev[0] + 2 * dev[1] + dev[2]


ANY = pl.BlockSpec(memory_space=pl.ANY)


def _exchange(srcs, dst_shapes, src_slice, dst_slice, name):
    na = len(srcs)

    def body(*refs):
        src_refs = refs[:na]
        dst_refs = refs[na:2 * na]
        send_sems, recv_sems, local_sems = refs[2 * na:]
        me = _lin(_me())
        copies = []
        for a in range(na):
            loc = pltpu.make_async_copy(src_slice(a, src_refs[a], me), dst_slice(a, dst_refs[a], me),
                                        local_sems.at[a])
            loc.start()
            copies.append(loc)
            for k in range(1, N_DEV):
                peer = _peer(k)
                cp = pltpu.make_async_remote_copy(
                    src_ref=src_slice(a, src_refs[a], _lin(peer)),
                    dst_ref=dst_slice(a, dst_refs[a], me),
                    send_sem=send_sems.at[a, k - 1], recv_sem=recv_sems.at[a, k - 1],
                    device_id=peer, device_id_type=pl.DeviceIdType.MESH)
                cp.start()
                copies.append(cp)
        for cp in copies:
            cp.wait()

    return _call(
        body, name=name,
        in_specs=[ANY] * na, out_specs=[ANY] * na,
        out_shape=[jax.ShapeDtypeStruct(shp, s.dtype) for shp, s in zip(dst_shapes, srcs)],
        scratch_shapes=[pltpu.SemaphoreType.DMA((na, N_DEV - 1)),
                        pltpu.SemaphoreType.DMA((na, N_DEV - 1)),
                        pltpu.SemaphoreType.DMA((na,))],
    )(*srcs)


def _gather_to_all(src, name):
    return _exchange([src], [(N_DEV,) + src.shape], lambda a, ref, p: ref,
                     lambda a, ref, me: ref.at[me], name)[0]


HBM = pl.BlockSpec(memory_space=pltpu.HBM)
SEM = pl.BlockSpec(memory_space=pltpu.SEMAPHORE)
EFFECT = pltpu.SideEffectType.DATAFLOW_SIDE_EFFECTING
N_PEER = N_DEV - 1
N_KIND = 3


def _peer_copies(kind, src_ref, land_ref, send, recv, src_slice, dst_slice):
    me = _lin(_me())
    copies = []
    for k in range(1, N_DEV):
        peer = _peer(k)
        copies.append(pltpu.make_async_remote_copy(
            src_ref=src_slice(kind, src_ref, _lin(peer)),
            dst_ref=dst_slice(kind, land_ref, me, k),
            send_sem=send.at[k - 1], recv_sem=recv.at[k - 1],
            device_id=peer, device_id_type=pl.DeviceIdType.MESH))
    return copies


def _own_copy(kind, src_ref, land_ref, send, src_slice, dst_slice):
    me = _lin(_me())
    return pltpu.make_async_copy(src_slice(kind, src_ref, me), dst_slice(kind, land_ref, me, 0),
                                 send.at[N_PEER])


def _split_start(srcs, kinds, land_shapes, src_slice, dst_slice, name, own=False):
    na = len(srcs)

    def body(*refs):
        src_refs, land_refs = refs[:na], refs[na:2 * na]
        sems = refs[2 * na:4 * na]
        token = refs[-1]
        for a in range(na):
            for cp in _peer_copies(kinds[a], src_refs[a], land_refs[a], sems[2 * a], sems[2 * a + 1],
                                   src_slice, dst_slice):
                cp.start()
            if own:
                _own_copy(kinds[a], src_refs[a], land_refs[a], sems[2 * a], src_slice, dst_slice).start()
        token[...] = jnp.zeros_like(token)

    sem_shape = pltpu.SemaphoreType.DMA((N_DEV,))
    lands = [lax.empty(shp, s.dtype) for shp, s in zip(land_shapes, srcs)]
    outs = _call(
        body, name=name,
        in_specs=[HBM] * (2 * na),
        out_specs=[SEM] * (2 * na) + [HBM] * (2 * na) + [pl.BlockSpec(memory_space=pltpu.VMEM)],
        out_shape=[sem_shape] * (2 * na)
        + [pltpu.HBM(s.shape, s.dtype) for s in srcs]
        + [pltpu.HBM(shp, s.dtype) for shp, s in zip(land_shapes, srcs)]
        + [jax.ShapeDtypeStruct((SUBLANES, LANES), F32)],
        input_output_aliases={i: 2 * na + i for i in range(2 * na)},
        compiler_params=pltpu.CompilerParams(has_side_effects=EFFECT),
    )(*[pltpu.with_memory_space_constraint(a, pltpu.HBM) for a in list(srcs) + lands])
    sems = [(outs[2 * a], outs[2 * a + 1]) for a in range(na)]
    thrus = outs[2 * na:3 * na]
    lands = outs[3 * na:4 * na]
    return sems, thrus, lands, outs[-1]


def _split_wait(sems, thrus, lands, kinds, src_slice, dst_slice, after, name, own=False):
    na = len(thrus)
    after = list(after) if isinstance(after, (list, tuple)) else [after]

    def body(*refs):
        src_refs, land_refs = refs[:na], refs[na:2 * na]
        sem_refs = refs[2 * na:4 * na]
        for a in range(na):
            for cp in _peer_copies(kinds[a], src_refs[a], land_refs[a], sem_refs[2 * a], sem_refs[2 * a + 1],
                                   src_slice, dst_slice):
                cp.wait_send()
                cp.wait_recv()
            if own:
                _own_copy(kinds[a], src_refs[a], land_refs[a], sem_refs[2 * a], src_slice, dst_slice).wait()

    outs = _call(
        body, name=name,
        in_specs=[HBM] * (2 * na) + [SEM] * (2 * na) + [ANY] * len(after),
        out_specs=[HBM] * (2 * na),
        out_shape=[pltpu.HBM(a.shape, a.dtype) for a in list(thrus) + list(lands)],
        input_output_aliases={i: i for i in range(2 * na)},
        compiler_params=pltpu.CompilerParams(has_side_effects=EFFECT),
    )(*thrus, *lands, *[s for pair in sems for s in pair], *after)
    return outs[na:]


def _shard_dims(c_in, r_kv, r_out):
    def sl(j, ref, p):
        if j == 0:
            return ref.at[:, pl.ds(pl.multiple_of(p * c_in, LANES), c_in)]
        r = r_kv if j == 1 else r_out
        return ref.at[pl.ds(pl.multiple_of(p * r, 2 * SUBLANES), r), :]
    return sl


def _adamw_math(w, g, m, v):
    m = ADAM_B1 * m + (1.0 - ADAM_B1) * g
    v = ADAM_B2 * v + (1.0 - ADAM_B2) * (g * g)
    m_hat = m / (1.0 - ADAM_B1 ** ADAM_STEP)
    v_hat = v / (1.0 - ADAM_B2 ** ADAM_STEP)
    delta = -ADAM_LR * (m_hat / (jnp.sqrt(v_hat) + ADAM_EPS) + ADAM_WD * w)
    return delta, m, v


def _reduce_adamw(parts, w, m, v, name):
    rows, cols = w.shape
    tr = rows
    for cand in (512, 256, 128, 64, 32, 16):
        if rows % cand == 0 and rows > cand:
            tr = cand
            break

    def body(p_ref, w_ref, m_ref, v_ref, g_out, d_out, m_out, v_out):
        g = p_ref[0].astype(F32)
        for s in range(1, N_DEV):
            g = g + p_ref[s].astype(F32)
        g_out[...] = g
        d_out[...], m_out[...], v_out[...] = _adamw_math(w_ref[...], g, m_ref[...], v_ref[...])

    blk = pl.BlockSpec((tr, cols), lambda i: (i, 0))
    return _call(
        body, name=name, grid=(rows // tr,),
        in_specs=[pl.BlockSpec((N_DEV, tr, cols), lambda i: (0, i, 0)), blk, blk, blk],
        out_specs=[blk] * 4,
        out_shape=[jax.ShapeDtypeStruct((rows, cols), F32)] * 4,
        compiler_params=_cparams("parallel"),
    )(parts, w, m, v)


def _reduce_adamw_layers(lands, owns, w, m, v, name, first=0, prev=None):
    depth, rows, cols = w.shape
    nl = len(lands)
    tr = rows
    for cand in (256, 192, 128):
        if rows % cand == 0:
            tr = cand
            break

    kept = [] if prev is None else list(prev)

    def body(*refs):
        land_refs, own_refs = refs[:nl], refs[nl:2 * nl]
        w_ref, m_ref, v_ref = refs[2 * nl:2 * nl + 3]
        g_out, d_out, m_out, v_out = refs[2 * nl + 3 + len(kept):]
        layer = pl.program_id(0)
        for a in range(nl):
            @pl.when(layer == a)
            def _(a=a):
                g = own_refs[a][...].astype(F32)
                for k in range(N_PEER):
                    g = g + land_refs[a][k].astype(F32)
                g_out[...] = g
                d_out[...], m_out[...], v_out[...] = _adamw_math(w_ref[...], g, m_ref[...], v_ref[...])

    def lmap(a):
        return lambda l, i: (0, jnp.where(l == a, i, 0), 0)

    def omap(a):
        return lambda l, i: (jnp.where(l == a, i, 0), 0)

    blk = pl.BlockSpec((None, tr, cols), lambda l, i: (l + first, i, 0))
    n_in = 2 * nl + 3
    return _call(
        body, name=name, grid=(nl, rows // tr),
        in_specs=[pl.BlockSpec((N_PEER, tr, cols), lmap(a)) for a in range(nl)]
        + [pl.BlockSpec((tr, cols), omap(a)) for a in range(nl)] + [blk, blk, blk] + [ANY] * len(kept),
        out_specs=[blk] * 4,
        out_shape=[jax.ShapeDtypeStruct((depth, rows, cols), F32)] * 4,
        input_output_aliases={n_in + i: i for i in range(len(kept))},
        compiler_params=_cparams("arbitrary", "arbitrary"),
    )(*lands, *owns, w, m, v, *kept)


SM_G, SM_B, SM_CONV, SM_REL = 0, 4, 8, 16
SM_ROWS = SM_REL + 2 * N_HEADS
REL_W = 384


def _pack_small(d_rel, d_conv, dg, db):
    buf = jnp.zeros((SM_ROWS, D_MODEL), F32)
    buf = buf.at[SM_G:SM_G + DEPTH].set(dg)
    buf = buf.at[SM_B:SM_B + DEPTH].set(db)
    buf = buf.at[SM_CONV:SM_CONV + 6].set(d_conv.reshape(6, E_MIX))
    buf = buf.at[SM_REL:, :N_REL].set(d_rel.reshape(2 * N_HEADS, N_REL))
    return buf


def kernel(x, mem, w_in, w_mem_kv, w_out, rel_bias, conv_w, ln_g, ln_b, loss_target, m_w_in, m_w_mem_kv, m_w_out, m_rel_bias, m_conv_w, m_ln_g, m_ln_b, v_w_in, v_w_mem_kv, v_w_out, v_rel_bias, v_conv_w, v_ln_g, v_ln_b):
    me = _lin(_me())
    c_in, r_kv, r_out, c_conv = w_in.shape[2], w_mem_kv.shape[1], w_out.shape[1], conv_w.shape[2]

    shard = _shard_dims(c_in, r_kv, r_out)
    own_start = lambda j: (0, me * c_in) if j == 0 else (me * (r_kv if j == 1 else r_out), 0)

    w_sh = [w_in.astype(BF16), w_mem_kv.astype(BF16), w_out.astype(BF16)]
    full_shapes = [(D_MODEL, N_DEV * c_in), (N_DEV * r_kv, w_mem_kv.shape[2]), (N_DEV * r_out, D_MODEL)]
    conv_kind = N_KIND
    conv_tile = jnp.pad(conv_w.reshape(6, c_conv), ((0, SUBLANES - 6), (0, 0)))
    ag_src = lambda j, ref, p: ref
    ag_dst = lambda j, ref, me_, k: ref.at[me_] if j == conv_kind else shard(j, ref, me_)
    kinds = list(range(N_KIND))
    ag_sems, ag_thrus, ag_lands, _ = _split_start(
        [conv_tile] + [w_sh[j][layer] for layer in range(DEPTH) for j in kinds], [conv_kind] + kinds * DEPTH,
        [(N_DEV,) + conv_tile.shape] + full_shapes * DEPTH, ag_src, ag_dst, "ag_start", own=True)
    conv_landed = []

    def get_weights(layer, x_layer):
        lo = 1 + layer * N_KIND

        def wait(idx, js, after, name):
            return _split_wait([ag_sems[a] for a in idx], [ag_thrus[a] for a in idx],
                               [ag_lands[a] for a in idx], js, ag_src, ag_dst, after, name, own=True)

        if layer == 0:
            conv_land, w_in_l = wait([0, lo], [conv_kind, 0], x_layer, "ag_wait_in_0")
            conv_landed.append(conv_land)
        else:
            w_in_l, = wait([lo], [0], x_layer, "ag_wait_in_%d" % layer)
        return w_in_l, lambda h: wait([lo + 1, lo + 2], [1, 2], h, "ag_wait_kv_out_%d" % layer)

    def conv_full():
        return jnp.transpose(conv_landed[0][:, :6], (1, 0, 2)).reshape(2, 3, N_DEV * c_conv)

    rs_src = shard
    rs_dst = lambda j, ref, me_, k: ref.at[k - 1]
    rs_shapes = [(N_PEER, D_MODEL, c_in), (N_PEER, r_kv, w_mem_kv.shape[2]), (N_PEER, r_out, D_MODEL)]
    own_sizes = [(D_MODEL, c_in), (r_kv, w_mem_kv.shape[2]), (r_out, D_MODEL)]
    pending = {}

    held = {}

    def put_grads(layer, js, arrays):
        if layer > 0 and js != [0]:
            held[layer] = (js, arrays)
            return None
        if layer > 0:
            js, arrays = held[layer][0] + js, held[layer][1] + arrays
        owns = [lax.dynamic_slice(a, own_start(j), own_sizes[j]) for j, a in zip(js, arrays)]
        sems, thrus, lands, token = _split_start(
            arrays, js, [rs_shapes[j] for j in js], rs_src, rs_dst,
            "rs_start_%d_%s" % (layer, "".join(str(j) for j in js)))
        entry = pending.setdefault(layer, ([], [], [], [], []))
        for lst, new in zip(entry, (js, sems, thrus, lands, owns)):
            lst.extend(new)
        return token


    loss, grad_x, d_rel, d_conv, dg, db = _local_step(
        x[0], mem[0], get_weights, put_grads, rel_bias, conv_full, ln_g, ln_b, loss_target[0])

    p_small = _gather_to_all(_pack_small(d_rel, d_conv, dg, db), "gather_small_grads")

    rs_lands, rs_owns = {}, {}

    def rs_wait(layer, want, after, name):
        js, sems, thrus, lands, owns = pending[layer]
        pos = [js.index(j) for j in want]
        got = _split_wait([sems[p] for p in pos], [thrus[p] for p in pos], [lands[p] for p in pos],
                          want, rs_src, rs_dst, after, name)
        for j, p, land in zip(want, pos, got):
            rs_lands[layer, j], rs_owns[layer, j] = land, owns[p]

    for layer in range(1, DEPTH):
        rs_wait(layer, kinds, grad_x, "rs_wait_%d" % layer)
    rs_wait(0, [1, 2], grad_x, "rs_wait_0_kv_out")

    def big(j, w, m, v, name, layers, prev=None):
        return _reduce_adamw_layers([rs_lands[layer, j] for layer in layers], [rs_owns[layer, j] for layer in layers],
                                    w, m, v, name, first=layers[0], prev=prev)

    every = list(range(DEPTH))
    g_kv, d_kv, nm_kv, nv_kv = big(1, w_mem_kv, m_w_mem_kv, v_w_mem_kv, "adamw_w_kv", every)
    g_out, d_out, nm_out, nv_out = big(2, w_out, m_w_out, v_w_out, "adamw_w_out", every)
    later = big(0, w_in, m_w_in, v_w_in, "adamw_w_in_later_layers", every[1:])

    def pack_state(rel, conv, g, b):
        conv_full = jnp.zeros((2, 3, E_MIX), F32)
        conv_full = lax.dynamic_update_slice(conv_full, conv, (0, 0, me * c_conv))
        return _pack_small(rel, conv_full, g, b)

    sm_w = pack_state(rel_bias, conv_w, ln_g, ln_b)
    sm_m = pack_state(m_rel_bias, m_conv_w, m_ln_g, m_ln_b)
    sm_v = pack_state(v_rel_bias, v_conv_w, v_ln_g, v_ln_b)
    sm_outs = _reduce_adamw(p_small, sm_w, sm_m, sm_v, "adamw_small")

    rs_wait(0, [0], [later[0], sm_outs[0], g_kv, g_out], "rs_wait_0_in")
    g_in, d_in, nm_in, nv_in = big(0, w_in, m_w_in, v_w_in, "adamw_w_in_layer_0", [0], prev=later)

    def unpack(buf):
        rel = buf[SM_REL:, :N_REL].reshape(2, N_HEADS, N_REL)
        conv = lax.dynamic_slice(buf[SM_CONV:SM_CONV + 6].reshape(2, 3, E_MIX), (0, 0, me * c_conv), (2, 3, c_conv))
        return rel, conv, buf[SM_G:SM_G + DEPTH], buf[SM_B:SM_B + DEPTH]

    g_sm, d_sm, nm_sm, nv_sm = [unpack(b) for b in sm_outs]

    loss = lax.psum(loss[0, 0], ("x", "y", "c"))
    return (loss, grad_x[None],
            g_in, g_kv, g_out, *g_sm,
            d_in, d_kv, d_out, *d_sm,
            nm_in, nm_kv, nm_out, *nm_sm,
            nv_in, nv_kv, nv_out, *nv_sm)
```

```python
import functools
import math

import jax
import jax.numpy as jnp
from jax import lax
from jax.experimental import pallas as pl
from jax.experimental.pallas import tpu as pltpu

F32 = jnp.float32
BF16 = jnp.bfloat16
MXU_DTYPE = jnp.bfloat16

N_DEV = 8
D_MODEL = 1024
DEPTH = 4
CHUNK = 64
N_PREV = 8
N_HEADS = 16
HEAD_DIM = 64
E_MIX = 1024
REL_CLIP = 128
N_REL = 2 * REL_CLIP + 1
N_REL_PAD = 384
N_MEM = 256
MEM_HEADS = 4
MEM_HEAD_DIM = 128
E_MEM = 512
E_BRANCH = E_MIX + E_MEM
N_IN = 3 * E_MIX + E_MEM + E_BRANCH
DN_ALPHA = (2.0 * DEPTH) ** 0.25
LN_EPS = 1e-5
NEG = -1e30

ADAM_LR = 0.001
ADAM_B1 = 0.9
ADAM_B2 = 0.999
ADAM_EPS = 1e-08
ADAM_WD = 0.01
ADAM_STEP = 10

LANES = 128
SUBLANES = 8
VMEM_LIMIT = 56 * 1024 * 1024

TQ = 4 * CHUNK
TKEYS = 3 * TQ
ROLL_W = 1024
TS = 256
QM_BLK = 3 * E_MIX // E_MEM
Z_BLK = QM_BLK + 1


def _call(body, **kw):
    return pl.pallas_call(body, **kw)


def _cparams(*sem):
    return pltpu.CompilerParams(dimension_semantics=sem, vmem_limit_bytes=VMEM_LIMIT)


def _dot(a, b):
    return jnp.dot(a, b, preferred_element_type=F32)


def _dot_nt(a, b):
    return lax.dot_general(a, b, (((1,), (1,)), ((), ())), preferred_element_type=F32)


def _dot_tn(a, b):
    return lax.dot_general(a, b, (((0,), (0,)), ((), ())), preferred_element_type=F32)


def _inproj(x, w):
    s, d = x.shape
    n = w.shape[1]
    tm = 512
    tn = 1024

    def body(x_ref, w_ref, o_ref, xt_ref):
        xb = x_ref[...].astype(xt_ref.dtype)
        xt_ref[...] = xb.T
        for j in range(n // tn):
            o_ref[:, j * tn:(j + 1) * tn] = _dot(xb, w_ref[:, j * tn:(j + 1) * tn]).astype(o_ref.dtype)

    return _call(
        body, name="inproj", grid=(s // tm,),
        in_specs=[pl.BlockSpec((tm, d), lambda i: (i, 0)),
                  pl.BlockSpec((d, n), lambda i: (0, 0), pipeline_mode=pl.Buffered(1))],
        out_specs=[pl.BlockSpec((tm, n), lambda i: (i, 0)),
                   pl.BlockSpec((d, tm), lambda i: (0, i))],
        out_shape=[jax.ShapeDtypeStruct((s, n), BF16), jax.ShapeDtypeStruct((d, s), BF16)],
        compiler_params=_cparams("parallel"),
    )(x, w)


def _small_matmul(a, b, trans_a, out_dtype, name):
    m = a.shape[1] if trans_a else a.shape[0]
    n = b.shape[1]

    def body(a_ref, b_ref, o_ref):
        av = a_ref[...].astype(MXU_DTYPE)
        bv = b_ref[...].astype(MXU_DTYPE)
        r = _dot_tn(av, bv) if trans_a else _dot(av, bv)
        o_ref[...] = r.astype(out_dtype)

    return _call(
        body, name=name,
        in_specs=[pl.BlockSpec(memory_space=pltpu.VMEM)] * 2,
        out_specs=pl.BlockSpec(memory_space=pltpu.VMEM),
        out_shape=jax.ShapeDtypeStruct((m, n), out_dtype),
        compiler_params=pltpu.CompilerParams(vmem_limit_bytes=VMEM_LIMIT),
    )(a, b)


def _piece_blocks(pieces, blk):
    offs, nbs, o = [], [], 0
    for p in pieces:
        nb = p.shape[1] // blk
        offs.append(o)
        nbs.append(nb)
        o += nb
    return offs, nbs, o


def _dx_matmul(pieces, w, addend, token=None):
    s = pieces[0].shape[0]
    d, n_in = w.shape
    tm = 512
    tw = 1024
    np_ = len(pieces)
    extra = [] if token is None else [token]

    def body(*refs):
        a_refs = refs[:np_]
        w_ref, add_ref = refs[np_:np_ + 2]
        o_ref, wt = refs[-2:]

        @pl.when(pl.program_id(0) == 0)
        def _():
            for j in range(n_in // tw):
                wt[j * tw:(j + 1) * tw, :] = w_ref[:, j * tw:(j + 1) * tw].T

        a = jnp.concatenate([r[...] for r in a_refs], axis=1)
        o_ref[...] = add_ref[...] + _dot(a, wt[...])

    in_specs = [pl.BlockSpec((tm, p.shape[1]), lambda i: (i, 0)) for p in pieces]
    in_specs += [pl.BlockSpec((d, n_in), lambda i: (0, 0), pipeline_mode=pl.Buffered(1)),
                 pl.BlockSpec((tm, d), lambda i: (i, 0))]
    in_specs += [pl.BlockSpec((SUBLANES, LANES), lambda i: (0, 0)) for _ in extra]
    return _call(
        body, name="dx_matmul", grid=(s // tm,),
        in_specs=in_specs,
        out_specs=pl.BlockSpec((tm, d), lambda i: (i, 0)),
        out_shape=jax.ShapeDtypeStruct((s, d), F32),
        scratch_shapes=[pltpu.VMEM((n_in, d), w.dtype)],
        compiler_params=_cparams("arbitrary"),
    )(*pieces, w, addend, *extra)


def _dw_matmul(xt, pieces):
    d, s = xt.shape
    tn = 512
    tk = min(2048, s)
    offs, nbs, nj = _piece_blocks(pieces, tn)
    np_ = len(pieces)
    nk = s // tk

    def body(*refs):
        x_ref = refs[0]
        b_refs = refs[1:1 + np_]
        o_ref, acc = refs[1 + np_:]
        j = pl.program_id(0)
        k = pl.program_id(1)

        @pl.when(k == 0)
        def _():
            acc[...] = jnp.zeros_like(acc)

        for p in range(np_):
            @pl.when((j >= offs[p]) & (j < offs[p] + nbs[p]))
            def _(p=p):
                acc[...] += _dot(x_ref[...], b_refs[p][...])

        @pl.when(k == nk - 1)
        def _():
            o_ref[...] = acc[...].astype(o_ref.dtype)

    def bmap(p):
        def f(j, k):
            inside = (j >= offs[p]) & (j < offs[p] + nbs[p])
            return (jnp.where(inside, k, 0), jnp.clip(j - offs[p], 0, nbs[p] - 1))
        return f

    in_specs = [pl.BlockSpec((d, tk), lambda j, k: (0, k))]
    in_specs += [pl.BlockSpec((tk, tn), bmap(p)) for p in range(np_)]
    return _call(
        body, name="dw_matmul", grid=(nj, nk),
        in_specs=in_specs,
        out_specs=pl.BlockSpec((d, tn), lambda j, k: (0, j)),
        out_shape=jax.ShapeDtypeStruct((d, nj * tn), BF16),
        scratch_shapes=[pltpu.VMEM((d, tn), F32)],
        compiler_params=_cparams("parallel", "arbitrary"),
    )(xt, *pieces)


def _rel_onehot():
    j = lax.broadcasted_iota(jnp.int32, (N_REL_PAD, ROLL_W), 1)
    kk = lax.broadcasted_iota(jnp.int32, (N_REL_PAD, ROLL_W), 0)
    dd = jnp.where(j < TKEYS, j, j - ROLL_W)
    idx = jnp.clip(N_PREV * CHUNK - dd, -REL_CLIP, REL_CLIP) + REL_CLIP
    return jnp.where(idx == kk, 1.0, 0.0).astype(F32)


def _band_mask():
    r = lax.broadcasted_iota(jnp.int32, (TQ, TKEYS), 0) // CHUNK
    m = lax.broadcasted_iota(jnp.int32, (TQ, TKEYS), 1) // CHUNK
    return (m >= r) & (m <= r + N_PREV)


def _tile_bias(table_pad):
    def body(t_ref, o_ref):
        g = jnp.dot(t_ref[...], _rel_onehot(), preferred_element_type=F32,
                    precision=lax.Precision.HIGHEST)
        band = _band_mask()
        for h in range(N_HEADS):
            gh = jnp.broadcast_to(g[h:h + 1, :], (TQ, ROLL_W))
            rolled = pltpu.roll(gh, 0, 1, stride=1, stride_axis=0)
            o_ref[h] = jnp.where(band, rolled[:, :TKEYS], NEG)

    return _call(
        body, name="tile_bias",
        in_specs=[pl.BlockSpec(memory_space=pltpu.VMEM)],
        out_specs=pl.BlockSpec(memory_space=pltpu.VMEM),
        out_shape=jax.ShapeDtypeStruct((N_HEADS, TQ, TKEYS), F32),
        compiler_params=pltpu.CompilerParams(vmem_limit_bytes=VMEM_LIMIT),
    )(table_pad)


def _tile_bias_bwd(dtb):
    def body(d_ref, o_ref, g_ref):
        zpad = jnp.zeros((TQ, ROLL_W - TKEYS), F32)
        rr = lax.broadcasted_iota(jnp.int32, (TQ, TQ), 0)
        cc = lax.broadcasted_iota(jnp.int32, (TQ, TQ), 1)
        flip = jnp.where(rr + cc == TQ - 1, 1.0, 0.0).astype(F32)
        for h in range(N_HEADS):
            xh = jnp.concatenate([d_ref[h], zpad], axis=1)
            xf = jnp.dot(flip, xh, preferred_element_type=F32, precision=lax.Precision.HIGHEST)
            rolled = pltpu.roll(xf, 0, 1, stride=1, stride_axis=0)
            g_ref[h:h + 1, :] = jnp.sum(rolled, axis=0, keepdims=True)
        g = pltpu.roll(g_ref[...], ROLL_W - (TQ - 1), 1)
        o_ref[...] = lax.dot_general(g, _rel_onehot(), (((1,), (1,)), ((), ())),
                                     preferred_element_type=F32, precision=lax.Precision.HIGHEST)

    return _call(
        body, name="tile_bias_bwd",
        in_specs=[pl.BlockSpec(memory_space=pltpu.VMEM)],
        out_specs=pl.BlockSpec(memory_space=pltpu.VMEM),
        out_shape=jax.ShapeDtypeStruct((N_HEADS, N_REL_PAD), F32),
        scratch_shapes=[pltpu.VMEM((N_HEADS, ROLL_W), F32)],
        compiler_params=pltpu.CompilerParams(vmem_limit_bytes=VMEM_LIMIT),
    )(dtb)


HB = 4
HBW = HB * HEAD_DIM
ATTN_SCALE = 0.125
assert ATTN_SCALE == 1.0 / math.sqrt(HEAD_DIM)


def _head_masks():
    lane = lax.broadcasted_iota(jnp.int32, (1, HBW), 1) // HEAD_DIM
    return [lane == hh for hh in range(HB)]


def _select_heads(masks, parts):
    out = parts[-1]
    for hh in range(HB - 2, -1, -1):
        out = jnp.where(masks[hh], parts[hh], out)
    return out


def _attn_probs(qm, kcat, tb, valid):
    s = _dot_nt(qm, kcat) + tb
    if valid is not None:
        s = jnp.where(valid, s, NEG)
    m = jnp.max(s, axis=-1, keepdims=True)
    e = jnp.exp(s - m)
    return e * (1.0 / jnp.sum(e, axis=-1, keepdims=True))


def _key_valid(i):
    col = lax.broadcasted_iota(jnp.int32, (TQ, TKEYS), 1)
    return col >= jnp.maximum(2 - i, 0) * TQ


def _kv_specs(col0, nt):
    def spec(back):
        return pl.BlockSpec((TQ, HBW), lambda hp, i: (jnp.clip(i - back, 0, nt - 1), col0 + hp))
    return [spec(2), spec(1), spec(0)]


def _attn_fwd(h, tb):
    s = h.shape[0]
    nt = s // TQ
    nhp = N_HEADS // HB

    def body(q_ref, k0, k1, k2, v0, v1, v2, tb_ref, o_ref, p_ref):
        i = pl.program_id(1)

        def tile(valid):
            masks = _head_masks()
            qs = q_ref[...].astype(MXU_DTYPE) * ATTN_SCALE
            kcat = jnp.concatenate([k0[...], k1[...], k2[...]], axis=0).astype(MXU_DTYPE)
            vcat = jnp.concatenate([v0[...], v1[...], v2[...]], axis=0).astype(MXU_DTYPE)
            outs = []
            for hh in range(HB):
                qm = jnp.where(masks[hh], qs, jnp.zeros_like(qs))
                pb = _attn_probs(qm, kcat, tb_ref[hh], valid).astype(MXU_DTYPE)
                p_ref[hh] = pb.astype(p_ref.dtype)
                outs.append(_dot(pb, vcat))
            o_ref[...] = _select_heads(masks, outs).astype(o_ref.dtype)

        @pl.when(i < 2)
        def _():
            tile(_key_valid(i))

        @pl.when(i >= 2)
        def _():
            tile(None)

    in_specs = [pl.BlockSpec((TQ, HBW), lambda hp, i: (i, hp))]
    in_specs += _kv_specs(nhp, nt) + _kv_specs(2 * nhp, nt)
    in_specs += [pl.BlockSpec((HB, TQ, TKEYS), lambda hp, i: (hp, 0, 0))]
    return _call(
        body, name="attn_fwd", grid=(nhp, nt),
        in_specs=in_specs,
        out_specs=[pl.BlockSpec((TQ, HBW), lambda hp, i: (i, hp)),
                   pl.BlockSpec((HB, TQ, TKEYS), lambda hp, i: (hp, i, 0))],
        out_shape=[jax.ShapeDtypeStruct((s, E_MIX), BF16), jax.ShapeDtypeStruct((N_HEADS, s, TKEYS), BF16)],
        compiler_params=_cparams("parallel", "parallel"),
    )(h, h, h, h, h, h, h, tb)


def _attn_bwd(h, probs, d_mix, token=None):
    s = h.shape[0]
    nt = s // TQ
    nhp = N_HEADS // HB
    extra = [] if token is None else [token]

    def body(q_ref, k0, k1, k2, v0, v1, v2, p_ref, do_ref, *rest):
        dq_ref, dk_ref, dv_ref, dtb_ref, dk_acc, dv_acc = rest[len(extra):]
        i = pl.program_id(1)

        @pl.when(i == 0)
        def _():
            dk_acc[...] = jnp.zeros_like(dk_acc)
            dv_acc[...] = jnp.zeros_like(dv_acc)
            dtb_ref[...] = jnp.zeros_like(dtb_ref)

        @pl.when((i > 0) & (i < nt))
        def _():
            dk_acc[i % 3] = jnp.zeros((TQ, HBW), F32)
            dv_acc[i % 3] = jnp.zeros((TQ, HBW), F32)

        @pl.when(i < nt)
        def _():
            masks = _head_masks()
            qs = q_ref[...].astype(MXU_DTYPE) * ATTN_SCALE
            do2 = do_ref[...].astype(MXU_DTYPE)
            kcat = jnp.concatenate([k0[...], k1[...], k2[...]], axis=0).astype(MXU_DTYPE)
            vcat = jnp.concatenate([v0[...], v1[...], v2[...]], axis=0).astype(MXU_DTYPE)
            ks = kcat * ATTN_SCALE
            dqs, dks, dvs = [], [], []
            for hh in range(HB):
                dom = jnp.where(masks[hh], do2, jnp.zeros_like(do2))
                pb = p_ref[hh]
                p = pb.astype(F32)
                dp = _dot_nt(dom, vcat)
                ds = p * (dp - jnp.sum(p * dp, axis=-1, keepdims=True))
                dtb_ref[hh] += ds
                dsb = ds.astype(MXU_DTYPE)
                dqs.append(_dot(dsb, ks))
                dks.append(_dot_tn(dsb, qs))
                dvs.append(_dot_tn(pb.astype(MXU_DTYPE), do2))
            dq_ref[...] = _select_heads(masks, dqs).astype(dq_ref.dtype)
            dkc = _select_heads(masks, dks)
            dvc = _select_heads(masks, dvs)
            for jj in range(3):
                slot = (i + 1 + jj) % 3
                dk_acc[slot] += dkc[jj * TQ:(jj + 1) * TQ]
                dv_acc[slot] += dvc[jj * TQ:(jj + 1) * TQ]

        @pl.when(i >= 2)
        def _():
            slot = (i - 2) % 3
            dk_ref[...] = dk_acc[slot].astype(dk_ref.dtype)
            dv_ref[...] = dv_acc[slot].astype(dv_ref.dtype)

    qmap = lambda hp, i: (jnp.minimum(i, nt - 1), hp)
    kvout = lambda hp, i: (jnp.maximum(i - 2, 0), hp)
    in_specs = [pl.BlockSpec((TQ, HBW), qmap)]
    in_specs += _kv_specs(nhp, nt) + _kv_specs(2 * nhp, nt)
    in_specs += [pl.BlockSpec((HB, TQ, TKEYS), lambda hp, i: (hp, jnp.minimum(i, nt - 1), 0)),
                 pl.BlockSpec((TQ, HBW), qmap)]
    in_specs += [pl.BlockSpec((SUBLANES, LANES), lambda hp, i: (0, 0)) for _ in extra]
    blk = (TQ, HBW)
    return _call(
        body, name="attn_bwd", grid=(nhp, nt + 2),
        in_specs=in_specs,
        out_specs=[pl.BlockSpec(blk, qmap), pl.BlockSpec(blk, kvout), pl.BlockSpec(blk, kvout),
                   pl.BlockSpec((HB, TQ, TKEYS), lambda hp, i: (hp, 0, 0))],
        out_shape=[jax.ShapeDtypeStruct((s, E_MIX), BF16)] * 3
        + [jax.ShapeDtypeStruct((N_HEADS, TQ, TKEYS), F32)],
        scratch_shapes=[pltpu.VMEM((3, TQ, HBW), F32)] * 2,
        compiler_params=_cparams("parallel", "arbitrary"),
    )(h, h, h, h, h, h, h, probs, d_mix, *extra)


CONV_TS = 512
HALO = 2 * SUBLANES


def _shift_down(prev, cur, k):
    rolled = pltpu.roll(cur, k, 0)
    row = lax.broadcasted_iota(jnp.int32, (HALO, cur.shape[1]), 0)
    top = jnp.where(row < k, pltpu.roll(prev, k, 0), rolled[:HALO])
    return jnp.concatenate([top, rolled[HALO:]], axis=0)


def _shift_up(cur, nxt, k):
    ts = cur.shape[0]
    rolled = pltpu.roll(cur, ts - k, 0)
    row = lax.broadcasted_iota(jnp.int32, (HALO, cur.shape[1]), 0)
    bottom = jnp.where(row >= HALO - k, pltpu.roll(nxt, HALO - k, 0), rolled[ts - HALO:])
    return jnp.concatenate([rolled[:ts - HALO], bottom], axis=0)


def _conv_specs(ts, nb):
    tile = lambda c: pl.BlockSpec((ts, E_MIX), lambda i: (i, c))
    prev = lambda c: pl.BlockSpec((HALO, E_MIX), lambda i: (jnp.maximum(i * (ts // HALO) - 1, 0), c))
    return tile, prev


def _conv_fwd(h, w8):
    s = h.shape[0]
    ts = CONV_TS
    nb = s // ts
    tile, prev = _conv_specs(ts, nb)

    def body(bg, cg, u, cgp, up, w_ref, o_ref):
        i = pl.program_id(0)
        a = cg[...].astype(F32) * u[...].astype(F32)
        ap = jnp.where(i > 0, cgp[...].astype(F32) * up[...].astype(F32), 0.0)
        w = w_ref[...]
        conv = w[0:1] * _shift_down(ap, a, 2) + w[1:2] * _shift_down(ap, a, 1) + w[2:3] * a
        o_ref[...] = (bg[...].astype(F32) * conv).astype(o_ref.dtype)

    return _call(
        body, name="conv_fwd", grid=(nb,),
        in_specs=[tile(0), tile(1), tile(2), prev(1), prev(2),
                  pl.BlockSpec((SUBLANES, E_MIX), lambda i: (0, 0))],
        out_specs=pl.BlockSpec((ts, E_MIX), lambda i: (i, 0)),
        out_shape=jax.ShapeDtypeStruct((s, E_MIX), BF16),
        compiler_params=_cparams("parallel"),
    )(h, h, h, h, h, w8)


def _conv_bwd(h, w8, d_mix):
    s = h.shape[0]
    ts = CONV_TS
    nb = s // ts
    tile, prev = _conv_specs(ts, nb)
    nrow = s // HALO
    nxt = lambda c: pl.BlockSpec((HALO, E_MIX), lambda i: (jnp.minimum((i + 1) * (ts // HALO), nrow - 1), c))

    def body(bg, cg, u, cgp, up, bgn, dmix, dmixn, w_ref, dbg_ref, dcg_ref, du_ref, dw_ref):
        i = pl.program_id(0)

        @pl.when(i == 0)
        def _():
            dw_ref[...] = jnp.zeros_like(dw_ref)

        cgv, uv = cg[...].astype(F32), u[...].astype(F32)
        a = cgv * uv
        ap = jnp.where(i > 0, cgp[...].astype(F32) * up[...].astype(F32), 0.0)
        a1 = _shift_down(ap, a, 1)
        a2 = _shift_down(ap, a, 2)
        w = w_ref[...]
        conv = w[0:1] * a2 + w[1:2] * a1 + w[2:3] * a
        dm = dmix[...].astype(F32)
        dbg_ref[...] = (dm * conv).astype(dbg_ref.dtype)
        dc = dm * bg[...].astype(F32)
        dcn = jnp.where(i < nb - 1, dmixn[...].astype(F32) * bgn[...].astype(F32), 0.0)
        da = w[2:3] * dc + w[1:2] * _shift_up(dc, dcn, 1) + w[0:1] * _shift_up(dc, dcn, 2)
        dcg_ref[...] = (da * uv).astype(dcg_ref.dtype)
        du_ref[...] = (da * cgv).astype(du_ref.dtype)
        dw_ref[0:1, :] += jnp.sum(dc * a2, axis=0, keepdims=True)
        dw_ref[1:2, :] += jnp.sum(dc * a1, axis=0, keepdims=True)
        dw_ref[2:3, :] += jnp.sum(dc * a, axis=0, keepdims=True)

    full = lambda: pl.BlockSpec((ts, E_MIX), lambda i: (i, 0))
    return _call(
        body, name="conv_bwd", grid=(nb,),
        in_specs=[tile(0), tile(1), tile(2), prev(1), prev(2), nxt(0),
                  full(), pl.BlockSpec((HALO, E_MIX), lambda i: (jnp.minimum((i + 1) * (ts // HALO), nrow - 1), 0)),
                  pl.BlockSpec((SUBLANES, E_MIX), lambda i: (0, 0))],
        out_specs=[full(), full(), full(), pl.BlockSpec((SUBLANES, E_MIX), lambda i: (0, 0))],
        out_shape=[jax.ShapeDtypeStruct((s, E_MIX), BF16)] * 3
        + [jax.ShapeDtypeStruct((SUBLANES, E_MIX), F32)],
        compiler_params=_cparams("arbitrary"),
    )(h, h, h, h, h, h, d_mix, d_mix, w8)


def _mem_probs(qh, kh):
    s = _dot_nt(qh, kh) / math.sqrt(MEM_HEAD_DIM)
    m = jnp.max(s, axis=-1, keepdims=True)
    e = jnp.exp(s - m)
    return e / jnp.sum(e, axis=-1, keepdims=True)


def _sigmoid(z):
    return 1.0 / (1.0 + jnp.exp(-z))


def _layer_out_fwd(x, h, mix, kv, w_out, g, b, target=None):
    s, d = x.shape
    ts = 2 * TS
    last = target is not None

    def body(x_ref, mix_ref, qm_ref, z0, z1, z2, kv_ref, wo_ref, g_ref, b_ref, *rest):
        xn_ref, r_ref, mem_ref = rest[last:last + 3]
        qm = qm_ref[...].astype(MXU_DTYPE)
        kvb = kv_ref[...].astype(MXU_DTYPE)
        mems = []
        for hh in range(MEM_HEADS):
            lo = hh * MEM_HEAD_DIM
            p = _mem_probs(qm[:, lo:lo + MEM_HEAD_DIM], kvb[:, lo:lo + MEM_HEAD_DIM])
            mems.append(_dot(p.astype(MXU_DTYPE), kvb[:, E_MEM + lo:E_MEM + lo + MEM_HEAD_DIM]))
        mem = jnp.concatenate(mems, axis=1).astype(mem_ref.dtype)
        mem_ref[...] = mem
        mixv = mix_ref[...].astype(F32)
        half = E_MIX // 2
        parts = [mixv[:, :half], mixv[:, half:], mem.astype(F32)]
        out = jnp.zeros((ts, d), F32)
        for c, zr in enumerate((z0, z1, z2)):
            zv = zr[...].astype(F32)
            y = (parts[c] * (zv * _sigmoid(zv))).astype(MXU_DTYPE)
            out += _dot(y, wo_ref[c * half:(c + 1) * half, :])
        r = DN_ALPHA * x_ref[...] + out
        r_ref[...] = r
        mu = jnp.mean(r, axis=-1, keepdims=True)
        rc = r - mu
        var = jnp.mean(rc * rc, axis=-1, keepdims=True)
        xn = rc * lax.rsqrt(var + LN_EPS) * g_ref[...] + b_ref[...]
        if not last:
            xn_ref[...] = xn
        else:
            t_ref, l_ref = rest[0], rest[4]

            @pl.when(pl.program_id(0) == 0)
            def _():
                l_ref[...] = jnp.zeros_like(l_ref)

            e = xn - t_ref[...]
            xn_ref[...] = e * (1.0 / d)
            l_ref[...] += (0.5 / d) * jnp.sum(jnp.sum(e * e, axis=1, keepdims=True), axis=0, keepdims=True)

    row = lambda w, c: pl.BlockSpec((ts, w), lambda i: (i, c))
    const = lambda shp: pl.BlockSpec(shp, lambda i: (0, 0))
    tail = [target] if last else []
    return _call(
        body, name="layer_out_fwd_loss" if last else "layer_out_fwd", grid=(s // ts,),
        in_specs=[row(d, 0), row(E_MIX, 0), row(E_MEM, QM_BLK),
                  row(E_MEM, Z_BLK), row(E_MEM, Z_BLK + 1), row(E_MEM, Z_BLK + 2),
                  const((N_MEM, 2 * E_MEM)), const((E_BRANCH, d)), const((1, d)), const((1, d))]
        + [row(d, 0) for _ in tail],
        out_specs=[row(d, 0), row(d, 0), row(E_MEM, 0)] + [const((1, 1)) for _ in tail],
        out_shape=[jax.ShapeDtypeStruct((s, d), F32), jax.ShapeDtypeStruct((s, d), F32),
                   jax.ShapeDtypeStruct((s, E_MEM), BF16)] + [jax.ShapeDtypeStruct((1, 1), F32) for _ in tail],
        compiler_params=_cparams("arbitrary" if last else "parallel"),
    )(x, mix, h, h, h, h, kv, w_out, g, b, *tail)


def _layer_out_bwd(dxn, r, g, h, mix, mem, kv, w_out_t):
    s, d = r.shape
    ts = 2 * TS
    nb = s // ts
    half = E_MIX // 2
    inv = 1.0 / math.sqrt(MEM_HEAD_DIM)

    def body(dxn_ref, r_ref, g_ref, mix_ref, mem_ref, qm_ref, z0, z1, z2, kv_ref, wo_ref,
             dxr_ref, dmix_ref, dqz_ref, dwo_ref, dkv_ref, dg_ref, db_ref, dw_acc):
        i = pl.program_id(0)

        @pl.when(i == 0)
        def _():
            dw_acc[...] = jnp.zeros_like(dw_acc)
            dkv_ref[...] = jnp.zeros_like(dkv_ref)
            dg_ref[...] = jnp.zeros_like(dg_ref)
            db_ref[...] = jnp.zeros_like(db_ref)

        dxn_v = dxn_ref[...]
        rv = r_ref[...]
        mu = jnp.mean(rv, axis=-1, keepdims=True)
        rc = rv - mu
        var = jnp.mean(rc * rc, axis=-1, keepdims=True)
        rstd = lax.rsqrt(var + LN_EPS)
        xhat = rc * rstd
        dg_ref[...] += jnp.sum(dxn_v * xhat, axis=0, keepdims=True)
        db_ref[...] += jnp.sum(dxn_v, axis=0, keepdims=True)
        dxh = dxn_v * g_ref[...]
        m1 = jnp.mean(dxh, axis=-1, keepdims=True)
        m2 = jnp.mean(dxh * xhat, axis=-1, keepdims=True)
        dr = rstd * (dxh - m1 - xhat * m2)
        dxr_ref[...] = DN_ALPHA * dr
        dout = dr.astype(MXU_DTYPE)
        mixv = mix_ref[...].astype(F32)
        parts = [mixv[:, :half], mixv[:, half:], mem_ref[...].astype(F32)]
        dcs = []
        for c, zr in enumerate((z0, z1, z2)):
            lo = c * half
            zv = zr[...].astype(F32)
            sg = _sigmoid(zv)
            sl = zv * sg
            dy = _dot(dout, wo_ref[:, lo:lo + half])
            y = (parts[c] * sl).astype(MXU_DTYPE)
            dw_acc[lo:lo + half, :] += _dot_tn(y, dout)
            dcs.append(dy * sl)
            dqz_ref[:, E_MEM + lo:E_MEM + lo + half] = (
                dy * parts[c] * (sg * (1.0 + zv * (1.0 - sg)))).astype(dqz_ref.dtype)
        dmix_ref[...] = jnp.concatenate(dcs[:2], axis=1).astype(dmix_ref.dtype)

        qm = qm_ref[...].astype(MXU_DTYPE)
        kvb = kv_ref[...].astype(MXU_DTYPE)
        dmb = dcs[2].astype(MXU_DTYPE)
        for hh in range(MEM_HEADS):
            lo = hh * MEM_HEAD_DIM
            qh = qm[:, lo:lo + MEM_HEAD_DIM]
            kh = kvb[:, lo:lo + MEM_HEAD_DIM]
            vh = kvb[:, E_MEM + lo:E_MEM + lo + MEM_HEAD_DIM]
            dmh = dmb[:, lo:lo + MEM_HEAD_DIM]
            p = _mem_probs(qh, kh)
            dp = _dot_nt(dmh, vh)
            ds = p * (dp - jnp.sum(p * dp, axis=-1, keepdims=True))
            dsb = (ds * inv).astype(MXU_DTYPE)
            dqz_ref[:, lo:lo + MEM_HEAD_DIM] = _dot(dsb, kh).astype(dqz_ref.dtype)
            dkv_ref[:, lo:lo + MEM_HEAD_DIM] += _dot_tn(dsb, qh)
            dkv_ref[:, E_MEM + lo:E_MEM + lo + MEM_HEAD_DIM] += _dot_tn(p.astype(MXU_DTYPE), dmh)

        @pl.when(i == nb - 1)
        def _():
            dwo_ref[...] = dw_acc[...].astype(dwo_ref.dtype)

    row = lambda w, c: pl.BlockSpec((ts, w), lambda i: (i, c))
    const = lambda shp: pl.BlockSpec(shp, lambda i: (0, 0))
    once = lambda shp: pl.BlockSpec(shp, lambda i: (0, 0), pipeline_mode=pl.Buffered(1))
    return _call(
        body, name="layer_out_bwd", grid=(nb,),
        in_specs=[row(d, 0), row(d, 0), const((1, d)), row(E_MIX, 0), row(E_MEM, 0),
                  row(E_MEM, QM_BLK), row(E_MEM, Z_BLK), row(E_MEM, Z_BLK + 1), row(E_MEM, Z_BLK + 2),
                  once((N_MEM, 2 * E_MEM)), once((d, E_BRANCH))],
        out_specs=[row(d, 0), row(E_MIX, 0), row(E_MEM + E_BRANCH, 0),
                   const((E_BRANCH, d)), const((N_MEM, 2 * E_MEM)), const((1, d)), const((1, d))],
        out_shape=[jax.ShapeDtypeStruct((s, d), F32), jax.ShapeDtypeStruct((s, E_MIX), BF16),
                   jax.ShapeDtypeStruct((s, E_MEM + E_BRANCH), BF16),
                   jax.ShapeDtypeStruct((E_BRANCH, d), BF16),
                   jax.ShapeDtypeStruct((N_MEM, 2 * E_MEM), F32),
                   jax.ShapeDtypeStruct((1, d), F32), jax.ShapeDtypeStruct((1, d), F32)],
        scratch_shapes=[pltpu.VMEM((E_BRANCH, d), F32)],
        compiler_params=_cparams("arbitrary"),
    )(dxn, r, g, mix, mem, h, h, h, h, kv, w_out_t)


def _local_step(x, mem, get_weights, put_grads, rel_bias, conv_w, ln_g, ln_b, target):
    biases = [_tile_bias(jnp.pad(rel_bias[a], ((0, 0), (0, N_REL_PAD - N_REL)))) for a in range(DEPTH // 2)]

    saved = []
    xl = x
    for layer in range(DEPTH):
        w_in_l, rest = get_weights(layer, xl if layer else biases)
        h, xt = _inproj(xl, w_in_l)
        w_kv_l, w_out_l = rest(h)
        if layer % 2 == 0:
            mix, aux = _attn_fwd(h, biases[layer // 2])
        else:
            aux = jnp.pad(conv_w()[layer // 2], ((0, SUBLANES - 3), (0, 0)))
            mix = _conv_fwd(h, aux)
        kv = _small_matmul(mem, w_kv_l, False, F32, "kv_mem")
        if layer < DEPTH - 1:
            x_next, r, mem_out = _layer_out_fwd(xl, h, mix, kv, w_out_l, ln_g[layer][None], ln_b[layer][None])
        else:
            dx, r, mem_out, loss = _layer_out_fwd(xl, h, mix, kv, w_out_l, ln_g[layer][None], ln_b[layer][None],
                                                  target)
            x_next = None
        saved.append((xt, h, aux, mix, kv, r, mem_out, w_in_l, w_out_l))
        xl = x_next

    dgs, dbs, d_rel, d_conv = [], [], [], []
    for layer in reversed(range(DEPTH)):
        xt, h, aux, mix, kv, r, mem_out, w_in_l, w_out_l = saved[layer]
        dx_res, d_mix, dqz, dwo, dkv, dg, db = _layer_out_bwd(
            dx, r, ln_g[layer][None], h, mix, mem_out, kv, w_out_l.T)
        early = put_grads(layer, [1, 2], [_small_matmul(mem, dkv, True, BF16, "dw_kv"), dwo])
        if layer % 2 == 0:
            dq, dk, dv, dtb = _attn_bwd(h, aux, d_mix, early)
            d_rel.append(_tile_bias_bwd(dtb)[:, :N_REL])
            pieces = [dq, dk, dv, dqz]
        else:
            dbg, dcg, du, dw8 = _conv_bwd(h, aux, d_mix)
            d_conv.append(dw8[:3])
            pieces = [dbg, dcg, du, dqz]
        token = put_grads(layer, [0], [_dw_matmul(xt, pieces)])
        dgs.append(dg[0])
        dbs.append(db[0])
        dx = _dx_matmul(pieces, w_in_l, dx_res, token)

    rev = lambda lst: jnp.stack(lst[::-1])
    return loss, dx, rev(d_rel), rev(d_conv), rev(dgs), rev(dbs)


def _me():
    return lax.axis_index("x"), lax.axis_index("y"), lax.axis_index("c")


def _peer(k):
    x, y, c = _me()
    kx, ky, kc = (k >> 2) & 1, (k >> 1) & 1, k & 1
    return (1 - x if kx else x, 1 - y if ky else y, 1 - c if kc else c)


def _lin(dev):
    return 4 * dev[0] + 2 * dev[1] + dev[2]


ANY = pl.BlockSpec(memory_space=pl.ANY)


def _exchange(srcs, dst_shapes, src_slice, dst_slice, name):
    na = len(srcs)

    def body(*refs):
        src_refs = refs[:na]
        dst_refs = refs[na:2 * na]
        send_sems, recv_sems, local_sems = refs[2 * na:]
        me = _lin(_me())
        copies = []
        for a in range(na):
            loc = pltpu.make_async_copy(src_slice(a, src_refs[a], me), dst_slice(a, dst_refs[a], me),
                                        local_sems.at[a])
            loc.start()
            copies.append(loc)
            for k in range(1, N_DEV):
                peer = _peer(k)
                cp = pltpu.make_async_remote_copy(
                    src_ref=src_slice(a, src_refs[a], _lin(peer)),
                    dst_ref=dst_slice(a, dst_refs[a], me),
                    send_sem=send_sems.at[a, k - 1], recv_sem=recv_sems.at[a, k - 1],
                    device_id=peer, device_id_type=pl.DeviceIdType.MESH)
                cp.start()
                copies.append(cp)
        for cp in copies:
            cp.wait()

    return _call(
        body, name=name,
        in_specs=[ANY] * na, out_specs=[ANY] * na,
        out_shape=[jax.ShapeDtypeStruct(shp, s.dtype) for shp, s in zip(dst_shapes, srcs)],
        scratch_shapes=[pltpu.SemaphoreType.DMA((na, N_DEV - 1)),
                        pltpu.SemaphoreType.DMA((na, N_DEV - 1)),
                        pltpu.SemaphoreType.DMA((na,))],
    )(*srcs)


def _gather_to_all(src, name):
    return _exchange([src], [(N_DEV,) + src.shape], lambda a, ref, p: ref,
                     lambda a, ref, me: ref.at[me], name)[0]


HBM = pl.BlockSpec(memory_space=pltpu.HBM)
SEM = pl.BlockSpec(memory_space=pltpu.SEMAPHORE)
EFFECT = pltpu.SideEffectType.DATAFLOW_SIDE_EFFECTING
N_PEER = N_DEV - 1
N_KIND = 3


def _peer_copies(kind, src_ref, land_ref, send, recv, src_slice, dst_slice):
    me = _lin(_me())
    copies = []
    for k in range(1, N_DEV):
        peer = _peer(k)
        copies.append(pltpu.make_async_remote_copy(
            src_ref=src_slice(kind, src_ref, _lin(peer)),
            dst_ref=dst_slice(kind, land_ref, me, k),
            send_sem=send.at[k - 1], recv_sem=recv.at[k - 1],
            device_id=peer, device_id_type=pl.DeviceIdType.MESH))
    return copies


def _own_copy(kind, src_ref, land_ref, send, src_slice, dst_slice):
    me = _lin(_me())
    return pltpu.make_async_copy(src_slice(kind, src_ref, me), dst_slice(kind, land_ref, me, 0),
                                 send.at[N_PEER])


def _split_start(srcs, kinds, land_shapes, src_slice, dst_slice, name, own=False):
    na = len(srcs)

    def body(*refs):
        src_refs, land_refs = refs[:na], refs[na:2 * na]
        sems = refs[2 * na:4 * na]
        token = refs[-1]
        for a in range(na):
            for cp in _peer_copies(kinds[a], src_refs[a], land_refs[a], sems[2 * a], sems[2 * a + 1],
                                   src_slice, dst_slice):
                cp.start()
            if own:
                _own_copy(kinds[a], src_refs[a], land_refs[a], sems[2 * a], src_slice, dst_slice).start()
        token[...] = jnp.zeros_like(token)

    sem_shape = pltpu.SemaphoreType.DMA((N_DEV,))
    lands = [lax.empty(shp, s.dtype) for shp, s in zip(land_shapes, srcs)]
    outs = _call(
        body, name=name,
        in_specs=[HBM] * (2 * na),
        out_specs=[SEM] * (2 * na) + [HBM] * (2 * na) + [pl.BlockSpec(memory_space=pltpu.VMEM)],
        out_shape=[sem_shape] * (2 * na)
        + [pltpu.HBM(s.shape, s.dtype) for s in srcs]
        + [pltpu.HBM(shp, s.dtype) for shp, s in zip(land_shapes, srcs)]
        + [jax.ShapeDtypeStruct((SUBLANES, LANES), F32)],
        input_output_aliases={i: 2 * na + i for i in range(2 * na)},
        compiler_params=pltpu.CompilerParams(has_side_effects=EFFECT),
    )(*[pltpu.with_memory_space_constraint(a, pltpu.HBM) for a in list(srcs) + lands])
    sems = [(outs[2 * a], outs[2 * a + 1]) for a in range(na)]
    thrus = outs[2 * na:3 * na]
    lands = outs[3 * na:4 * na]
    return sems, thrus, lands, outs[-1]


def _split_wait(sems, thrus, lands, kinds, src_slice, dst_slice, after, name, own=False):
    na = len(thrus)
    after = list(after) if isinstance(after, (list, tuple)) else [after]

    def body(*refs):
        src_refs, land_refs = refs[:na], refs[na:2 * na]
        sem_refs = refs[2 * na:4 * na]
        for a in range(na):
            for cp in _peer_copies(kinds[a], src_refs[a], land_refs[a], sem_refs[2 * a], sem_refs[2 * a + 1],
                                   src_slice, dst_slice):
                cp.wait_send()
                cp.wait_recv()
            if own:
                _own_copy(kinds[a], src_refs[a], land_refs[a], sem_refs[2 * a], src_slice, dst_slice).wait()

    outs = _call(
        body, name=name,
        in_specs=[HBM] * (2 * na) + [SEM] * (2 * na) + [ANY] * len(after),
        out_specs=[HBM] * (2 * na),
        out_shape=[pltpu.HBM(a.shape, a.dtype) for a in list(thrus) + list(lands)],
        input_output_aliases={i: i for i in range(2 * na)},
        compiler_params=pltpu.CompilerParams(has_side_effects=EFFECT),
    )(*thrus, *lands, *[s for pair in sems for s in pair], *after)
    return outs[na:]


def _shard_dims(c_in, r_kv, r_out):
    def sl(j, ref, p):
        if j == 0:
            return ref.at[:, pl.ds(pl.multiple_of(p * c_in, LANES), c_in)]
        r = r_kv if j == 1 else r_out
        return ref.at[pl.ds(pl.multiple_of(p * r, 2 * SUBLANES), r), :]
    return sl


def _adamw_math(w, g, m, v):
    m = ADAM_B1 * m + (1.0 - ADAM_B1) * g
    v = ADAM_B2 * v + (1.0 - ADAM_B2) * (g * g)
    m_hat = m / (1.0 - ADAM_B1 ** ADAM_STEP)
    v_hat = v / (1.0 - ADAM_B2 ** ADAM_STEP)
    delta = -ADAM_LR * (m_hat / (jnp.sqrt(v_hat) + ADAM_EPS) + ADAM_WD * w)
    return delta, m, v


def _reduce_adamw(parts, w, m, v, name):
    rows, cols = w.shape
    tr = rows
    for cand in (512, 256, 128, 64, 32, 16):
        if rows % cand == 0 and rows > cand:
            tr = cand
            break

    def body(p_ref, w_ref, m_ref, v_ref, g_out, d_out, m_out, v_out):
        g = p_ref[0].astype(F32)
        for s in range(1, N_DEV):
            g = g + p_ref[s].astype(F32)
        g_out[...] = g
        d_out[...], m_out[...], v_out[...] = _adamw_math(w_ref[...], g, m_ref[...], v_ref[...])

    blk = pl.BlockSpec((tr, cols), lambda i: (i, 0))
    return _call(
        body, name=name, grid=(rows // tr,),
        in_specs=[pl.BlockSpec((N_DEV, tr, cols), lambda i: (0, i, 0)), blk, blk, blk],
        out_specs=[blk] * 4,
        out_shape=[jax.ShapeDtypeStruct((rows, cols), F32)] * 4,
        compiler_params=_cparams("parallel"),
    )(parts, w, m, v)


def _reduce_adamw_layers(lands, owns, w, m, v, name, first=0, prev=None):
    depth, rows, cols = w.shape
    nl = len(lands)
    tr = rows
    for cand in (256, 192, 128):
        if rows % cand == 0:
            tr = cand
            break

    kept = [] if prev is None else list(prev)

    def body(*refs):
        land_refs, own_refs = refs[:nl], refs[nl:2 * nl]
        w_ref, m_ref, v_ref = refs[2 * nl:2 * nl + 3]
        g_out, d_out, m_out, v_out = refs[2 * nl + 3 + len(kept):]
        layer = pl.program_id(0)
        for a in range(nl):
            @pl.when(layer == a)
            def _(a=a):
                g = own_refs[a][...].astype(F32)
                for k in range(N_PEER):
                    g = g + land_refs[a][k].astype(F32)
                g_out[...] = g
                d_out[...], m_out[...], v_out[...] = _adamw_math(w_ref[...], g, m_ref[...], v_ref[...])

    def lmap(a):
        return lambda l, i: (0, jnp.where(l == a, i, 0), 0)

    def omap(a):
        return lambda l, i: (jnp.where(l == a, i, 0), 0)

    blk = pl.BlockSpec((None, tr, cols), lambda l, i: (l + first, i, 0))
    n_in = 2 * nl + 3
    return _call(
        body, name=name, grid=(nl, rows // tr),
        in_specs=[pl.BlockSpec((N_PEER, tr, cols), lmap(a)) for a in range(nl)]
        + [pl.BlockSpec((tr, cols), omap(a)) for a in range(nl)] + [blk, blk, blk] + [ANY] * len(kept),
        out_specs=[blk] * 4,
        out_shape=[jax.ShapeDtypeStruct((depth, rows, cols), F32)] * 4,
        input_output_aliases={n_in + i: i for i in range(len(kept))},
        compiler_params=_cparams("arbitrary", "arbitrary"),
    )(*lands, *owns, w, m, v, *kept)


SM_G, SM_B, SM_CONV, SM_REL = 0, 4, 8, 16
SM_ROWS = SM_REL + 2 * N_HEADS
REL_W = 384


def _pack_small(d_rel, d_conv, dg, db):
    buf = jnp.zeros((SM_ROWS, D_MODEL), F32)
    buf = buf.at[SM_G:SM_G + DEPTH].set(dg)
    buf = buf.at[SM_B:SM_B + DEPTH].set(db)
    buf = buf.at[SM_CONV:SM_CONV + 6].set(d_conv.reshape(6, E_MIX))
    buf = buf.at[SM_REL:, :N_REL].set(d_rel.reshape(2 * N_HEADS, N_REL))
    return buf


def kernel(x, mem, w_in, w_mem_kv, w_out, rel_bias, conv_w, ln_g, ln_b, loss_target, m_w_in, m_w_mem_kv, m_w_out, m_rel_bias, m_conv_w, m_ln_g, m_ln_b, v_w_in, v_w_mem_kv, v_w_out, v_rel_bias, v_conv_w, v_ln_g, v_ln_b):
    me = _lin(_me())
    c_in, r_kv, r_out, c_conv = w_in.shape[2], w_mem_kv.shape[1], w_out.shape[1], conv_w.shape[2]

    shard = _shard_dims(c_in, r_kv, r_out)
    own_start = lambda j: (0, me * c_in) if j == 0 else (me * (r_kv if j == 1 else r_out), 0)

    w_sh = [w_in.astype(BF16), w_mem_kv.astype(BF16), w_out.astype(BF16)]
    full_shapes = [(D_MODEL, N_DEV * c_in), (N_DEV * r_kv, w_mem_kv.shape[2]), (N_DEV * r_out, D_MODEL)]
    conv_kind = N_KIND
    conv_tile = jnp.pad(conv_w.reshape(6, c_conv), ((0, SUBLANES - 6), (0, 0)))
    ag_src = lambda j, ref, p: ref
    ag_dst = lambda j, ref, me_, k: ref.at[me_] if j == conv_kind else shard(j, ref, me_)
    kinds = list(range(N_KIND))
    ag_sems, ag_thrus, ag_lands, _ = _split_start(
        [conv_tile] + [w_sh[j][layer] for layer in range(DEPTH) for j in kinds], [conv_kind] + kinds * DEPTH,
        [(N_DEV,) + conv_tile.shape] + full_shapes * DEPTH, ag_src, ag_dst, "ag_start", own=True)
    conv_landed = []

    def get_weights(layer, x_layer):
        lo = 1 + layer * N_KIND

        def wait(idx, js, after, name):
            return _split_wait([ag_sems[a] for a in idx], [ag_thrus[a] for a in idx],
                               [ag_lands[a] for a in idx], js, ag_src, ag_dst, after, name, own=True)

        if layer == 0:
            conv_land, w_in_l = wait([0, lo], [conv_kind, 0], x_layer, "ag_wait_in_0")
            conv_landed.append(conv_land)
        else:
            w_in_l, = wait([lo], [0], x_layer, "ag_wait_in_%d" % layer)
        return w_in_l, lambda h: wait([lo + 1, lo + 2], [1, 2], h, "ag_wait_kv_out_%d" % layer)

    def conv_full():
        return jnp.transpose(conv_landed[0][:, :6], (1, 0, 2)).reshape(2, 3, N_DEV * c_conv)

    rs_src = shard
    rs_dst = lambda j, ref, me_, k: ref.at[k - 1]
    rs_shapes = [(N_PEER, D_MODEL, c_in), (N_PEER, r_kv, w_mem_kv.shape[2]), (N_PEER, r_out, D_MODEL)]
    own_sizes = [(D_MODEL, c_in), (r_kv, w_mem_kv.shape[2]), (r_out, D_MODEL)]
    pending = {}

    held = {}

    def put_grads(layer, js, arrays):
        if layer > 0 and js != [0]:
            held[layer] = (js, arrays)
            return None
        if layer > 0:
            js, arrays = held[layer][0] + js, held[layer][1] + arrays
        owns = [lax.dynamic_slice(a, own_start(j), own_sizes[j]) for j, a in zip(js, arrays)]
        sems, thrus, lands, token = _split_start(
            arrays, js, [rs_shapes[j] for j in js], rs_src, rs_dst,
            "rs_start_%d_%s" % (layer, "".join(str(j) for j in js)))
        entry = pending.setdefault(layer, ([], [], [], [], []))
        for lst, new in zip(entry, (js, sems, thrus, lands, owns)):
            lst.extend(new)
        return token


    loss, grad_x, d_rel, d_conv, dg, db = _local_step(
        x[0], mem[0], get_weights, put_grads, rel_bias, conv_full, ln_g, ln_b, loss_target[0])

    p_small = _gather_to_all(_pack_small(d_rel, d_conv, dg, db), "gather_small_grads")

    rs_lands, rs_owns = {}, {}

    def rs_wait(layer, want, after, name):
        js, sems, thrus, lands, owns = pending[layer]
        pos = [js.index(j) for j in want]
        got = _split_wait([sems[p] for p in pos], [thrus[p] for p in pos], [lands[p] for p in pos],
                          want, rs_src, rs_dst, after, name)
        for j, p, land in zip(want, pos, got):
            rs_lands[layer, j], rs_owns[layer, j] = land, owns[p]

    for layer in range(1, DEPTH):
        rs_wait(layer, kinds, grad_x, "rs_wait_%d" % layer)
    rs_wait(0, [1, 2], grad_x, "rs_wait_0_kv_out")

    def big(j, w, m, v, name, layers, prev=None):
        return _reduce_adamw_layers([rs_lands[layer, j] for layer in layers], [rs_owns[layer, j] for layer in layers],
                                    w, m, v, name, first=layers[0], prev=prev)

    every = list(range(DEPTH))
    g_kv, d_kv, nm_kv, nv_kv = big(1, w_mem_kv, m_w_mem_kv, v_w_mem_kv, "adamw_w_kv", every)
    g_out, d_out, nm_out, nv_out = big(2, w_out, m_w_out, v_w_out, "adamw_w_out", every)
    later = big(0, w_in, m_w_in, v_w_in, "adamw_w_in_later_layers", every[1:])

    def pack_state(rel, conv, g, b):
        conv_full = jnp.zeros((2, 3, E_MIX), F32)
        conv_full = lax.dynamic_update_slice(conv_full, conv, (0, 0, me * c_conv))
        return _pack_small(rel, conv_full, g, b)

    sm_w = pack_state(rel_bias, conv_w, ln_g, ln_b)
    sm_m = pack_state(m_rel_bias, m_conv_w, m_ln_g, m_ln_b)
    sm_v = pack_state(v_rel_bias, v_conv_w, v_ln_g, v_ln_b)
    sm_outs = _reduce_adamw(p_small, sm_w, sm_m, sm_v, "adamw_small")

    rs_wait(0, [0], [later[0], sm_outs[0], g_kv, g_out], "rs_wait_0_in")
    g_in, d_in, nm_in, nv_in = big(0, w_in, m_w_in, v_w_in, "adamw_w_in_layer_0", [0], prev=later)

    def unpack(buf):
        rel = buf[SM_REL:, :N_REL].reshape(2, N_HEADS, N_REL)
        conv = lax.dynamic_slice(buf[SM_CONV:SM_CONV + 6].reshape(2, 3, E_MIX), (0, 0, me * c_conv), (2, 3, c_conv))
        return rel, conv, buf[SM_G:SM_G + DEPTH], buf[SM_B:SM_B + DEPTH]

    g_sm, d_sm, nm_sm, nv_sm = [unpack(b) for b in sm_outs]

    loss = lax.psum(loss[0, 0], ("x", "y", "c"))
    return (loss, grad_x[None],
            g_in, g_kv, g_out, *g_sm,
            d_in, d_kv, d_out, *d_sm,
            nm_in, nm_kv, nm_out, *nm_sm,
            nv_in, nv_kv, nv_out, *nv_sm)
```

```python
import functools
import math

import jax
import jax.numpy as jnp
from jax import lax
from jax.experimental import pallas as pl
from jax.experimental.pallas import tpu as pltpu

F32 = jnp.float32
BF16 = jnp.bfloat16
MXU_DTYPE = jnp.bfloat16

N_DEV = 8
D_MODEL = 1024
DEPTH = 4
CHUNK = 64
N_PREV = 8
N_HEADS = 16
HEAD_DIM = 64
E_MIX = 1024
REL_CLIP = 128
N_REL = 2 * REL_CLIP + 1
N_REL_PAD = 384
N_MEM = 256
MEM_HEADS = 4
MEM_HEAD_DIM = 128
E_MEM = 512
E_BRANCH = E_MIX + E_MEM
N_IN = 3 * E_MIX + E_MEM + E_BRANCH
DN_ALPHA = (2.0 * DEPTH) ** 0.25
LN_EPS = 1e-5
NEG = -1e30

ADAM_LR = 0.001
ADAM_B1 = 0.9
ADAM_B2 = 0.999
ADAM_EPS = 1e-08
ADAM_WD = 0.01
ADAM_STEP = 10

LANES = 128
SUBLANES = 8
VMEM_LIMIT = 56 * 1024 * 1024

TQ = 4 * CHUNK
TKEYS = 3 * TQ
ROLL_W = 1024
TS = 256
QM_BLK = 3 * E_MIX // E_MEM
Z_BLK = QM_BLK + 1


def _call(body, **kw):
    return pl.pallas_call(body, **kw)


def _cparams(*sem):
    return pltpu.CompilerParams(dimension_semantics=sem, vmem_limit_bytes=VMEM_LIMIT)


def _dot(a, b):
    return jnp.dot(a, b, preferred_element_type=F32)


def _dot_nt(a, b):
    return lax.dot_general(a, b, (((1,), (1,)), ((), ())), preferred_element_type=F32)


def _dot_tn(a, b):
    return lax.dot_general(a, b, (((0,), (0,)), ((), ())), preferred_element_type=F32)


def _inproj(x, w):
    s, d = x.shape
    n = w.shape[1]
    tm = 512
    tn = 1024

    def body(x_ref, w_ref, o_ref, xt_ref):
        xb = x_ref[...].astype(xt_ref.dtype)
        xt_ref[...] = xb.T
        for j in range(n // tn):
            o_ref[:, j * tn:(j + 1) * tn] = _dot(xb, w_ref[:, j * tn:(j + 1) * tn]).astype(o_ref.dtype)

    return _call(
        body, name="inproj", grid=(s // tm,),
        in_specs=[pl.BlockSpec((tm, d), lambda i: (i, 0)),
                  pl.BlockSpec((d, n), lambda i: (0, 0), pipeline_mode=pl.Buffered(1))],
        out_specs=[pl.BlockSpec((tm, n), lambda i: (i, 0)),
                   pl.BlockSpec((d, tm), lambda i: (0, i))],
        out_shape=[jax.ShapeDtypeStruct((s, n), BF16), jax.ShapeDtypeStruct((d, s), BF16)],
        compiler_params=_cparams("parallel"),
    )(x, w)


def _small_matmul(a, b, trans_a, out_dtype, name):
    m = a.shape[1] if trans_a else a.shape[0]
    n = b.shape[1]

    def body(a_ref, b_ref, o_ref):
        av = a_ref[...].astype(MXU_DTYPE)
        bv = b_ref[...].astype(MXU_DTYPE)
        r = _dot_tn(av, bv) if trans_a else _dot(av, bv)
        o_ref[...] = r.astype(out_dtype)

    return _call(
        body, name=name,
        in_specs=[pl.BlockSpec(memory_space=pltpu.VMEM)] * 2,
        out_specs=pl.BlockSpec(memory_space=pltpu.VMEM),
        out_shape=jax.ShapeDtypeStruct((m, n), out_dtype),
        compiler_params=pltpu.CompilerParams(vmem_limit_bytes=VMEM_LIMIT),
    )(a, b)


def _piece_blocks(pieces, blk):
    offs, nbs, o = [], [], 0
    for p in pieces:
        nb = p.shape[1] // blk
        offs.append(o)
        nbs.append(nb)
        o += nb
    return offs, nbs, o


def _dx_matmul(pieces, w, addend, token=None):
    s = pieces[0].shape[0]
    d, n_in = w.shape
    tm = 512
    tw = 1024
    np_ = len(pieces)
    extra = [] if token is None else [token]

    def body(*refs):
        a_refs = refs[:np_]
        w_ref, add_ref = refs[np_:np_ + 2]
        o_ref, wt = refs[-2:]

        @pl.when(pl.program_id(0) == 0)
        def _():
            for j in range(n_in // tw):
                wt[j * tw:(j + 1) * tw, :] = w_ref[:, j * tw:(j + 1) * tw].T

        a = jnp.concatenate([r[...] for r in a_refs], axis=1)
        o_ref[...] = add_ref[...] + _dot(a, wt[...])

    in_specs = [pl.BlockSpec((tm, p.shape[1]), lambda i: (i, 0)) for p in pieces]
    in_specs += [pl.BlockSpec((d, n_in), lambda i: (0, 0), pipeline_mode=pl.Buffered(1)),
                 pl.BlockSpec((tm, d), lambda i: (i, 0))]
    in_specs += [pl.BlockSpec((SUBLANES, LANES), lambda i: (0, 0)) for _ in extra]
    return _call(
        body, name="dx_matmul", grid=(s // tm,),
        in_specs=in_specs,
        out_specs=pl.BlockSpec((tm, d), lambda i: (i, 0)),
        out_shape=jax.ShapeDtypeStruct((s, d), F32),
        scratch_shapes=[pltpu.VMEM((n_in, d), w.dtype)],
        compiler_params=_cparams("arbitrary"),
    )(*pieces, w, addend, *extra)


def _dw_matmul(xt, pieces):
    d, s = xt.shape
    tn = 256
    offs, nbs, nj = _piece_blocks(pieces, tn)
    np_ = len(pieces)

    def body(*refs):
        x_ref = refs[0]
        b_refs = refs[1:1 + np_]
        o_ref = refs[1 + np_]
        j = pl.program_id(0)
        for p in range(np_):
            @pl.when((j >= offs[p]) & (j < offs[p] + nbs[p]))
            def _(p=p):
                o_ref[...] = _dot(x_ref[...], b_refs[p][...]).astype(o_ref.dtype)

    def bmap(p):
        return lambda j: (0, jnp.clip(j - offs[p], 0, nbs[p] - 1))

    in_specs = [pl.BlockSpec((d, s), lambda j: (0, 0), pipeline_mode=pl.Buffered(1))]
    in_specs += [pl.BlockSpec((s, tn), bmap(p)) for p in range(np_)]
    return _call(
        body, name="dw_matmul", grid=(nj,),
        in_specs=in_specs,
        out_specs=pl.BlockSpec((d, tn), lambda j: (0, j)),
        out_shape=jax.ShapeDtypeStruct((d, nj * tn), BF16),
        compiler_params=_cparams("parallel"),
    )(xt, *pieces)


def _rel_onehot():
    j = lax.broadcasted_iota(jnp.int32, (N_REL_PAD, ROLL_W), 1)
    kk = lax.broadcasted_iota(jnp.int32, (N_REL_PAD, ROLL_W), 0)
    dd = jnp.where(j < TKEYS, j, j - ROLL_W)
    idx = jnp.clip(N_PREV * CHUNK - dd, -REL_CLIP, REL_CLIP) + REL_CLIP
    return jnp.where(idx == kk, 1.0, 0.0).astype(F32)


def _band_mask():
    r = lax.broadcasted_iota(jnp.int32, (TQ, TKEYS), 0) // CHUNK
    m = lax.broadcasted_iota(jnp.int32, (TQ, TKEYS), 1) // CHUNK
    return (m >= r) & (m <= r + N_PREV)


def _tile_bias(table_pad):
    def body(t_ref, o_ref):
        g = jnp.dot(t_ref[...], _rel_onehot(), preferred_element_type=F32,
                    precision=lax.Precision.HIGHEST)
        band = _band_mask()
        for h in range(N_HEADS):
            gh = jnp.broadcast_to(g[h:h + 1, :], (TQ, ROLL_W))
            rolled = pltpu.roll(gh, 0, 1, stride=1, stride_axis=0)
            o_ref[h] = jnp.where(band, rolled[:, :TKEYS], NEG)

    return _call(
        body, name="tile_bias",
        in_specs=[pl.BlockSpec(memory_space=pltpu.VMEM)],
        out_specs=pl.BlockSpec(memory_space=pltpu.VMEM),
        out_shape=jax.ShapeDtypeStruct((N_HEADS, TQ, TKEYS), F32),
        compiler_params=pltpu.CompilerParams(vmem_limit_bytes=VMEM_LIMIT),
    )(table_pad)


def _tile_bias_bwd(dtb):
    def body(d_ref, o_ref, g_ref):
        zpad = jnp.zeros((TQ, ROLL_W - TKEYS), F32)
        rr = lax.broadcasted_iota(jnp.int32, (TQ, TQ), 0)
        cc = lax.broadcasted_iota(jnp.int32, (TQ, TQ), 1)
        flip = jnp.where(rr + cc == TQ - 1, 1.0, 0.0).astype(F32)
        for h in range(N_HEADS):
            xh = jnp.concatenate([d_ref[h], zpad], axis=1)
            xf = jnp.dot(flip, xh, preferred_element_type=F32, precision=lax.Precision.HIGHEST)
            rolled = pltpu.roll(xf, 0, 1, stride=1, stride_axis=0)
            g_ref[h:h + 1, :] = jnp.sum(rolled, axis=0, keepdims=True)
        g = pltpu.roll(g_ref[...], ROLL_W - (TQ - 1), 1)
        o_ref[...] = lax.dot_general(g, _rel_onehot(), (((1,), (1,)), ((), ())),
                                     preferred_element_type=F32, precision=lax.Precision.HIGHEST)

    return _call(
        body, name="tile_bias_bwd",
        in_specs=[pl.BlockSpec(memory_space=pltpu.VMEM)],
        out_specs=pl.BlockSpec(memory_space=pltpu.VMEM),
        out_shape=jax.ShapeDtypeStruct((N_HEADS, N_REL_PAD), F32),
        scratch_shapes=[pltpu.VMEM((N_HEADS, ROLL_W), F32)],
        compiler_params=pltpu.CompilerParams(vmem_limit_bytes=VMEM_LIMIT),
    )(dtb)


HB = 4
HBW = HB * HEAD_DIM
ATTN_SCALE = 0.125
assert ATTN_SCALE == 1.0 / math.sqrt(HEAD_DIM)


def _head_masks():
    lane = lax.broadcasted_iota(jnp.int32, (1, HBW), 1) // HEAD_DIM
    return [lane == hh for hh in range(HB)]


def _select_heads(masks, parts):
    out = parts[-1]
    for hh in range(HB - 2, -1, -1):
        out = jnp.where(masks[hh], parts[hh], out)
    return out


def _attn_probs(qm, kcat, tb, valid):
    s = _dot_nt(qm, kcat) + tb
    if valid is not None:
        s = jnp.where(valid, s, NEG)
    m = jnp.max(s, axis=-1, keepdims=True)
    e = jnp.exp(s - m)
    return e * (1.0 / jnp.sum(e, axis=-1, keepdims=True))


def _key_valid(i):
    col = lax.broadcasted_iota(jnp.int32, (TQ, TKEYS), 1)
    return col >= jnp.maximum(2 - i, 0) * TQ


def _kv_specs(col0, nt):
    def spec(back):
        return pl.BlockSpec((TQ, HBW), lambda hp, i: (jnp.clip(i - back, 0, nt - 1), col0 + hp))
    return [spec(2), spec(1), spec(0)]


def _attn_fwd(h, tb):
    s = h.shape[0]
    nt = s // TQ
    nhp = N_HEADS // HB

    def body(q_ref, k0, k1, k2, v0, v1, v2, tb_ref, o_ref, p_ref):
        i = pl.program_id(1)

        def tile(valid):
            masks = _head_masks()
            qs = q_ref[...].astype(MXU_DTYPE) * ATTN_SCALE
            kcat = jnp.concatenate([k0[...], k1[...], k2[...]], axis=0).astype(MXU_DTYPE)
            vcat = jnp.concatenate([v0[...], v1[...], v2[...]], axis=0).astype(MXU_DTYPE)
            outs = []
            for hh in range(HB):
                qm = jnp.where(masks[hh], qs, jnp.zeros_like(qs))
                pb = _attn_probs(qm, kcat, tb_ref[hh], valid).astype(MXU_DTYPE)
                p_ref[hh] = pb.astype(p_ref.dtype)
                outs.append(_dot(pb, vcat))
            o_ref[...] = _select_heads(masks, outs).astype(o_ref.dtype)

        @pl.when(i < 2)
        def _():
            tile(_key_valid(i))

        @pl.when(i >= 2)
        def _():
            tile(None)

    in_specs = [pl.BlockSpec((TQ, HBW), lambda hp, i: (i, hp))]
    in_specs += _kv_specs(nhp, nt) + _kv_specs(2 * nhp, nt)
    in_specs += [pl.BlockSpec((HB, TQ, TKEYS), lambda hp, i: (hp, 0, 0))]
    return _call(
        body, name="attn_fwd", grid=(nhp, nt),
        in_specs=in_specs,
        out_specs=[pl.BlockSpec((TQ, HBW), lambda hp, i: (i, hp)),
                   pl.BlockSpec((HB, TQ, TKEYS), lambda hp, i: (hp, i, 0))],
        out_shape=[jax.ShapeDtypeStruct((s, E_MIX), BF16), jax.ShapeDtypeStruct((N_HEADS, s, TKEYS), BF16)],
        compiler_params=_cparams("parallel", "parallel"),
    )(h, h, h, h, h, h, h, tb)


def _attn_bwd(h, probs, d_mix, token=None):
    s = h.shape[0]
    nt = s // TQ
    nhp = N_HEADS // HB
    extra = [] if token is None else [token]

    def body(q_ref, k0, k1, k2, v0, v1, v2, p_ref, do_ref, *rest):
        dq_ref, dk_ref, dv_ref, dtb_ref, dk_acc, dv_acc = rest[len(extra):]
        i = pl.program_id(1)

        @pl.when(i == 0)
        def _():
            dk_acc[...] = jnp.zeros_like(dk_acc)
            dv_acc[...] = jnp.zeros_like(dv_acc)
            dtb_ref[...] = jnp.zeros_like(dtb_ref)

        @pl.when((i > 0) & (i < nt))
        def _():
            dk_acc[i % 3] = jnp.zeros((TQ, HBW), F32)
            dv_acc[i % 3] = jnp.zeros((TQ, HBW), F32)

        @pl.when(i < nt)
        def _():
            masks = _head_masks()
            qs = q_ref[...].astype(MXU_DTYPE) * ATTN_SCALE
            do2 = do_ref[...].astype(MXU_DTYPE)
            kcat = jnp.concatenate([k0[...], k1[...], k2[...]], axis=0).astype(MXU_DTYPE)
            vcat = jnp.concatenate([v0[...], v1[...], v2[...]], axis=0).astype(MXU_DTYPE)
            ks = kcat * ATTN_SCALE
            dqs, dks, dvs = [], [], []
            for hh in range(HB):
                dom = jnp.where(masks[hh], do2, jnp.zeros_like(do2))
                pb = p_ref[hh]
                p = pb.astype(F32)
                dp = _dot_nt(dom, vcat)
                ds = p * (dp - jnp.sum(p * dp, axis=-1, keepdims=True))
                dtb_ref[hh] += ds
                dsb = ds.astype(MXU_DTYPE)
                dqs.append(_dot(dsb, ks))
                dks.append(_dot_tn(dsb, qs))
                dvs.append(_dot_tn(pb.astype(MXU_DTYPE), do2))
            dq_ref[...] = _select_heads(masks, dqs).astype(dq_ref.dtype)
            dkc = _select_heads(masks, dks)
            dvc = _select_heads(masks, dvs)
            for jj in range(3):
                slot = (i + 1 + jj) % 3
                dk_acc[slot] += dkc[jj * TQ:(jj + 1) * TQ]
                dv_acc[slot] += dvc[jj * TQ:(jj + 1) * TQ]

        @pl.when(i >= 2)
        def _():
            slot = (i - 2) % 3
            dk_ref[...] = dk_acc[slot].astype(dk_ref.dtype)
            dv_ref[...] = dv_acc[slot].astype(dv_ref.dtype)

    qmap = lambda hp, i: (jnp.minimum(i, nt - 1), hp)
    kvout = lambda hp, i: (jnp.maximum(i - 2, 0), hp)
    in_specs = [pl.BlockSpec((TQ, HBW), qmap)]
    in_specs += _kv_specs(nhp, nt) + _kv_specs(2 * nhp, nt)
    in_specs += [pl.BlockSpec((HB, TQ, TKEYS), lambda hp, i: (hp, jnp.minimum(i, nt - 1), 0)),
                 pl.BlockSpec((TQ, HBW), qmap)]
    in_specs += [pl.BlockSpec((SUBLANES, LANES), lambda hp, i: (0, 0)) for _ in extra]
    blk = (TQ, HBW)
    return _call(
        body, name="attn_bwd", grid=(nhp, nt + 2),
        in_specs=in_specs,
        out_specs=[pl.BlockSpec(blk, qmap), pl.BlockSpec(blk, kvout), pl.BlockSpec(blk, kvout),
                   pl.BlockSpec((HB, TQ, TKEYS), lambda hp, i: (hp, 0, 0))],
        out_shape=[jax.ShapeDtypeStruct((s, E_MIX), BF16)] * 3
        + [jax.ShapeDtypeStruct((N_HEADS, TQ, TKEYS), F32)],
        scratch_shapes=[pltpu.VMEM((3, TQ, HBW), F32)] * 2,
        compiler_params=_cparams("parallel", "arbitrary"),
    )(h, h, h, h, h, h, h, probs, d_mix, *extra)


CONV_TS = 512
HALO = 2 * SUBLANES


def _shift_down(prev, cur, k):
    rolled = pltpu.roll(cur, k, 0)
    row = lax.broadcasted_iota(jnp.int32, (HALO, cur.shape[1]), 0)
    top = jnp.where(row < k, pltpu.roll(prev, k, 0), rolled[:HALO])
    return jnp.concatenate([top, rolled[HALO:]], axis=0)


def _shift_up(cur, nxt, k):
    ts = cur.shape[0]
    rolled = pltpu.roll(cur, ts - k, 0)
    row = lax.broadcasted_iota(jnp.int32, (HALO, cur.shape[1]), 0)
    bottom = jnp.where(row >= HALO - k, pltpu.roll(nxt, HALO - k, 0), rolled[ts - HALO:])
    return jnp.concatenate([rolled[:ts - HALO], bottom], axis=0)


def _conv_specs(ts, nb):
    tile = lambda c: pl.BlockSpec((ts, E_MIX), lambda i: (i, c))
    prev = lambda c: pl.BlockSpec((HALO, E_MIX), lambda i: (jnp.maximum(i * (ts // HALO) - 1, 0), c))
    return tile, prev


def _conv_fwd(h, w8):
    s = h.shape[0]
    ts = CONV_TS
    nb = s // ts
    tile, prev = _conv_specs(ts, nb)

    def body(bg, cg, u, cgp, up, w_ref, o_ref):
        i = pl.program_id(0)
        a = cg[...].astype(F32) * u[...].astype(F32)
        ap = jnp.where(i > 0, cgp[...].astype(F32) * up[...].astype(F32), 0.0)
        w = w_ref[...]
        conv = w[0:1] * _shift_down(ap, a, 2) + w[1:2] * _shift_down(ap, a, 1) + w[2:3] * a
        o_ref[...] = (bg[...].astype(F32) * conv).astype(o_ref.dtype)

    return _call(
        body, name="conv_fwd", grid=(nb,),
        in_specs=[tile(0), tile(1), tile(2), prev(1), prev(2),
                  pl.BlockSpec((SUBLANES, E_MIX), lambda i: (0, 0))],
        out_specs=pl.BlockSpec((ts, E_MIX), lambda i: (i, 0)),
        out_shape=jax.ShapeDtypeStruct((s, E_MIX), BF16),
        compiler_params=_cparams("parallel"),
    )(h, h, h, h, h, w8)


def _conv_bwd(h, w8, d_mix):
    s = h.shape[0]
    ts = CONV_TS
    nb = s // ts
    tile, prev = _conv_specs(ts, nb)
    nrow = s // HALO
    nxt = lambda c: pl.BlockSpec((HALO, E_MIX), lambda i: (jnp.minimum((i + 1) * (ts // HALO), nrow - 1), c))

    def body(bg, cg, u, cgp, up, bgn, dmix, dmixn, w_ref, dbg_ref, dcg_ref, du_ref, dw_ref):
        i = pl.program_id(0)

        @pl.when(i == 0)
        def _():
            dw_ref[...] = jnp.zeros_like(dw_ref)

        cgv, uv = cg[...].astype(F32), u[...].astype(F32)
        a = cgv * uv
        ap = jnp.where(i > 0, cgp[...].astype(F32) * up[...].astype(F32), 0.0)
        a1 = _shift_down(ap, a, 1)
        a2 = _shift_down(ap, a, 2)
        w = w_ref[...]
        conv = w[0:1] * a2 + w[1:2] * a1 + w[2:3] * a
        dm = dmix[...].astype(F32)
        dbg_ref[...] = (dm * conv).astype(dbg_ref.dtype)
        dc = dm * bg[...].astype(F32)
        dcn = jnp.where(i < nb - 1, dmixn[...].astype(F32) * bgn[...].astype(F32), 0.0)
        da = w[2:3] * dc + w[1:2] * _shift_up(dc, dcn, 1) + w[0:1] * _shift_up(dc, dcn, 2)
        dcg_ref[...] = (da * uv).astype(dcg_ref.dtype)
        du_ref[...] = (da * cgv).astype(du_ref.dtype)
        dw_ref[0:1, :] += jnp.sum(dc * a2, axis=0, keepdims=True)
        dw_ref[1:2, :] += jnp.sum(dc * a1, axis=0, keepdims=True)
        dw_ref[2:3, :] += jnp.sum(dc * a, axis=0, keepdims=True)

    full = lambda: pl.BlockSpec((ts, E_MIX), lambda i: (i, 0))
    return _call(
        body, name="conv_bwd", grid=(nb,),
        in_specs=[tile(0), tile(1), tile(2), prev(1), prev(2), nxt(0),
                  full(), pl.BlockSpec((HALO, E_MIX), lambda i: (jnp.minimum((i + 1) * (ts // HALO), nrow - 1), 0)),
                  pl.BlockSpec((SUBLANES, E_MIX), lambda i: (0, 0))],
        out_specs=[full(), full(), full(), pl.BlockSpec((SUBLANES, E_MIX), lambda i: (0, 0))],
        out_shape=[jax.ShapeDtypeStruct((s, E_MIX), BF16)] * 3
        + [jax.ShapeDtypeStruct((SUBLANES, E_MIX), F32)],
        compiler_params=_cparams("arbitrary"),
    )(h, h, h, h, h, h, d_mix, d_mix, w8)


def _mem_probs(qh, kh):
    s = _dot_nt(qh, kh) / math.sqrt(MEM_HEAD_DIM)
    m = jnp.max(s, axis=-1, keepdims=True)
    e = jnp.exp(s - m)
    return e / jnp.sum(e, axis=-1, keepdims=True)


def _sigmoid(z):
    return 1.0 / (1.0 + jnp.exp(-z))


def _layer_out_fwd(x, h, mix, kv, w_out, g, b, target=None):
    s, d = x.shape
    ts = 2 * TS
    last = target is not None

    def body(x_ref, mix_ref, qm_ref, z0, z1, z2, kv_ref, wo_ref, g_ref, b_ref, *rest):
        xn_ref, r_ref, mem_ref = rest[last:last + 3]
        qm = qm_ref[...].astype(MXU_DTYPE)
        kvb = kv_ref[...].astype(MXU_DTYPE)
        mems = []
        for hh in range(MEM_HEADS):
            lo = hh * MEM_HEAD_DIM
            p = _mem_probs(qm[:, lo:lo + MEM_HEAD_DIM], kvb[:, lo:lo + MEM_HEAD_DIM])
            mems.append(_dot(p.astype(MXU_DTYPE), kvb[:, E_MEM + lo:E_MEM + lo + MEM_HEAD_DIM]))
        mem = jnp.concatenate(mems, axis=1).astype(mem_ref.dtype)
        mem_ref[...] = mem
        mixv = mix_ref[...].astype(F32)
        half = E_MIX // 2
        parts = [mixv[:, :half], mixv[:, half:], mem.astype(F32)]
        out = jnp.zeros((ts, d), F32)
        for c, zr in enumerate((z0, z1, z2)):
            zv = zr[...].astype(F32)
            y = (parts[c] * (zv * _sigmoid(zv))).astype(MXU_DTYPE)
            out += _dot(y, wo_ref[c * half:(c + 1) * half, :])
        r = DN_ALPHA * x_ref[...] + out
        r_ref[...] = r
        mu = jnp.mean(r, axis=-1, keepdims=True)
        rc = r - mu
        var = jnp.mean(rc * rc, axis=-1, keepdims=True)
        xn = rc * lax.rsqrt(var + LN_EPS) * g_ref[...] + b_ref[...]
        if not last:
            xn_ref[...] = xn
        else:
            t_ref, l_ref = rest[0], rest[4]

            @pl.when(pl.program_id(0) == 0)
            def _():
                l_ref[...] = jnp.zeros_like(l_ref)

            e = xn - t_ref[...]
            xn_ref[...] = e * (1.0 / d)
            l_ref[...] += (0.5 / d) * jnp.sum(jnp.sum(e * e, axis=1, keepdims=True), axis=0, keepdims=True)

    row = lambda w, c: pl.BlockSpec((ts, w), lambda i: (i, c))
    const = lambda shp: pl.BlockSpec(shp, lambda i: (0, 0))
    tail = [target] if last else []
    return _call(
        body, name="layer_out_fwd_loss" if last else "layer_out_fwd", grid=(s // ts,),
        in_specs=[row(d, 0), row(E_MIX, 0), row(E_MEM, QM_BLK),
                  row(E_MEM, Z_BLK), row(E_MEM, Z_BLK + 1), row(E_MEM, Z_BLK + 2),
                  const((N_MEM, 2 * E_MEM)), const((E_BRANCH, d)), const((1, d)), const((1, d))]
        + [row(d, 0) for _ in tail],
        out_specs=[row(d, 0), row(d, 0), row(E_MEM, 0)] + [const((1, 1)) for _ in tail],
        out_shape=[jax.ShapeDtypeStruct((s, d), F32), jax.ShapeDtypeStruct((s, d), F32),
                   jax.ShapeDtypeStruct((s, E_MEM), BF16)] + [jax.ShapeDtypeStruct((1, 1), F32) for _ in tail],
        compiler_params=_cparams("arbitrary" if last else "parallel"),
    )(x, mix, h, h, h, h, kv, w_out, g, b, *tail)


def _layer_out_bwd(dxn, r, g, h, mix, mem, kv, w_out_t):
    s, d = r.shape
    ts = 2 * TS
    nb = s // ts
    half = E_MIX // 2
    inv = 1.0 / math.sqrt(MEM_HEAD_DIM)

    def body(dxn_ref, r_ref, g_ref, mix_ref, mem_ref, qm_ref, z0, z1, z2, kv_ref, wo_ref,
             dxr_ref, dmix_ref, dqz_ref, dwo_ref, dkv_ref, dg_ref, db_ref, dw_acc):
        i = pl.program_id(0)

        @pl.when(i == 0)
        def _():
            dw_acc[...] = jnp.zeros_like(dw_acc)
            dkv_ref[...] = jnp.zeros_like(dkv_ref)
            dg_ref[...] = jnp.zeros_like(dg_ref)
            db_ref[...] = jnp.zeros_like(db_ref)

        dxn_v = dxn_ref[...]
        rv = r_ref[...]
        mu = jnp.mean(rv, axis=-1, keepdims=True)
        rc = rv - mu
        var = jnp.mean(rc * rc, axis=-1, keepdims=True)
        rstd = lax.rsqrt(var + LN_EPS)
        xhat = rc * rstd
        dg_ref[...] += jnp.sum(dxn_v * xhat, axis=0, keepdims=True)
        db_ref[...] += jnp.sum(dxn_v, axis=0, keepdims=True)
        dxh = dxn_v * g_ref[...]
        m1 = jnp.mean(dxh, axis=-1, keepdims=True)
        m2 = jnp.mean(dxh * xhat, axis=-1, keepdims=True)
        dr = rstd * (dxh - m1 - xhat * m2)
        dxr_ref[...] = DN_ALPHA * dr
        dout = dr.astype(MXU_DTYPE)
        mixv = mix_ref[...].astype(F32)
        parts = [mixv[:, :half], mixv[:, half:], mem_ref[...].astype(F32)]
        dcs = []
        for c, zr in enumerate((z0, z1, z2)):
            lo = c * half
            zv = zr[...].astype(F32)
            sg = _sigmoid(zv)
            sl = zv * sg
            dy = _dot(dout, wo_ref[:, lo:lo + half])
            y = (parts[c] * sl).astype(MXU_DTYPE)
            dw_acc[lo:lo + half, :] += _dot_tn(y, dout)
            dcs.append(dy * sl)
            dqz_ref[:, E_MEM + lo:E_MEM + lo + half] = (
                dy * parts[c] * (sg * (1.0 + zv * (1.0 - sg)))).astype(dqz_ref.dtype)
        dmix_ref[...] = jnp.concatenate(dcs[:2], axis=1).astype(dmix_ref.dtype)

        qm = qm_ref[...].astype(MXU_DTYPE)
        kvb = kv_ref[...].astype(MXU_DTYPE)
        dmb = dcs[2].astype(MXU_DTYPE)
        for hh in range(MEM_HEADS):
            lo = hh * MEM_HEAD_DIM
            qh = qm[:, lo:lo + MEM_HEAD_DIM]
            kh = kvb[:, lo:lo + MEM_HEAD_DIM]
            vh = kvb[:, E_MEM + lo:E_MEM + lo + MEM_HEAD_DIM]
            dmh = dmb[:, lo:lo + MEM_HEAD_DIM]
            p = _mem_probs(qh, kh)
            dp = _dot_nt(dmh, vh)
            ds = p * (dp - jnp.sum(p * dp, axis=-1, keepdims=True))
            dsb = (ds * inv).astype(MXU_DTYPE)
            dqz_ref[:, lo:lo + MEM_HEAD_DIM] = _dot(dsb, kh).astype(dqz_ref.dtype)
            dkv_ref[:, lo:lo + MEM_HEAD_DIM] += _dot_tn(dsb, qh)
            dkv_ref[:, E_MEM + lo:E_MEM + lo + MEM_HEAD_DIM] += _dot_tn(p.astype(MXU_DTYPE), dmh)

        @pl.when(i == nb - 1)
        def _():
            dwo_ref[...] = dw_acc[...].astype(dwo_ref.dtype)

    row = lambda w, c: pl.BlockSpec((ts, w), lambda i: (i, c))
    const = lambda shp: pl.BlockSpec(shp, lambda i: (0, 0))
    once = lambda shp: pl.BlockSpec(shp, lambda i: (0, 0), pipeline_mode=pl.Buffered(1))
    return _call(
        body, name="layer_out_bwd", grid=(nb,),
        in_specs=[row(d, 0), row(d, 0), const((1, d)), row(E_MIX, 0), row(E_MEM, 0),
                  row(E_MEM, QM_BLK), row(E_MEM, Z_BLK), row(E_MEM, Z_BLK + 1), row(E_MEM, Z_BLK + 2),
                  once((N_MEM, 2 * E_MEM)), once((d, E_BRANCH))],
        out_specs=[row(d, 0), row(E_MIX, 0), row(E_MEM + E_BRANCH, 0),
                   const((E_BRANCH, d)), const((N_MEM, 2 * E_MEM)), const((1, d)), const((1, d))],
        out_shape=[jax.ShapeDtypeStruct((s, d), F32), jax.ShapeDtypeStruct((s, E_MIX), BF16),
                   jax.ShapeDtypeStruct((s, E_MEM + E_BRANCH), BF16),
                   jax.ShapeDtypeStruct((E_BRANCH, d), BF16),
                   jax.ShapeDtypeStruct((N_MEM, 2 * E_MEM), F32),
                   jax.ShapeDtypeStruct((1, d), F32), jax.ShapeDtypeStruct((1, d), F32)],
        scratch_shapes=[pltpu.VMEM((E_BRANCH, d), F32)],
        compiler_params=_cparams("arbitrary"),
    )(dxn, r, g, mix, mem, h, h, h, h, kv, w_out_t)


def _local_step(x, mem, get_weights, put_grads, rel_bias, conv_w, ln_g, ln_b, target):
    biases = [_tile_bias(jnp.pad(rel_bias[a], ((0, 0), (0, N_REL_PAD - N_REL)))) for a in range(DEPTH // 2)]

    saved = []
    xl = x
    for layer in range(DEPTH):
        w_in_l, rest = get_weights(layer, xl if layer else biases)
        h, xt = _inproj(xl, w_in_l)
        w_kv_l, w_out_l = rest(h)
        if layer % 2 == 0:
            mix, aux = _attn_fwd(h, biases[layer // 2])
        else:
            aux = jnp.pad(conv_w()[layer // 2], ((0, SUBLANES - 3), (0, 0)))
            mix = _conv_fwd(h, aux)
        kv = _small_matmul(mem, w_kv_l, False, F32, "kv_mem")
        if layer < DEPTH - 1:
            x_next, r, mem_out = _layer_out_fwd(xl, h, mix, kv, w_out_l, ln_g[layer][None], ln_b[layer][None])
        else:
            dx, r, mem_out, loss = _layer_out_fwd(xl, h, mix, kv, w_out_l, ln_g[layer][None], ln_b[layer][None],
                                                  target)
            x_next = None
        saved.append((xt, h, aux, mix, kv, r, mem_out, w_in_l, w_out_l))
        xl = x_next

    dgs, dbs, d_rel, d_conv = [], [], [], []
    for layer in reversed(range(DEPTH)):
        xt, h, aux, mix, kv, r, mem_out, w_in_l, w_out_l = saved[layer]
        dx_res, d_mix, dqz, dwo, dkv, dg, db = _layer_out_bwd(
            dx, r, ln_g[layer][None], h, mix, mem_out, kv, w_out_l.T)
        early = put_grads(layer, [1, 2], [_small_matmul(mem, dkv, True, BF16, "dw_kv"), dwo])
        if layer % 2 == 0:
            dq, dk, dv, dtb = _attn_bwd(h, aux, d_mix, early)
            d_rel.append(_tile_bias_bwd(dtb)[:, :N_REL])
            pieces = [dq, dk, dv, dqz]
        else:
            dbg, dcg, du, dw8 = _conv_bwd(h, aux, d_mix)
            d_conv.append(dw8[:3])
            pieces = [dbg, dcg, du, dqz]
        token = put_grads(layer, [0], [_dw_matmul(xt, pieces)])
        dgs.append(dg[0])
        dbs.append(db[0])
        dx = _dx_matmul(pieces, w_in_l, dx_res, token)

    rev = lambda lst: jnp.stack(lst[::-1])
    return loss, dx, rev(d_rel), rev(d_conv), rev(dgs), rev(dbs)


def _me():
    return lax.axis_index("x"), lax.axis_index("y"), lax.axis_index("c")


def _peer(k):
    x, y, c = _me()
    kx, ky, kc = (k >> 2) & 1, (k >> 1) & 1, k & 1
    return (1 - x if kx else x, 1 - y if ky else y, 1 - c if kc else c)


def _lin(dev):
    return 4 * dev[0] + 2 * dev[1] + dev[2]


ANY = pl.BlockSpec(memory_space=pl.ANY)


def _exchange(srcs, dst_shapes, src_slice, dst_slice, name):
    na = len(srcs)

    def body(*refs):
        src_refs = refs[:na]
        dst_refs = refs[na:2 * na]
        send_sems, recv_sems, local_sems = refs[2 * na:]
        me = _lin(_me())
        copies = []
        for a in range(na):
            loc = pltpu.make_async_copy(src_slice(a, src_refs[a], me), dst_slice(a, dst_refs[a], me),
                                        local_sems.at[a])
            loc.start()
            copies.append(loc)
            for k in range(1, N_DEV):
                peer = _peer(k)
                cp = pltpu.make_async_remote_copy(
                    src_ref=src_slice(a, src_refs[a], _lin(peer)),
                    dst_ref=dst_slice(a, dst_refs[a], me),
                    send_sem=send_sems.at[a, k - 1], recv_sem=recv_sems.at[a, k - 1],
                    device_id=peer, device_id_type=pl.DeviceIdType.MESH)
                cp.start()
                copies.append(cp)
        for cp in copies:
            cp.wait()

    return _call(
        body, name=name,
        in_specs=[ANY] * na, out_specs=[ANY] * na,
        out_shape=[jax.ShapeDtypeStruct(shp, s.dtype) for shp, s in zip(dst_shapes, srcs)],
        scratch_shapes=[pltpu.SemaphoreType.DMA((na, N_DEV - 1)),
                        pltpu.SemaphoreType.DMA((na, N_DEV - 1)),
                        pltpu.SemaphoreType.DMA((na,))],
    )(*srcs)


def _gather_to_all(src, name):
    return _exchange([src], [(N_DEV,) + src.shape], lambda a, ref, p: ref,
                     lambda a, ref, me: ref.at[me], name)[0]


HBM = pl.BlockSpec(memory_space=pltpu.HBM)
SEM = pl.BlockSpec(memory_space=pltpu.SEMAPHORE)
EFFECT = pltpu.SideEffectType.DATAFLOW_SIDE_EFFECTING
N_PEER = N_DEV - 1
N_KIND = 3


def _peer_copies(kind, src_ref, land_ref, send, recv, src_slice, dst_slice):
    me = _lin(_me())
    copies = []
    for k in range(1, N_DEV):
        peer = _peer(k)
        copies.append(pltpu.make_async_remote_copy(
            src_ref=src_slice(kind, src_ref, _lin(peer)),
            dst_ref=dst_slice(kind, land_ref, me, k),
            send_sem=send.at[k - 1], recv_sem=recv.at[k - 1],
            device_id=peer, device_id_type=pl.DeviceIdType.MESH))
    return copies


def _own_copy(kind, src_ref, land_ref, send, src_slice, dst_slice):
    me = _lin(_me())
    return pltpu.make_async_copy(src_slice(kind, src_ref, me), dst_slice(kind, land_ref, me, 0),
                                 send.at[N_PEER])


def _split_start(srcs, kinds, land_shapes, src_slice, dst_slice, name, own=False):
    na = len(srcs)

    def body(*refs):
        src_refs, land_refs = refs[:na], refs[na:2 * na]
        sems = refs[2 * na:4 * na]
        token = refs[-1]
        for a in range(na):
            for cp in _peer_copies(kinds[a], src_refs[a], land_refs[a], sems[2 * a], sems[2 * a + 1],
                                   src_slice, dst_slice):
                cp.start()
            if own:
                _own_copy(kinds[a], src_refs[a], land_refs[a], sems[2 * a], src_slice, dst_slice).start()
        token[...] = jnp.zeros_like(token)

    sem_shape = pltpu.SemaphoreType.DMA((N_DEV,))
    lands = [lax.empty(shp, s.dtype) for shp, s in zip(land_shapes, srcs)]
    outs = _call(
        body, name=name,
        in_specs=[HBM] * (2 * na),
        out_specs=[SEM] * (2 * na) + [HBM] * (2 * na) + [pl.BlockSpec(memory_space=pltpu.VMEM)],
        out_shape=[sem_shape] * (2 * na)
        + [pltpu.HBM(s.shape, s.dtype) for s in srcs]
        + [pltpu.HBM(shp, s.dtype) for shp, s in zip(land_shapes, srcs)]
        + [jax.ShapeDtypeStruct((SUBLANES, LANES), F32)],
        input_output_aliases={i: 2 * na + i for i in range(2 * na)},
        compiler_params=pltpu.CompilerParams(has_side_effects=EFFECT),
    )(*[pltpu.with_memory_space_constraint(a, pltpu.HBM) for a in list(srcs) + lands])
    sems = [(outs[2 * a], outs[2 * a + 1]) for a in range(na)]
    thrus = outs[2 * na:3 * na]
    lands = outs[3 * na:4 * na]
    return sems, thrus, lands, outs[-1]


def _split_wait(sems, thrus, lands, kinds, src_slice, dst_slice, after, name, own=False):
    na = len(thrus)
    after = list(after) if isinstance(after, (list, tuple)) else [after]

    def body(*refs):
        src_refs, land_refs = refs[:na], refs[na:2 * na]
        sem_refs = refs[2 * na:4 * na]
        for a in range(na):
            for cp in _peer_copies(kinds[a], src_refs[a], land_refs[a], sem_refs[2 * a], sem_refs[2 * a + 1],
                                   src_slice, dst_slice):
                cp.wait_send()
                cp.wait_recv()
            if own:
                _own_copy(kinds[a], src_refs[a], land_refs[a], sem_refs[2 * a], src_slice, dst_slice).wait()

    outs = _call(
        body, name=name,
        in_specs=[HBM] * (2 * na) + [SEM] * (2 * na) + [ANY] * len(after),
        out_specs=[HBM] * (2 * na),
        out_shape=[pltpu.HBM(a.shape, a.dtype) for a in list(thrus) + list(lands)],
        input_output_aliases={i: i for i in range(2 * na)},
        compiler_params=pltpu.CompilerParams(has_side_effects=EFFECT),
    )(*thrus, *lands, *[s for pair in sems for s in pair], *after)
    return outs[na:]


def _shard_dims(c_in, r_kv, r_out):
    def sl(j, ref, p):
        if j == 0:
            return ref.at[:, pl.ds(pl.multiple_of(p * c_in, LANES), c_in)]
        r = r_kv if j == 1 else r_out
        return ref.at[pl.ds(pl.multiple_of(p * r, 2 * SUBLANES), r), :]
    return sl


def _adamw_math(w, g, m, v):
    m = ADAM_B1 * m + (1.0 - ADAM_B1) * g
    v = ADAM_B2 * v + (1.0 - ADAM_B2) * (g * g)
    m_hat = m / (1.0 - ADAM_B1 ** ADAM_STEP)
    v_hat = v / (1.0 - ADAM_B2 ** ADAM_STEP)
    delta = -ADAM_LR * (m_hat / (jnp.sqrt(v_hat) + ADAM_EPS) + ADAM_WD * w)
    return delta, m, v


def _reduce_adamw(parts, w, m, v, name):
    rows, cols = w.shape
    tr = rows
    for cand in (512, 256, 128, 64, 32, 16):
        if rows % cand == 0 and rows > cand:
            tr = cand
            break

    def body(p_ref, w_ref, m_ref, v_ref, g_out, d_out, m_out, v_out):
        g = p_ref[0].astype(F32)
        for s in range(1, N_DEV):
            g = g + p_ref[s].astype(F32)
        g_out[...] = g
        d_out[...], m_out[...], v_out[...] = _adamw_math(w_ref[...], g, m_ref[...], v_ref[...])

    blk = pl.BlockSpec((tr, cols), lambda i: (i, 0))
    return _call(
        body, name=name, grid=(rows // tr,),
        in_specs=[pl.BlockSpec((N_DEV, tr, cols), lambda i: (0, i, 0)), blk, blk, blk],
        out_specs=[blk] * 4,
        out_shape=[jax.ShapeDtypeStruct((rows, cols), F32)] * 4,
        compiler_params=_cparams("parallel"),
    )(parts, w, m, v)


def _reduce_adamw_layers(lands, owns, w, m, v, name, first=0, prev=None):
    depth, rows, cols = w.shape
    nl = len(lands)
    tr = rows
    for cand in (256, 192, 128):
        if rows % cand == 0:
            tr = cand
            break

    kept = [] if prev is None else list(prev)

    def body(*refs):
        land_refs, own_refs = refs[:nl], refs[nl:2 * nl]
        w_ref, m_ref, v_ref = refs[2 * nl:2 * nl + 3]
        g_out, d_out, m_out, v_out = refs[2 * nl + 3 + len(kept):]
        layer = pl.program_id(0)
        for a in range(nl):
            @pl.when(layer == a)
            def _(a=a):
                g = own_refs[a][...].astype(F32)
                for k in range(N_PEER):
                    g = g + land_refs[a][k].astype(F32)
                g_out[...] = g
                d_out[...], m_out[...], v_out[...] = _adamw_math(w_ref[...], g, m_ref[...], v_ref[...])

    def lmap(a):
        return lambda l, i: (0, jnp.where(l == a, i, 0), 0)

    def omap(a):
        return lambda l, i: (jnp.where(l == a, i, 0), 0)

    blk = pl.BlockSpec((None, tr, cols), lambda l, i: (l + first, i, 0))
    n_in = 2 * nl + 3
    return _call(
        body, name=name, grid=(nl, rows // tr),
        in_specs=[pl.BlockSpec((N_PEER, tr, cols), lmap(a)) for a in range(nl)]
        + [pl.BlockSpec((tr, cols), omap(a)) for a in range(nl)] + [blk, blk, blk] + [ANY] * len(kept),
        out_specs=[blk] * 4,
        out_shape=[jax.ShapeDtypeStruct((depth, rows, cols), F32)] * 4,
        input_output_aliases={n_in + i: i for i in range(len(kept))},
        compiler_params=_cparams("arbitrary", "arbitrary"),
    )(*lands, *owns, w, m, v, *kept)


SM_G, SM_B, SM_CONV, SM_REL = 0, 4, 8, 16
SM_ROWS = SM_REL + 2 * N_HEADS
REL_W = 384


def _pack_small(d_rel, d_conv, dg, db):
    buf = jnp.zeros((SM_ROWS, D_MODEL), F32)
    buf = buf.at[SM_G:SM_G + DEPTH].set(dg)
    buf = buf.at[SM_B:SM_B + DEPTH].set(db)
    buf = buf.at[SM_CONV:SM_CONV + 6].set(d_conv.reshape(6, E_MIX))
    buf = buf.at[SM_REL:, :N_REL].set(d_rel.reshape(2 * N_HEADS, N_REL))
    return buf


def kernel(x, mem, w_in, w_mem_kv, w_out, rel_bias, conv_w, ln_g, ln_b, loss_target, m_w_in, m_w_mem_kv, m_w_out, m_rel_bias, m_conv_w, m_ln_g, m_ln_b, v_w_in, v_w_mem_kv, v_w_out, v_rel_bias, v_conv_w, v_ln_g, v_ln_b):
    me = _lin(_me())
    c_in, r_kv, r_out, c_conv = w_in.shape[2], w_mem_kv.shape[1], w_out.shape[1], conv_w.shape[2]

    shard = _shard_dims(c_in, r_kv, r_out)
    own_start = lambda j: (0, me * c_in) if j == 0 else (me * (r_kv if j == 1 else r_out), 0)

    w_sh = [w_in.astype(BF16), w_mem_kv.astype(BF16), w_out.astype(BF16)]
    full_shapes = [(D_MODEL, N_DEV * c_in), (N_DEV * r_kv, w_mem_kv.shape[2]), (N_DEV * r_out, D_MODEL)]
    conv_kind = N_KIND
    conv_tile = jnp.pad(conv_w.reshape(6, c_conv), ((0, SUBLANES - 6), (0, 0)))
    ag_src = lambda j, ref, p: ref
    ag_dst = lambda j, ref, me_, k: ref.at[me_] if j == conv_kind else shard(j, ref, me_)
    kinds = list(range(N_KIND))
    ag_sems, ag_thrus, ag_lands, _ = _split_start(
        [conv_tile] + [w_sh[j][layer] for layer in range(DEPTH) for j in kinds], [conv_kind] + kinds * DEPTH,
        [(N_DEV,) + conv_tile.shape] + full_shapes * DEPTH, ag_src, ag_dst, "ag_start", own=True)
    conv_landed = []

    def get_weights(layer, x_layer):
        lo = 1 + layer * N_KIND

        def wait(idx, js, after, name):
            return _split_wait([ag_sems[a] for a in idx], [ag_thrus[a] for a in idx],
                               [ag_lands[a] for a in idx], js, ag_src, ag_dst, after, name, own=True)

        if layer == 0:
            conv_land, w_in_l = wait([0, lo], [conv_kind, 0], x_layer, "ag_wait_in_0")
            conv_landed.append(conv_land)
        else:
            w_in_l, = wait([lo], [0], x_layer, "ag_wait_in_%d" % layer)
        return w_in_l, lambda h: wait([lo + 1, lo + 2], [1, 2], h, "ag_wait_kv_out_%d" % layer)

    def conv_full():
        return jnp.transpose(conv_landed[0][:, :6], (1, 0, 2)).reshape(2, 3, N_DEV * c_conv)

    rs_src = shard
    rs_dst = lambda j, ref, me_, k: ref.at[k - 1]
    rs_shapes = [(N_PEER, D_MODEL, c_in), (N_PEER, r_kv, w_mem_kv.shape[2]), (N_PEER, r_out, D_MODEL)]
    own_sizes = [(D_MODEL, c_in), (r_kv, w_mem_kv.shape[2]), (r_out, D_MODEL)]
    pending = {}

    held = {}

    def put_grads(layer, js, arrays):
        if layer > 0 and js != [0]:
            held[layer] = (js, arrays)
            return None
        if layer > 0:
            js, arrays = held[layer][0] + js, held[layer][1] + arrays
        owns = [lax.dynamic_slice(a, own_start(j), own_sizes[j]) for j, a in zip(js, arrays)]
        sems, thrus, lands, token = _split_start(
            arrays, js, [rs_shapes[j] for j in js], rs_src, rs_dst,
            "rs_start_%d_%s" % (layer, "".join(str(j) for j in js)))
        entry = pending.setdefault(layer, ([], [], [], [], []))
        for lst, new in zip(entry, (js, sems, thrus, lands, owns)):
            lst.extend(new)
        return token


    loss, grad_x, d_rel, d_conv, dg, db = _local_step(
        x[0], mem[0], get_weights, put_grads, rel_bias, conv_full, ln_g, ln_b, loss_target[0])

    p_small = _gather_to_all(_pack_small(d_rel, d_conv, dg, db), "gather_small_grads")

    rs_lands, rs_owns = {}, {}

    def rs_wait(layer, want, after, name):
        js, sems, thrus, lands, owns = pending[layer]
        pos = [js.index(j) for j in want]
        got = _split_wait([sems[p] for p in pos], [thrus[p] for p in pos], [lands[p] for p in pos],
                          want, rs_src, rs_dst, after, name)
        for j, p, land in zip(want, pos, got):
            rs_lands[layer, j], rs_owns[layer, j] = land, owns[p]

    for layer in range(1, DEPTH):
        rs_wait(layer, kinds, grad_x, "rs_wait_%d" % layer)
    rs_wait(0, [1, 2], grad_x, "rs_wait_0_kv_out")

    def big(j, w, m, v, name, layers, prev=None):
        return _reduce_adamw_layers([rs_lands[layer, j] for layer in layers], [rs_owns[layer, j] for layer in layers],
                                    w, m, v, name, first=layers[0], prev=prev)

    every = list(range(DEPTH))
    g_kv, d_kv, nm_kv, nv_kv = big(1, w_mem_kv, m_w_mem_kv, v_w_mem_kv, "adamw_w_kv", every)
    g_out, d_out, nm_out, nv_out = big(2, w_out, m_w_out, v_w_out, "adamw_w_out", every)
    later = big(0, w_in, m_w_in, v_w_in, "adamw_w_in_later_layers", every[1:])

    def pack_state(rel, conv, g, b):
        conv_full = jnp.zeros((2, 3, E_MIX), F32)
        conv_full = lax.dynamic_update_slice(conv_full, conv, (0, 0, me * c_conv))
        return _pack_small(rel, conv_full, g, b)

    sm_w = pack_state(rel_bias, conv_w, ln_g, ln_b)
    sm_m = pack_state(m_rel_bias, m_conv_w, m_ln_g, m_ln_b)
    sm_v = pack_state(v_rel_bias, v_conv_w, v_ln_g, v_ln_b)
    sm_outs = _reduce_adamw(p_small, sm_w, sm_m, sm_v, "adamw_small")

    rs_wait(0, [0], [later[0], sm_outs[0], g_kv, g_out], "rs_wait_0_in")
    g_in, d_in, nm_in, nv_in = big(0, w_in, m_w_in, v_w_in, "adamw_w_in_layer_0", [0], prev=later)

    def unpack(buf):
        rel = buf[SM_REL:, :N_REL].reshape(2, N_HEADS, N_REL)
        conv = lax.dynamic_slice(buf[SM_CONV:SM_CONV + 6].reshape(2, 3, E_MIX), (0, 0, me * c_conv), (2, 3, c_conv))
        return rel, conv, buf[SM_G:SM_G + DEPTH], buf[SM_B:SM_B + DEPTH]

    g_sm, d_sm, nm_sm, nv_sm = [unpack(b) for b in sm_outs]

    loss = lax.psum(loss[0, 0], ("x", "y", "c"))
    return (loss, grad_x[None],
            g_in, g_kv, g_out, *g_sm,
            d_in, d_kv, d_out, *d_sm,
            nm_in, nm_kv, nm_out, *nm_sm,
            nv_in, nv_kv, nv_out, *nv_sm)
```

```python
import functools
import math

import jax
import jax.numpy as jnp
from jax import lax
from jax.experimental import pallas as pl
from jax.experimental.pallas import tpu as pltpu

F32 = jnp.float32
BF16 = jnp.bfloat16
MXU_DTYPE = jnp.bfloat16

N_DEV = 8
D_MODEL = 1024
DEPTH = 4
CHUNK = 64
N_PREV = 8
N_HEADS = 16
HEAD_DIM = 64
E_MIX = 1024
REL_CLIP = 128
N_REL = 2 * REL_CLIP + 1
N_REL_PAD = 384
N_MEM = 256
MEM_HEADS = 4
MEM_HEAD_DIM = 128
E_MEM = 512
E_BRANCH = E_MIX + E_MEM
N_IN = 3 * E_MIX + E_MEM + E_BRANCH
DN_ALPHA = (2.0 * DEPTH) ** 0.25
LN_EPS = 1e-5
NEG = -1e30

ADAM_LR = 0.001
ADAM_B1 = 0.9
ADAM_B2 = 0.999
ADAM_EPS = 1e-08
ADAM_WD = 0.01
ADAM_STEP = 10

LANES = 128
SUBLANES = 8
VMEM_LIMIT = 56 * 1024 * 1024

TQ = 4 * CHUNK
TKEYS = 3 * TQ
ROLL_W = 1024
TS = 256
QM_BLK = 3 * E_MIX // E_MEM
Z_BLK = QM_BLK + 1


def _call(body, **kw):
    return pl.pallas_call(body, **kw)


def _cparams(*sem):
    return pltpu.CompilerParams(dimension_semantics=sem, vmem_limit_bytes=VMEM_LIMIT)


def _dot(a, b):
    return jnp.dot(a, b, preferred_element_type=F32)


def _dot_nt(a, b):
    return lax.dot_general(a, b, (((1,), (1,)), ((), ())), preferred_element_type=F32)


def _dot_tn(a, b):
    return lax.dot_general(a, b, (((0,), (0,)), ((), ())), preferred_element_type=F32)


def _inproj(x, w):
    s, d = x.shape
    n = w.shape[1]
    tm = 1024
    tn = 1024

    def body(x_ref, w_ref, o_ref, xt_ref):
        xb = x_ref[...].astype(xt_ref.dtype)
        xt_ref[...] = xb.T
        for j in range(n // tn):
            o_ref[:, j * tn:(j + 1) * tn] = _dot(xb, w_ref[:, j * tn:(j + 1) * tn]).astype(o_ref.dtype)

    return _call(
        body, name="inproj", grid=(s // tm,),
        in_specs=[pl.BlockSpec((tm, d), lambda i: (i, 0)),
                  pl.BlockSpec((d, n), lambda i: (0, 0), pipeline_mode=pl.Buffered(1))],
        out_specs=[pl.BlockSpec((tm, n), lambda i: (i, 0)),
                   pl.BlockSpec((d, tm), lambda i: (0, i))],
        out_shape=[jax.ShapeDtypeStruct((s, n), BF16), jax.ShapeDtypeStruct((d, s), BF16)],
        compiler_params=_cparams("parallel"),
    )(x, w)


def _small_matmul(a, b, trans_a, out_dtype, name):
    m = a.shape[1] if trans_a else a.shape[0]
    n = b.shape[1]

    def body(a_ref, b_ref, o_ref):
        av = a_ref[...].astype(MXU_DTYPE)
        bv = b_ref[...].astype(MXU_DTYPE)
        r = _dot_tn(av, bv) if trans_a else _dot(av, bv)
        o_ref[...] = r.astype(out_dtype)

    return _call(
        body, name=name,
        in_specs=[pl.BlockSpec(memory_space=pltpu.VMEM)] * 2,
        out_specs=pl.BlockSpec(memory_space=pltpu.VMEM),
        out_shape=jax.ShapeDtypeStruct((m, n), out_dtype),
        compiler_params=pltpu.CompilerParams(vmem_limit_bytes=VMEM_LIMIT),
    )(a, b)


def _piece_blocks(pieces, blk):
    offs, nbs, o = [], [], 0
    for p in pieces:
        nb = p.shape[1] // blk
        offs.append(o)
        nbs.append(nb)
        o += nb
    return offs, nbs, o


def _dx_matmul(pieces, w, addend, token=None):
    s = pieces[0].shape[0]
    d, n_in = w.shape
    tm = 512
    tw = 1024
    np_ = len(pieces)
    extra = [] if token is None else [token]

    def body(*refs):
        a_refs = refs[:np_]
        w_ref, add_ref = refs[np_:np_ + 2]
        o_ref, wt = refs[-2:]

        @pl.when(pl.program_id(0) == 0)
        def _():
            for j in range(n_in // tw):
                wt[j * tw:(j + 1) * tw, :] = w_ref[:, j * tw:(j + 1) * tw].T

        a = jnp.concatenate([r[...] for r in a_refs], axis=1)
        o_ref[...] = add_ref[...] + _dot(a, wt[...])

    in_specs = [pl.BlockSpec((tm, p.shape[1]), lambda i: (i, 0)) for p in pieces]
    in_specs += [pl.BlockSpec((d, n_in), lambda i: (0, 0), pipeline_mode=pl.Buffered(1)),
                 pl.BlockSpec((tm, d), lambda i: (i, 0))]
    in_specs += [pl.BlockSpec((SUBLANES, LANES), lambda i: (0, 0)) for _ in extra]
    return _call(
        body, name="dx_matmul", grid=(s // tm,),
        in_specs=in_specs,
        out_specs=pl.BlockSpec((tm, d), lambda i: (i, 0)),
        out_shape=jax.ShapeDtypeStruct((s, d), F32),
        scratch_shapes=[pltpu.VMEM((n_in, d), w.dtype)],
        compiler_params=_cparams("arbitrary"),
    )(*pieces, w, addend, *extra)


def _dw_matmul(xt, pieces):
    d, s = xt.shape
    tn = 256
    offs, nbs, nj = _piece_blocks(pieces, tn)
    np_ = len(pieces)

    def body(*refs):
        x_ref = refs[0]
        b_refs = refs[1:1 + np_]
        o_ref = refs[1 + np_]
        j = pl.program_id(0)
        for p in range(np_):
            @pl.when((j >= offs[p]) & (j < offs[p] + nbs[p]))
            def _(p=p):
                o_ref[...] = _dot(x_ref[...], b_refs[p][...]).astype(o_ref.dtype)

    def bmap(p):
        return lambda j: (0, jnp.clip(j - offs[p], 0, nbs[p] - 1))

    in_specs = [pl.BlockSpec((d, s), lambda j: (0, 0), pipeline_mode=pl.Buffered(1))]
    in_specs += [pl.BlockSpec((s, tn), bmap(p)) for p in range(np_)]
    return _call(
        body, name="dw_matmul", grid=(nj,),
        in_specs=in_specs,
        out_specs=pl.BlockSpec((d, tn), lambda j: (0, j)),
        out_shape=jax.ShapeDtypeStruct((d, nj * tn), BF16),
        compiler_params=_cparams("parallel"),
    )(xt, *pieces)


def _rel_onehot():
    j = lax.broadcasted_iota(jnp.int32, (N_REL_PAD, ROLL_W), 1)
    kk = lax.broadcasted_iota(jnp.int32, (N_REL_PAD, ROLL_W), 0)
    dd = jnp.where(j < TKEYS, j, j - ROLL_W)
    idx = jnp.clip(N_PREV * CHUNK - dd, -REL_CLIP, REL_CLIP) + REL_CLIP
    return jnp.where(idx == kk, 1.0, 0.0).astype(F32)


def _band_mask():
    r = lax.broadcasted_iota(jnp.int32, (TQ, TKEYS), 0) // CHUNK
    m = lax.broadcasted_iota(jnp.int32, (TQ, TKEYS), 1) // CHUNK
    return (m >= r) & (m <= r + N_PREV)


def _tile_bias(table_pad):
    def body(t_ref, o_ref):
        g = jnp.dot(t_ref[...], _rel_onehot(), preferred_element_type=F32,
                    precision=lax.Precision.HIGHEST)
        band = _band_mask()
        for h in range(N_HEADS):
            gh = jnp.broadcast_to(g[h:h + 1, :], (TQ, ROLL_W))
            rolled = pltpu.roll(gh, 0, 1, stride=1, stride_axis=0)
            o_ref[h] = jnp.where(band, rolled[:, :TKEYS], NEG)

    return _call(
        body, name="tile_bias",
        in_specs=[pl.BlockSpec(memory_space=pltpu.VMEM)],
        out_specs=pl.BlockSpec(memory_space=pltpu.VMEM),
        out_shape=jax.ShapeDtypeStruct((N_HEADS, TQ, TKEYS), F32),
        compiler_params=pltpu.CompilerParams(vmem_limit_bytes=VMEM_LIMIT),
    )(table_pad)


def _tile_bias_bwd(dtb):
    def body(d_ref, o_ref, g_ref):
        zpad = jnp.zeros((TQ, ROLL_W - TKEYS), F32)
        rr = lax.broadcasted_iota(jnp.int32, (TQ, TQ), 0)
        cc = lax.broadcasted_iota(jnp.int32, (TQ, TQ), 1)
        flip = jnp.where(rr + cc == TQ - 1, 1.0, 0.0).astype(F32)
        for h in range(N_HEADS):
            xh = jnp.concatenate([d_ref[h], zpad], axis=1)
            xf = jnp.dot(flip, xh, preferred_element_type=F32, precision=lax.Precision.HIGHEST)
            rolled = pltpu.roll(xf, 0, 1, stride=1, stride_axis=0)
            g_ref[h:h + 1, :] = jnp.sum(rolled, axis=0, keepdims=True)
        g = pltpu.roll(g_ref[...], ROLL_W - (TQ - 1), 1)
        o_ref[...] = lax.dot_general(g, _rel_onehot(), (((1,), (1,)), ((), ())),
                                     preferred_element_type=F32, precision=lax.Precision.HIGHEST)

    return _call(
        body, name="tile_bias_bwd",
        in_specs=[pl.BlockSpec(memory_space=pltpu.VMEM)],
        out_specs=pl.BlockSpec(memory_space=pltpu.VMEM),
        out_shape=jax.ShapeDtypeStruct((N_HEADS, N_REL_PAD), F32),
        scratch_shapes=[pltpu.VMEM((N_HEADS, ROLL_W), F32)],
        compiler_params=pltpu.CompilerParams(vmem_limit_bytes=VMEM_LIMIT),
    )(dtb)


HB = 4
HBW = HB * HEAD_DIM
ATTN_SCALE = 0.125
assert ATTN_SCALE == 1.0 / math.sqrt(HEAD_DIM)


def _head_masks():
    lane = lax.broadcasted_iota(jnp.int32, (1, HBW), 1) // HEAD_DIM
    return [lane == hh for hh in range(HB)]


def _select_heads(masks, parts):
    out = parts[-1]
    for hh in range(HB - 2, -1, -1):
        out = jnp.where(masks[hh], parts[hh], out)
    return out


def _attn_probs(qm, kcat, tb, valid):
    s = _dot_nt(qm, kcat) + tb
    if valid is not None:
        s = jnp.where(valid, s, NEG)
    m = jnp.max(s, axis=-1, keepdims=True)
    e = jnp.exp(s - m)
    return e * (1.0 / jnp.sum(e, axis=-1, keepdims=True))


def _key_valid(i):
    col = lax.broadcasted_iota(jnp.int32, (TQ, TKEYS), 1)
    return col >= jnp.maximum(2 - i, 0) * TQ


def _kv_specs(col0, nt):
    def spec(back):
        return pl.BlockSpec((TQ, HBW), lambda hp, i: (jnp.clip(i - back, 0, nt - 1), col0 + hp))
    return [spec(2), spec(1), spec(0)]


def _attn_fwd(h, tb):
    s = h.shape[0]
    nt = s // TQ
    nhp = N_HEADS // HB

    def body(q_ref, k0, k1, k2, v0, v1, v2, tb_ref, o_ref, p_ref):
        i = pl.program_id(1)

        def tile(valid):
            masks = _head_masks()
            qs = q_ref[...].astype(MXU_DTYPE) * ATTN_SCALE
            kcat = jnp.concatenate([k0[...], k1[...], k2[...]], axis=0).astype(MXU_DTYPE)
            vcat = jnp.concatenate([v0[...], v1[...], v2[...]], axis=0).astype(MXU_DTYPE)
            outs = []
            for hh in range(HB):
                qm = jnp.where(masks[hh], qs, jnp.zeros_like(qs))
                pb = _attn_probs(qm, kcat, tb_ref[hh], valid).astype(MXU_DTYPE)
                p_ref[hh] = pb.astype(p_ref.dtype)
                outs.append(_dot(pb, vcat))
            o_ref[...] = _select_heads(masks, outs).astype(o_ref.dtype)

        @pl.when(i < 2)
        def _():
            tile(_key_valid(i))

        @pl.when(i >= 2)
        def _():
            tile(None)

    in_specs = [pl.BlockSpec((TQ, HBW), lambda hp, i: (i, hp))]
    in_specs += _kv_specs(nhp, nt) + _kv_specs(2 * nhp, nt)
    in_specs += [pl.BlockSpec((HB, TQ, TKEYS), lambda hp, i: (hp, 0, 0))]
    return _call(
        body, name="attn_fwd", grid=(nhp, nt),
        in_specs=in_specs,
        out_specs=[pl.BlockSpec((TQ, HBW), lambda hp, i: (i, hp)),
                   pl.BlockSpec((HB, TQ, TKEYS), lambda hp, i: (hp, i, 0))],
        out_shape=[jax.ShapeDtypeStruct((s, E_MIX), BF16), jax.ShapeDtypeStruct((N_HEADS, s, TKEYS), BF16)],
        compiler_params=_cparams("parallel", "parallel"),
    )(h, h, h, h, h, h, h, tb)


def _attn_bwd(h, probs, d_mix, token=None):
    s = h.shape[0]
    nt = s // TQ
    nhp = N_HEADS // HB
    extra = [] if token is None else [token]

    def body(q_ref, k0, k1, k2, v0, v1, v2, p_ref, do_ref, *rest):
        dq_ref, dk_ref, dv_ref, dtb_ref, dk_acc, dv_acc = rest[len(extra):]
        i = pl.program_id(1)

        @pl.when(i == 0)
        def _():
            dk_acc[...] = jnp.zeros_like(dk_acc)
            dv_acc[...] = jnp.zeros_like(dv_acc)
            dtb_ref[...] = jnp.zeros_like(dtb_ref)

        @pl.when((i > 0) & (i < nt))
        def _():
            dk_acc[i % 3] = jnp.zeros((TQ, HBW), F32)
            dv_acc[i % 3] = jnp.zeros((TQ, HBW), F32)

        @pl.when(i < nt)
        def _():
            masks = _head_masks()
            qs = q_ref[...].astype(MXU_DTYPE) * ATTN_SCALE
            do2 = do_ref[...].astype(MXU_DTYPE)
            kcat = jnp.concatenate([k0[...], k1[...], k2[...]], axis=0).astype(MXU_DTYPE)
            vcat = jnp.concatenate([v0[...], v1[...], v2[...]], axis=0).astype(MXU_DTYPE)
            ks = kcat * ATTN_SCALE
            dqs, dks, dvs = [], [], []
            for hh in range(HB):
                dom = jnp.where(masks[hh], do2, jnp.zeros_like(do2))
                pb = p_ref[hh]
                p = pb.astype(F32)
                dp = _dot_nt(dom, vcat)
                ds = p * (dp - jnp.sum(p * dp, axis=-1, keepdims=True))
                dtb_ref[hh] += ds
                dsb = ds.astype(MXU_DTYPE)
                dqs.append(_dot(dsb, ks))
                dks.append(_dot_tn(dsb, qs))
                dvs.append(_dot_tn(pb.astype(MXU_DTYPE), do2))
            dq_ref[...] = _select_heads(masks, dqs).astype(dq_ref.dtype)
            dkc = _select_heads(masks, dks)
            dvc = _select_heads(masks, dvs)
            for jj in range(3):
                slot = (i + 1 + jj) % 3
                dk_acc[slot] += dkc[jj * TQ:(jj + 1) * TQ]
                dv_acc[slot] += dvc[jj * TQ:(jj + 1) * TQ]

        @pl.when(i >= 2)
        def _():
            slot = (i - 2) % 3
            dk_ref[...] = dk_acc[slot].astype(dk_ref.dtype)
            dv_ref[...] = dv_acc[slot].astype(dv_ref.dtype)

    qmap = lambda hp, i: (jnp.minimum(i, nt - 1), hp)
    kvout = lambda hp, i: (jnp.maximum(i - 2, 0), hp)
    in_specs = [pl.BlockSpec((TQ, HBW), qmap)]
    in_specs += _kv_specs(nhp, nt) + _kv_specs(2 * nhp, nt)
    in_specs += [pl.BlockSpec((HB, TQ, TKEYS), lambda hp, i: (hp, jnp.minimum(i, nt - 1), 0)),
                 pl.BlockSpec((TQ, HBW), qmap)]
    in_specs += [pl.BlockSpec((SUBLANES, LANES), lambda hp, i: (0, 0)) for _ in extra]
    blk = (TQ, HBW)
    return _call(
        body, name="attn_bwd", grid=(nhp, nt + 2),
        in_specs=in_specs,
        out_specs=[pl.BlockSpec(blk, qmap), pl.BlockSpec(blk, kvout), pl.BlockSpec(blk, kvout),
                   pl.BlockSpec((HB, TQ, TKEYS), lambda hp, i: (hp, 0, 0))],
        out_shape=[jax.ShapeDtypeStruct((s, E_MIX), BF16)] * 3
        + [jax.ShapeDtypeStruct((N_HEADS, TQ, TKEYS), F32)],
        scratch_shapes=[pltpu.VMEM((3, TQ, HBW), F32)] * 2,
        compiler_params=_cparams("parallel", "arbitrary"),
    )(h, h, h, h, h, h, h, probs, d_mix, *extra)


CONV_TS = 512
HALO = 2 * SUBLANES


def _shift_down(prev, cur, k):
    rolled = pltpu.roll(cur, k, 0)
    row = lax.broadcasted_iota(jnp.int32, (HALO, cur.shape[1]), 0)
    top = jnp.where(row < k, pltpu.roll(prev, k, 0), rolled[:HALO])
    return jnp.concatenate([top, rolled[HALO:]], axis=0)


def _shift_up(cur, nxt, k):
    ts = cur.shape[0]
    rolled = pltpu.roll(cur, ts - k, 0)
    row = lax.broadcasted_iota(jnp.int32, (HALO, cur.shape[1]), 0)
    bottom = jnp.where(row >= HALO - k, pltpu.roll(nxt, HALO - k, 0), rolled[ts - HALO:])
    return jnp.concatenate([rolled[:ts - HALO], bottom], axis=0)


def _conv_specs(ts, nb):
    tile = lambda c: pl.BlockSpec((ts, E_MIX), lambda i: (i, c))
    prev = lambda c: pl.BlockSpec((HALO, E_MIX), lambda i: (jnp.maximum(i * (ts // HALO) - 1, 0), c))
    return tile, prev


def _conv_fwd(h, w8):
    s = h.shape[0]
    ts = CONV_TS
    nb = s // ts
    tile, prev = _conv_specs(ts, nb)

    def body(bg, cg, u, cgp, up, w_ref, o_ref):
        i = pl.program_id(0)
        a = cg[...].astype(F32) * u[...].astype(F32)
        ap = jnp.where(i > 0, cgp[...].astype(F32) * up[...].astype(F32), 0.0)
        w = w_ref[...]
        conv = w[0:1] * _shift_down(ap, a, 2) + w[1:2] * _shift_down(ap, a, 1) + w[2:3] * a
        o_ref[...] = (bg[...].astype(F32) * conv).astype(o_ref.dtype)

    return _call(
        body, name="conv_fwd", grid=(nb,),
        in_specs=[tile(0), tile(1), tile(2), prev(1), prev(2),
                  pl.BlockSpec((SUBLANES, E_MIX), lambda i: (0, 0))],
        out_specs=pl.BlockSpec((ts, E_MIX), lambda i: (i, 0)),
        out_shape=jax.ShapeDtypeStruct((s, E_MIX), BF16),
        compiler_params=_cparams("parallel"),
    )(h, h, h, h, h, w8)


def _conv_bwd(h, w8, d_mix):
    s = h.shape[0]
    ts = CONV_TS
    nb = s // ts
    tile, prev = _conv_specs(ts, nb)
    nrow = s // HALO
    nxt = lambda c: pl.BlockSpec((HALO, E_MIX), lambda i: (jnp.minimum((i + 1) * (ts // HALO), nrow - 1), c))

    def body(bg, cg, u, cgp, up, bgn, dmix, dmixn, w_ref, dbg_ref, dcg_ref, du_ref, dw_ref):
        i = pl.program_id(0)

        @pl.when(i == 0)
        def _():
            dw_ref[...] = jnp.zeros_like(dw_ref)

        cgv, uv = cg[...].astype(F32), u[...].astype(F32)
        a = cgv * uv
        ap = jnp.where(i > 0, cgp[...].astype(F32) * up[...].astype(F32), 0.0)
        a1 = _shift_down(ap, a, 1)
        a2 = _shift_down(ap, a, 2)
        w = w_ref[...]
        conv = w[0:1] * a2 + w[1:2] * a1 + w[2:3] * a
        dm = dmix[...].astype(F32)
        dbg_ref[...] = (dm * conv).astype(dbg_ref.dtype)
        dc = dm * bg[...].astype(F32)
        dcn = jnp.where(i < nb - 1, dmixn[...].astype(F32) * bgn[...].astype(F32), 0.0)
        da = w[2:3] * dc + w[1:2] * _shift_up(dc, dcn, 1) + w[0:1] * _shift_up(dc, dcn, 2)
        dcg_ref[...] = (da * uv).astype(dcg_ref.dtype)
        du_ref[...] = (da * cgv).astype(du_ref.dtype)
        dw_ref[0:1, :] += jnp.sum(dc * a2, axis=0, keepdims=True)
        dw_ref[1:2, :] += jnp.sum(dc * a1, axis=0, keepdims=True)
        dw_ref[2:3, :] += jnp.sum(dc * a, axis=0, keepdims=True)

    full = lambda: pl.BlockSpec((ts, E_MIX), lambda i: (i, 0))
    return _call(
        body, name="conv_bwd", grid=(nb,),
        in_specs=[tile(0), tile(1), tile(2), prev(1), prev(2), nxt(0),
                  full(), pl.BlockSpec((HALO, E_MIX), lambda i: (jnp.minimum((i + 1) * (ts // HALO), nrow - 1), 0)),
                  pl.BlockSpec((SUBLANES, E_MIX), lambda i: (0, 0))],
        out_specs=[full(), full(), full(), pl.BlockSpec((SUBLANES, E_MIX), lambda i: (0, 0))],
        out_shape=[jax.ShapeDtypeStruct((s, E_MIX), BF16)] * 3
        + [jax.ShapeDtypeStruct((SUBLANES, E_MIX), F32)],
        compiler_params=_cparams("arbitrary"),
    )(h, h, h, h, h, h, d_mix, d_mix, w8)


def _mem_probs(qh, kh):
    s = _dot_nt(qh, kh) / math.sqrt(MEM_HEAD_DIM)
    m = jnp.max(s, axis=-1, keepdims=True)
    e = jnp.exp(s - m)
    return e / jnp.sum(e, axis=-1, keepdims=True)


def _sigmoid(z):
    return 1.0 / (1.0 + jnp.exp(-z))


def _layer_out_fwd(x, h, mix, kv, w_out, g, b, target=None):
    s, d = x.shape
    ts = 2 * TS
    last = target is not None

    def body(x_ref, mix_ref, qm_ref, z0, z1, z2, kv_ref, wo_ref, g_ref, b_ref, *rest):
        xn_ref, r_ref, mem_ref = rest[last:last + 3]
        qm = qm_ref[...].astype(MXU_DTYPE)
        kvb = kv_ref[...].astype(MXU_DTYPE)
        mems = []
        for hh in range(MEM_HEADS):
            lo = hh * MEM_HEAD_DIM
            p = _mem_probs(qm[:, lo:lo + MEM_HEAD_DIM], kvb[:, lo:lo + MEM_HEAD_DIM])
            mems.append(_dot(p.astype(MXU_DTYPE), kvb[:, E_MEM + lo:E_MEM + lo + MEM_HEAD_DIM]))
        mem = jnp.concatenate(mems, axis=1).astype(mem_ref.dtype)
        mem_ref[...] = mem
        mixv = mix_ref[...].astype(F32)
        half = E_MIX // 2
        parts = [mixv[:, :half], mixv[:, half:], mem.astype(F32)]
        out = jnp.zeros((ts, d), F32)
        for c, zr in enumerate((z0, z1, z2)):
            zv = zr[...].astype(F32)
            y = (parts[c] * (zv * _sigmoid(zv))).astype(MXU_DTYPE)
            out += _dot(y, wo_ref[c * half:(c + 1) * half, :])
        r = DN_ALPHA * x_ref[...] + out
        r_ref[...] = r
        mu = jnp.mean(r, axis=-1, keepdims=True)
        rc = r - mu
        var = jnp.mean(rc * rc, axis=-1, keepdims=True)
        xn = rc * lax.rsqrt(var + LN_EPS) * g_ref[...] + b_ref[...]
        if not last:
            xn_ref[...] = xn
        else:
            t_ref, l_ref = rest[0], rest[4]

            @pl.when(pl.program_id(0) == 0)
            def _():
                l_ref[...] = jnp.zeros_like(l_ref)

            e = xn - t_ref[...]
            xn_ref[...] = e * (1.0 / d)
            l_ref[...] += (0.5 / d) * jnp.sum(jnp.sum(e * e, axis=1, keepdims=True), axis=0, keepdims=True)

    row = lambda w, c: pl.BlockSpec((ts, w), lambda i: (i, c))
    const = lambda shp: pl.BlockSpec(shp, lambda i: (0, 0))
    tail = [target] if last else []
    return _call(
        body, name="layer_out_fwd_loss" if last else "layer_out_fwd", grid=(s // ts,),
        in_specs=[row(d, 0), row(E_MIX, 0), row(E_MEM, QM_BLK),
                  row(E_MEM, Z_BLK), row(E_MEM, Z_BLK + 1), row(E_MEM, Z_BLK + 2),
                  const((N_MEM, 2 * E_MEM)), const((E_BRANCH, d)), const((1, d)), const((1, d))]
        + [row(d, 0) for _ in tail],
        out_specs=[row(d, 0), row(d, 0), row(E_MEM, 0)] + [const((1, 1)) for _ in tail],
        out_shape=[jax.ShapeDtypeStruct((s, d), F32), jax.ShapeDtypeStruct((s, d), F32),
                   jax.ShapeDtypeStruct((s, E_MEM), BF16)] + [jax.ShapeDtypeStruct((1, 1), F32) for _ in tail],
        compiler_params=_cparams("arbitrary" if last else "parallel"),
    )(x, mix, h, h, h, h, kv, w_out, g, b, *tail)


def _layer_out_bwd(dxn, r, g, h, mix, mem, kv, w_out_t):
    s, d = r.shape
    ts = 2 * TS
    nb = s // ts
    half = E_MIX // 2
    inv = 1.0 / math.sqrt(MEM_HEAD_DIM)

    def body(dxn_ref, r_ref, g_ref, mix_ref, mem_ref, qm_ref, z0, z1, z2, kv_ref, wo_ref,
             dxr_ref, dmix_ref, dqz_ref, dwo_ref, dkv_ref, dg_ref, db_ref, dw_acc):
        i = pl.program_id(0)

        @pl.when(i == 0)
        def _():
            dw_acc[...] = jnp.zeros_like(dw_acc)
            dkv_ref[...] = jnp.zeros_like(dkv_ref)
            dg_ref[...] = jnp.zeros_like(dg_ref)
            db_ref[...] = jnp.zeros_like(db_ref)

        dxn_v = dxn_ref[...]
        rv = r_ref[...]
        mu = jnp.mean(rv, axis=-1, keepdims=True)
        rc = rv - mu
        var = jnp.mean(rc * rc, axis=-1, keepdims=True)
        rstd = lax.rsqrt(var + LN_EPS)
        xhat = rc * rstd
        dg_ref[...] += jnp.sum(dxn_v * xhat, axis=0, keepdims=True)
        db_ref[...] += jnp.sum(dxn_v, axis=0, keepdims=True)
        dxh = dxn_v * g_ref[...]
        m1 = jnp.mean(dxh, axis=-1, keepdims=True)
        m2 = jnp.mean(dxh * xhat, axis=-1, keepdims=True)
        dr = rstd * (dxh - m1 - xhat * m2)
        dxr_ref[...] = DN_ALPHA * dr
        dout = dr.astype(MXU_DTYPE)
        mixv = mix_ref[...].astype(F32)
        parts = [mixv[:, :half], mixv[:, half:], mem_ref[...].astype(F32)]
        dcs = []
        for c, zr in enumerate((z0, z1, z2)):
            lo = c * half
            zv = zr[...].astype(F32)
            sg = _sigmoid(zv)
            sl = zv * sg
            dy = _dot(dout, wo_ref[:, lo:lo + half])
            y = (parts[c] * sl).astype(MXU_DTYPE)
            dw_acc[lo:lo + half, :] += _dot_tn(y, dout)
            dcs.append(dy * sl)
            dqz_ref[:, E_MEM + lo:E_MEM + lo + half] = (
                dy * parts[c] * (sg * (1.0 + zv * (1.0 - sg)))).astype(dqz_ref.dtype)
        dmix_ref[...] = jnp.concatenate(dcs[:2], axis=1).astype(dmix_ref.dtype)

        qm = qm_ref[...].astype(MXU_DTYPE)
        kvb = kv_ref[...].astype(MXU_DTYPE)
        dmb = dcs[2].astype(MXU_DTYPE)
        for hh in range(MEM_HEADS):
            lo = hh * MEM_HEAD_DIM
            qh = qm[:, lo:lo + MEM_HEAD_DIM]
            kh = kvb[:, lo:lo + MEM_HEAD_DIM]
            vh = kvb[:, E_MEM + lo:E_MEM + lo + MEM_HEAD_DIM]
            dmh = dmb[:, lo:lo + MEM_HEAD_DIM]
            p = _mem_probs(qh, kh)
            dp = _dot_nt(dmh, vh)
            ds = p * (dp - jnp.sum(p * dp, axis=-1, keepdims=True))
            dsb = (ds * inv).astype(MXU_DTYPE)
            dqz_ref[:, lo:lo + MEM_HEAD_DIM] = _dot(dsb, kh).astype(dqz_ref.dtype)
            dkv_ref[:, lo:lo + MEM_HEAD_DIM] += _dot_tn(dsb, qh)
            dkv_ref[:, E_MEM + lo:E_MEM + lo + MEM_HEAD_DIM] += _dot_tn(p.astype(MXU_DTYPE), dmh)

        @pl.when(i == nb - 1)
        def _():
            dwo_ref[...] = dw_acc[...].astype(dwo_ref.dtype)

    row = lambda w, c: pl.BlockSpec((ts, w), lambda i: (i, c))
    const = lambda shp: pl.BlockSpec(shp, lambda i: (0, 0))
    once = lambda shp: pl.BlockSpec(shp, lambda i: (0, 0), pipeline_mode=pl.Buffered(1))
    return _call(
        body, name="layer_out_bwd", grid=(nb,),
        in_specs=[row(d, 0), row(d, 0), const((1, d)), row(E_MIX, 0), row(E_MEM, 0),
                  row(E_MEM, QM_BLK), row(E_MEM, Z_BLK), row(E_MEM, Z_BLK + 1), row(E_MEM, Z_BLK + 2),
                  once((N_MEM, 2 * E_MEM)), once((d, E_BRANCH))],
        out_specs=[row(d, 0), row(E_MIX, 0), row(E_MEM + E_BRANCH, 0),
                   const((E_BRANCH, d)), const((N_MEM, 2 * E_MEM)), const((1, d)), const((1, d))],
        out_shape=[jax.ShapeDtypeStruct((s, d), F32), jax.ShapeDtypeStruct((s, E_MIX), BF16),
                   jax.ShapeDtypeStruct((s, E_MEM + E_BRANCH), BF16),
                   jax.ShapeDtypeStruct((E_BRANCH, d), BF16),
                   jax.ShapeDtypeStruct((N_MEM, 2 * E_MEM), F32),
                   jax.ShapeDtypeStruct((1, d), F32), jax.ShapeDtypeStruct((1, d), F32)],
        scratch_shapes=[pltpu.VMEM((E_BRANCH, d), F32)],
        compiler_params=_cparams("arbitrary"),
    )(dxn, r, g, mix, mem, h, h, h, h, kv, w_out_t)


def _local_step(x, mem, get_weights, put_grads, rel_bias, conv_w, ln_g, ln_b, target):
    biases = [_tile_bias(jnp.pad(rel_bias[a], ((0, 0), (0, N_REL_PAD - N_REL)))) for a in range(DEPTH // 2)]

    saved = []
    xl = x
    for layer in range(DEPTH):
        w_in_l, rest = get_weights(layer, xl if layer else biases)
        h, xt = _inproj(xl, w_in_l)
        w_kv_l, w_out_l = rest(h)
        if layer % 2 == 0:
            mix, aux = _attn_fwd(h, biases[layer // 2])
        else:
            aux = jnp.pad(conv_w()[layer // 2], ((0, SUBLANES - 3), (0, 0)))
            mix = _conv_fwd(h, aux)
        kv = _small_matmul(mem, w_kv_l, False, F32, "kv_mem")
        if layer < DEPTH - 1:
            x_next, r, mem_out = _layer_out_fwd(xl, h, mix, kv, w_out_l, ln_g[layer][None], ln_b[layer][None])
        else:
            dx, r, mem_out, loss = _layer_out_fwd(xl, h, mix, kv, w_out_l, ln_g[layer][None], ln_b[layer][None],
                                                  target)
            x_next = None
        saved.append((xt, h, aux, mix, kv, r, mem_out, w_in_l, w_out_l))
        xl = x_next

    dgs, dbs, d_rel, d_conv = [], [], [], []
    for layer in reversed(range(DEPTH)):
        xt, h, aux, mix, kv, r, mem_out, w_in_l, w_out_l = saved[layer]
        dx_res, d_mix, dqz, dwo, dkv, dg, db = _layer_out_bwd(
            dx, r, ln_g[layer][None], h, mix, mem_out, kv, w_out_l.T)
        early = put_grads(layer, [1, 2], [_small_matmul(mem, dkv, True, BF16, "dw_kv"), dwo])
        if layer % 2 == 0:
            dq, dk, dv, dtb = _attn_bwd(h, aux, d_mix, early)
            d_rel.append(_tile_bias_bwd(dtb)[:, :N_REL])
            pieces = [dq, dk, dv, dqz]
        else:
            dbg, dcg, du, dw8 = _conv_bwd(h, aux, d_mix)
            d_conv.append(dw8[:3])
            pieces = [dbg, dcg, du, dqz]
        token = put_grads(layer, [0], [_dw_matmul(xt, pieces)])
        dgs.append(dg[0])
        dbs.append(db[0])
        dx = _dx_matmul(pieces, w_in_l, dx_res, token)

    rev = lambda lst: jnp.stack(lst[::-1])
    return loss, dx, rev(d_rel), rev(d_conv), rev(dgs), rev(dbs)


def _me():
    return lax.axis_index("x"), lax.axis_index("y"), lax.axis_index("c")


def _peer(k):
    x, y, c = _me()
    kx, ky, kc = (k >> 2) & 1, (k >> 1) & 1, k & 1
    return (1 - x if kx else x, 1 - y if ky else y, 1 - c if kc else c)


def _lin(dev):
    return 4 * dev[0] + 2 * dev[1] + dev[2]


ANY = pl.BlockSpec(memory_space=pl.ANY)


def _exchange(srcs, dst_shapes, src_slice, dst_slice, name):
    na = len(srcs)

    def body(*refs):
        src_refs = refs[:na]
        dst_refs = refs[na:2 * na]
        send_sems, recv_sems, local_sems = refs[2 * na:]
        me = _lin(_me())
        copies = []
        for a in range(na):
            loc = pltpu.make_async_copy(src_slice(a, src_refs[a], me), dst_slice(a, dst_refs[a], me),
                                        local_sems.at[a])
            loc.start()
            copies.append(loc)
            for k in range(1, N_DEV):
                peer = _peer(k)
                cp = pltpu.make_async_remote_copy(
                    src_ref=src_slice(a, src_refs[a], _lin(peer)),
                    dst_ref=dst_slice(a, dst_refs[a], me),
                    send_sem=send_sems.at[a, k - 1], recv_sem=recv_sems.at[a, k - 1],
                    device_id=peer, device_id_type=pl.DeviceIdType.MESH)
                cp.start()
                copies.append(cp)
        for cp in copies:
            cp.wait()

    return _call(
        body, name=name,
        in_specs=[ANY] * na, out_specs=[ANY] * na,
        out_shape=[jax.ShapeDtypeStruct(shp, s.dtype) for shp, s in zip(dst_shapes, srcs)],
        scratch_shapes=[pltpu.SemaphoreType.DMA((na, N_DEV - 1)),
                        pltpu.SemaphoreType.DMA((na, N_DEV - 1)),
                        pltpu.SemaphoreType.DMA((na,))],
    )(*srcs)


def _gather_to_all(src, name):
    return _exchange([src], [(N_DEV,) + src.shape], lambda a, ref, p: ref,
                     lambda a, ref, me: ref.at[me], name)[0]


HBM = pl.BlockSpec(memory_space=pltpu.HBM)
SEM = pl.BlockSpec(memory_space=pltpu.SEMAPHORE)
EFFECT = pltpu.SideEffectType.DATAFLOW_SIDE_EFFECTING
N_PEER = N_DEV - 1
N_KIND = 3


def _peer_copies(kind, src_ref, land_ref, send, recv, src_slice, dst_slice):
    me = _lin(_me())
    copies = []
    for k in range(1, N_DEV):
        peer = _peer(k)
        copies.append(pltpu.make_async_remote_copy(
            src_ref=src_slice(kind, src_ref, _lin(peer)),
            dst_ref=dst_slice(kind, land_ref, me, k),
            send_sem=send.at[k - 1], recv_sem=recv.at[k - 1],
            device_id=peer, device_id_type=pl.DeviceIdType.MESH))
    return copies


def _own_copy(kind, src_ref, land_ref, send, src_slice, dst_slice):
    me = _lin(_me())
    return pltpu.make_async_copy(src_slice(kind, src_ref, me), dst_slice(kind, land_ref, me, 0),
                                 send.at[N_PEER])


def _split_start(srcs, kinds, land_shapes, src_slice, dst_slice, name, own=False):
    na = len(srcs)

    def body(*refs):
        src_refs, land_refs = refs[:na], refs[na:2 * na]
        sems = refs[2 * na:4 * na]
        token = refs[-1]
        for a in range(na):
            for cp in _peer_copies(kinds[a], src_refs[a], land_refs[a], sems[2 * a], sems[2 * a + 1],
                                   src_slice, dst_slice):
                cp.start()
            if own:
                _own_copy(kinds[a], src_refs[a], land_refs[a], sems[2 * a], src_slice, dst_slice).start()
        token[...] = jnp.zeros_like(token)

    sem_shape = pltpu.SemaphoreType.DMA((N_DEV,))
    lands = [lax.empty(shp, s.dtype) for shp, s in zip(land_shapes, srcs)]
    outs = _call(
        body, name=name,
        in_specs=[HBM] * (2 * na),
        out_specs=[SEM] * (2 * na) + [HBM] * (2 * na) + [pl.BlockSpec(memory_space=pltpu.VMEM)],
        out_shape=[sem_shape] * (2 * na)
        + [pltpu.HBM(s.shape, s.dtype) for s in srcs]
        + [pltpu.HBM(shp, s.dtype) for shp, s in zip(land_shapes, srcs)]
        + [jax.ShapeDtypeStruct((SUBLANES, LANES), F32)],
        input_output_aliases={i: 2 * na + i for i in range(2 * na)},
        compiler_params=pltpu.CompilerParams(has_side_effects=EFFECT),
    )(*[pltpu.with_memory_space_constraint(a, pltpu.HBM) for a in list(srcs) + lands])
    sems = [(outs[2 * a], outs[2 * a + 1]) for a in range(na)]
    thrus = outs[2 * na:3 * na]
    lands = outs[3 * na:4 * na]
    return sems, thrus, lands, outs[-1]


def _split_wait(sems, thrus, lands, kinds, src_slice, dst_slice, after, name, own=False):
    na = len(thrus)
    after = list(after) if isinstance(after, (list, tuple)) else [after]

    def body(*refs):
        src_refs, land_refs = refs[:na], refs[na:2 * na]
        sem_refs = refs[2 * na:4 * na]
        for a in range(na):
            for cp in _peer_copies(kinds[a], src_refs[a], land_refs[a], sem_refs[2 * a], sem_refs[2 * a + 1],
                                   src_slice, dst_slice):
                cp.wait_send()
                cp.wait_recv()
            if own:
                _own_copy(kinds[a], src_refs[a], land_refs[a], sem_refs[2 * a], src_slice, dst_slice).wait()

    outs = _call(
        body, name=name,
        in_specs=[HBM] * (2 * na) + [SEM] * (2 * na) + [ANY] * len(after),
        out_specs=[HBM] * (2 * na),
        out_shape=[pltpu.HBM(a.shape, a.dtype) for a in list(thrus) + list(lands)],
        input_output_aliases={i: i for i in range(2 * na)},
        compiler_params=pltpu.CompilerParams(has_side_effects=EFFECT),
    )(*thrus, *lands, *[s for pair in sems for s in pair], *after)
    return outs[na:]


def _shard_dims(c_in, r_kv, r_out):
    def sl(j, ref, p):
        if j == 0:
            return ref.at[:, pl.ds(pl.multiple_of(p * c_in, LANES), c_in)]
        r = r_kv if j == 1 else r_out
        return ref.at[pl.ds(pl.multiple_of(p * r, 2 * SUBLANES), r), :]
    return sl


def _adamw_math(w, g, m, v):
    m = ADAM_B1 * m + (1.0 - ADAM_B1) * g
    v = ADAM_B2 * v + (1.0 - ADAM_B2) * (g * g)
    m_hat = m / (1.0 - ADAM_B1 ** ADAM_STEP)
    v_hat = v / (1.0 - ADAM_B2 ** ADAM_STEP)
    delta = -ADAM_LR * (m_hat / (jnp.sqrt(v_hat) + ADAM_EPS) + ADAM_WD * w)
    return delta, m, v


def _reduce_adamw(parts, w, m, v, name):
    rows, cols = w.shape
    tr = rows
    for cand in (512, 256, 128, 64, 32, 16):
        if rows % cand == 0 and rows > cand:
            tr = cand
            break

    def body(p_ref, w_ref, m_ref, v_ref, g_out, d_out, m_out, v_out):
        g = p_ref[0].astype(F32)
        for s in range(1, N_DEV):
            g = g + p_ref[s].astype(F32)
        g_out[...] = g
        d_out[...], m_out[...], v_out[...] = _adamw_math(w_ref[...], g, m_ref[...], v_ref[...])

    blk = pl.BlockSpec((tr, cols), lambda i: (i, 0))
    return _call(
        body, name=name, grid=(rows // tr,),
        in_specs=[pl.BlockSpec((N_DEV, tr, cols), lambda i: (0, i, 0)), blk, blk, blk],
        out_specs=[blk] * 4,
        out_shape=[jax.ShapeDtypeStruct((rows, cols), F32)] * 4,
        compiler_params=_cparams("parallel"),
    )(parts, w, m, v)


def _reduce_adamw_layers(lands, owns, w, m, v, name, first=0, prev=None):
    depth, rows, cols = w.shape
    nl = len(lands)
    tr = rows
    for cand in (256, 192, 128):
        if rows % cand == 0:
            tr = cand
            break

    kept = [] if prev is None else list(prev)

    def body(*refs):
        land_refs, own_refs = refs[:nl], refs[nl:2 * nl]
        w_ref, m_ref, v_ref = refs[2 * nl:2 * nl + 3]
        g_out, d_out, m_out, v_out = refs[2 * nl + 3 + len(kept):]
        layer = pl.program_id(0)
        for a in range(nl):
            @pl.when(layer == a)
            def _(a=a):
                g = own_refs[a][...].astype(F32)
                for k in range(N_PEER):
                    g = g + land_refs[a][k].astype(F32)
                g_out[...] = g
                d_out[...], m_out[...], v_out[...] = _adamw_math(w_ref[...], g, m_ref[...], v_ref[...])

    def lmap(a):
        return lambda l, i: (0, jnp.where(l == a, i, 0), 0)

    def omap(a):
        return lambda l, i: (jnp.where(l == a, i, 0), 0)

    blk = pl.BlockSpec((None, tr, cols), lambda l, i: (l + first, i, 0))
    n_in = 2 * nl + 3
    return _call(
        body, name=name, grid=(nl, rows // tr),
        in_specs=[pl.BlockSpec((N_PEER, tr, cols), lmap(a)) for a in range(nl)]
        + [pl.BlockSpec((tr, cols), omap(a)) for a in range(nl)] + [blk, blk, blk] + [ANY] * len(kept),
        out_specs=[blk] * 4,
        out_shape=[jax.ShapeDtypeStruct((depth, rows, cols), F32)] * 4,
        input_output_aliases={n_in + i: i for i in range(len(kept))},
        compiler_params=_cparams("arbitrary", "arbitrary"),
    )(*lands, *owns, w, m, v, *kept)


SM_G, SM_B, SM_CONV, SM_REL = 0, 4, 8, 16
SM_ROWS = SM_REL + 2 * N_HEADS
REL_W = 384


def _pack_small(d_rel, d_conv, dg, db):
    buf = jnp.zeros((SM_ROWS, D_MODEL), F32)
    buf = buf.at[SM_G:SM_G + DEPTH].set(dg)
    buf = buf.at[SM_B:SM_B + DEPTH].set(db)
    buf = buf.at[SM_CONV:SM_CONV + 6].set(d_conv.reshape(6, E_MIX))
    buf = buf.at[SM_REL:, :N_REL].set(d_rel.reshape(2 * N_HEADS, N_REL))
    return buf


def kernel(x, mem, w_in, w_mem_kv, w_out, rel_bias, conv_w, ln_g, ln_b, loss_target, m_w_in, m_w_mem_kv, m_w_out, m_rel_bias, m_conv_w, m_ln_g, m_ln_b, v_w_in, v_w_mem_kv, v_w_out, v_rel_bias, v_conv_w, v_ln_g, v_ln_b):
    me = _lin(_me())
    c_in, r_kv, r_out, c_conv = w_in.shape[2], w_mem_kv.shape[1], w_out.shape[1], conv_w.shape[2]

    shard = _shard_dims(c_in, r_kv, r_out)
    own_start = lambda j: (0, me * c_in) if j == 0 else (me * (r_kv if j == 1 else r_out), 0)

    w_sh = [w_in.astype(BF16), w_mem_kv.astype(BF16), w_out.astype(BF16)]
    full_shapes = [(D_MODEL, N_DEV * c_in), (N_DEV * r_kv, w_mem_kv.shape[2]), (N_DEV * r_out, D_MODEL)]
    conv_kind = N_KIND
    conv_tile = jnp.pad(conv_w.reshape(6, c_conv), ((0, SUBLANES - 6), (0, 0)))
    ag_src = lambda j, ref, p: ref
    ag_dst = lambda j, ref, me_, k: ref.at[me_] if j == conv_kind else shard(j, ref, me_)
    kinds = list(range(N_KIND))
    ag_sems, ag_thrus, ag_lands, _ = _split_start(
        [conv_tile] + [w_sh[j][layer] for layer in range(DEPTH) for j in kinds], [conv_kind] + kinds * DEPTH,
        [(N_DEV,) + conv_tile.shape] + full_shapes * DEPTH, ag_src, ag_dst, "ag_start", own=True)
    conv_landed = []

    def get_weights(layer, x_layer):
        lo = 1 + layer * N_KIND

        def wait(idx, js, after, name):
            return _split_wait([ag_sems[a] for a in idx], [ag_thrus[a] for a in idx],
                               [ag_lands[a] for a in idx], js, ag_src, ag_dst, after, name, own=True)

        if layer == 0:
            conv_land, w_in_l = wait([0, lo], [conv_kind, 0], x_layer, "ag_wait_in_0")
            conv_landed.append(conv_land)
        else:
            w_in_l, = wait([lo], [0], x_layer, "ag_wait_in_%d" % layer)
        return w_in_l, lambda h: wait([lo + 1, lo + 2], [1, 2], h, "ag_wait_kv_out_%d" % layer)

    def conv_full():
        return jnp.transpose(conv_landed[0][:, :6], (1, 0, 2)).reshape(2, 3, N_DEV * c_conv)

    rs_src = shard
    rs_dst = lambda j, ref, me_, k: ref.at[k - 1]
    rs_shapes = [(N_PEER, D_MODEL, c_in), (N_PEER, r_kv, w_mem_kv.shape[2]), (N_PEER, r_out, D_MODEL)]
    own_sizes = [(D_MODEL, c_in), (r_kv, w_mem_kv.shape[2]), (r_out, D_MODEL)]
    pending = {}

    held = {}

    def put_grads(layer, js, arrays):
        if layer > 0 and js != [0]:
            held[layer] = (js, arrays)
            return None
        if layer > 0:
            js, arrays = held[layer][0] + js, held[layer][1] + arrays
        owns = [lax.dynamic_slice(a, own_start(j), own_sizes[j]) for j, a in zip(js, arrays)]
        sems, thrus, lands, token = _split_start(
            arrays, js, [rs_shapes[j] for j in js], rs_src, rs_dst,
            "rs_start_%d_%s" % (layer, "".join(str(j) for j in js)))
        entry = pending.setdefault(layer, ([], [], [], [], []))
        for lst, new in zip(entry, (js, sems, thrus, lands, owns)):
            lst.extend(new)
        return token


    loss, grad_x, d_rel, d_conv, dg, db = _local_step(
        x[0], mem[0], get_weights, put_grads, rel_bias, conv_full, ln_g, ln_b, loss_target[0])

    p_small = _gather_to_all(_pack_small(d_rel, d_conv, dg, db), "gather_small_grads")

    rs_lands, rs_owns = {}, {}

    def rs_wait(layer, want, after, name):
        js, sems, thrus, lands, owns = pending[layer]
        pos = [js.index(j) for j in want]
        got = _split_wait([sems[p] for p in pos], [thrus[p] for p in pos], [lands[p] for p in pos],
                          want, rs_src, rs_dst, after, name)
        for j, p, land in zip(want, pos, got):
            rs_lands[layer, j], rs_owns[layer, j] = land, owns[p]

    for layer in range(1, DEPTH):
        rs_wait(layer, kinds, grad_x, "rs_wait_%d" % layer)
    rs_wait(0, [1, 2], grad_x, "rs_wait_0_kv_out")

    def big(j, w, m, v, name, layers, prev=None):
        return _reduce_adamw_layers([rs_lands[layer, j] for layer in layers], [rs_owns[layer, j] for layer in layers],
                                    w, m, v, name, first=layers[0], prev=prev)

    every = list(range(DEPTH))
    g_kv, d_kv, nm_kv, nv_kv = big(1, w_mem_kv, m_w_mem_kv, v_w_mem_kv, "adamw_w_kv", every)
    g_out, d_out, nm_out, nv_out = big(2, w_out, m_w_out, v_w_out, "adamw_w_out", every)
    later = big(0, w_in, m_w_in, v_w_in, "adamw_w_in_later_layers", every[1:])

    def pack_state(rel, conv, g, b):
        conv_full = jnp.zeros((2, 3, E_MIX), F32)
        conv_full = lax.dynamic_update_slice(conv_full, conv, (0, 0, me * c_conv))
        return _pack_small(rel, conv_full, g, b)

    sm_w = pack_state(rel_bias, conv_w, ln_g, ln_b)
    sm_m = pack_state(m_rel_bias, m_conv_w, m_ln_g, m_ln_b)
    sm_v = pack_state(v_rel_bias, v_conv_w, v_ln_g, v_ln_b)
    sm_outs = _reduce_adamw(p_small, sm_w, sm_m, sm_v, "adamw_small")

    rs_wait(0, [0], [later[0], sm_outs[0], g_kv, g_out], "rs_wait_0_in")
    g_in, d_in, nm_in, nv_in = big(0, w_in, m_w_in, v_w_in, "adamw_w_in_layer_0", [0], prev=later)

    def unpack(buf):
        rel = buf[SM_REL:, :N_REL].reshape(2, N_HEADS, N_REL)
        conv = lax.dynamic_slice(buf[SM_CONV:SM_CONV + 6].reshape(2, 3, E_MIX), (0, 0, me * c_conv), (2, 3, c_conv))
        return rel, conv, buf[SM_G:SM_G + DEPTH], buf[SM_B:SM_B + DEPTH]

    g_sm, d_sm, nm_sm, nv_sm = [unpack(b) for b in sm_outs]

    loss = lax.psum(loss[0, 0], ("x", "y", "c"))
    return (loss, grad_x[None],
            g_in, g_kv, g_out, *g_sm,
            d_in, d_kv, d_out, *d_sm,
            nm_in, nm_kv, nm_out, *nm_sm,
            nv_in, nv_kv, nv_out, *nv_sm)
```

```python
import functools
import math

import jax
import jax.numpy as jnp
from jax import lax
from jax.experimental import pallas as pl
from jax.experimental.pallas import tpu as pltpu

F32 = jnp.float32
BF16 = jnp.bfloat16
MXU_DTYPE = jnp.bfloat16

N_DEV = 8
D_MODEL = 1024
DEPTH = 4
CHUNK = 64
N_PREV = 8
N_HEADS = 16
HEAD_DIM = 64
E_MIX = 1024
REL_CLIP = 128
N_REL = 2 * REL_CLIP + 1
N_REL_PAD = 384
N_MEM = 256
MEM_HEADS = 4
MEM_HEAD_DIM = 128
E_MEM = 512
E_BRANCH = E_MIX + E_MEM
N_IN = 3 * E_MIX + E_MEM + E_BRANCH
DN_ALPHA = (2.0 * DEPTH) ** 0.25
LN_EPS = 1e-5
NEG = -1e30

ADAM_LR = 0.001
ADAM_B1 = 0.9
ADAM_B2 = 0.999
ADAM_EPS = 1e-08
ADAM_WD = 0.01
ADAM_STEP = 10

LANES = 128
SUBLANES = 8
VMEM_LIMIT = 56 * 1024 * 1024

TQ = 4 * CHUNK
TKEYS = 3 * TQ
ROLL_W = 1024
TS = 256
QM_BLK = 3 * E_MIX // E_MEM
Z_BLK = QM_BLK + 1


def _call(body, **kw):
    return pl.pallas_call(body, **kw)


def _cparams(*sem):
    return pltpu.CompilerParams(dimension_semantics=sem, vmem_limit_bytes=VMEM_LIMIT)


def _dot(a, b):
    return jnp.dot(a, b, preferred_element_type=F32)


def _dot_nt(a, b):
    return lax.dot_general(a, b, (((1,), (1,)), ((), ())), preferred_element_type=F32)


def _dot_tn(a, b):
    return lax.dot_general(a, b, (((0,), (0,)), ((), ())), preferred_element_type=F32)


def _inproj(x, w):
    s, d = x.shape
    n = w.shape[1]
    tm = 1024
    tn = 1024

    def body(x_ref, w_ref, o_ref, xt_ref):
        xb = x_ref[...].astype(xt_ref.dtype)
        xt_ref[...] = xb.T
        for j in range(n // tn):
            o_ref[:, j * tn:(j + 1) * tn] = _dot(xb, w_ref[:, j * tn:(j + 1) * tn]).astype(o_ref.dtype)

    return _call(
        body, name="inproj", grid=(s // tm,),
        in_specs=[pl.BlockSpec((tm, d), lambda i: (i, 0)),
                  pl.BlockSpec((d, n), lambda i: (0, 0), pipeline_mode=pl.Buffered(1))],
        out_specs=[pl.BlockSpec((tm, n), lambda i: (i, 0)),
                   pl.BlockSpec((d, tm), lambda i: (0, i))],
        out_shape=[jax.ShapeDtypeStruct((s, n), BF16), jax.ShapeDtypeStruct((d, s), BF16)],
        compiler_params=_cparams("parallel"),
    )(x, w)


def _small_matmul(a, b, trans_a, out_dtype, name):
    m = a.shape[1] if trans_a else a.shape[0]
    n = b.shape[1]

    def body(a_ref, b_ref, o_ref):
        av = a_ref[...].astype(MXU_DTYPE)
        bv = b_ref[...].astype(MXU_DTYPE)
        r = _dot_tn(av, bv) if trans_a else _dot(av, bv)
        o_ref[...] = r.astype(out_dtype)

    return _call(
        body, name=name,
        in_specs=[pl.BlockSpec(memory_space=pltpu.VMEM)] * 2,
        out_specs=pl.BlockSpec(memory_space=pltpu.VMEM),
        out_shape=jax.ShapeDtypeStruct((m, n), out_dtype),
        compiler_params=pltpu.CompilerParams(vmem_limit_bytes=VMEM_LIMIT),
    )(a, b)


def _piece_blocks(pieces, blk):
    offs, nbs, o = [], [], 0
    for p in pieces:
        nb = p.shape[1] // blk
        offs.append(o)
        nbs.append(nb)
        o += nb
    return offs, nbs, o


def _dx_matmul(pieces, w, addend, token=None):
    s = pieces[0].shape[0]
    d, n_in = w.shape
    tm = 512
    tw = 1024
    np_ = len(pieces)
    extra = [t for t in (token if isinstance(token, (list, tuple)) else [token]) if t is not None]

    def body(*refs):
        a_refs = refs[:np_]
        w_ref, add_ref = refs[np_:np_ + 2]
        o_ref, wt = refs[-2:]

        @pl.when(pl.program_id(0) == 0)
        def _():
            for j in range(n_in // tw):
                wt[j * tw:(j + 1) * tw, :] = w_ref[:, j * tw:(j + 1) * tw].T

        a = jnp.concatenate([r[...] for r in a_refs], axis=1)
        o_ref[...] = add_ref[...] + _dot(a, wt[...])

    in_specs = [pl.BlockSpec((tm, p.shape[1]), lambda i: (i, 0)) for p in pieces]
    in_specs += [pl.BlockSpec((d, n_in), lambda i: (0, 0), pipeline_mode=pl.Buffered(1)),
                 pl.BlockSpec((tm, d), lambda i: (i, 0))]
    in_specs += [pl.BlockSpec((SUBLANES, LANES), lambda i: (0, 0)) for _ in extra]
    return _call(
        body, name="dx_matmul", grid=(s // tm,),
        in_specs=in_specs,
        out_specs=pl.BlockSpec((tm, d), lambda i: (i, 0)),
        out_shape=jax.ShapeDtypeStruct((s, d), F32),
        scratch_shapes=[pltpu.VMEM((n_in, d), w.dtype)],
        compiler_params=_cparams("arbitrary"),
    )(*pieces, w, addend, *extra)


def _dw_matmul(xt, pieces):
    d, s = xt.shape
    tn = 256
    offs, nbs, nj = _piece_blocks(pieces, tn)
    np_ = len(pieces)

    def body(*refs):
        x_ref = refs[0]
        b_refs = refs[1:1 + np_]
        o_ref = refs[1 + np_]
        j = pl.program_id(0)
        for p in range(np_):
            @pl.when((j >= offs[p]) & (j < offs[p] + nbs[p]))
            def _(p=p):
                o_ref[...] = _dot(x_ref[...], b_refs[p][...]).astype(o_ref.dtype)

    def bmap(p):
        return lambda j: (0, jnp.clip(j - offs[p], 0, nbs[p] - 1))

    in_specs = [pl.BlockSpec((d, s), lambda j: (0, 0), pipeline_mode=pl.Buffered(1))]
    in_specs += [pl.BlockSpec((s, tn), bmap(p)) for p in range(np_)]
    return _call(
        body, name="dw_matmul", grid=(nj,),
        in_specs=in_specs,
        out_specs=pl.BlockSpec((d, tn), lambda j: (0, j)),
        out_shape=jax.ShapeDtypeStruct((d, nj * tn), BF16),
        compiler_params=_cparams("parallel"),
    )(xt, *pieces)


def _rel_onehot():
    j = lax.broadcasted_iota(jnp.int32, (N_REL_PAD, ROLL_W), 1)
    kk = lax.broadcasted_iota(jnp.int32, (N_REL_PAD, ROLL_W), 0)
    dd = jnp.where(j < TKEYS, j, j - ROLL_W)
    idx = jnp.clip(N_PREV * CHUNK - dd, -REL_CLIP, REL_CLIP) + REL_CLIP
    return jnp.where(idx == kk, 1.0, 0.0).astype(F32)


def _band_mask():
    r = lax.broadcasted_iota(jnp.int32, (TQ, TKEYS), 0) // CHUNK
    m = lax.broadcasted_iota(jnp.int32, (TQ, TKEYS), 1) // CHUNK
    return (m >= r) & (m <= r + N_PREV)


def _tile_bias(table_pad):
    def body(t_ref, o_ref):
        g = jnp.dot(t_ref[...], _rel_onehot(), preferred_element_type=F32,
                    precision=lax.Precision.HIGHEST)
        band = _band_mask()
        for h in range(N_HEADS):
            gh = jnp.broadcast_to(g[h:h + 1, :], (TQ, ROLL_W))
            rolled = pltpu.roll(gh, 0, 1, stride=1, stride_axis=0)
            o_ref[h] = jnp.where(band, rolled[:, :TKEYS], NEG)

    return _call(
        body, name="tile_bias",
        in_specs=[pl.BlockSpec(memory_space=pltpu.VMEM)],
        out_specs=pl.BlockSpec(memory_space=pltpu.VMEM),
        out_shape=jax.ShapeDtypeStruct((N_HEADS, TQ, TKEYS), F32),
        compiler_params=pltpu.CompilerParams(vmem_limit_bytes=VMEM_LIMIT),
    )(table_pad)


def _tile_bias_bwd(dtb):
    def body(d_ref, o_ref, g_ref):
        zpad = jnp.zeros((TQ, ROLL_W - TKEYS), F32)
        rr = lax.broadcasted_iota(jnp.int32, (TQ, TQ), 0)
        cc = lax.broadcasted_iota(jnp.int32, (TQ, TQ), 1)
        flip = jnp.where(rr + cc == TQ - 1, 1.0, 0.0).astype(F32)
        for h in range(N_HEADS):
            xh = jnp.concatenate([d_ref[h], zpad], axis=1)
            xf = jnp.dot(flip, xh, preferred_element_type=F32, precision=lax.Precision.HIGHEST)
            rolled = pltpu.roll(xf, 0, 1, stride=1, stride_axis=0)
            g_ref[h:h + 1, :] = jnp.sum(rolled, axis=0, keepdims=True)
        g = pltpu.roll(g_ref[...], ROLL_W - (TQ - 1), 1)
        o_ref[...] = lax.dot_general(g, _rel_onehot(), (((1,), (1,)), ((), ())),
                                     preferred_element_type=F32, precision=lax.Precision.HIGHEST)

    return _call(
        body, name="tile_bias_bwd",
        in_specs=[pl.BlockSpec(memory_space=pltpu.VMEM)],
        out_specs=pl.BlockSpec(memory_space=pltpu.VMEM),
        out_shape=jax.ShapeDtypeStruct((N_HEADS, N_REL_PAD), F32),
        scratch_shapes=[pltpu.VMEM((N_HEADS, ROLL_W), F32)],
        compiler_params=pltpu.CompilerParams(vmem_limit_bytes=VMEM_LIMIT),
    )(dtb)


HB = 4
HBW = HB * HEAD_DIM
ATTN_SCALE = 0.125
assert ATTN_SCALE == 1.0 / math.sqrt(HEAD_DIM)


def _head_masks():
    lane = lax.broadcasted_iota(jnp.int32, (1, HBW), 1) // HEAD_DIM
    return [lane == hh for hh in range(HB)]


def _select_heads(masks, parts):
    out = parts[-1]
    for hh in range(HB - 2, -1, -1):
        out = jnp.where(masks[hh], parts[hh], out)
    return out


def _attn_probs(qm, kcat, tb, valid):
    s = _dot_nt(qm, kcat) + tb
    if valid is not None:
        s = jnp.where(valid, s, NEG)
    m = jnp.max(s, axis=-1, keepdims=True)
    e = jnp.exp(s - m)
    return e * (1.0 / jnp.sum(e, axis=-1, keepdims=True))


def _key_valid(i):
    col = lax.broadcasted_iota(jnp.int32, (TQ, TKEYS), 1)
    return col >= jnp.maximum(2 - i, 0) * TQ


def _kv_specs(col0, nt):
    def spec(back):
        return pl.BlockSpec((TQ, HBW), lambda hp, i: (jnp.clip(i - back, 0, nt - 1), col0 + hp))
    return [spec(2), spec(1), spec(0)]


def _attn_fwd(h, tb):
    s = h.shape[0]
    nt = s // TQ
    nhp = N_HEADS // HB

    def body(q_ref, k0, k1, k2, v0, v1, v2, tb_ref, o_ref, p_ref):
        i = pl.program_id(1)

        def tile(valid):
            masks = _head_masks()
            qs = q_ref[...].astype(MXU_DTYPE) * ATTN_SCALE
            kcat = jnp.concatenate([k0[...], k1[...], k2[...]], axis=0).astype(MXU_DTYPE)
            vcat = jnp.concatenate([v0[...], v1[...], v2[...]], axis=0).astype(MXU_DTYPE)
            outs = []
            for hh in range(HB):
                qm = jnp.where(masks[hh], qs, jnp.zeros_like(qs))
                pb = _attn_probs(qm, kcat, tb_ref[hh], valid).astype(MXU_DTYPE)
                p_ref[hh] = pb.astype(p_ref.dtype)
                outs.append(_dot(pb, vcat))
            o_ref[...] = _select_heads(masks, outs).astype(o_ref.dtype)

        @pl.when(i < 2)
        def _():
            tile(_key_valid(i))

        @pl.when(i >= 2)
        def _():
            tile(None)

    in_specs = [pl.BlockSpec((TQ, HBW), lambda hp, i: (i, hp))]
    in_specs += _kv_specs(nhp, nt) + _kv_specs(2 * nhp, nt)
    in_specs += [pl.BlockSpec((HB, TQ, TKEYS), lambda hp, i: (hp, 0, 0))]
    return _call(
        body, name="attn_fwd", grid=(nhp, nt),
        in_specs=in_specs,
        out_specs=[pl.BlockSpec((TQ, HBW), lambda hp, i: (i, hp)),
                   pl.BlockSpec((HB, TQ, TKEYS), lambda hp, i: (hp, i, 0))],
        out_shape=[jax.ShapeDtypeStruct((s, E_MIX), BF16), jax.ShapeDtypeStruct((N_HEADS, s, TKEYS), BF16)],
        compiler_params=_cparams("parallel", "parallel"),
    )(h, h, h, h, h, h, h, tb)


def _attn_bwd(h, probs, d_mix, token=None):
    s = h.shape[0]
    nt = s // TQ
    nhp = N_HEADS // HB
    extra = [] if token is None else [token]

    def body(q_ref, k0, k1, k2, v0, v1, v2, p_ref, do_ref, *rest):
        dq_ref, dk_ref, dv_ref, dtb_ref, dk_acc, dv_acc = rest[len(extra):]
        i = pl.program_id(1)

        @pl.when(i == 0)
        def _():
            dk_acc[...] = jnp.zeros_like(dk_acc)
            dv_acc[...] = jnp.zeros_like(dv_acc)
            dtb_ref[...] = jnp.zeros_like(dtb_ref)

        @pl.when((i > 0) & (i < nt))
        def _():
            dk_acc[i % 3] = jnp.zeros((TQ, HBW), F32)
            dv_acc[i % 3] = jnp.zeros((TQ, HBW), F32)

        @pl.when(i < nt)
        def _():
            masks = _head_masks()
            qs = q_ref[...].astype(MXU_DTYPE) * ATTN_SCALE
            do2 = do_ref[...].astype(MXU_DTYPE)
            kcat = jnp.concatenate([k0[...], k1[...], k2[...]], axis=0).astype(MXU_DTYPE)
            vcat = jnp.concatenate([v0[...], v1[...], v2[...]], axis=0).astype(MXU_DTYPE)
            ks = kcat * ATTN_SCALE
            dqs, dks, dvs = [], [], []
            for hh in range(HB):
                dom = jnp.where(masks[hh], do2, jnp.zeros_like(do2))
                pb = p_ref[hh]
                p = pb.astype(F32)
                dp = _dot_nt(dom, vcat)
                ds = p * (dp - jnp.sum(p * dp, axis=-1, keepdims=True))
                dtb_ref[hh] += ds
                dsb = ds.astype(MXU_DTYPE)
                dqs.append(_dot(dsb, ks))
                dks.append(_dot_tn(dsb, qs))
                dvs.append(_dot_tn(pb.astype(MXU_DTYPE), do2))
            dq_ref[...] = _select_heads(masks, dqs).astype(dq_ref.dtype)
            dkc = _select_heads(masks, dks)
            dvc = _select_heads(masks, dvs)
            for jj in range(3):
                slot = (i + 1 + jj) % 3
                dk_acc[slot] += dkc[jj * TQ:(jj + 1) * TQ]
                dv_acc[slot] += dvc[jj * TQ:(jj + 1) * TQ]

        @pl.when(i >= 2)
        def _():
            slot = (i - 2) % 3
            dk_ref[...] = dk_acc[slot].astype(dk_ref.dtype)
            dv_ref[...] = dv_acc[slot].astype(dv_ref.dtype)

    qmap = lambda hp, i: (jnp.minimum(i, nt - 1), hp)
    kvout = lambda hp, i: (jnp.maximum(i - 2, 0), hp)
    in_specs = [pl.BlockSpec((TQ, HBW), qmap)]
    in_specs += _kv_specs(nhp, nt) + _kv_specs(2 * nhp, nt)
    in_specs += [pl.BlockSpec((HB, TQ, TKEYS), lambda hp, i: (hp, jnp.minimum(i, nt - 1), 0)),
                 pl.BlockSpec((TQ, HBW), qmap)]
    in_specs += [pl.BlockSpec((SUBLANES, LANES), lambda hp, i: (0, 0)) for _ in extra]
    blk = (TQ, HBW)
    return _call(
        body, name="attn_bwd", grid=(nhp, nt + 2),
        in_specs=in_specs,
        out_specs=[pl.BlockSpec(blk, qmap), pl.BlockSpec(blk, kvout), pl.BlockSpec(blk, kvout),
                   pl.BlockSpec((HB, TQ, TKEYS), lambda hp, i: (hp, 0, 0))],
        out_shape=[jax.ShapeDtypeStruct((s, E_MIX), BF16)] * 3
        + [jax.ShapeDtypeStruct((N_HEADS, TQ, TKEYS), F32)],
        scratch_shapes=[pltpu.VMEM((3, TQ, HBW), F32)] * 2,
        compiler_params=_cparams("parallel", "arbitrary"),
    )(h, h, h, h, h, h, h, probs, d_mix, *extra)


CONV_TS = 512
HALO = 2 * SUBLANES


def _shift_down(prev, cur, k):
    rolled = pltpu.roll(cur, k, 0)
    row = lax.broadcasted_iota(jnp.int32, (HALO, cur.shape[1]), 0)
    top = jnp.where(row < k, pltpu.roll(prev, k, 0), rolled[:HALO])
    return jnp.concatenate([top, rolled[HALO:]], axis=0)


def _shift_up(cur, nxt, k):
    ts = cur.shape[0]
    rolled = pltpu.roll(cur, ts - k, 0)
    row = lax.broadcasted_iota(jnp.int32, (HALO, cur.shape[1]), 0)
    bottom = jnp.where(row >= HALO - k, pltpu.roll(nxt, HALO - k, 0), rolled[ts - HALO:])
    return jnp.concatenate([rolled[:ts - HALO], bottom], axis=0)


def _conv_specs(ts, nb):
    tile = lambda c: pl.BlockSpec((ts, E_MIX), lambda i: (i, c))
    prev = lambda c: pl.BlockSpec((HALO, E_MIX), lambda i: (jnp.maximum(i * (ts // HALO) - 1, 0), c))
    return tile, prev


def _conv_fwd(h, w8):
    s = h.shape[0]
    ts = CONV_TS
    nb = s // ts
    tile, prev = _conv_specs(ts, nb)

    def body(bg, cg, u, cgp, up, w_ref, o_ref):
        i = pl.program_id(0)
        a = cg[...].astype(F32) * u[...].astype(F32)
        ap = jnp.where(i > 0, cgp[...].astype(F32) * up[...].astype(F32), 0.0)
        w = w_ref[...]
        conv = w[0:1] * _shift_down(ap, a, 2) + w[1:2] * _shift_down(ap, a, 1) + w[2:3] * a
        o_ref[...] = (bg[...].astype(F32) * conv).astype(o_ref.dtype)

    return _call(
        body, name="conv_fwd", grid=(nb,),
        in_specs=[tile(0), tile(1), tile(2), prev(1), prev(2),
                  pl.BlockSpec((SUBLANES, E_MIX), lambda i: (0, 0))],
        out_specs=pl.BlockSpec((ts, E_MIX), lambda i: (i, 0)),
        out_shape=jax.ShapeDtypeStruct((s, E_MIX), BF16),
        compiler_params=_cparams("parallel"),
    )(h, h, h, h, h, w8)


def _conv_bwd(h, w8, d_mix):
    s = h.shape[0]
    ts = CONV_TS
    nb = s // ts
    tile, prev = _conv_specs(ts, nb)
    nrow = s // HALO
    nxt = lambda c: pl.BlockSpec((HALO, E_MIX), lambda i: (jnp.minimum((i + 1) * (ts // HALO), nrow - 1), c))

    def body(bg, cg, u, cgp, up, bgn, dmix, dmixn, w_ref, dbg_ref, dcg_ref, du_ref, dw_ref):
        i = pl.program_id(0)

        @pl.when(i == 0)
        def _():
            dw_ref[...] = jnp.zeros_like(dw_ref)

        cgv, uv = cg[...].astype(F32), u[...].astype(F32)
        a = cgv * uv
        ap = jnp.where(i > 0, cgp[...].astype(F32) * up[...].astype(F32), 0.0)
        a1 = _shift_down(ap, a, 1)
        a2 = _shift_down(ap, a, 2)
        w = w_ref[...]
        conv = w[0:1] * a2 + w[1:2] * a1 + w[2:3] * a
        dm = dmix[...].astype(F32)
        dbg_ref[...] = (dm * conv).astype(dbg_ref.dtype)
        dc = dm * bg[...].astype(F32)
        dcn = jnp.where(i < nb - 1, dmixn[...].astype(F32) * bgn[...].astype(F32), 0.0)
        da = w[2:3] * dc + w[1:2] * _shift_up(dc, dcn, 1) + w[0:1] * _shift_up(dc, dcn, 2)
        dcg_ref[...] = (da * uv).astype(dcg_ref.dtype)
        du_ref[...] = (da * cgv).astype(du_ref.dtype)
        dw_ref[0:1, :] += jnp.sum(dc * a2, axis=0, keepdims=True)
        dw_ref[1:2, :] += jnp.sum(dc * a1, axis=0, keepdims=True)
        dw_ref[2:3, :] += jnp.sum(dc * a, axis=0, keepdims=True)

    full = lambda: pl.BlockSpec((ts, E_MIX), lambda i: (i, 0))
    return _call(
        body, name="conv_bwd", grid=(nb,),
        in_specs=[tile(0), tile(1), tile(2), prev(1), prev(2), nxt(0),
                  tile(0), pl.BlockSpec((HALO, E_MIX), lambda i: (jnp.minimum((i + 1) * (ts // HALO), nrow - 1), 0)),
                  pl.BlockSpec((SUBLANES, E_MIX), lambda i: (0, 0))],
        out_specs=[full(), full(), full(), pl.BlockSpec((SUBLANES, E_MIX), lambda i: (0, 0))],
        out_shape=[jax.ShapeDtypeStruct((s, E_MIX), BF16)] * 3
        + [jax.ShapeDtypeStruct((SUBLANES, E_MIX), F32)],
        compiler_params=_cparams("arbitrary"),
    )(h, h, h, h, h, h, d_mix, d_mix, w8)


def _mem_probs(qh, kh):
    s = _dot_nt(qh, kh) / math.sqrt(MEM_HEAD_DIM)
    m = jnp.max(s, axis=-1, keepdims=True)
    e = jnp.exp(s - m)
    return e / jnp.sum(e, axis=-1, keepdims=True)


def _sigmoid(z):
    return 1.0 / (1.0 + jnp.exp(-z))


def _layer_out_fwd(x, h, mix, kv, w_out, g, b, target=None):
    s, d = x.shape
    ts = 2 * TS
    last = target is not None

    def body(x_ref, mix_ref, qm_ref, z0, z1, z2, kv_ref, wo_ref, g_ref, b_ref, *rest):
        xn_ref, r_ref, mem_ref = rest[last:last + 3]
        qm = qm_ref[...].astype(MXU_DTYPE)
        kvb = kv_ref[...].astype(MXU_DTYPE)
        mems = []
        for hh in range(MEM_HEADS):
            lo = hh * MEM_HEAD_DIM
            p = _mem_probs(qm[:, lo:lo + MEM_HEAD_DIM], kvb[:, lo:lo + MEM_HEAD_DIM])
            mems.append(_dot(p.astype(MXU_DTYPE), kvb[:, E_MEM + lo:E_MEM + lo + MEM_HEAD_DIM]))
        mem = jnp.concatenate(mems, axis=1).astype(mem_ref.dtype)
        mem_ref[...] = mem
        mixv = mix_ref[...].astype(F32)
        half = E_MIX // 2
        parts = [mixv[:, :half], mixv[:, half:], mem.astype(F32)]
        out = jnp.zeros((ts, d), F32)
        for c, zr in enumerate((z0, z1, z2)):
            zv = zr[...].astype(F32)
            y = (parts[c] * (zv * _sigmoid(zv))).astype(MXU_DTYPE)
            out += _dot(y, wo_ref[c * half:(c + 1) * half, :])
        r = DN_ALPHA * x_ref[...] + out
        r_ref[...] = r
        mu = jnp.mean(r, axis=-1, keepdims=True)
        rc = r - mu
        var = jnp.mean(rc * rc, axis=-1, keepdims=True)
        xn = rc * lax.rsqrt(var + LN_EPS) * g_ref[...] + b_ref[...]
        if not last:
            xn_ref[...] = xn
        else:
            t_ref, l_ref = rest[0], rest[4]

            @pl.when(pl.program_id(0) == 0)
            def _():
                l_ref[...] = jnp.zeros_like(l_ref)

            e = xn - t_ref[...]
            xn_ref[...] = e * (1.0 / d)
            l_ref[...] += (0.5 / d) * jnp.sum(jnp.sum(e * e, axis=1, keepdims=True), axis=0, keepdims=True)

    row = lambda w, c: pl.BlockSpec((ts, w), lambda i: (i, c))
    const = lambda shp: pl.BlockSpec(shp, lambda i: (0, 0))
    tail = [target] if last else []
    return _call(
        body, name="layer_out_fwd_loss" if last else "layer_out_fwd", grid=(s // ts,),
        in_specs=[row(d, 0), row(E_MIX, 0), row(E_MEM, QM_BLK),
                  row(E_MEM, Z_BLK), row(E_MEM, Z_BLK + 1), row(E_MEM, Z_BLK + 2),
                  const((N_MEM, 2 * E_MEM)), const((E_BRANCH, d)), const((1, d)), const((1, d))]
        + [row(d, 0) for _ in tail],
        out_specs=[row(d, 0), row(d, 0), row(E_MEM, 0)] + [const((1, 1)) for _ in tail],
        out_shape=[jax.ShapeDtypeStruct((s, d), F32), jax.ShapeDtypeStruct((s, d), F32),
                   jax.ShapeDtypeStruct((s, E_MEM), BF16)] + [jax.ShapeDtypeStruct((1, 1), F32) for _ in tail],
        compiler_params=_cparams("arbitrary" if last else "parallel"),
    )(x, mix, h, h, h, h, kv, w_out, g, b, *tail)


def _layer_out_bwd(dxn, r, g, h, mix, mem, kv, w_out_t):
    s, d = r.shape
    ts = 2 * TS
    nb = s // ts
    half = E_MIX // 2
    inv = 1.0 / math.sqrt(MEM_HEAD_DIM)

    def body(dxn_ref, r_ref, g_ref, mix_ref, mem_ref, qm_ref, z0, z1, z2, kv_ref, wo_ref,
             dxr_ref, dmix_ref, dqz_ref, dwo_ref, dkv_ref, dg_ref, db_ref, dw_acc):
        i = pl.program_id(0)

        @pl.when(i == 0)
        def _():
            dw_acc[...] = jnp.zeros_like(dw_acc)
            dkv_ref[...] = jnp.zeros_like(dkv_ref)
            dg_ref[...] = jnp.zeros_like(dg_ref)
            db_ref[...] = jnp.zeros_like(db_ref)

        dxn_v = dxn_ref[...]
        rv = r_ref[...]
        mu = jnp.mean(rv, axis=-1, keepdims=True)
        rc = rv - mu
        var = jnp.mean(rc * rc, axis=-1, keepdims=True)
        rstd = lax.rsqrt(var + LN_EPS)
        xhat = rc * rstd
        dg_ref[...] += jnp.sum(dxn_v * xhat, axis=0, keepdims=True)
        db_ref[...] += jnp.sum(dxn_v, axis=0, keepdims=True)
        dxh = dxn_v * g_ref[...]
        m1 = jnp.mean(dxh, axis=-1, keepdims=True)
        m2 = jnp.mean(dxh * xhat, axis=-1, keepdims=True)
        dr = rstd * (dxh - m1 - xhat * m2)
        dxr_ref[...] = DN_ALPHA * dr
        dout = dr.astype(MXU_DTYPE)
        mixv = mix_ref[...].astype(F32)
        parts = [mixv[:, :half], mixv[:, half:], mem_ref[...].astype(F32)]
        dcs = []
        for c, zr in enumerate((z0, z1, z2)):
            lo = c * half
            zv = zr[...].astype(F32)
            sg = _sigmoid(zv)
            sl = zv * sg
            dy = _dot(dout, wo_ref[:, lo:lo + half])
            y = (parts[c] * sl).astype(MXU_DTYPE)
            dw_acc[lo:lo + half, :] += _dot_tn(y, dout)
            dcs.append(dy * sl)
            dqz_ref[:, E_MEM + lo:E_MEM + lo + half] = (
                dy * parts[c] * (sg * (1.0 + zv * (1.0 - sg)))).astype(dqz_ref.dtype)
        dmix_ref[...] = jnp.concatenate(dcs[:2], axis=1).astype(dmix_ref.dtype)

        qm = qm_ref[...].astype(MXU_DTYPE)
        kvb = kv_ref[...].astype(MXU_DTYPE)
        dmb = dcs[2].astype(MXU_DTYPE)
        for hh in range(MEM_HEADS):
            lo = hh * MEM_HEAD_DIM
            qh = qm[:, lo:lo + MEM_HEAD_DIM]
            kh = kvb[:, lo:lo + MEM_HEAD_DIM]
            vh = kvb[:, E_MEM + lo:E_MEM + lo + MEM_HEAD_DIM]
            dmh = dmb[:, lo:lo + MEM_HEAD_DIM]
            p = _mem_probs(qh, kh)
            dp = _dot_nt(dmh, vh)
            ds = p * (dp - jnp.sum(p * dp, axis=-1, keepdims=True))
            dsb = (ds * inv).astype(MXU_DTYPE)
            dqz_ref[:, lo:lo + MEM_HEAD_DIM] = _dot(dsb, kh).astype(dqz_ref.dtype)
            dkv_ref[:, lo:lo + MEM_HEAD_DIM] += _dot_tn(dsb, qh)
            dkv_ref[:, E_MEM + lo:E_MEM + lo + MEM_HEAD_DIM] += _dot_tn(p.astype(MXU_DTYPE), dmh)

        @pl.when(i == nb - 1)
        def _():
            dwo_ref[...] = dw_acc[...].astype(dwo_ref.dtype)

    row = lambda w, c: pl.BlockSpec((ts, w), lambda i: (i, c))
    const = lambda shp: pl.BlockSpec(shp, lambda i: (0, 0))
    once = lambda shp: pl.BlockSpec(shp, lambda i: (0, 0), pipeline_mode=pl.Buffered(1))
    return _call(
        body, name="layer_out_bwd", grid=(nb,),
        in_specs=[row(d, 0), row(d, 0), const((1, d)), row(E_MIX, 0), row(E_MEM, 0),
                  row(E_MEM, QM_BLK), row(E_MEM, Z_BLK), row(E_MEM, Z_BLK + 1), row(E_MEM, Z_BLK + 2),
                  once((N_MEM, 2 * E_MEM)), once((d, E_BRANCH))],
        out_specs=[row(d, 0), row(E_MIX, 0), row(E_MEM + E_BRANCH, 0),
                   const((E_BRANCH, d)), const((N_MEM, 2 * E_MEM)), const((1, d)), const((1, d))],
        out_shape=[jax.ShapeDtypeStruct((s, d), F32), jax.ShapeDtypeStruct((s, E_MIX), BF16),
                   jax.ShapeDtypeStruct((s, E_MEM + E_BRANCH), BF16),
                   jax.ShapeDtypeStruct((E_BRANCH, d), BF16),
                   jax.ShapeDtypeStruct((N_MEM, 2 * E_MEM), F32),
                   jax.ShapeDtypeStruct((1, d), F32), jax.ShapeDtypeStruct((1, d), F32)],
        scratch_shapes=[pltpu.VMEM((E_BRANCH, d), F32)],
        compiler_params=_cparams("arbitrary"),
    )(dxn, r, g, mix, mem, h, h, h, h, kv, w_out_t)


def _local_step(x, mem, get_weights, put_grads, rel_bias, conv_w, ln_g, ln_b, target):
    biases = [_tile_bias(jnp.pad(rel_bias[a], ((0, 0), (0, N_REL_PAD - N_REL)))) for a in range(DEPTH // 2)]

    saved = []
    xl = x
    for layer in range(DEPTH):
        w_in_l, rest = get_weights(layer, xl if layer else biases)
        h, xt = _inproj(xl, w_in_l)
        w_kv_l, w_out_l = rest(h)
        if layer % 2 == 0:
            mix, aux = _attn_fwd(h, biases[layer // 2])
        else:
            aux = jnp.pad(conv_w()[layer // 2], ((0, SUBLANES - 3), (0, 0)))
            mix = _conv_fwd(h, aux)
        kv = _small_matmul(mem, w_kv_l, False, F32, "kv_mem")
        if layer < DEPTH - 1:
            x_next, r, mem_out = _layer_out_fwd(xl, h, mix, kv, w_out_l, ln_g[layer][None], ln_b[layer][None])
        else:
            dx, r, mem_out, loss = _layer_out_fwd(xl, h, mix, kv, w_out_l, ln_g[layer][None], ln_b[layer][None],
                                                  target)
            x_next = None
        saved.append((xt, h, aux, mix, kv, r, mem_out, w_in_l, w_out_l))
        xl = x_next

    dgs, dbs, d_rel, d_conv = [], [], [], []
    for layer in reversed(range(DEPTH)):
        xt, h, aux, mix, kv, r, mem_out, w_in_l, w_out_l = saved[layer]
        dx_res, d_mix, dqz, dwo, dkv, dg, db = _layer_out_bwd(
            dx, r, ln_g[layer][None], h, mix, mem_out, kv, w_out_l.T)
        early = put_grads(layer, [1, 2], [_small_matmul(mem, dkv, True, BF16, "dw_kv"), dwo])
        if layer % 2 == 0:
            dq, dk, dv, dtb = _attn_bwd(h, aux, d_mix, early)
            d_rel.append(_tile_bias_bwd(dtb)[:, :N_REL])
            pieces = [dq, dk, dv, dqz]
        else:
            dbg, dcg, du, dw8 = _conv_bwd(h, aux, d_mix)
            d_conv.append(dw8[:3])
            pieces = [dbg, dcg, du, dqz]
        dgs.append(dg[0])
        dbs.append(db[0])
        tokens = []
        if layer == 0:
            rev = lambda lst: jnp.stack(lst[::-1])
            tokens.append(put_grads(layer, [N_KIND], [_pack_small(rev(d_rel), rev(d_conv), rev(dgs), rev(dbs))]))
        tokens.append(put_grads(layer, [0], [_dw_matmul(xt, pieces)]))
        dx = _dx_matmul(pieces, w_in_l, dx_res, tokens)

    return loss, dx


def _me():
    return lax.axis_index("x"), lax.axis_index("y"), lax.axis_index("c")


def _peer(k):
    x, y, c = _me()
    kx, ky, kc = (k >> 2) & 1, (k >> 1) & 1, k & 1
    return (1 - x if kx else x, 1 - y if ky else y, 1 - c if kc else c)


def _lin(dev):
    return 4 * dev[0] + 2 * dev[1] + dev[2]


ANY = pl.BlockSpec(memory_space=pl.ANY)


def _exchange(srcs, dst_shapes, src_slice, dst_slice, name):
    na = len(srcs)

    def body(*refs):
        src_refs = refs[:na]
        dst_refs = refs[na:2 * na]
        send_sems, recv_sems, local_sems = refs[2 * na:]
        me = _lin(_me())
        copies = []
        for a in range(na):
            loc = pltpu.make_async_copy(src_slice(a, src_refs[a], me), dst_slice(a, dst_refs[a], me),
                                        local_sems.at[a])
            loc.start()
            copies.append(loc)
            for k in range(1, N_DEV):
                peer = _peer(k)
                cp = pltpu.make_async_remote_copy(
                    src_ref=src_slice(a, src_refs[a], _lin(peer)),
                    dst_ref=dst_slice(a, dst_refs[a], me),
                    send_sem=send_sems.at[a, k - 1], recv_sem=recv_sems.at[a, k - 1],
                    device_id=peer, device_id_type=pl.DeviceIdType.MESH)
                cp.start()
                copies.append(cp)
        for cp in copies:
            cp.wait()

    return _call(
        body, name=name,
        in_specs=[ANY] * na, out_specs=[ANY] * na,
        out_shape=[jax.ShapeDtypeStruct(shp, s.dtype) for shp, s in zip(dst_shapes, srcs)],
        scratch_shapes=[pltpu.SemaphoreType.DMA((na, N_DEV - 1)),
                        pltpu.SemaphoreType.DMA((na, N_DEV - 1)),
                        pltpu.SemaphoreType.DMA((na,))],
    )(*srcs)


def _gather_to_all(src, name):
    return _exchange([src], [(N_DEV,) + src.shape], lambda a, ref, p: ref,
                     lambda a, ref, me: ref.at[me], name)[0]


HBM = pl.BlockSpec(memory_space=pltpu.HBM)
SEM = pl.BlockSpec(memory_space=pltpu.SEMAPHORE)
EFFECT = pltpu.SideEffectType.DATAFLOW_SIDE_EFFECTING
N_PEER = N_DEV - 1
N_KIND = 3


def _peer_copies(kind, src_ref, land_ref, send, recv, src_slice, dst_slice):
    me = _lin(_me())
    copies = []
    for k in range(1, N_DEV):
        peer = _peer(k)
        copies.append(pltpu.make_async_remote_copy(
            src_ref=src_slice(kind, src_ref, _lin(peer)),
            dst_ref=dst_slice(kind, land_ref, me, k),
            send_sem=send.at[k - 1], recv_sem=recv.at[k - 1],
            device_id=peer, device_id_type=pl.DeviceIdType.MESH))
    return copies


def _own_copy(kind, src_ref, land_ref, send, src_slice, dst_slice):
    me = _lin(_me())
    return pltpu.make_async_copy(src_slice(kind, src_ref, me), dst_slice(kind, land_ref, me, 0),
                                 send.at[N_PEER])


def _split_start(srcs, kinds, land_shapes, src_slice, dst_slice, name, own=False):
    na = len(srcs)

    def body(*refs):
        src_refs, land_refs = refs[:na], refs[na:2 * na]
        sems = refs[2 * na:4 * na]
        token = refs[-1]
        for a in range(na):
            for cp in _peer_copies(kinds[a], src_refs[a], land_refs[a], sems[2 * a], sems[2 * a + 1],
                                   src_slice, dst_slice):
                cp.start()
            if own:
                _own_copy(kinds[a], src_refs[a], land_refs[a], sems[2 * a], src_slice, dst_slice).start()
        token[...] = jnp.zeros_like(token)

    sem_shape = pltpu.SemaphoreType.DMA((N_DEV,))
    lands = [lax.empty(shp, s.dtype) for shp, s in zip(land_shapes, srcs)]
    outs = _call(
        body, name=name,
        in_specs=[HBM] * (2 * na),
        out_specs=[SEM] * (2 * na) + [HBM] * (2 * na) + [pl.BlockSpec(memory_space=pltpu.VMEM)],
        out_shape=[sem_shape] * (2 * na)
        + [pltpu.HBM(s.shape, s.dtype) for s in srcs]
        + [pltpu.HBM(shp, s.dtype) for shp, s in zip(land_shapes, srcs)]
        + [jax.ShapeDtypeStruct((SUBLANES, LANES), F32)],
        input_output_aliases={i: 2 * na + i for i in range(2 * na)},
        compiler_params=pltpu.CompilerParams(has_side_effects=EFFECT),
    )(*[pltpu.with_memory_space_constraint(a, pltpu.HBM) for a in list(srcs) + lands])
    sems = [(outs[2 * a], outs[2 * a + 1]) for a in range(na)]
    thrus = outs[2 * na:3 * na]
    lands = outs[3 * na:4 * na]
    return sems, thrus, lands, outs[-1]


def _split_wait(sems, thrus, lands, kinds, src_slice, dst_slice, after, name, own=False):
    na = len(thrus)
    after = list(after) if isinstance(after, (list, tuple)) else [after]

    def body(*refs):
        src_refs, land_refs = refs[:na], refs[na:2 * na]
        sem_refs = refs[2 * na:4 * na]
        for a in range(na):
            for cp in _peer_copies(kinds[a], src_refs[a], land_refs[a], sem_refs[2 * a], sem_refs[2 * a + 1],
                                   src_slice, dst_slice):
                cp.wait_send()
                cp.wait_recv()
            if own:
                _own_copy(kinds[a], src_refs[a], land_refs[a], sem_refs[2 * a], src_slice, dst_slice).wait()

    outs = _call(
        body, name=name,
        in_specs=[HBM] * (2 * na) + [SEM] * (2 * na) + [ANY] * len(after),
        out_specs=[HBM] * (2 * na),
        out_shape=[pltpu.HBM(a.shape, a.dtype) for a in list(thrus) + list(lands)],
        input_output_aliases={i: i for i in range(2 * na)},
        compiler_params=pltpu.CompilerParams(has_side_effects=EFFECT),
    )(*thrus, *lands, *[s for pair in sems for s in pair], *after)
    return outs[na:]


def _shard_dims(c_in, r_kv, r_out):
    def sl(j, ref, p):
        if j == 0:
            return ref.at[:, pl.ds(pl.multiple_of(p * c_in, LANES), c_in)]
        r = r_kv if j == 1 else r_out
        return ref.at[pl.ds(pl.multiple_of(p * r, 2 * SUBLANES), r), :]
    return sl


def _adamw_math(w, g, m, v):
    m = ADAM_B1 * m + (1.0 - ADAM_B1) * g
    v = ADAM_B2 * v + (1.0 - ADAM_B2) * (g * g)
    m_hat = m / (1.0 - ADAM_B1 ** ADAM_STEP)
    v_hat = v / (1.0 - ADAM_B2 ** ADAM_STEP)
    delta = -ADAM_LR * (m_hat / (jnp.sqrt(v_hat) + ADAM_EPS) + ADAM_WD * w)
    return delta, m, v


def _reduce_adamw(parts, w, m, v, name):
    rows, cols = w.shape
    tr = rows
    for cand in (512, 256, 128, 64, 32, 16):
        if rows % cand == 0 and rows > cand:
            tr = cand
            break

    def body(p_ref, w_ref, m_ref, v_ref, g_out, d_out, m_out, v_out):
        g = p_ref[0].astype(F32)
        for s in range(1, N_DEV):
            g = g + p_ref[s].astype(F32)
        g_out[...] = g
        d_out[...], m_out[...], v_out[...] = _adamw_math(w_ref[...], g, m_ref[...], v_ref[...])

    blk = pl.BlockSpec((tr, cols), lambda i: (i, 0))
    return _call(
        body, name=name, grid=(rows // tr,),
        in_specs=[pl.BlockSpec((N_DEV, tr, cols), lambda i: (0, i, 0)), blk, blk, blk],
        out_specs=[blk] * 4,
        out_shape=[jax.ShapeDtypeStruct((rows, cols), F32)] * 4,
        compiler_params=_cparams("parallel"),
    )(parts, w, m, v)


def _reduce_adamw_layers(lands, owns, w, m, v, name, first=0, prev=None):
    depth, rows, cols = w.shape
    nl = len(lands)
    tr = rows
    for cand in (256, 192, 128):
        if rows % cand == 0:
            tr = cand
            break

    kept = [] if prev is None else list(prev)

    def body(*refs):
        land_refs, own_refs = refs[:nl], refs[nl:2 * nl]
        w_ref, m_ref, v_ref = refs[2 * nl:2 * nl + 3]
        g_out, d_out, m_out, v_out = refs[2 * nl + 3 + len(kept):]
        layer = pl.program_id(0)
        for a in range(nl):
            @pl.when(layer == a)
            def _(a=a):
                g = own_refs[a][...].astype(F32)
                for k in range(N_PEER):
                    g = g + land_refs[a][k].astype(F32)
                g_out[...] = g
                d_out[...], m_out[...], v_out[...] = _adamw_math(w_ref[...], g, m_ref[...], v_ref[...])

    def lmap(a):
        return lambda l, i: (0, jnp.where(l == a, i, 0), 0)

    def omap(a):
        return lambda l, i: (jnp.where(l == a, i, 0), 0)

    blk = pl.BlockSpec((None, tr, cols), lambda l, i: (l + first, i, 0))
    n_in = 2 * nl + 3
    return _call(
        body, name=name, grid=(nl, rows // tr),
        in_specs=[pl.BlockSpec((N_PEER, tr, cols), lmap(a)) for a in range(nl)]
        + [pl.BlockSpec((tr, cols), omap(a)) for a in range(nl)] + [blk, blk, blk] + [ANY] * len(kept),
        out_specs=[blk] * 4,
        out_shape=[jax.ShapeDtypeStruct((depth, rows, cols), F32)] * 4,
        input_output_aliases={n_in + i: i for i in range(len(kept))},
        compiler_params=_cparams("arbitrary", "arbitrary"),
    )(*lands, *owns, w, m, v, *kept)


SM_G, SM_B, SM_CONV, SM_REL = 0, 4, 8, 16
SM_ROWS = SM_REL + 2 * N_HEADS
REL_W = 384


def _pack_small(d_rel, d_conv, dg, db):
    buf = jnp.zeros((SM_ROWS, D_MODEL), F32)
    buf = buf.at[SM_G:SM_G + DEPTH].set(dg)
    buf = buf.at[SM_B:SM_B + DEPTH].set(db)
    buf = buf.at[SM_CONV:SM_CONV + 6].set(d_conv.reshape(6, E_MIX))
    buf = buf.at[SM_REL:, :N_REL].set(d_rel.reshape(2 * N_HEADS, N_REL))
    return buf


def kernel(x, mem, w_in, w_mem_kv, w_out, rel_bias, conv_w, ln_g, ln_b, loss_target, m_w_in, m_w_mem_kv, m_w_out, m_rel_bias, m_conv_w, m_ln_g, m_ln_b, v_w_in, v_w_mem_kv, v_w_out, v_rel_bias, v_conv_w, v_ln_g, v_ln_b):
    me = _lin(_me())
    c_in, r_kv, r_out, c_conv = w_in.shape[2], w_mem_kv.shape[1], w_out.shape[1], conv_w.shape[2]

    shard = _shard_dims(c_in, r_kv, r_out)
    own_start = lambda j: (0, me * c_in) if j == 0 else (me * (r_kv if j == 1 else r_out), 0)

    w_sh = [w_in.astype(BF16), w_mem_kv.astype(BF16), w_out.astype(BF16)]
    full_shapes = [(D_MODEL, N_DEV * c_in), (N_DEV * r_kv, w_mem_kv.shape[2]), (N_DEV * r_out, D_MODEL)]
    conv_kind = N_KIND
    conv_tile = jnp.pad(conv_w.reshape(6, c_conv), ((0, SUBLANES - 6), (0, 0)))
    ag_src = lambda j, ref, p: ref
    ag_dst = lambda j, ref, me_, k: ref.at[me_] if j == conv_kind else shard(j, ref, me_)
    kinds = list(range(N_KIND))
    ag_sems, ag_thrus, ag_lands, _ = _split_start(
        [conv_tile] + [w_sh[j][layer] for layer in range(DEPTH) for j in kinds], [conv_kind] + kinds * DEPTH,
        [(N_DEV,) + conv_tile.shape] + full_shapes * DEPTH, ag_src, ag_dst, "ag_start", own=True)
    conv_landed = []

    def get_weights(layer, x_layer):
        lo = 1 + layer * N_KIND

        def wait(idx, js, after, name):
            return _split_wait([ag_sems[a] for a in idx], [ag_thrus[a] for a in idx],
                               [ag_lands[a] for a in idx], js, ag_src, ag_dst, after, name, own=True)

        if layer == 0:
            conv_land, w_in_l = wait([0, lo], [conv_kind, 0], x_layer, "ag_wait_in_0")
            conv_landed.append(conv_land)
        else:
            w_in_l, = wait([lo], [0], x_layer, "ag_wait_in_%d" % layer)
        return w_in_l, lambda h: wait([lo + 1, lo + 2], [1, 2], h, "ag_wait_kv_out_%d" % layer)

    def conv_full():
        return jnp.transpose(conv_landed[0][:, :6], (1, 0, 2)).reshape(2, 3, N_DEV * c_conv)

    rs_src = shard
    rs_dst = lambda j, ref, me_, k: ref.at[k - 1]
    rs_shapes = [(N_PEER, D_MODEL, c_in), (N_PEER, r_kv, w_mem_kv.shape[2]), (N_PEER, r_out, D_MODEL)]
    own_sizes = [(D_MODEL, c_in), (r_kv, w_mem_kv.shape[2]), (r_out, D_MODEL)]
    pending = {}

    held = {}

    small_flight = []

    def put_grads(layer, js, arrays):
        if js == [conv_kind]:
            sems, thrus, lands, token = _split_start(arrays, js, [(N_DEV,) + arrays[0].shape], ag_src, ag_dst,
                                                     "small_grads_start", own=True)
            small_flight.append((sems, thrus, lands))
            return token
        if layer > 0 and js != [0]:
            held[layer] = (js, arrays)
            return None
        if layer > 0:
            js, arrays = held[layer][0] + js, held[layer][1] + arrays
        owns = [lax.dynamic_slice(a, own_start(j), own_sizes[j]) for j, a in zip(js, arrays)]
        sems, thrus, lands, token = _split_start(
            arrays, js, [rs_shapes[j] for j in js], rs_src, rs_dst,
            "rs_start_%d_%s" % (layer, "".join(str(j) for j in js)))
        entry = pending.setdefault(layer, ([], [], [], [], []))
        for lst, new in zip(entry, (js, sems, thrus, lands, owns)):
            lst.extend(new)
        return token


    loss, grad_x = _local_step(
        x[0], mem[0], get_weights, put_grads, rel_bias, conv_full, ln_g, ln_b, loss_target[0])

    sems, thrus, lands = small_flight[0]
    p_small, = _split_wait(sems, thrus, lands, [conv_kind], ag_src, ag_dst, grad_x, "small_grads_wait", own=True)

    rs_lands, rs_owns = {}, {}

    def rs_wait(layer, want, after, name):
        js, sems, thrus, lands, owns = pending[layer]
        pos = [js.index(j) for j in want]
        got = _split_wait([sems[p] for p in pos], [thrus[p] for p in pos], [lands[p] for p in pos],
                          want, rs_src, rs_dst, after, name)
        for j, p, land in zip(want, pos, got):
            rs_lands[layer, j], rs_owns[layer, j] = land, owns[p]

    for layer in range(1, DEPTH):
        rs_wait(layer, kinds, grad_x, "rs_wait_%d" % layer)
    rs_wait(0, [1, 2], grad_x, "rs_wait_0_kv_out")

    def big(j, w, m, v, name, layers, prev=None):
        return _reduce_adamw_layers([rs_lands[layer, j] for layer in layers], [rs_owns[layer, j] for layer in layers],
                                    w, m, v, name, first=layers[0], prev=prev)

    every = list(range(DEPTH))
    g_kv, d_kv, nm_kv, nv_kv = big(1, w_mem_kv, m_w_mem_kv, v_w_mem_kv, "adamw_w_kv", every)
    g_out, d_out, nm_out, nv_out = big(2, w_out, m_w_out, v_w_out, "adamw_w_out", every)
    later = big(0, w_in, m_w_in, v_w_in, "adamw_w_in_later_layers", every[1:])

    def pack_state(rel, conv, g, b):
        conv_full = jnp.zeros((2, 3, E_MIX), F32)
        conv_full = lax.dynamic_update_slice(conv_full, conv, (0, 0, me * c_conv))
        return _pack_small(rel, conv_full, g, b)

    sm_w = pack_state(rel_bias, conv_w, ln_g, ln_b)
    sm_m = pack_state(m_rel_bias, m_conv_w, m_ln_g, m_ln_b)
    sm_v = pack_state(v_rel_bias, v_conv_w, v_ln_g, v_ln_b)
    sm_outs = _reduce_adamw(p_small, sm_w, sm_m, sm_v, "adamw_small")

    rs_wait(0, [0], [later[0], sm_outs[0], g_kv, g_out], "rs_wait_0_in")
    g_in, d_in, nm_in, nv_in = big(0, w_in, m_w_in, v_w_in, "adamw_w_in_layer_0", [0], prev=later)

    def unpack(buf):
        rel = buf[SM_REL:, :N_REL].reshape(2, N_HEADS, N_REL)
        conv = lax.dynamic_slice(buf[SM_CONV:SM_CONV + 6].reshape(2, 3, E_MIX), (0, 0, me * c_conv), (2, 3, c_conv))
        return rel, conv, buf[SM_G:SM_G + DEPTH], buf[SM_B:SM_B + DEPTH]

    g_sm, d_sm, nm_sm, nv_sm = [unpack(b) for b in sm_outs]

    loss = lax.psum(loss[0, 0], ("x", "y", "c"))
    return (loss, grad_x[None],
            g_in, g_kv, g_out, *g_sm,
            d_in, d_kv, d_out, *d_sm,
            nm_in, nm_kv, nm_out, *nm_sm,
            nv_in, nv_kv, nv_out, *nv_sm)
```

```python
import functools
import math

import jax
import jax.numpy as jnp
from jax import lax
from jax.experimental import pallas as pl
from jax.experimental.pallas import tpu as pltpu

F32 = jnp.float32
BF16 = jnp.bfloat16
MXU_DTYPE = jnp.bfloat16

N_DEV = 8
D_MODEL = 1024
DEPTH = 4
CHUNK = 64
N_PREV = 8
N_HEADS = 16
HEAD_DIM = 64
E_MIX = 1024
REL_CLIP = 128
N_REL = 2 * REL_CLIP + 1
N_REL_PAD = 384
N_MEM = 256
MEM_HEADS = 4
MEM_HEAD_DIM = 128
E_MEM = 512
E_BRANCH = E_MIX + E_MEM
N_IN = 3 * E_MIX + E_MEM + E_BRANCH
DN_ALPHA = (2.0 * DEPTH) ** 0.25
LN_EPS = 1e-5
NEG = -1e30

ADAM_LR = 0.001
ADAM_B1 = 0.9
ADAM_B2 = 0.999
ADAM_EPS = 1e-08
ADAM_WD = 0.01
ADAM_STEP = 10

LANES = 128
SUBLANES = 8
VMEM_LIMIT = 56 * 1024 * 1024

TQ = 4 * CHUNK
TKEYS = 3 * TQ
ROLL_W = 1024
TS = 256
QM_BLK = 3 * E_MIX // E_MEM
Z_BLK = QM_BLK + 1


def _call(body, **kw):
    return pl.pallas_call(body, **kw)


def _cparams(*sem):
    return pltpu.CompilerParams(dimension_semantics=sem, vmem_limit_bytes=VMEM_LIMIT)


def _dot(a, b):
    return jnp.dot(a, b, preferred_element_type=F32)


def _dot_nt(a, b):
    return lax.dot_general(a, b, (((1,), (1,)), ((), ())), preferred_element_type=F32)


def _dot_tn(a, b):
    return lax.dot_general(a, b, (((0,), (0,)), ((), ())), preferred_element_type=F32)


def _inproj(x, w):
    s, d = x.shape
    n = w.shape[1]
    tm = 1024
    tn = 1024

    def body(x_ref, w_ref, o_ref, xt_ref):
        xb = x_ref[...].astype(xt_ref.dtype)
        xt_ref[...] = xb.T
        for j in range(n // tn):
            o_ref[:, j * tn:(j + 1) * tn] = _dot(xb, w_ref[:, j * tn:(j + 1) * tn]).astype(o_ref.dtype)

    return _call(
        body, name="inproj", grid=(s // tm,),
        in_specs=[pl.BlockSpec((tm, d), lambda i: (i, 0)),
                  pl.BlockSpec((d, n), lambda i: (0, 0), pipeline_mode=pl.Buffered(1))],
        out_specs=[pl.BlockSpec((tm, n), lambda i: (i, 0)),
                   pl.BlockSpec((d, tm), lambda i: (0, i))],
        out_shape=[jax.ShapeDtypeStruct((s, n), BF16), jax.ShapeDtypeStruct((d, s), BF16)],
        compiler_params=_cparams("parallel"),
    )(x, w)


def _small_matmul(a, b, trans_a, out_dtype, name):
    m = a.shape[1] if trans_a else a.shape[0]
    n = b.shape[1]

    def body(a_ref, b_ref, o_ref):
        av = a_ref[...].astype(MXU_DTYPE)
        bv = b_ref[...].astype(MXU_DTYPE)
        r = _dot_tn(av, bv) if trans_a else _dot(av, bv)
        o_ref[...] = r.astype(out_dtype)

    return _call(
        body, name=name,
        in_specs=[pl.BlockSpec(memory_space=pltpu.VMEM)] * 2,
        out_specs=pl.BlockSpec(memory_space=pltpu.VMEM),
        out_shape=jax.ShapeDtypeStruct((m, n), out_dtype),
        compiler_params=pltpu.CompilerParams(vmem_limit_bytes=VMEM_LIMIT),
    )(a, b)


def _piece_blocks(pieces, blk):
    offs, nbs, o = [], [], 0
    for p in pieces:
        nb = p.shape[1] // blk
        offs.append(o)
        nbs.append(nb)
        o += nb
    return offs, nbs, o


def _dx_matmul(pieces, w, addend, token=None):
    s = pieces[0].shape[0]
    d, n_in = w.shape
    tm = 512
    tw = 1024
    np_ = len(pieces)
    extra = [t for t in (token if isinstance(token, (list, tuple)) else [token]) if t is not None]

    def body(*refs):
        a_refs = refs[:np_]
        w_ref, add_ref = refs[np_:np_ + 2]
        o_ref, wt = refs[-2:]

        @pl.when(pl.program_id(0) == 0)
        def _():
            for j in range(n_in // tw):
                wt[j * tw:(j + 1) * tw, :] = w_ref[:, j * tw:(j + 1) * tw].T

        a = jnp.concatenate([r[...] for r in a_refs], axis=1)
        o_ref[...] = add_ref[...] + _dot(a, wt[...])

    in_specs = [pl.BlockSpec((tm, p.shape[1]), lambda i: (i, 0)) for p in pieces]
    in_specs += [pl.BlockSpec((d, n_in), lambda i: (0, 0), pipeline_mode=pl.Buffered(1)),
                 pl.BlockSpec((tm, d), lambda i: (i, 0))]
    in_specs += [pl.BlockSpec((SUBLANES, LANES), lambda i: (0, 0)) for _ in extra]
    return _call(
        body, name="dx_matmul", grid=(s // tm,),
        in_specs=in_specs,
        out_specs=pl.BlockSpec((tm, d), lambda i: (i, 0)),
        out_shape=jax.ShapeDtypeStruct((s, d), F32),
        scratch_shapes=[pltpu.VMEM((n_in, d), w.dtype)],
        compiler_params=_cparams("arbitrary"),
    )(*pieces, w, addend, *extra)


def _dw_matmul(xt, pieces):
    d, s = xt.shape
    tn = 256
    offs, nbs, nj = _piece_blocks(pieces, tn)
    np_ = len(pieces)

    def body(*refs):
        x_ref = refs[0]
        b_refs = refs[1:1 + np_]
        o_ref = refs[1 + np_]
        j = pl.program_id(0)
        for p in range(np_):
            @pl.when((j >= offs[p]) & (j < offs[p] + nbs[p]))
            def _(p=p):
                o_ref[...] = _dot(x_ref[...], b_refs[p][...]).astype(o_ref.dtype)

    def bmap(p):
        return lambda j: (0, jnp.clip(j - offs[p], 0, nbs[p] - 1))

    in_specs = [pl.BlockSpec((d, s), lambda j: (0, 0), pipeline_mode=pl.Buffered(1))]
    in_specs += [pl.BlockSpec((s, tn), bmap(p)) for p in range(np_)]
    return _call(
        body, name="dw_matmul", grid=(nj,),
        in_specs=in_specs,
        out_specs=pl.BlockSpec((d, tn), lambda j: (0, j)),
        out_shape=jax.ShapeDtypeStruct((d, nj * tn), BF16),
        compiler_params=_cparams("parallel"),
    )(xt, *pieces)


def _rel_onehot():
    j = lax.broadcasted_iota(jnp.int32, (N_REL_PAD, ROLL_W), 1)
    kk = lax.broadcasted_iota(jnp.int32, (N_REL_PAD, ROLL_W), 0)
    dd = jnp.where(j < TKEYS, j, j - ROLL_W)
    idx = jnp.clip(N_PREV * CHUNK - dd, -REL_CLIP, REL_CLIP) + REL_CLIP
    return jnp.where(idx == kk, 1.0, 0.0).astype(F32)


def _band_mask():
    r = lax.broadcasted_iota(jnp.int32, (TQ, TKEYS), 0) // CHUNK
    m = lax.broadcasted_iota(jnp.int32, (TQ, TKEYS), 1) // CHUNK
    return (m >= r) & (m <= r + N_PREV)


def _tile_bias(table_pad):
    def body(t_ref, o_ref):
        g = jnp.dot(t_ref[...], _rel_onehot(), preferred_element_type=F32,
                    precision=lax.Precision.HIGHEST)
        band = _band_mask()
        for h in range(N_HEADS):
            gh = jnp.broadcast_to(g[h:h + 1, :], (TQ, ROLL_W))
            rolled = pltpu.roll(gh, 0, 1, stride=1, stride_axis=0)
            o_ref[h] = jnp.where(band, rolled[:, :TKEYS], NEG)

    return _call(
        body, name="tile_bias",
        in_specs=[pl.BlockSpec(memory_space=pltpu.VMEM)],
        out_specs=pl.BlockSpec(memory_space=pltpu.VMEM),
        out_shape=jax.ShapeDtypeStruct((N_HEADS, TQ, TKEYS), F32),
        compiler_params=pltpu.CompilerParams(vmem_limit_bytes=VMEM_LIMIT),
    )(table_pad)


def _tile_bias_bwd(dtb):
    def body(d_ref, o_ref, g_ref):
        zpad = jnp.zeros((TQ, ROLL_W - TKEYS), F32)
        rr = lax.broadcasted_iota(jnp.int32, (TQ, TQ), 0)
        cc = lax.broadcasted_iota(jnp.int32, (TQ, TQ), 1)
        flip = jnp.where(rr + cc == TQ - 1, 1.0, 0.0).astype(F32)
        for h in range(N_HEADS):
            xh = jnp.concatenate([d_ref[h], zpad], axis=1)
            xf = jnp.dot(flip, xh, preferred_element_type=F32, precision=lax.Precision.HIGHEST)
            rolled = pltpu.roll(xf, 0, 1, stride=1, stride_axis=0)
            g_ref[h:h + 1, :] = jnp.sum(rolled, axis=0, keepdims=True)
        g = pltpu.roll(g_ref[...], ROLL_W - (TQ - 1), 1)
        o_ref[...] = lax.dot_general(g, _rel_onehot(), (((1,), (1,)), ((), ())),
                                     preferred_element_type=F32, precision=lax.Precision.HIGHEST)

    return _call(
        body, name="tile_bias_bwd",
        in_specs=[pl.BlockSpec(memory_space=pltpu.VMEM)],
        out_specs=pl.BlockSpec(memory_space=pltpu.VMEM),
        out_shape=jax.ShapeDtypeStruct((N_HEADS, N_REL_PAD), F32),
        scratch_shapes=[pltpu.VMEM((N_HEADS, ROLL_W), F32)],
        compiler_params=pltpu.CompilerParams(vmem_limit_bytes=VMEM_LIMIT),
    )(dtb)


HB = 4
HBW = HB * HEAD_DIM
ATTN_SCALE = 0.125
assert ATTN_SCALE == 1.0 / math.sqrt(HEAD_DIM)


def _head_masks():
    lane = lax.broadcasted_iota(jnp.int32, (1, HBW), 1) // HEAD_DIM
    return [lane == hh for hh in range(HB)]


def _select_heads(masks, parts):
    out = parts[-1]
    for hh in range(HB - 2, -1, -1):
        out = jnp.where(masks[hh], parts[hh], out)
    return out


def _attn_probs(qm, kcat, tb, valid):
    s = _dot_nt(qm, kcat) + tb
    if valid is not None:
        s = jnp.where(valid, s, NEG)
    m = jnp.max(s, axis=-1, keepdims=True)
    e = jnp.exp(s - m)
    return e * (1.0 / jnp.sum(e, axis=-1, keepdims=True))


def _key_valid(i):
    col = lax.broadcasted_iota(jnp.int32, (TQ, TKEYS), 1)
    return col >= jnp.maximum(2 - i, 0) * TQ


def _kv_specs(col0, nt):
    def spec(back):
        return pl.BlockSpec((TQ, HBW), lambda hp, i: (jnp.clip(i - back, 0, nt - 1), col0 + hp))
    return [spec(2), spec(1), spec(0)]


def _attn_fwd(h, tb):
    s = h.shape[0]
    nt = s // TQ
    nhp = N_HEADS // HB

    def body(q_ref, k0, k1, k2, v0, v1, v2, tb_ref, o_ref, p_ref):
        i = pl.program_id(1)

        def tile(valid):
            masks = _head_masks()
            qs = q_ref[...].astype(MXU_DTYPE) * ATTN_SCALE
            kcat = jnp.concatenate([k0[...], k1[...], k2[...]], axis=0).astype(MXU_DTYPE)
            vcat = jnp.concatenate([v0[...], v1[...], v2[...]], axis=0).astype(MXU_DTYPE)
            outs = []
            for hh in range(HB):
                qm = jnp.where(masks[hh], qs, jnp.zeros_like(qs))
                pb = _attn_probs(qm, kcat, tb_ref[hh], valid).astype(MXU_DTYPE)
                p_ref[hh] = pb.astype(p_ref.dtype)
                outs.append(_dot(pb, vcat))
            o_ref[...] = _select_heads(masks, outs).astype(o_ref.dtype)

        @pl.when(i < 2)
        def _():
            tile(_key_valid(i))

        @pl.when(i >= 2)
        def _():
            tile(None)

    in_specs = [pl.BlockSpec((TQ, HBW), lambda hp, i: (i, hp))]
    in_specs += _kv_specs(nhp, nt) + _kv_specs(2 * nhp, nt)
    in_specs += [pl.BlockSpec((HB, TQ, TKEYS), lambda hp, i: (hp, 0, 0))]
    return _call(
        body, name="attn_fwd", grid=(nhp, nt),
        in_specs=in_specs,
        out_specs=[pl.BlockSpec((TQ, HBW), lambda hp, i: (i, hp)),
                   pl.BlockSpec((HB, TQ, TKEYS), lambda hp, i: (hp, i, 0))],
        out_shape=[jax.ShapeDtypeStruct((s, E_MIX), BF16), jax.ShapeDtypeStruct((N_HEADS, s, TKEYS), BF16)],
        compiler_params=_cparams("parallel", "parallel"),
    )(h, h, h, h, h, h, h, tb)


def _attn_bwd(h, probs, d_mix, token=None):
    s = h.shape[0]
    nt = s // TQ
    nhp = N_HEADS // HB
    extra = [] if token is None else [token]

    def body(q_ref, k0, k1, k2, v0, v1, v2, p_ref, do_ref, *rest):
        dq_ref, dk_ref, dv_ref, dtb_ref, dk_acc, dv_acc = rest[len(extra):]
        i = pl.program_id(1)

        @pl.when(i == 0)
        def _():
            dk_acc[...] = jnp.zeros_like(dk_acc)
            dv_acc[...] = jnp.zeros_like(dv_acc)
            dtb_ref[...] = jnp.zeros_like(dtb_ref)

        @pl.when((i > 0) & (i < nt))
        def _():
            dk_acc[i % 3] = jnp.zeros((TQ, HBW), F32)
            dv_acc[i % 3] = jnp.zeros((TQ, HBW), F32)

        @pl.when(i < nt)
        def _():
            masks = _head_masks()
            qs = q_ref[...].astype(MXU_DTYPE) * ATTN_SCALE
            do2 = do_ref[...].astype(MXU_DTYPE)
            kcat = jnp.concatenate([k0[...], k1[...], k2[...]], axis=0).astype(MXU_DTYPE)
            vcat = jnp.concatenate([v0[...], v1[...], v2[...]], axis=0).astype(MXU_DTYPE)
            ks = kcat * ATTN_SCALE
            dqs, dks, dvs = [], [], []
            for hh in range(HB):
                dom = jnp.where(masks[hh], do2, jnp.zeros_like(do2))
                pb = p_ref[hh]
                p = pb.astype(F32)
                dp = _dot_nt(dom, vcat)
                ds = p * (dp - jnp.sum(p * dp, axis=-1, keepdims=True))
                dtb_ref[hh] += ds
                dsb = ds.astype(MXU_DTYPE)
                dqs.append(_dot(dsb, ks))
                dks.append(_dot_tn(dsb, qs))
                dvs.append(_dot_tn(pb.astype(MXU_DTYPE), do2))
            dq_ref[...] = _select_heads(masks, dqs).astype(dq_ref.dtype)
            dkc = _select_heads(masks, dks)
            dvc = _select_heads(masks, dvs)
            for jj in range(3):
                slot = (i + 1 + jj) % 3
                dk_acc[slot] += dkc[jj * TQ:(jj + 1) * TQ]
                dv_acc[slot] += dvc[jj * TQ:(jj + 1) * TQ]

        @pl.when(i >= 2)
        def _():
            slot = (i - 2) % 3
            dk_ref[...] = dk_acc[slot].astype(dk_ref.dtype)
            dv_ref[...] = dv_acc[slot].astype(dv_ref.dtype)

    qmap = lambda hp, i: (jnp.minimum(i, nt - 1), hp)
    kvout = lambda hp, i: (jnp.maximum(i - 2, 0), hp)
    in_specs = [pl.BlockSpec((TQ, HBW), qmap)]
    in_specs += _kv_specs(nhp, nt) + _kv_specs(2 * nhp, nt)
    in_specs += [pl.BlockSpec((HB, TQ, TKEYS), lambda hp, i: (hp, jnp.minimum(i, nt - 1), 0)),
                 pl.BlockSpec((TQ, HBW), qmap)]
    in_specs += [pl.BlockSpec((SUBLANES, LANES), lambda hp, i: (0, 0)) for _ in extra]
    blk = (TQ, HBW)
    return _call(
        body, name="attn_bwd", grid=(nhp, nt + 2),
        in_specs=in_specs,
        out_specs=[pl.BlockSpec(blk, qmap), pl.BlockSpec(blk, kvout), pl.BlockSpec(blk, kvout),
                   pl.BlockSpec((HB, TQ, TKEYS), lambda hp, i: (hp, 0, 0))],
        out_shape=[jax.ShapeDtypeStruct((s, E_MIX), BF16)] * 3
        + [jax.ShapeDtypeStruct((N_HEADS, TQ, TKEYS), F32)],
        scratch_shapes=[pltpu.VMEM((3, TQ, HBW), F32)] * 2,
        compiler_params=_cparams("parallel", "arbitrary"),
    )(h, h, h, h, h, h, h, probs, d_mix, *extra)


CONV_TS = 512
HALO = 2 * SUBLANES


def _shift_down(prev, cur, k):
    rolled = pltpu.roll(cur, k, 0)
    row = lax.broadcasted_iota(jnp.int32, (HALO, cur.shape[1]), 0)
    top = jnp.where(row < k, pltpu.roll(prev, k, 0), rolled[:HALO])
    return jnp.concatenate([top, rolled[HALO:]], axis=0)


def _shift_up(cur, nxt, k):
    ts = cur.shape[0]
    rolled = pltpu.roll(cur, ts - k, 0)
    row = lax.broadcasted_iota(jnp.int32, (HALO, cur.shape[1]), 0)
    bottom = jnp.where(row >= HALO - k, pltpu.roll(nxt, HALO - k, 0), rolled[ts - HALO:])
    return jnp.concatenate([rolled[:ts - HALO], bottom], axis=0)


def _conv_specs(ts, nb):
    tile = lambda c: pl.BlockSpec((ts, E_MIX), lambda i: (i, c))
    prev = lambda c: pl.BlockSpec((HALO, E_MIX), lambda i: (jnp.maximum(i * (ts // HALO) - 1, 0), c))
    return tile, prev


def _conv_fwd(h, w8):
    s = h.shape[0]
    ts = CONV_TS
    nb = s // ts
    tile, prev = _conv_specs(ts, nb)

    def body(bg, cg, u, cgp, up, w_ref, o_ref):
        i = pl.program_id(0)
        a = cg[...].astype(F32) * u[...].astype(F32)
        ap = jnp.where(i > 0, cgp[...].astype(F32) * up[...].astype(F32), 0.0)
        w = w_ref[...]
        conv = w[0:1] * _shift_down(ap, a, 2) + w[1:2] * _shift_down(ap, a, 1) + w[2:3] * a
        o_ref[...] = (bg[...].astype(F32) * conv).astype(o_ref.dtype)

    return _call(
        body, name="conv_fwd", grid=(nb,),
        in_specs=[tile(0), tile(1), tile(2), prev(1), prev(2),
                  pl.BlockSpec((SUBLANES, E_MIX), lambda i: (0, 0))],
        out_specs=pl.BlockSpec((ts, E_MIX), lambda i: (i, 0)),
        out_shape=jax.ShapeDtypeStruct((s, E_MIX), BF16),
        compiler_params=_cparams("parallel"),
    )(h, h, h, h, h, w8)


def _conv_bwd(h, w8, d_mix):
    s = h.shape[0]
    ts = CONV_TS
    nb = s // ts
    tile, prev = _conv_specs(ts, nb)
    nrow = s // HALO
    nxt = lambda c: pl.BlockSpec((HALO, E_MIX), lambda i: (jnp.minimum((i + 1) * (ts // HALO), nrow - 1), c))

    def body(bg, cg, u, cgp, up, bgn, dmix, dmixn, w_ref, dbg_ref, dcg_ref, du_ref, dw_ref):
        i = pl.program_id(0)

        @pl.when(i == 0)
        def _():
            dw_ref[...] = jnp.zeros_like(dw_ref)

        cgv, uv = cg[...].astype(F32), u[...].astype(F32)
        a = cgv * uv
        ap = jnp.where(i > 0, cgp[...].astype(F32) * up[...].astype(F32), 0.0)
        a1 = _shift_down(ap, a, 1)
        a2 = _shift_down(ap, a, 2)
        w = w_ref[...]
        conv = w[0:1] * a2 + w[1:2] * a1 + w[2:3] * a
        dm = dmix[...].astype(F32)
        dbg_ref[...] = (dm * conv).astype(dbg_ref.dtype)
        dc = dm * bg[...].astype(F32)
        dcn = jnp.where(i < nb - 1, dmixn[...].astype(F32) * bgn[...].astype(F32), 0.0)
        da = w[2:3] * dc + w[1:2] * _shift_up(dc, dcn, 1) + w[0:1] * _shift_up(dc, dcn, 2)
        dcg_ref[...] = (da * uv).astype(dcg_ref.dtype)
        du_ref[...] = (da * cgv).astype(du_ref.dtype)
        dw_ref[0:1, :] += jnp.sum(dc * a2, axis=0, keepdims=True)
        dw_ref[1:2, :] += jnp.sum(dc * a1, axis=0, keepdims=True)
        dw_ref[2:3, :] += jnp.sum(dc * a, axis=0, keepdims=True)

    full = lambda: pl.BlockSpec((ts, E_MIX), lambda i: (i, 0))
    return _call(
        body, name="conv_bwd", grid=(nb,),
        in_specs=[tile(0), tile(1), tile(2), prev(1), prev(2), nxt(0),
                  tile(0), pl.BlockSpec((HALO, E_MIX), lambda i: (jnp.minimum((i + 1) * (ts // HALO), nrow - 1), 0)),
                  pl.BlockSpec((SUBLANES, E_MIX), lambda i: (0, 0))],
        out_specs=[full(), full(), full(), pl.BlockSpec((SUBLANES, E_MIX), lambda i: (0, 0))],
        out_shape=[jax.ShapeDtypeStruct((s, E_MIX), BF16)] * 3
        + [jax.ShapeDtypeStruct((SUBLANES, E_MIX), F32)],
        compiler_params=_cparams("arbitrary"),
    )(h, h, h, h, h, h, d_mix, d_mix, w8)


def _mem_probs(qh, kh):
    s = _dot_nt(qh, kh) / math.sqrt(MEM_HEAD_DIM)
    m = jnp.max(s, axis=-1, keepdims=True)
    e = jnp.exp(s - m)
    return e / jnp.sum(e, axis=-1, keepdims=True)


def _sigmoid(z):
    return 1.0 / (1.0 + jnp.exp(-z))


def _layer_out_fwd(x, h, mix, kv, w_out, g, b, target=None):
    s, d = x.shape
    ts = 2 * TS
    last = target is not None

    def body(x_ref, mix_ref, qm_ref, z0, z1, z2, kv_ref, wo_ref, g_ref, b_ref, *rest):
        xn_ref, r_ref, mem_ref = rest[last:last + 3]
        qm = qm_ref[...].astype(MXU_DTYPE)
        kvb = kv_ref[...].astype(MXU_DTYPE)
        mems = []
        for hh in range(MEM_HEADS):
            lo = hh * MEM_HEAD_DIM
            p = _mem_probs(qm[:, lo:lo + MEM_HEAD_DIM], kvb[:, lo:lo + MEM_HEAD_DIM])
            mems.append(_dot(p.astype(MXU_DTYPE), kvb[:, E_MEM + lo:E_MEM + lo + MEM_HEAD_DIM]))
        mem = jnp.concatenate(mems, axis=1).astype(mem_ref.dtype)
        mem_ref[...] = mem
        mixv = mix_ref[...].astype(F32)
        half = E_MIX // 2
        parts = [mixv[:, :half], mixv[:, half:], mem.astype(F32)]
        out = jnp.zeros((ts, d), F32)
        for c, zr in enumerate((z0, z1, z2)):
            zv = zr[...].astype(F32)
            y = (parts[c] * (zv * _sigmoid(zv))).astype(MXU_DTYPE)
            out += _dot(y, wo_ref[c * half:(c + 1) * half, :])
        r = DN_ALPHA * x_ref[...] + out
        r_ref[...] = r
        mu = jnp.mean(r, axis=-1, keepdims=True)
        rc = r - mu
        var = jnp.mean(rc * rc, axis=-1, keepdims=True)
        xn = rc * lax.rsqrt(var + LN_EPS) * g_ref[...] + b_ref[...]
        if not last:
            xn_ref[...] = xn
        else:
            t_ref, l_ref = rest[0], rest[4]

            @pl.when(pl.program_id(0) == 0)
            def _():
                l_ref[...] = jnp.zeros_like(l_ref)

            e = xn - t_ref[...]
            xn_ref[...] = e * (1.0 / d)
            l_ref[...] += (0.5 / d) * jnp.sum(jnp.sum(e * e, axis=1, keepdims=True), axis=0, keepdims=True)

    row = lambda w, c: pl.BlockSpec((ts, w), lambda i: (i, c))
    const = lambda shp: pl.BlockSpec(shp, lambda i: (0, 0))
    tail = [target] if last else []
    return _call(
        body, name="layer_out_fwd_loss" if last else "layer_out_fwd", grid=(s // ts,),
        in_specs=[row(d, 0), row(E_MIX, 0), row(E_MEM, QM_BLK),
                  row(E_MEM, Z_BLK), row(E_MEM, Z_BLK + 1), row(E_MEM, Z_BLK + 2),
                  const((N_MEM, 2 * E_MEM)), const((E_BRANCH, d)), const((1, d)), const((1, d))]
        + [row(d, 0) for _ in tail],
        out_specs=[row(d, 0), row(d, 0), row(E_MEM, 0)] + [const((1, 1)) for _ in tail],
        out_shape=[jax.ShapeDtypeStruct((s, d), F32), jax.ShapeDtypeStruct((s, d), F32),
                   jax.ShapeDtypeStruct((s, E_MEM), BF16)] + [jax.ShapeDtypeStruct((1, 1), F32) for _ in tail],
        compiler_params=_cparams("arbitrary" if last else "parallel"),
    )(x, mix, h, h, h, h, kv, w_out, g, b, *tail)


def _layer_out_bwd(dxn, r, g, h, mix, mem, kv, w_out_t):
    s, d = r.shape
    ts = 2 * TS
    nb = s // ts
    half = E_MIX // 2
    inv = 1.0 / math.sqrt(MEM_HEAD_DIM)

    def body(dxn_ref, r_ref, g_ref, mix_ref, mem_ref, qm_ref, z0, z1, z2, kv_ref, wo_ref,
             dxr_ref, dmix_ref, dqz_ref, dwo_ref, dkv_ref, dg_ref, db_ref, dw_acc):
        i = pl.program_id(0)

        @pl.when(i == 0)
        def _():
            dw_acc[...] = jnp.zeros_like(dw_acc)
            dkv_ref[...] = jnp.zeros_like(dkv_ref)
            dg_ref[...] = jnp.zeros_like(dg_ref)
            db_ref[...] = jnp.zeros_like(db_ref)

        dxn_v = dxn_ref[...]
        rv = r_ref[...]
        mu = jnp.mean(rv, axis=-1, keepdims=True)
        rc = rv - mu
        var = jnp.mean(rc * rc, axis=-1, keepdims=True)
        rstd = lax.rsqrt(var + LN_EPS)
        xhat = rc * rstd
        dg_ref[...] += jnp.sum(dxn_v * xhat, axis=0, keepdims=True)
        db_ref[...] += jnp.sum(dxn_v, axis=0, keepdims=True)
        dxh = dxn_v * g_ref[...]
        m1 = jnp.mean(dxh, axis=-1, keepdims=True)
        m2 = jnp.mean(dxh * xhat, axis=-1, keepdims=True)
        dr = rstd * (dxh - m1 - xhat * m2)
        dxr_ref[...] = DN_ALPHA * dr
        dout = dr.astype(MXU_DTYPE)
        mixv = mix_ref[...].astype(F32)
        parts = [mixv[:, :half], mixv[:, half:], mem_ref[...].astype(F32)]
        dcs = []
        for c, zr in enumerate((z0, z1, z2)):
            lo = c * half
            zv = zr[...].astype(F32)
            sg = _sigmoid(zv)
            sl = zv * sg
            dy = _dot(dout, wo_ref[:, lo:lo + half])
            y = (parts[c] * sl).astype(MXU_DTYPE)
            dw_acc[lo:lo + half, :] += _dot_tn(y, dout)
            dcs.append(dy * sl)
            dqz_ref[:, E_MEM + lo:E_MEM + lo + half] = (
                dy * parts[c] * (sg * (1.0 + zv * (1.0 - sg)))).astype(dqz_ref.dtype)
        dmix_ref[...] = jnp.concatenate(dcs[:2], axis=1).astype(dmix_ref.dtype)

        qm = qm_ref[...].astype(MXU_DTYPE)
        kvb = kv_ref[...].astype(MXU_DTYPE)
        dmb = dcs[2].astype(MXU_DTYPE)
        for hh in range(MEM_HEADS):
            lo = hh * MEM_HEAD_DIM
            qh = qm[:, lo:lo + MEM_HEAD_DIM]
            kh = kvb[:, lo:lo + MEM_HEAD_DIM]
            vh = kvb[:, E_MEM + lo:E_MEM + lo + MEM_HEAD_DIM]
            dmh = dmb[:, lo:lo + MEM_HEAD_DIM]
            p = _mem_probs(qh, kh)
            dp = _dot_nt(dmh, vh)
            ds = p * (dp - jnp.sum(p * dp, axis=-1, keepdims=True))
            dsb = (ds * inv).astype(MXU_DTYPE)
            dqz_ref[:, lo:lo + MEM_HEAD_DIM] = _dot(dsb, kh).astype(dqz_ref.dtype)
            dkv_ref[:, lo:lo + MEM_HEAD_DIM] += _dot_tn(dsb, qh)
            dkv_ref[:, E_MEM + lo:E_MEM + lo + MEM_HEAD_DIM] += _dot_tn(p.astype(MXU_DTYPE), dmh)

        @pl.when(i == nb - 1)
        def _():
            dwo_ref[...] = dw_acc[...].astype(dwo_ref.dtype)

    row = lambda w, c: pl.BlockSpec((ts, w), lambda i: (i, c))
    const = lambda shp: pl.BlockSpec(shp, lambda i: (0, 0))
    once = lambda shp: pl.BlockSpec(shp, lambda i: (0, 0), pipeline_mode=pl.Buffered(1))
    return _call(
        body, name="layer_out_bwd", grid=(nb,),
        in_specs=[row(d, 0), row(d, 0), const((1, d)), row(E_MIX, 0), row(E_MEM, 0),
                  row(E_MEM, QM_BLK), row(E_MEM, Z_BLK), row(E_MEM, Z_BLK + 1), row(E_MEM, Z_BLK + 2),
                  once((N_MEM, 2 * E_MEM)), once((d, E_BRANCH))],
        out_specs=[row(d, 0), row(E_MIX, 0), row(E_MEM + E_BRANCH, 0),
                   const((E_BRANCH, d)), const((N_MEM, 2 * E_MEM)), const((1, d)), const((1, d))],
        out_shape=[jax.ShapeDtypeStruct((s, d), F32), jax.ShapeDtypeStruct((s, E_MIX), BF16),
                   jax.ShapeDtypeStruct((s, E_MEM + E_BRANCH), BF16),
                   jax.ShapeDtypeStruct((E_BRANCH, d), BF16),
                   jax.ShapeDtypeStruct((N_MEM, 2 * E_MEM), F32),
                   jax.ShapeDtypeStruct((1, d), F32), jax.ShapeDtypeStruct((1, d), F32)],
        scratch_shapes=[pltpu.VMEM((E_BRANCH, d), F32)],
        compiler_params=_cparams("arbitrary"),
    )(dxn, r, g, mix, mem, h, h, h, h, kv, w_out_t)


def _local_step(x, mem, get_weights, put_grads, rel_bias, conv_w, ln_g, ln_b, target):
    biases = [_tile_bias(jnp.pad(rel_bias[a], ((0, 0), (0, N_REL_PAD - N_REL)))) for a in range(DEPTH // 2)]

    saved = []
    xl = x
    for layer in range(DEPTH):
        w_in_l, rest = get_weights(layer, xl if layer else biases)
        h, xt = _inproj(xl, w_in_l)
        w_kv_l, w_out_l = rest(h)
        if layer % 2 == 0:
            mix, aux = _attn_fwd(h, biases[layer // 2])
        else:
            aux = jnp.pad(conv_w()[layer // 2], ((0, SUBLANES - 3), (0, 0)))
            mix = _conv_fwd(h, aux)
        kv = _small_matmul(mem, w_kv_l, False, F32, "kv_mem")
        if layer < DEPTH - 1:
            x_next, r, mem_out = _layer_out_fwd(xl, h, mix, kv, w_out_l, ln_g[layer][None], ln_b[layer][None])
        else:
            dx, r, mem_out, loss = _layer_out_fwd(xl, h, mix, kv, w_out_l, ln_g[layer][None], ln_b[layer][None],
                                                  target)
            x_next = None
        saved.append((xt, h, aux, mix, kv, r, mem_out, w_in_l, w_out_l))
        xl = x_next

    dgs, dbs, d_rel, d_conv = [], [], [], []
    for layer in reversed(range(DEPTH)):
        xt, h, aux, mix, kv, r, mem_out, w_in_l, w_out_l = saved[layer]
        dx_res, d_mix, dqz, dwo, dkv, dg, db = _layer_out_bwd(
            dx, r, ln_g[layer][None], h, mix, mem_out, kv, w_out_l.T)
        early = put_grads(layer, [1, 2], [_small_matmul(mem, dkv, True, BF16, "dw_kv"), dwo])
        if layer % 2 == 0:
            dq, dk, dv, dtb = _attn_bwd(h, aux, d_mix, early)
            d_rel.append(_tile_bias_bwd(dtb)[:, :N_REL])
            pieces = [dq, dk, dv, dqz]
        else:
            dbg, dcg, du, dw8 = _conv_bwd(h, aux, d_mix)
            d_conv.append(dw8[:3])
            pieces = [dbg, dcg, du, dqz]
        dgs.append(dg[0])
        dbs.append(db[0])
        tokens = []
        if layer == 0:
            rev = lambda lst: jnp.stack(lst[::-1])
            tokens.append(put_grads(layer, [N_KIND], [_pack_small(rev(d_rel), rev(d_conv), rev(dgs), rev(dbs), loss)]))
        tokens.append(put_grads(layer, [0], [_dw_matmul(xt, pieces)]))
        dx = _dx_matmul(pieces, w_in_l, dx_res, tokens)

    return loss, dx


def _me():
    return lax.axis_index("x"), lax.axis_index("y"), lax.axis_index("c")


def _peer(k):
    x, y, c = _me()
    kx, ky, kc = (k >> 2) & 1, (k >> 1) & 1, k & 1
    return (1 - x if kx else x, 1 - y if ky else y, 1 - c if kc else c)


def _lin(dev):
    return 4 * dev[0] + 2 * dev[1] + dev[2]


ANY = pl.BlockSpec(memory_space=pl.ANY)


def _exchange(srcs, dst_shapes, src_slice, dst_slice, name):
    na = len(srcs)

    def body(*refs):
        src_refs = refs[:na]
        dst_refs = refs[na:2 * na]
        send_sems, recv_sems, local_sems = refs[2 * na:]
        me = _lin(_me())
        copies = []
        for a in range(na):
            loc = pltpu.make_async_copy(src_slice(a, src_refs[a], me), dst_slice(a, dst_refs[a], me),
                                        local_sems.at[a])
            loc.start()
            copies.append(loc)
            for k in range(1, N_DEV):
                peer = _peer(k)
                cp = pltpu.make_async_remote_copy(
                    src_ref=src_slice(a, src_refs[a], _lin(peer)),
                    dst_ref=dst_slice(a, dst_refs[a], me),
                    send_sem=send_sems.at[a, k - 1], recv_sem=recv_sems.at[a, k - 1],
                    device_id=peer, device_id_type=pl.DeviceIdType.MESH)
                cp.start()
                copies.append(cp)
        for cp in copies:
            cp.wait()

    return _call(
        body, name=name,
        in_specs=[ANY] * na, out_specs=[ANY] * na,
        out_shape=[jax.ShapeDtypeStruct(shp, s.dtype) for shp, s in zip(dst_shapes, srcs)],
        scratch_shapes=[pltpu.SemaphoreType.DMA((na, N_DEV - 1)),
                        pltpu.SemaphoreType.DMA((na, N_DEV - 1)),
                        pltpu.SemaphoreType.DMA((na,))],
    )(*srcs)


def _gather_to_all(src, name):
    return _exchange([src], [(N_DEV,) + src.shape], lambda a, ref, p: ref,
                     lambda a, ref, me: ref.at[me], name)[0]


HBM = pl.BlockSpec(memory_space=pltpu.HBM)
SEM = pl.BlockSpec(memory_space=pltpu.SEMAPHORE)
EFFECT = pltpu.SideEffectType.DATAFLOW_SIDE_EFFECTING
N_PEER = N_DEV - 1
N_KIND = 3


def _peer_copies(kind, src_ref, land_ref, send, recv, src_slice, dst_slice):
    me = _lin(_me())
    copies = []
    for k in range(1, N_DEV):
        peer = _peer(k)
        copies.append(pltpu.make_async_remote_copy(
            src_ref=src_slice(kind, src_ref, _lin(peer)),
            dst_ref=dst_slice(kind, land_ref, me, k),
            send_sem=send.at[k - 1], recv_sem=recv.at[k - 1],
            device_id=peer, device_id_type=pl.DeviceIdType.MESH))
    return copies


def _own_copy(kind, src_ref, land_ref, send, src_slice, dst_slice):
    me = _lin(_me())
    return pltpu.make_async_copy(src_slice(kind, src_ref, me), dst_slice(kind, land_ref, me, 0),
                                 send.at[N_PEER])


def _split_start(srcs, kinds, land_shapes, src_slice, dst_slice, name, own=False):
    na = len(srcs)

    def body(*refs):
        src_refs, land_refs = refs[:na], refs[na:2 * na]
        sems = refs[2 * na:4 * na]
        token = refs[-1]
        for a in range(na):
            for cp in _peer_copies(kinds[a], src_refs[a], land_refs[a], sems[2 * a], sems[2 * a + 1],
                                   src_slice, dst_slice):
                cp.start()
            if own:
                _own_copy(kinds[a], src_refs[a], land_refs[a], sems[2 * a], src_slice, dst_slice).start()
        token[...] = jnp.zeros_like(token)

    sem_shape = pltpu.SemaphoreType.DMA((N_DEV,))
    lands = [lax.empty(shp, s.dtype) for shp, s in zip(land_shapes, srcs)]
    outs = _call(
        body, name=name,
        in_specs=[HBM] * (2 * na),
        out_specs=[SEM] * (2 * na) + [HBM] * (2 * na) + [pl.BlockSpec(memory_space=pltpu.VMEM)],
        out_shape=[sem_shape] * (2 * na)
        + [pltpu.HBM(s.shape, s.dtype) for s in srcs]
        + [pltpu.HBM(shp, s.dtype) for shp, s in zip(land_shapes, srcs)]
        + [jax.ShapeDtypeStruct((SUBLANES, LANES), F32)],
        input_output_aliases={i: 2 * na + i for i in range(2 * na)},
        compiler_params=pltpu.CompilerParams(has_side_effects=EFFECT),
    )(*[pltpu.with_memory_space_constraint(a, pltpu.HBM) for a in list(srcs) + lands])
    sems = [(outs[2 * a], outs[2 * a + 1]) for a in range(na)]
    thrus = outs[2 * na:3 * na]
    lands = outs[3 * na:4 * na]
    return sems, thrus, lands, outs[-1]


def _split_wait(sems, thrus, lands, kinds, src_slice, dst_slice, after, name, own=False):
    na = len(thrus)
    after = list(after) if isinstance(after, (list, tuple)) else [after]

    def body(*refs):
        src_refs, land_refs = refs[:na], refs[na:2 * na]
        sem_refs = refs[2 * na:4 * na]
        for a in range(na):
            for cp in _peer_copies(kinds[a], src_refs[a], land_refs[a], sem_refs[2 * a], sem_refs[2 * a + 1],
                                   src_slice, dst_slice):
                cp.wait_send()
                cp.wait_recv()
            if own:
                _own_copy(kinds[a], src_refs[a], land_refs[a], sem_refs[2 * a], src_slice, dst_slice).wait()

    outs = _call(
        body, name=name,
        in_specs=[HBM] * (2 * na) + [SEM] * (2 * na) + [ANY] * len(after),
        out_specs=[HBM] * (2 * na),
        out_shape=[pltpu.HBM(a.shape, a.dtype) for a in list(thrus) + list(lands)],
        input_output_aliases={i: i for i in range(2 * na)},
        compiler_params=pltpu.CompilerParams(has_side_effects=EFFECT),
    )(*thrus, *lands, *[s for pair in sems for s in pair], *after)
    return outs[na:]


def _shard_dims(c_in, r_kv, r_out):
    def sl(j, ref, p):
        if j == 0:
            return ref.at[:, pl.ds(pl.multiple_of(p * c_in, LANES), c_in)]
        r = r_kv if j == 1 else r_out
        return ref.at[pl.ds(pl.multiple_of(p * r, 2 * SUBLANES), r), :]
    return sl


def _adamw_math(w, g, m, v):
    m = ADAM_B1 * m + (1.0 - ADAM_B1) * g
    v = ADAM_B2 * v + (1.0 - ADAM_B2) * (g * g)
    m_hat = m / (1.0 - ADAM_B1 ** ADAM_STEP)
    v_hat = v / (1.0 - ADAM_B2 ** ADAM_STEP)
    delta = -ADAM_LR * (m_hat / (jnp.sqrt(v_hat) + ADAM_EPS) + ADAM_WD * w)
    return delta, m, v


def _reduce_adamw(parts, w, m, v, name):
    rows, cols = w.shape
    tr = rows
    for cand in (512, 256, 128, 64, 32, 16):
        if rows % cand == 0 and rows > cand:
            tr = cand
            break

    def body(p_ref, w_ref, m_ref, v_ref, g_out, d_out, m_out, v_out):
        g = p_ref[0].astype(F32)
        for s in range(1, N_DEV):
            g = g + p_ref[s].astype(F32)
        g_out[...] = g
        d_out[...], m_out[...], v_out[...] = _adamw_math(w_ref[...], g, m_ref[...], v_ref[...])

    blk = pl.BlockSpec((tr, cols), lambda i: (i, 0))
    return _call(
        body, name=name, grid=(rows // tr,),
        in_specs=[pl.BlockSpec((N_DEV, tr, cols), lambda i: (0, i, 0)), blk, blk, blk],
        out_specs=[blk] * 4,
        out_shape=[jax.ShapeDtypeStruct((rows, cols), F32)] * 4,
        compiler_params=_cparams("parallel"),
    )(parts, w, m, v)


def _reduce_adamw_layers(lands, owns, w, m, v, name, first=0, prev=None):
    depth, rows, cols = w.shape
    nl = len(lands)
    tr = rows
    for cand in (256, 192, 128):
        if rows % cand == 0:
            tr = cand
            break

    kept = [] if prev is None else list(prev)

    def body(*refs):
        land_refs, own_refs = refs[:nl], refs[nl:2 * nl]
        w_ref, m_ref, v_ref = refs[2 * nl:2 * nl + 3]
        g_out, d_out, m_out, v_out = refs[2 * nl + 3 + len(kept):]
        layer = pl.program_id(0)
        for a in range(nl):
            @pl.when(layer == a)
            def _(a=a):
                g = own_refs[a][...].astype(F32)
                for k in range(N_PEER):
                    g = g + land_refs[a][k].astype(F32)
                g_out[...] = g
                d_out[...], m_out[...], v_out[...] = _adamw_math(w_ref[...], g, m_ref[...], v_ref[...])

    def lmap(a):
        return lambda l, i: (0, jnp.where(l == a, i, 0), 0)

    def omap(a):
        return lambda l, i: (jnp.where(l == a, i, 0), 0)

    blk = pl.BlockSpec((None, tr, cols), lambda l, i: (l + first, i, 0))
    n_in = 2 * nl + 3
    return _call(
        body, name=name, grid=(nl, rows // tr),
        in_specs=[pl.BlockSpec((N_PEER, tr, cols), lmap(a)) for a in range(nl)]
        + [pl.BlockSpec((tr, cols), omap(a)) for a in range(nl)] + [blk, blk, blk] + [ANY] * len(kept),
        out_specs=[blk] * 4,
        out_shape=[jax.ShapeDtypeStruct((depth, rows, cols), F32)] * 4,
        input_output_aliases={n_in + i: i for i in range(len(kept))},
        compiler_params=_cparams("arbitrary", "arbitrary"),
    )(*lands, *owns, w, m, v, *kept)


SM_G, SM_B, SM_CONV, SM_REL = 0, 4, 8, 16
SM_ROWS = SM_REL + 2 * N_HEADS
REL_W = 384


SM_LOSS = SM_CONV + 6


def _pack_small(d_rel, d_conv, dg, db, loss=None):
    buf = jnp.zeros((SM_ROWS, D_MODEL), F32)
    if loss is not None:
        buf = buf.at[SM_LOSS:SM_LOSS + 1, 0:1].set(loss)
    buf = buf.at[SM_G:SM_G + DEPTH].set(dg)
    buf = buf.at[SM_B:SM_B + DEPTH].set(db)
    buf = buf.at[SM_CONV:SM_CONV + 6].set(d_conv.reshape(6, E_MIX))
    buf = buf.at[SM_REL:, :N_REL].set(d_rel.reshape(2 * N_HEADS, N_REL))
    return buf


def kernel(x, mem, w_in, w_mem_kv, w_out, rel_bias, conv_w, ln_g, ln_b, loss_target, m_w_in, m_w_mem_kv, m_w_out, m_rel_bias, m_conv_w, m_ln_g, m_ln_b, v_w_in, v_w_mem_kv, v_w_out, v_rel_bias, v_conv_w, v_ln_g, v_ln_b):
    me = _lin(_me())
    c_in, r_kv, r_out, c_conv = w_in.shape[2], w_mem_kv.shape[1], w_out.shape[1], conv_w.shape[2]

    shard = _shard_dims(c_in, r_kv, r_out)
    own_start = lambda j: (0, me * c_in) if j == 0 else (me * (r_kv if j == 1 else r_out), 0)

    w_sh = [w_in.astype(BF16), w_mem_kv.astype(BF16), w_out.astype(BF16)]
    full_shapes = [(D_MODEL, N_DEV * c_in), (N_DEV * r_kv, w_mem_kv.shape[2]), (N_DEV * r_out, D_MODEL)]
    conv_kind = N_KIND
    conv_tile = jnp.pad(conv_w.reshape(6, c_conv), ((0, SUBLANES - 6), (0, 0)))
    ag_src = lambda j, ref, p: ref
    ag_dst = lambda j, ref, me_, k: ref.at[me_] if j == conv_kind else shard(j, ref, me_)
    kinds = list(range(N_KIND))
    ag_sems, ag_thrus, ag_lands, _ = _split_start(
        [conv_tile] + [w_sh[j][layer] for layer in range(DEPTH) for j in kinds], [conv_kind] + kinds * DEPTH,
        [(N_DEV,) + conv_tile.shape] + full_shapes * DEPTH, ag_src, ag_dst, "ag_start", own=True)
    conv_landed = []

    def get_weights(layer, x_layer):
        lo = 1 + layer * N_KIND

        def wait(idx, js, after, name):
            return _split_wait([ag_sems[a] for a in idx], [ag_thrus[a] for a in idx],
                               [ag_lands[a] for a in idx], js, ag_src, ag_dst, after, name, own=True)

        if layer == 0:
            conv_land, w_in_l = wait([0, lo], [conv_kind, 0], x_layer, "ag_wait_in_0")
            conv_landed.append(conv_land)
        else:
            w_in_l, = wait([lo], [0], x_layer, "ag_wait_in_%d" % layer)
        return w_in_l, lambda h: wait([lo + 1, lo + 2], [1, 2], h, "ag_wait_kv_out_%d" % layer)

    def conv_full():
        return jnp.transpose(conv_landed[0][:, :6], (1, 0, 2)).reshape(2, 3, N_DEV * c_conv)

    rs_src = shard
    rs_dst = lambda j, ref, me_, k: ref.at[k - 1]
    rs_shapes = [(N_PEER, D_MODEL, c_in), (N_PEER, r_kv, w_mem_kv.shape[2]), (N_PEER, r_out, D_MODEL)]
    own_sizes = [(D_MODEL, c_in), (r_kv, w_mem_kv.shape[2]), (r_out, D_MODEL)]
    pending = {}

    held = {}

    small_flight = []

    def put_grads(layer, js, arrays):
        if js == [conv_kind]:
            sems, thrus, lands, token = _split_start(arrays, js, [(N_DEV,) + arrays[0].shape], ag_src, ag_dst,
                                                     "small_grads_start", own=True)
            small_flight.append((sems, thrus, lands))
            return token
        if layer > 0 and js != [0]:
            held[layer] = (js, arrays)
            return None
        if layer > 0:
            js, arrays = held[layer][0] + js, held[layer][1] + arrays
        owns = [lax.dynamic_slice(a, own_start(j), own_sizes[j]) for j, a in zip(js, arrays)]
        sems, thrus, lands, token = _split_start(
            arrays, js, [rs_shapes[j] for j in js], rs_src, rs_dst,
            "rs_start_%d_%s" % (layer, "".join(str(j) for j in js)))
        entry = pending.setdefault(layer, ([], [], [], [], []))
        for lst, new in zip(entry, (js, sems, thrus, lands, owns)):
            lst.extend(new)
        return token


    loss, grad_x = _local_step(
        x[0], mem[0], get_weights, put_grads, rel_bias, conv_full, ln_g, ln_b, loss_target[0])

    sems, thrus, lands = small_flight[0]
    p_small, = _split_wait(sems, thrus, lands, [conv_kind], ag_src, ag_dst, grad_x, "small_grads_wait", own=True)

    rs_lands, rs_owns = {}, {}

    def rs_wait(layer, want, after, name):
        js, sems, thrus, lands, owns = pending[layer]
        pos = [js.index(j) for j in want]
        got = _split_wait([sems[p] for p in pos], [thrus[p] for p in pos], [lands[p] for p in pos],
                          want, rs_src, rs_dst, after, name)
        for j, p, land in zip(want, pos, got):
            rs_lands[layer, j], rs_owns[layer, j] = land, owns[p]

    for layer in range(1, DEPTH):
        rs_wait(layer, kinds, grad_x, "rs_wait_%d" % layer)
    rs_wait(0, [1, 2], grad_x, "rs_wait_0_kv_out")

    def big(j, w, m, v, name, layers, prev=None):
        return _reduce_adamw_layers([rs_lands[layer, j] for layer in layers], [rs_owns[layer, j] for layer in layers],
                                    w, m, v, name, first=layers[0], prev=prev)

    every = list(range(DEPTH))
    g_kv, d_kv, nm_kv, nv_kv = big(1, w_mem_kv, m_w_mem_kv, v_w_mem_kv, "adamw_w_kv", every)
    g_out, d_out, nm_out, nv_out = big(2, w_out, m_w_out, v_w_out, "adamw_w_out", every)
    later = big(0, w_in, m_w_in, v_w_in, "adamw_w_in_later_layers", every[1:])

    def pack_state(rel, conv, g, b):
        conv_full = jnp.zeros((2, 3, E_MIX), F32)
        conv_full = lax.dynamic_update_slice(conv_full, conv, (0, 0, me * c_conv))
        return _pack_small(rel, conv_full, g, b)

    sm_w = pack_state(rel_bias, conv_w, ln_g, ln_b)
    sm_m = pack_state(m_rel_bias, m_conv_w, m_ln_g, m_ln_b)
    sm_v = pack_state(v_rel_bias, v_conv_w, v_ln_g, v_ln_b)
    sm_outs = _reduce_adamw(p_small, sm_w, sm_m, sm_v, "adamw_small")

    rs_wait(0, [0], [later[0], sm_outs[0], g_kv, g_out], "rs_wait_0_in")
    g_in, d_in, nm_in, nv_in = big(0, w_in, m_w_in, v_w_in, "adamw_w_in_layer_0", [0], prev=later)

    def unpack(buf):
        rel = buf[SM_REL:, :N_REL].reshape(2, N_HEADS, N_REL)
        conv = lax.dynamic_slice(buf[SM_CONV:SM_CONV + 6].reshape(2, 3, E_MIX), (0, 0, me * c_conv), (2, 3, c_conv))
        return rel, conv, buf[SM_G:SM_G + DEPTH], buf[SM_B:SM_B + DEPTH]

    g_sm, d_sm, nm_sm, nv_sm = [unpack(b) for b in sm_outs]

    loss = sm_outs[0][SM_LOSS, 0]
    return (loss, grad_x[None],
            g_in, g_kv, g_out, *g_sm,
            d_in, d_kv, d_out, *d_sm,
            nm_in, nm_kv, nm_out, *nm_sm,
            nv_in, nv_kv, nv_out, *nv_sm)
```
